```python
import jax, jax.numpy as jnp
from jax import lax
import numpy as np

D_MODEL = 1024
BATCH = 8
SEQ = 4096
DEPTH = 2

CTX_LEN = 256
GRID_W = 64
HEAD_DIM = 64
H_ML = 4
H_NA = 6
H_GQ = 6
H_KV = 2
D_ML = H_ML * HEAD_DIM
D_NA = H_NA * HEAD_DIM
D_GQ = H_GQ * HEAD_DIM
D_KV = H_KV * HEAD_DIM
GQ_GROUP = H_GQ // H_KV
N_BRANCH = 3
D_FF = 2816
NA_WIN_R = 8
NA_WIN_C = 16
ML_CHUNK = 64
Q_BLOCK = 128
ROPE_THETA = 10000.0
EPS = 1e-6
N_MOD = 9
ATTN_SCALE = HEAD_DIM ** -0.5
KV_SPLITS = (D_ML, D_ML, 4 * H_ML, D_NA, D_NA, D_KV, D_KV)
Q_SPLITS = (D_ML, D_ML, D_NA, D_GQ, N_BRANCH * D_MODEL)
N_KV_COLS = sum(KV_SPLITS)
N_IN = N_KV_COLS + sum(Q_SPLITS)

kernel_name = 'hybrid_mlstm_natten_gqa_prefix_block'


def _split(a, sizes):
    return jnp.split(a, np.cumsum(sizes)[:-1].tolist(), axis=-1)


def _heads(a, n):
    return a.reshape(a.shape[:-1] + (n, HEAD_DIM))


def rms_norm(x, w):
    x32 = x.astype(jnp.float32)
    y = x32 * lax.rsqrt(jnp.mean(x32 * x32, axis=-1, keepdims=True) + EPS)
    return (y * w.astype(jnp.float32)).astype(x.dtype)


def modulate(x, w_norm, shift, scale):
    return rms_norm(x, w_norm) * (1 + scale) + shift


def swiglu(h, w_in, w_out):
    g, u = jnp.split(h @ w_in, 2, axis=-1)
    return (jax.nn.silu(g) * u) @ w_out


def axial_rope(n_tok):
    t = jnp.arange(n_tok, dtype=jnp.int32)
    row = (t // GRID_W).astype(jnp.float32)
    col = (t % GRID_W).astype(jnp.float32)
    n_freq = HEAD_DIM // 4
    inv = ROPE_THETA ** (-jnp.arange(n_freq, dtype=jnp.float32) / n_freq)
    ang = jnp.concatenate([row[:, None] * inv, col[:, None] * inv], axis=-1)
    return jnp.cos(ang)[:, None, :], jnp.sin(ang)[:, None, :]


def apply_rope(x, cos, sin):
    x32 = x.astype(jnp.float32)
    x1, x2 = jnp.split(x32, 2, axis=-1)
    return jnp.concatenate([x1 * cos - x2 * sin, x1 * sin + x2 * cos], axis=-1).astype(x.dtype)


def attend(q, k, v):
    s = jnp.einsum('btkgd,bnkd->bkgtn', q, k).astype(jnp.float32) * ATTN_SCALE
    p = jax.nn.softmax(s, axis=-1).astype(v.dtype)
    return jnp.einsum('bkgtn,bnkd->btkgd', p, v)


def gqa_latent(q, k, v, kc, vc):
    B, S = q.shape[:2]
    k_all = jnp.concatenate([k, kc], axis=1)
    v_all = jnp.concatenate([v, vc], axis=1)
    qb = q.reshape(B, S // Q_BLOCK, Q_BLOCK, H_KV, GQ_GROUP, HEAD_DIM).swapaxes(0, 1)
    out = lax.map(lambda qi: attend(qi, k_all, v_all), qb)
    return out.swapaxes(0, 1).reshape(B, S, D_GQ)


def na_latent(q, k, v, kc, vc, rpb, n_rows):
    B, S, H, d = q.shape
    wr = min(NA_WIN_R, n_rows)
    n_win = wr * NA_WIN_C
    col = np.arange(GRID_W)
    col_idx = np.clip(col - NA_WIN_C // 2, 0, GRID_W - NA_WIN_C)[:, None] + np.arange(NA_WIN_C)[None, :]
    col_off = col_idx - col[:, None] + NA_WIN_C - 1
    grid = lambda a: a.reshape(B, n_rows, GRID_W, H, d)
    kg, vg = grid(k), grid(v)
    qg = jnp.swapaxes(grid(q), 0, 1)

    def row_block(args):
        r, q_r = args
        rs = jnp.clip(r - wr // 2, 0, n_rows - wr)

        def window(a):
            band = lax.dynamic_slice_in_dim(a, rs, wr, axis=1)
            return jnp.swapaxes(band[:, :, col_idx], 1, 2).reshape(B, GRID_W, n_win, H, d)

        kw, vw = window(kg), window(vg)
        row_off = rs + jnp.arange(wr) - r + NA_WIN_R - 1
        bias = rpb[:, row_off][:, :, col_off]
        bias = jnp.swapaxes(bias, 1, 2).reshape(H, GRID_W, n_win).astype(jnp.float32)
        s_win = jnp.einsum('bwhd,bwnhd->bhwn', q_r, kw).astype(jnp.float32) * ATTN_SCALE + bias
        s_ctx = jnp.einsum('bwhd,blhd->bhwl', q_r, kc).astype(jnp.float32) * ATTN_SCALE
        p = jax.nn.softmax(jnp.concatenate([s_win, s_ctx], axis=-1), axis=-1).astype(v.dtype)
        return (jnp.einsum('bhwn,bwnhd->bwhd', p[..., :n_win], vw)
                + jnp.einsum('bhwl,blhd->bwhd', p[..., n_win:], vc))

    out = lax.map(row_block, (jnp.arange(n_rows, dtype=jnp.int32), qg))
    return jnp.swapaxes(out, 0, 1).reshape(B, S, H * d)


def mlstm_scan(q, k, v, ig, lf, state):
    B, T, H, d = k.shape
    nc = T // ML_CHUNK
    chunked = lambda a: jnp.swapaxes(a.reshape((B, nc, ML_CHUNK) + a.shape[2:]), 0, 1)
    with_out = q is not None
    xs = (chunked(k), chunked(v), chunked(ig), chunked(lf)) + ((chunked(q),) if with_out else ())
    tri = jnp.asarray(np.tril(np.ones((ML_CHUNK, ML_CHUNK), dtype=bool)))

    def step(carry, xs_c):
        C, n, m = carry
        kc, vc, ic, fc = xs_c[:4]
        b = jnp.cumsum(fc, axis=1)
        b_tot = b[:, -1]
        w_end = b_tot[:, None] - b + ic
        m_new = jnp.maximum(b_tot + m, jnp.max(w_end, axis=1))
        a_end = jnp.exp(w_end - m_new[:, None])
        decay = jnp.exp(b_tot + m - m_new)
        C_new = decay[..., None, None] * C + jnp.einsum('blh,blhk,blhv->bhkv', a_end, kc, vc)
        n_new = decay[..., None] * n + jnp.einsum('blh,blhk->bhk', a_end, kc)
        if not with_out:
            return (C_new, n_new, m_new), None
        q_c = xs_c[4]
        m_inter = b + m[:, None]
        logw = b[:, :, None] - b[:, None, :] + ic[:, None]
        logw = jnp.where(tri[None, :, :, None], logw, -jnp.inf)
        m_j = jnp.maximum(m_inter, jnp.max(logw, axis=2))
        w = jnp.exp(logw - m_j[:, :, None])
        g = jnp.exp(m_inter - m_j)
        qk = jnp.einsum('bjhd,bshd->bjsh', q_c, kc) * w
        num = jnp.einsum('bjsh,bshd->bjhd', qk, vc) + g[..., None] * jnp.einsum('bjhk,bhkv->bjhv', q_c, C)
        den = jnp.sum(qk, axis=2) + g * jnp.einsum('bjhk,bhk->bjh', q_c, n)
        h = num / jnp.maximum(jnp.abs(den), jnp.exp(-m_j))[..., None]
        return (C_new, n_new, m_new), h

    state, hs = lax.scan(step, state, xs)
    h = None if hs is None else jnp.swapaxes(hs, 0, 1).reshape(B, T, H, d)
    return state, h


def mlstm_gates(pre, bias):
    g = (pre.astype(jnp.float32) + bias.astype(jnp.float32)).reshape(pre.shape[:-1] + (4, H_ML))
    fwd = (g[..., 0, :], jax.nn.log_sigmoid(g[..., 1, :]))
    bwd = (g[..., 2, :], jax.nn.log_sigmoid(g[..., 3, :]))
    return fwd, bwd


def _flip(a):
    return None if a is None else jnp.flip(a, axis=1)


def mlstm_bidir(q, k, v, g_pre, qc, kc, vc, gc_pre, gate_b):
    f32 = jnp.float32
    B = k.shape[0]
    q, k, v = q.astype(f32), k.astype(f32) * ATTN_SCALE, v.astype(f32)
    kc, vc = kc.astype(f32) * ATTN_SCALE, vc.astype(f32)
    qc = None if qc is None else qc.astype(f32)
    (ig_f, lf_f), (ig_b, lf_b) = mlstm_gates(g_pre, gate_b)
    (igc_f, lfc_f), (igc_b, lfc_b) = mlstm_gates(gc_pre, gate_b)
    init = (jnp.zeros((B, H_ML, HEAD_DIM, HEAD_DIM), f32), jnp.zeros((B, H_ML, HEAD_DIM), f32),
            jnp.zeros((B, H_ML), f32))
    st_f, hc_f = mlstm_scan(qc, kc, vc, igc_f, lfc_f, init)
    _, h_f = mlstm_scan(q, k, v, ig_f, lf_f, st_f)
    st_b, hc_b = mlstm_scan(_flip(qc), _flip(kc), _flip(vc), _flip(igc_b), _flip(lfc_b), init)
    _, h_b = mlstm_scan(_flip(q), _flip(k), _flip(v), _flip(ig_b), _flip(lf_b), st_b)
    h = h_f + _flip(h_b)
    h_c = None if qc is None else hc_f + _flip(hc_b)
    return h, h_c


def mlstm_out(h, o, norm_w):
    y = rms_norm(h, norm_w.reshape(H_ML, HEAD_DIM)).astype(o.dtype)
    return y.reshape(o.shape) * jax.nn.sigmoid(o)


def merge_branches(br_g, o_ml, o_na, o_gq, w_br_ml, w_br_na, w_br_gq, w_out):
    g_ml, g_na, g_gq = jnp.split(jax.nn.sigmoid(br_g), N_BRANCH, axis=-1)
    y = g_ml * (o_ml @ w_br_ml) + g_na * (o_na @ w_br_na) + g_gq * (o_gq @ w_br_gq)
    return y @ w_out


def token_mixer(hx, hc, n_rows, cos, sin, w_in, ml_gate_b, ml_norm_w, na_qk_w, na_rpb, gq_qk_w,
                w_br_ml, w_br_na, w_br_gq, w_out, ctx_out):
    B, S, _ = hx.shape
    L = hc.shape[1]
    px = hx @ w_in
    pc = hc @ (w_in if ctx_out else w_in[:, :N_KV_COLS])
    ml_k, ml_v, ml_g, na_k, na_v, gq_k, gq_v = _split(px[..., :N_KV_COLS], KV_SPLITS)
    ml_q, ml_o, na_q, gq_q, br_g = _split(px[..., N_KV_COLS:], Q_SPLITS)
    cml_k, cml_v, cml_g, cna_k, cna_v, cgq_k, cgq_v = _split(pc[..., :N_KV_COLS], KV_SPLITS)
    if ctx_out:
        cml_q, cml_o, cna_q, cgq_q, cbr_g = _split(pc[..., N_KV_COLS:], Q_SPLITS)

    h_ml, hc_ml = mlstm_bidir(_heads(ml_q, H_ML), _heads(ml_k, H_ML), _heads(ml_v, H_ML), ml_g,
                              _heads(cml_q, H_ML) if ctx_out else None,
                              _heads(cml_k, H_ML), _heads(cml_v, H_ML), cml_g, ml_gate_b)
    o_ml = mlstm_out(h_ml, ml_o, ml_norm_w)

    nq = rms_norm(_heads(na_q, H_NA), na_qk_w[0])
    nk = rms_norm(_heads(na_k, H_NA), na_qk_w[1])
    cnk = rms_norm(_heads(cna_k, H_NA), na_qk_w[1])
    cnv = _heads(cna_v, H_NA)
    o_na = na_latent(nq, nk, _heads(na_v, H_NA), cnk, cnv, na_rpb, n_rows)

    gq = apply_rope(rms_norm(_heads(gq_q, H_GQ), gq_qk_w[0]), cos, sin)
    gk = apply_rope(rms_norm(_heads(gq_k, H_KV), gq_qk_w[1]), cos, sin)
    cgk = rms_norm(_heads(cgq_k, H_KV), gq_qk_w[1])
    cgv = _heads(cgq_v, H_KV)
    o_gq = gqa_latent(gq, gk, _heads(gq_v, H_KV), cgk, cgv)

    yx = merge_branches(br_g, o_ml, o_na, o_gq, w_br_ml, w_br_na, w_br_gq, w_out)
    if not ctx_out:
        return yx, None
    co_ml = mlstm_out(hc_ml, cml_o, ml_norm_w)
    cq_na = rms_norm(_heads(cna_q, H_NA), na_qk_w[0])[:, :, :, None, :]
    co_na = attend(cq_na, cnk, cnv).reshape(B, L, D_NA)
    cq_gq = rms_norm(_heads(cgq_q, H_GQ), gq_qk_w[0]).reshape(B, L, H_KV, GQ_GROUP, HEAD_DIM)
    co_gq = attend(cq_gq, cgk, cgv).reshape(B, L, D_GQ)
    yc = merge_branches(cbr_g, co_ml, co_na, co_gq, w_br_ml, w_br_na, w_br_gq, w_out)
    return yx, yc


def setup_inputs(seed: int = 0) -> dict:
    key = jax.random.key(seed)
    ks = jax.random.split(key, 20)
    nrm = lambda k, shape, s: s * jax.random.normal(k, shape, jnp.float32)
    gate_base = np.concatenate([np.zeros(H_ML), np.linspace(3.0, 6.0, H_ML),
                                np.zeros(H_ML), np.linspace(3.0, 6.0, H_ML)]).astype(np.float32)
    return {
        'x': nrm(ks[0], (BATCH, SEQ, D_MODEL), 1.0),
        'c': nrm(ks[1], (BATCH, D_MODEL), 1.0),
        'ctx': nrm(ks[2], (BATCH, CTX_LEN, D_MODEL), 1.0),
        'c_ctx': nrm(ks[3], (D_MODEL,), 1.0),
        'ada_w': nrm(ks[4], (DEPTH, D_MODEL, N_MOD * D_MODEL), 0.5 * D_MODEL ** -0.5),
        'ada_b': nrm(ks[5], (DEPTH, N_MOD * D_MODEL), 0.01),
        'norm_w': 1.0 + nrm(ks[6], (DEPTH, 3, D_MODEL), 0.02),
        'ffn_w_in': nrm(ks[7], (DEPTH, 2, D_MODEL, 2 * D_FF), D_MODEL ** -0.5),
        'ffn_w_out': nrm(ks[8], (DEPTH, 2, D_FF, D_MODEL), D_FF ** -0.5),
        'mix_w_in': nrm(ks[9], (DEPTH, D_MODEL, N_IN), D_MODEL ** -0.5),
        'ml_gate_b': jnp.asarray(gate_base)[None, :] + nrm(ks[10], (DEPTH, 4 * H_ML), 0.01),
        'ml_norm_w': 1.0 + nrm(ks[11], (DEPTH, D_ML), 0.02),
        'na_qk_w': 1.0 + nrm(ks[12], (DEPTH, 2, HEAD_DIM), 0.02),
        'na_rpb': nrm(ks[13], (DEPTH, H_NA, 2 * NA_WIN_R - 1, 2 * NA_WIN_C - 1), 0.2),
        'gq_qk_w': 1.0 + nrm(ks[14], (DEPTH, 2, HEAD_DIM), 0.02),
        'w_br_ml': nrm(ks[15], (DEPTH, D_ML, D_MODEL), D_ML ** -0.5),
        'w_br_na': nrm(ks[16], (DEPTH, D_NA, D_MODEL), D_NA ** -0.5),
        'w_br_gq': nrm(ks[17], (DEPTH, D_GQ, D_MODEL), D_GQ ** -0.5),
        'w_out': nrm(ks[18], (DEPTH, D_MODEL, D_MODEL), D_MODEL ** -0.5),
    }


def reference(x, c, ctx, c_ctx, ada_w, ada_b, norm_w, ffn_w_in, ffn_w_out, mix_w_in, ml_gate_b,
              ml_norm_w, na_qk_w, na_rpb, gq_qk_w, w_br_ml, w_br_na, w_br_gq, w_out):
    n_rows = x.shape[1] // GRID_W
    cos, sin = axial_rope(x.shape[1])
    xc = ctx
    silu_c = jax.nn.silu(c)
    silu_cc = jax.nn.silu(c_ctx)
    for l in range(DEPTH):
        ctx_out = l < DEPTH - 1
        mx = jnp.split((silu_c @ ada_w[l] + ada_b[l])[:, None, :], N_MOD, axis=-1)
        mc = jnp.split(silu_cc @ ada_w[l] + ada_b[l], N_MOD, axis=-1)
        x = x + 0.5 * mx[2] * swiglu(modulate(x, norm_w[l, 0], mx[0], mx[1]), ffn_w_in[l, 0], ffn_w_out[l, 0])
        xc = xc + 0.5 * mc[2] * swiglu(modulate(xc, norm_w[l, 0], mc[0], mc[1]), ffn_w_in[l, 0], ffn_w_out[l, 0])
        yx, yc = token_mixer(modulate(x, norm_w[l, 1], mx[3], mx[4]), modulate(xc, norm_w[l, 1], mc[3], mc[4]),
                             n_rows, cos, sin, mix_w_in[l], ml_gate_b[l], ml_norm_w[l], na_qk_w[l], na_rpb[l],
                             gq_qk_w[l], w_br_ml[l], w_br_na[l], w_br_gq[l], w_out[l], ctx_out)
        x = x + mx[5] * yx
        x = x + 0.5 * mx[8] * swiglu(modulate(x, norm_w[l, 2], mx[6], mx[7]), ffn_w_in[l, 1], ffn_w_out[l, 1])
        if ctx_out:
            xc = xc + mc[5] * yc
            xc = xc + 0.5 * mc[8] * swiglu(modulate(xc, norm_w[l, 2], mc[6], mc[7]), ffn_w_in[l, 1], ffn_w_out[l, 1])
    return x
```

```python
import functools

import numpy as np
import jax
import jax.numpy as jnp
from jax import lax
from jax.experimental import pallas as pl
from jax.experimental.pallas import tpu as pltpu

F32 = jnp.float32
BF16 = jnp.bfloat16

HEAD_DIM = 64
LANES = 128
H_ML, H_NA, H_GQ, H_KV = 4, 6, 6, 2
D_ML, D_NA, D_GQ, D_KV = 256, 384, 384, 128
GRID_W = 64
NA_WIN_R, NA_WIN_C = 8, 16
ROPE_THETA = 10000.0
EPS = 1e-6
N_MOD = 9
ATTN_SCALE = HEAD_DIM ** -0.5
NEG = -1e30

ML_CHUNK = 256
NA_QROWS = 8
NA_BAND = 16
TM = 512
GQ_TQ = 256
MOD_ROWS = 16
VMEM_LIMIT = 56 * 1024 * 1024

C_ML, C_MLG, C_NA, C_GQ, C_BRG = 0, 1024, 1152, 2304, 2944
N_PROJ = 6016
GQ_HEAD_ORDER = (0, 3, 1, 4, 2, 5)


def _dot(a, b):
    return jnp.dot(a, b, preferred_element_type=F32)


def _dot_nt(a, b):
    return lax.dot_general(a, b, (((1,), (1,)), ((), ())), preferred_element_type=F32)


def _sigmoid(x):
    return 1.0 / (1.0 + jnp.exp(-x))


def _log_sigmoid(x):
    return jnp.minimum(x, 0.0) - jnp.log1p(jnp.exp(-jnp.abs(x)))


def _split_bf16(x):
    hi = x.astype(BF16)
    lo = (x - hi.astype(F32)).astype(BF16)
    return hi, lo


def _modnorm(x, nw, shift, scale):
    ms = jnp.mean(x * x, axis=-1, keepdims=True)
    return (x * lax.rsqrt(ms + EPS) * nw) * (1.0 + scale) + shift


def _head_norm(t, gmat, wrow):
    hi, lo = _split_bf16(t * t)
    ss = _dot(hi, gmat) + _dot(lo, gmat)
    return t * lax.rsqrt(ss * (1.0 / HEAD_DIM) + EPS) * wrow


def _resident(shape):
    return pl.BlockSpec(shape, lambda *_: (0,) * len(shape), pipeline_mode=pl.Buffered(1))


def _params(sem):
    return pltpu.CompilerParams(dimension_semantics=sem, vmem_limit_bytes=VMEM_LIMIT)


def _ada_kernel(c_ref, w_ref, b_ref, o_ref):
    c = c_ref[...]
    s = (c * _sigmoid(c)).astype(BF16)
    o_ref[0] = _dot(s, w_ref[0].astype(BF16)) + b_ref[0]


def _ada(cvec, ada_w, ada_b):
    depth, d, n = ada_w.shape
    tn = n // 8
    return pl.pallas_call(
        _ada_kernel,
        grid=(depth, n // tn),
        in_specs=[pl.BlockSpec((MOD_ROWS, d), lambda l, j: (0, 0)),
                  pl.BlockSpec((1, d, tn), lambda l, j: (l, 0, j)),
                  pl.BlockSpec((1, 1, tn), lambda l, j: (l, 0, j))],
        out_specs=pl.BlockSpec((1, MOD_ROWS, tn), lambda l, j: (l, 0, j)),
        out_shape=jax.ShapeDtypeStruct((depth, MOD_ROWS, n), F32),
        compiler_params=_params(("arbitrary", "arbitrary")),
        name="ada_mod",
    )(cvec, ada_w, ada_b.reshape(depth, 1, n))


def _mod_index(tiles_per_batch, ctx_row):
    if tiles_per_batch is None:
        return lambda i, *_: (ctx_row, 0, 0)
    return lambda i, *_: (i // tiles_per_batch, 0, 0)


def _ffn_kernel(x_ref, mod_ref, nw_ref, wg_ref, wu_ref, wo_ref, o_ref, h_scr, acc_scr, *, k0, nrm, nj):
    j = pl.program_id(1)

    @pl.when(j == 0)
    def _():
        h = _modnorm(x_ref[...], nw_ref[nrm:nrm + 1, :], mod_ref[0, k0:k0 + 1, :], mod_ref[0, k0 + 1:k0 + 2, :])
        h_scr[...] = h.astype(BF16)

    h = h_scr[...]
    g = _dot(h, wg_ref[...])
    u = _dot(h, wu_ref[...])
    a = (g * _sigmoid(g) * u).astype(BF16)
    y = _dot(a, wo_ref[...])

    @pl.when(j == 0)
    def _():
        acc_scr[...] = y

    @pl.when(j > 0)
    def _():
        acc_scr[...] += y

    @pl.when(j == nj - 1)
    def _():
        o_ref[...] = x_ref[...] + (0.5 * mod_ref[0, k0 + 2:k0 + 3, :]) * acc_scr[...]


def _ffn(x, mod, nw, w_in, w_out, *, k0, nrm, tiles_per_batch, ctx_row):
    t, d = x.shape
    dff = w_out.shape[0]
    nj = 2
    tf = dff // nj
    tm = min(TM, t)
    kern = functools.partial(_ffn_kernel, k0=k0, nrm=nrm, nj=nj)
    return pl.pallas_call(
        kern,
        grid=(t // tm, nj),
        in_specs=[pl.BlockSpec((tm, d), lambda i, j: (i, 0)),
                  pl.BlockSpec((1, N_MOD, d), _mod_index(tiles_per_batch, ctx_row)),
                  pl.BlockSpec((3, d), lambda i, j: (0, 0)),
                  pl.BlockSpec((d, tf), lambda i, j: (0, j)),
                  pl.BlockSpec((d, tf), lambda i, j: (0, nj + j)),
                  pl.BlockSpec((tf, d), lambda i, j: (j, 0))],
        out_specs=pl.BlockSpec((tm, d), lambda i, j: (i, 0)),
        out_shape=jax.ShapeDtypeStruct((t, d), F32),
        scratch_shapes=[pltpu.VMEM((tm, d), BF16), pltpu.VMEM((tm, d), F32)],
        compiler_params=_params(("parallel", "arbitrary")),
        name="ffn_swiglu",
    )(x, mod, nw, w_in, w_in, w_out)


def _proj_kernel(*refs, rope):
    if rope:
        (x_ref, mod_ref, nw_ref, w_ref, g_ref, qkw_ref, cos_ref, sin_ref,
         ml_ref, mlg_ref, na_ref, gq_ref, brg_ref) = refs
    else:
        (x_ref, mod_ref, nw_ref, w_ref, g_ref, qkw_ref,
         ml_ref, mlg_ref, na_ref, gq_ref, brg_ref) = refs
    h = _modnorm(x_ref[...], nw_ref[1:2, :], mod_ref[0, 3:4, :], mod_ref[0, 4:5, :]).astype(BF16)
    gmat = g_ref[...]

    ml_ref[...] = _dot(h, w_ref[:, C_ML:C_MLG]).astype(BF16)
    mlg_ref[...] = _dot(h, w_ref[:, C_MLG:C_NA])

    na = _dot(h, w_ref[:, C_NA:C_GQ])
    for j in range(6):
        sl = slice(LANES * j, LANES * (j + 1))
        wsl = slice(LANES * (j % 3), LANES * (j % 3 + 1))
        na_ref[:, sl] = _head_norm(na[:, sl], gmat, qkw_ref[j // 3:j // 3 + 1, wsl]).astype(BF16)
    na_ref[:, 2 * D_NA:] = na[:, 2 * D_NA:].astype(BF16)

    gq = _dot(h, w_ref[:, C_GQ:C_BRG])
    if rope:
        lane = lax.broadcasted_iota(jnp.int32, (1, LANES), 1)
        first_half = (lane % HEAD_DIM) < (HEAD_DIM // 2)
        cos = cos_ref[...]
        sin = sin_ref[...]
    for j in range(4):
        sl = slice(LANES * j, LANES * (j + 1))
        t = _head_norm(gq[:, sl], gmat, qkw_ref[2:3, sl])
        if rope:
            rot = jnp.where(first_half, pltpu.roll(t, LANES - HEAD_DIM // 2, axis=1),
                            pltpu.roll(t, HEAD_DIM // 2, axis=1))
            t = t * cos + rot * sin
        gq_ref[:, sl] = t.astype(BF16)
    gq_ref[:, D_GQ + D_KV:] = gq[:, D_GQ + D_KV:].astype(BF16)

    for j in range(3):
        sl = slice(1024 * j, 1024 * (j + 1))
        brg_ref[:, sl] = _sigmoid(_dot(h, w_ref[:, C_BRG + 1024 * j:C_BRG + 1024 * (j + 1)])).astype(BF16)


def _proj(x, mod, nw, w, gmat, qkw, rope_tabs, *, tiles_per_batch, ctx_row):
    t, d = x.shape
    tm = min(TM, t)
    rope = rope_tabs is not None
    in_specs = [pl.BlockSpec((tm, d), lambda i: (i, 0)),
                pl.BlockSpec((1, N_MOD, d), _mod_index(tiles_per_batch, ctx_row)),
                pl.BlockSpec((3, d), lambda i: (0, 0)),
                _resident((d, N_PROJ)),
                _resident((LANES, LANES)),
                _resident(qkw.shape)]
    args = [x, mod, nw, w, gmat, qkw]
    if rope:
        in_specs += [pl.BlockSpec((tm, LANES), lambda i: (i % tiles_per_batch, 0))] * 2
        args += list(rope_tabs)
    widths = (1024, LANES, 3 * D_NA, D_GQ + 2 * D_KV, 3072)
    dtypes = (BF16, F32, BF16, BF16, BF16)
    return pl.pallas_call(
        functools.partial(_proj_kernel, rope=rope),
        grid=(t // tm,),
        in_specs=in_specs,
        out_specs=[pl.BlockSpec((tm, wd), lambda i: (i, 0)) for wd in widths],
        out_shape=[jax.ShapeDtypeStruct((t, wd), dt) for wd, dt in zip(widths, dtypes)],
        compiler_params=_params(("parallel",)),
        name="mix_in_proj",
    )(*args)


def _ml_chunk(d, q_ref, k_ref, v_ref, g_ref, o_ref, tri_ref, bias_ref, st_ref, m_ref):
    lc = q_ref.shape[1]
    gates = g_ref[0] + bias_ref[...]
    logf = _log_sigmoid(gates)
    tri = tri_ref[d]
    hi, lo = _split_bf16(logf)
    bcum = _dot(tri, hi) + _dot(tri, lo)
    gates_t = gates.T
    bcum_t = bcum.T
    row = lax.broadcasted_iota(jnp.int32, (lc, lc), 0)
    col = lax.broadcasted_iota(jnp.int32, (lc, lc), 1)
    visible = (col <= row) if d == 0 else (col >= row)
    lane = lax.broadcasted_iota(jnp.int32, (1, LANES), 1)
    low = lane < HEAD_DIM
    last = lc - 1 if d == 0 else 0
    ones = jnp.ones((lc, LANES), BF16)

    for p in range(H_ML // 2):
        per_head = []
        for h in (2 * p, 2 * p + 1):
            ci, cf = 8 * d + h, 8 * d + 4 + h
            r = 4 * d + h
            ig_row = gates_t[ci:ci + 1, :]
            b_row = bcum_t[cf:cf + 1, :]
            ig_col = gates[:, ci:ci + 1]
            b_col = bcum[:, cf:cf + 1]
            b_tot = b_row[:, last:last + 1]
            m_prev = m_ref[r:r + 1, 0:1]
            w_end_row = b_tot - b_row + ig_row
            m_new = jnp.maximum(b_tot + m_prev, jnp.max(w_end_row, axis=1, keepdims=True))
            a_col = jnp.exp(b_tot - b_col + ig_col - m_new)
            decay = jnp.exp(b_tot + m_prev - m_new)
            m_inter = b_col + m_prev
            logw = jnp.where(visible, b_col - b_row + ig_row, NEG)
            m_j = jnp.maximum(m_inter, jnp.max(logw, axis=1, keepdims=True))
            w = jnp.exp(logw - m_j)
            g_col = jnp.exp(m_inter - m_j)
            m_ref[r:r + 1, :] = jnp.broadcast_to(m_new, (1, LANES))
            per_head.append((w, a_col, g_col, m_j, decay))
        (w_a, a_a, g_a, mj_a, dec_a), (w_b, a_b, g_b, mj_b, dec_b) = per_head

        sl = slice(LANES * p, LANES * (p + 1))
        q = q_ref[0, :, sl]
        k = k_ref[0, :, sl] * ATTN_SCALE
        v_aug = jnp.concatenate([v_ref[0, :, sl], ones], axis=1)
        zero = jnp.zeros_like(q)
        s_a = _dot_nt(jnp.where(low, q, zero), k)
        s_b = _dot_nt(jnp.where(low, zero, q), k)
        r_a = _dot((s_a * w_a).astype(BF16), v_aug)
        r_b = _dot((s_b * w_b).astype(BF16), v_aug)
        state = st_ref[d, p]
        r_i = _dot(q, state.astype(BF16))
        g_pair = jnp.where(low, g_a, g_b)
        num = jnp.where(low, r_a[:, :LANES], r_b[:, :LANES]) + g_pair * r_i[:, :LANES]
        den = jnp.where(low, r_a[:, LANES:], r_b[:, LANES:]) + g_pair * r_i[:, LANES:]
        floor = jnp.exp(-jnp.where(low, mj_a, mj_b))
        o_ref[0, :, sl] = (num / jnp.maximum(jnp.abs(den), floor)).astype(BF16)

        kw_t = (k.astype(F32) * jnp.where(low, a_a, a_b)).T.astype(BF16)
        upd = _dot(kw_t, v_aug)
        srow = lax.broadcasted_iota(jnp.int32, (LANES, 2 * LANES), 0)
        scol = lax.broadcasted_iota(jnp.int32, (LANES, 2 * LANES), 1)
        same_head = (srow < HEAD_DIM) == ((scol % LANES) < HEAD_DIM)
        dec_rows = jnp.where(srow[:, 0:1] < HEAD_DIM, dec_a, dec_b)
        st_ref[d, p] = dec_rows * state + jnp.where(same_head, upd, 0.0)


def _mlstm_kernel(qf, kf, vf, gf, qb, kb, vb, gb, qc, kc, vc, gc, tri_ref, bias_ref,
                  hf_ref, hb_ref, hcf_ref, hcb_ref, st_ref, m_ref):
    c = pl.program_id(1)

    @pl.when(c == 0)
    def _():
        st_ref[...] = jnp.zeros_like(st_ref)
        m_ref[...] = jnp.zeros_like(m_ref)
        _ml_chunk(0, qc, kc, vc, gc, hcf_ref, tri_ref, bias_ref, st_ref, m_ref)
        _ml_chunk(1, qc, kc, vc, gc, hcb_ref, tri_ref, bias_ref, st_ref, m_ref)

    @pl.when(c > 0)
    def _():
        _ml_chunk(0, qf, kf, vf, gf, hf_ref, tri_ref, bias_ref, st_ref, m_ref)
        _ml_chunk(1, qb, kb, vb, gb, hb_ref, tri_ref, bias_ref, st_ref, m_ref)


def _mlstm(p_ml, p_mlg, pc_ml, pc_mlg, tri, bias):
    b, s, _ = p_ml.shape
    l = pc_ml.shape[1]
    lc = ML_CHUNK
    assert l == lc and s % lc == 0
    nl = s // lc
    fwd = lambda c: jnp.maximum(c - 1, 0)
    bwd = lambda c: nl - 1 - jnp.maximum(c - 1, 0)

    def lat(idx, blk, width):
        return pl.BlockSpec((1, lc, width), lambda i, c: (i, idx(c), blk))

    def ctx(blk, width):
        return pl.BlockSpec((1, lc, width), lambda i, c: (i, 0, blk))

    in_specs = ([lat(fwd, 0, D_ML), lat(fwd, 1, D_ML), lat(fwd, 2, D_ML), lat(fwd, 0, LANES)]
                + [lat(bwd, 0, D_ML), lat(bwd, 1, D_ML), lat(bwd, 2, D_ML), lat(bwd, 0, LANES)]
                + [ctx(0, D_ML), ctx(1, D_ML), ctx(2, D_ML), ctx(0, LANES)]
                + [_resident((2, lc, lc)), _resident((1, LANES))])
    out_specs = [lat(fwd, 0, D_ML), lat(bwd, 0, D_ML), ctx(0, D_ML), ctx(0, D_ML)]
    out_shape = [jax.ShapeDtypeStruct((b, s, D_ML), BF16)] * 2 + [jax.ShapeDtypeStruct((b, l, D_ML), BF16)] * 2
    return pl.pallas_call(
        _mlstm_kernel,
        grid=(b, nl + 1),
        in_specs=in_specs,
        out_specs=out_specs,
        out_shape=out_shape,
        scratch_shapes=[pltpu.VMEM((2, H_ML // 2, LANES, 2 * LANES), F32), pltpu.VMEM((8, LANES), F32)],
        compiler_params=_params(("parallel", "arbitrary")),
        name="mlstm_bidir",
    )(p_ml, p_ml, p_ml, p_mlg, p_ml, p_ml, p_ml, p_mlg, pc_ml, pc_ml, pc_ml, pc_mlg, tri, bias)


def _softmax_pv(scores, values):
    m = functools.reduce(jnp.maximum, [jnp.max(s, axis=1, keepdims=True) for s in scores])
    ps = [jnp.exp(s - m) for s in scores]
    denom = functools.reduce(jnp.add, [jnp.sum(p, axis=1, keepdims=True) for p in ps])
    out = functools.reduce(jnp.add, [_dot(p.astype(BF16), v) for p, v in zip(ps, values)])
    return out / denom


def _half_masks(q):
    lane = lax.broadcasted_iota(jnp.int32, (1, LANES), 1)
    low = lane < HEAD_DIM
    zero = jnp.zeros_like(q)
    return low, jnp.where(low, q, zero), jnp.where(low, zero, q)


def _gqa_kernel(q_ref, k_ref, v_ref, kc_ref, vc_ref, o_ref):
    k, v, kc, vc = k_ref[0], v_ref[0], kc_ref[0], vc_ref[0]
    for j in range(D_GQ // LANES):
        sl = slice(LANES * j, LANES * (j + 1))
        low, q_lo, q_hi = _half_masks(q_ref[0, :, sl] * ATTN_SCALE)
        outs = [_softmax_pv([_dot_nt(qm, k), _dot_nt(qm, kc)], [v, vc]) for qm in (q_lo, q_hi)]
        o_ref[0, :, sl] = jnp.where(low, outs[0], outs[1]).astype(BF16)


def _gqa(p_gq, pc_gq):
    b, s, _ = p_gq.shape
    l = pc_gq.shape[1]
    tq = min(GQ_TQ, s)
    kblk, vblk = D_GQ // LANES, D_GQ // LANES + 1
    return pl.pallas_call(
        _gqa_kernel,
        grid=(b, s // tq),
        in_specs=[pl.BlockSpec((1, tq, D_GQ), lambda i, t: (i, t, 0)),
                  pl.BlockSpec((1, s, LANES), lambda i, t: (i, 0, kblk)),
                  pl.BlockSpec((1, s, LANES), lambda i, t: (i, 0, vblk)),
                  pl.BlockSpec((1, l, LANES), lambda i, t: (i, 0, kblk)),
                  pl.BlockSpec((1, l, LANES), lambda i, t: (i, 0, vblk))],
        out_specs=pl.BlockSpec((1, tq, D_GQ), lambda i, t: (i, t, 0)),
        out_shape=jax.ShapeDtypeStruct((b, s, D_GQ), BF16),
        compiler_params=_params(("parallel", "arbitrary")),
        name="gqa_latent",
    )(p_gq, p_gq, p_gq, pc_gq, pc_gq)


def _ctx_attn_kernel(q_ref, k_ref, v_ref, o_ref, *, shared_kv):
    for j in range(q_ref.shape[2] // LANES):
        sl = slice(LANES * j, LANES * (j + 1))
        ksl = slice(0, LANES) if shared_kv else sl
        k, v = k_ref[0, :, ksl], v_ref[0, :, ksl]
        low, q_lo, q_hi = _half_masks(q_ref[0, :, sl] * ATTN_SCALE)
        outs = [_softmax_pv([_dot_nt(qm, k)], [v]) for qm in (q_lo, q_hi)]
        o_ref[0, :, sl] = jnp.where(low, outs[0], outs[1]).astype(BF16)


def _ctx_attn(pc, *, qw, kw, shared_kv):
    b, l, _ = pc.shape
    kb = qw // kw
    return pl.pallas_call(
        functools.partial(_ctx_attn_kernel, shared_kv=shared_kv),
        grid=(b,),
        in_specs=[pl.BlockSpec((1, l, qw), lambda i: (i, 0, 0)),
                  pl.BlockSpec((1, l, kw), lambda i: (i, 0, kb)),
                  pl.BlockSpec((1, l, kw), lambda i: (i, 0, kb + 1))],
        out_specs=pl.BlockSpec((1, l, qw), lambda i: (i, 0, 0)),
        out_shape=jax.ShapeDtypeStruct((b, l, qw), BF16),
        compiler_params=_params(("parallel",)),
        name="ctx_attn",
    )(pc, pc, pc)


def _na_kernel(q_ref, k0, k1, k2, k3, v0, v1, v2, v3, kc_ref, vc_ref, tz_ref, o_ref, *, n_rows):
    i = pl.program_id(1)
    r0 = i * NA_QROWS
    start = jnp.clip(r0 - NA_WIN_R // 2, 0, n_rows - NA_BAND)
    delta = start - r0
    nq, nk = NA_QROWS * GRID_W, NA_BAND * GRID_W

    qrow = r0 + lax.broadcasted_iota(jnp.int32, (nq, nk), 0) // GRID_W
    krow = start + lax.broadcasted_iota(jnp.int32, (nq, nk), 1) // GRID_W
    first = jnp.clip(qrow - NA_WIN_R // 2, 0, n_rows - NA_WIN_R)
    row_mask = jnp.where((krow >= first) & (krow < first + NA_WIN_R), 0.0, NEG)

    kband = jnp.concatenate([r[0] for r in (k0, k1, k2, k3)], axis=0)
    vband = jnp.concatenate([r[0] for r in (v0, v1, v2, v3)], axis=0)
    kc, vc = kc_ref[0], vc_ref[0]
    for j in range(D_NA // LANES):
        sl = slice(LANES * j, LANES * (j + 1))
        low, q_lo, q_hi = _half_masks(q_ref[0, :, sl] * ATTN_SCALE)
        outs = []
        for half, qm in enumerate((q_lo, q_hi)):
            h = 2 * j + half
            slabs = []
            for a in range(NA_QROWS):
                pieces = [tz_ref[h, jnp.clip(delta + 2 * bp - a, -8, 7) + 8] for bp in range(NA_BAND // 2)]
                slabs.append(jnp.concatenate(pieces, axis=1))
            bias = jnp.concatenate(slabs, axis=0) + row_mask
            s_win = _dot_nt(qm, kband[:, sl]) + bias
            s_ctx = _dot_nt(qm, kc[:, sl])
            outs.append(_softmax_pv([s_win, s_ctx], [vband[:, sl], vc[:, sl]]))
        o_ref[0, :, sl] = jnp.where(low, outs[0], outs[1]).astype(BF16)


def _na(p_na, pc_na, tz):
    b, s, _ = p_na.shape
    l = pc_na.shape[1]
    n_rows = s // GRID_W
    assert n_rows % NA_QROWS == 0 and n_rows >= NA_BAND
    nq = NA_QROWS * GRID_W
    sub = (NA_BAND // 4) * GRID_W
    rows_per_sub = NA_BAND // 4

    def band(t, blk):
        def idx(i, r):
            start = jnp.clip(r * NA_QROWS - NA_WIN_R // 2, 0, n_rows - NA_BAND)
            return (i, start // rows_per_sub + t, blk)
        return pl.BlockSpec((1, sub, D_NA), idx)

    in_specs = ([pl.BlockSpec((1, nq, D_NA), lambda i, r: (i, r, 0))]
                + [band(t, 1) for t in range(4)] + [band(t, 2) for t in range(4)]
                + [pl.BlockSpec((1, l, D_NA), lambda i, r: (i, 0, 1)),
                   pl.BlockSpec((1, l, D_NA), lambda i, r: (i, 0, 2)),
                   _resident(tz.shape)])
    return pl.pallas_call(
        functools.partial(_na_kernel, n_rows=n_rows),
        grid=(b, n_rows // NA_QROWS),
        in_specs=in_specs,
        out_specs=pl.BlockSpec((1, nq, D_NA), lambda i, r: (i, r, 0)),
        out_shape=jax.ShapeDtypeStruct((b, s, D_NA), BF16),
        compiler_params=_params(("parallel", "arbitrary")),
        name="na_latent",
    )(p_na, *([p_na] * 8), pc_na, pc_na, tz)


def _merge_kernel(x_ref, mod_ref, brg_ref, hf_ref, hb_ref, og_ref, mlw_ref, g_ref, na_ref, gq_ref,
                  wml_ref, wna_ref, wgq_ref, wo_ref, o_ref):
    gmat = g_ref[...]
    h = hf_ref[...].astype(F32) + hb_ref[...].astype(F32)
    og = og_ref[...].astype(F32)
    y = None
    for p in range(D_ML // LANES):
        sl = slice(LANES * p, LANES * (p + 1))
        o_ml = (_head_norm(h[:, sl], gmat, mlw_ref[:, sl]) * _sigmoid(og[:, sl])).astype(BF16)
        part = _dot(o_ml, wml_ref[sl, :])
        y = part if y is None else y + part
    d = x_ref.shape[1]
    y = brg_ref[:, 0:d].astype(F32) * y
    y = y + brg_ref[:, d:2 * d].astype(F32) * _dot(na_ref[...], wna_ref[...])
    y = y + brg_ref[:, 2 * d:3 * d].astype(F32) * _dot(gq_ref[...], wgq_ref[...])
    o_ref[...] = x_ref[...] + mod_ref[0, 5:6, :] * _dot(y.astype(BF16), wo_ref[...])


def _merge(x, mod, brg, hf, hb, p_ml, mlw, gmat, o_na, o_gq, wml, wna, wgq, wo, *, tiles_per_batch, ctx_row):
    t, d = x.shape
    tm = min(TM, t)
    row = lambda wd, blk=0: pl.BlockSpec((tm, wd), lambda i: (i, blk))
    return pl.pallas_call(
        _merge_kernel,
        grid=(t // tm,),
        in_specs=[row(d),
                  pl.BlockSpec((1, N_MOD, d), _mod_index(tiles_per_batch, ctx_row)),
                  row(3 * d), row(D_ML), row(D_ML), row(D_ML, 3),
                  _resident((1, D_ML)), _resident((LANES, LANES)),
                  row(D_NA), row(D_GQ),
                  _resident((D_ML, d)), _resident((D_NA, d)), _resident((D_GQ, d)), _resident((d, d))],
        out_specs=row(d),
        out_shape=jax.ShapeDtypeStruct((t, d), F32),
        compiler_params=_params(("parallel",)),
        name="branch_merge",
    )(x, mod, brg, hf, hb, p_ml, mlw, gmat, o_na, o_gq, wml, wna, wgq, wo)


def _proj_weight(w):
    d = w.shape[0]
    o = 0
    seg = {}
    for name, width in (("ml_k", D_ML), ("ml_v", D_ML), ("ml_g", 4 * H_ML), ("na_k", D_NA), ("na_v", D_NA),
                        ("gq_k", D_KV), ("gq_v", D_KV), ("ml_q", D_ML), ("ml_o", D_ML), ("na_q", D_NA),
                        ("gq_q", D_GQ), ("br_g", 3 * d)):
        seg[name] = w[:, o:o + width]
        o += width
    gq_q = jnp.concatenate([seg["gq_q"][:, HEAD_DIM * h:HEAD_DIM * (h + 1)] for h in GQ_HEAD_ORDER], axis=1)
    pad = jnp.zeros((d, LANES - 4 * H_ML), w.dtype)
    out = jnp.concatenate([seg["ml_q"], seg["ml_k"], seg["ml_v"], seg["ml_o"], seg["ml_g"], pad,
                           seg["na_q"], seg["na_k"], seg["na_v"], gq_q, seg["gq_k"], seg["gq_v"], seg["br_g"]],
                          axis=1)
    return out.astype(BF16)


def _rope_tables(n_tok):
    t = jnp.arange(n_tok, dtype=jnp.int32)
    row = (t // GRID_W).astype(F32)
    col = (t % GRID_W).astype(F32)
    n_freq = HEAD_DIM // 4
    inv = ROPE_THETA ** (-jnp.arange(n_freq, dtype=F32) / n_freq)
    ang = jnp.concatenate([row[:, None] * inv, col[:, None] * inv], axis=-1)
    cos, sin = jnp.cos(ang), jnp.sin(ang)
    cos_t = jnp.tile(cos, (1, LANES // (HEAD_DIM // 2)))
    sin_t = jnp.tile(jnp.concatenate([-sin, sin], axis=-1), (1, LANES // HEAD_DIM))
    return cos_t, sin_t


def _na_bias_table(rpb):
    col = np.arange(GRID_W)
    first = np.clip(col - NA_WIN_C // 2, 0, GRID_W - NA_WIN_C)
    in_win = (col[None, :] >= first[:, None]) & (col[None, :] < first[:, None] + NA_WIN_C)
    off = np.clip(col[None, :] - col[:, None] + NA_WIN_C - 1, 0, 2 * NA_WIN_C - 2)
    rows = jnp.pad(rpb, ((0, 0), (1, 1), (0, 0)))
    row_ok = np.zeros((2 * NA_WIN_R + 1,), bool)
    row_ok[1:-1] = True
    full = rows[:, :, off]
    full = jnp.where(jnp.asarray(in_win[None, None] & row_ok[None, :, None, None]), full, NEG)
    return jnp.concatenate([full[:, :-1], full[:, 1:]], axis=-1).astype(F32)


def kernel(x, c, ctx, c_ctx, ada_w, ada_b, norm_w, ffn_w_in, ffn_w_out, mix_w_in, ml_gate_b, ml_norm_w,
           na_qk_w, na_rpb, gq_qk_w, w_br_ml, w_br_na, w_br_gq, w_out):
    b, s, d = x.shape
    l = ctx.shape[1]
    depth = ada_w.shape[0]
    assert b < MOD_ROWS and s % TM == 0 and (b * l) % min(TM, b * l) == 0
    ctx_row = b
    tiles_per_batch = s // TM

    cvec = jnp.zeros((MOD_ROWS, d), F32).at[:b].set(c).at[b].set(c_ctx)
    mod = _ada(cvec, ada_w, ada_b).reshape(depth, MOD_ROWS, N_MOD, d)

    lane = np.arange(LANES)
    gmat = jnp.asarray((lane[:, None] // HEAD_DIM) == (lane[None, :] // HEAD_DIM), BF16)
    idx = np.arange(ML_CHUNK)
    tri = jnp.asarray(np.stack([idx[:, None] >= idx[None, :], idx[:, None] <= idx[None, :]]), BF16)
    rope_tabs = _rope_tables(s)

    xl = x.reshape(b * s, d)
    xc = ctx.reshape(b * l, d)
    lat = dict(tiles_per_batch=tiles_per_batch, ctx_row=ctx_row)
    con = dict(tiles_per_batch=None, ctx_row=ctx_row)
    for li in range(depth):
        ctx_out = li < depth - 1
        w_in = [ffn_w_in[li, t].astype(BF16) for t in range(2)]
        w_o = [ffn_w_out[li, t].astype(BF16) for t in range(2)]
        w_proj = _proj_weight(mix_w_in[li])
        qkw = jnp.zeros((8, 4 * LANES), F32)
        qkw = qkw.at[0, :D_NA].set(jnp.tile(na_qk_w[li, 0], H_NA)).at[1, :D_NA].set(jnp.tile(na_qk_w[li, 1], H_NA))
        qkw = qkw.at[2, :D_GQ].set(jnp.tile(gq_qk_w[li, 0], H_GQ)).at[2, D_GQ:].set(jnp.tile(gq_qk_w[li, 1], H_KV))
        gate_b = jnp.zeros((1, LANES), F32).at[0, :4 * H_ML].set(ml_gate_b[li])
        mlw = ml_norm_w[li].reshape(1, D_ML)
        tz = _na_bias_table(na_rpb[li])
        wml, wna, wo = w_br_ml[li].astype(BF16), w_br_na[li].astype(BF16), w_out[li].astype(BF16)
        wgq = jnp.concatenate([w_br_gq[li, HEAD_DIM * h:HEAD_DIM * (h + 1)] for h in GQ_HEAD_ORDER],
                              axis=0).astype(BF16)
        m, nw = mod[li], norm_w[li]

        xl = _ffn(xl, m, nw, w_in[0], w_o[0], k0=0, nrm=0, **lat)
        xc = _ffn(xc, m, nw, w_in[0], w_o[0], k0=0, nrm=0, **con)

        p_ml, p_mlg, p_na, p_gq, p_brg = _proj(xl, m, nw, w_proj, gmat, qkw, rope_tabs, **lat)
        pc_ml, pc_mlg, pc_na, pc_gq, pc_brg = _proj(xc, m, nw, w_proj, gmat, qkw, None, **con)
        seq = lambda a: a.reshape(b, s, a.shape[-1])
        cseq = lambda a: a.reshape(b, l, a.shape[-1])

        hf, hb, hcf, hcb = _mlstm(seq(p_ml), seq(p_mlg), cseq(pc_ml), cseq(pc_mlg), tri, gate_b)
        o_na = _na(seq(p_na), cseq(pc_na), tz)
        o_gq = _gqa(seq(p_gq), cseq(pc_gq))
        flat = lambda a: a.reshape(-1, a.shape[-1])
        xl = _merge(xl, m, p_brg, flat(hf), flat(hb), p_ml, mlw, gmat, flat(o_na), flat(o_gq),
                    wml, wna, wgq, wo, **lat)
        xl = _ffn(xl, m, nw, w_in[1], w_o[1], k0=6, nrm=2, **lat)
        if ctx_out:
            co_na = _ctx_attn(cseq(pc_na), qw=D_NA, kw=D_NA, shared_kv=False)
            co_gq = _ctx_attn(cseq(pc_gq), qw=D_GQ, kw=D_KV, shared_kv=True)
            xc = _merge(xc, m, pc_brg, flat(hcf), flat(hcb), pc_ml, mlw, gmat, flat(co_na), flat(co_gq),
                        wml, wna, wgq, wo, **con)
            xc = _ffn(xc, m, nw, w_in[1], w_o[1], k0=6, nrm=2, **con)
    return xl.reshape(b, s, d)
```

```python
import functools

import numpy as np
import jax
import jax.numpy as jnp
from jax import lax
from jax.experimental import pallas as pl
from jax.experimental.pallas import tpu as pltpu

F32 = jnp.float32
BF16 = jnp.bfloat16

HEAD_DIM = 64
LANES = 128
H_ML, H_NA, H_GQ, H_KV = 4, 6, 6, 2
D_ML, D_NA, D_GQ, D_KV = 256, 384, 384, 128
GRID_W = 64
NA_WIN_R, NA_WIN_C = 8, 16
ROPE_THETA = 10000.0
EPS = 1e-6
N_MOD = 9
ATTN_SCALE = HEAD_DIM ** -0.5
NEG = -1e30

ML_CHUNK = 256
NA_QROWS = 8
NA_BAND = 16
TM = 512
GQ_TQ = 256
FFN_CHUNK = 768
Q_PRESCALE = ATTN_SCALE * float(np.log2(np.e))
MOD_ROWS = 16
VMEM_LIMIT = 56 * 1024 * 1024

C_ML, C_MLG, C_NA, C_GQ, C_BRG = 0, 1024, 1152, 2304, 2944
N_PROJ = 6016
GQ_HEAD_ORDER = (0, 3, 1, 4, 2, 5)


def _dot(a, b):
    return jnp.dot(a, b, preferred_element_type=F32)


def _dot_nt(a, b):
    return lax.dot_general(a, b, (((1,), (1,)), ((), ())), preferred_element_type=F32)


def _sigmoid(x):
    return 1.0 / (1.0 + jnp.exp(-x))


def _log_sigmoid(x):
    return jnp.minimum(x, 0.0) - jnp.log1p(jnp.exp(-jnp.abs(x)))


def _split_bf16(x):
    hi = x.astype(BF16)
    lo = (x - hi.astype(F32)).astype(BF16)
    return hi, lo


def _modnorm(x, nw, shift, scale):
    ms = jnp.mean(x * x, axis=-1, keepdims=True)
    return (x * lax.rsqrt(ms + EPS) * nw) * (1.0 + scale) + shift


def _head_norm(t, gmat, wrow):
    hi, lo = _split_bf16(t * t)
    ss = _dot(hi, gmat) + _dot(lo, gmat)
    return t * lax.rsqrt(ss * (1.0 / HEAD_DIM) + EPS) * wrow


def _resident(shape):
    return pl.BlockSpec(shape, lambda *_: (0,) * len(shape), pipeline_mode=pl.Buffered(1))


def _params(sem):
    return pltpu.CompilerParams(dimension_semantics=sem, vmem_limit_bytes=VMEM_LIMIT)


def _ada_kernel(c_ref, w_ref, b_ref, o_ref):
    c = c_ref[...]
    s = (c * _sigmoid(c)).astype(BF16)
    o_ref[0] = _dot(s, w_ref[0].astype(BF16)) + b_ref[0]


def _ada(cvec, ada_w, ada_b):
    depth, d, n = ada_w.shape
    tn = n // 8
    return pl.pallas_call(
        _ada_kernel,
        grid=(depth, n // tn),
        in_specs=[pl.BlockSpec((MOD_ROWS, d), lambda l, j: (0, 0)),
                  pl.BlockSpec((1, d, tn), lambda l, j: (l, 0, j)),
                  pl.BlockSpec((1, 1, tn), lambda l, j: (l, 0, j))],
        out_specs=pl.BlockSpec((1, MOD_ROWS, tn), lambda l, j: (l, 0, j)),
        out_shape=jax.ShapeDtypeStruct((depth, MOD_ROWS, n), F32),
        compiler_params=_params(("arbitrary", "arbitrary")),
        name="ada_mod",
    )(cvec, ada_w, ada_b.reshape(depth, 1, n))


def _mod_index(tiles_per_batch, ctx_row):
    if tiles_per_batch is None:
        return lambda i, *_: (ctx_row, 0, 0)
    return lambda i, *_: (i // tiles_per_batch, 0, 0)


def _ffn_kernel(x_ref, mod_ref, nw_ref, wi_ref, wo_ref, o_ref, *, k0, nrm, chunks):
    x = x_ref[...]
    h = _modnorm(x, nw_ref[nrm:nrm + 1, :], mod_ref[0, k0:k0 + 1, :], mod_ref[0, k0 + 1:k0 + 2, :]).astype(BF16)
    dff = wo_ref.shape[0]
    y = None
    for c0, c1 in chunks:
        g = _dot(h, wi_ref[:, c0:c1])
        u = _dot(h, wi_ref[:, dff + c0:dff + c1])
        part = _dot((g * _sigmoid(g) * u).astype(BF16), wo_ref[c0:c1, :])
        y = part if y is None else y + part
    o_ref[...] = x + (0.5 * mod_ref[0, k0 + 2:k0 + 3, :]) * y


def _ffn(x, mod, nw, w_in, w_out, *, k0, nrm, tiles_per_batch, ctx_row):
    t, d = x.shape
    dff = w_out.shape[0]
    tm = min(TM, t)
    edges = list(range(0, dff, FFN_CHUNK)) + [dff]
    chunks = tuple(zip(edges[:-1], edges[1:]))
    kern = functools.partial(_ffn_kernel, k0=k0, nrm=nrm, chunks=chunks)
    return pl.pallas_call(
        kern,
        grid=(t // tm,),
        in_specs=[pl.BlockSpec((tm, d), lambda i: (i, 0)),
                  pl.BlockSpec((1, N_MOD, d), _mod_index(tiles_per_batch, ctx_row)),
                  pl.BlockSpec((3, d), lambda i: (0, 0)),
                  _resident((d, 2 * dff)),
                  _resident((dff, d))],
        out_specs=pl.BlockSpec((tm, d), lambda i: (i, 0)),
        out_shape=jax.ShapeDtypeStruct((t, d), F32),
        compiler_params=_params(("parallel",)),
        name="ffn_swiglu",
    )(x, mod, nw, w_in, w_out)


def _proj_kernel(*refs, rope):
    if rope:
        (x_ref, mod_ref, nw_ref, w_ref, g_ref, qkw_ref, cos_ref, sin_ref,
         ml_ref, mlg_ref, na_ref, gq_ref, brg_ref) = refs
    else:
        (x_ref, mod_ref, nw_ref, w_ref, g_ref, qkw_ref,
         ml_ref, mlg_ref, na_ref, gq_ref, brg_ref) = refs
    h = _modnorm(x_ref[...], nw_ref[1:2, :], mod_ref[0, 3:4, :], mod_ref[0, 4:5, :]).astype(BF16)
    gmat = g_ref[...]

    ml_ref[...] = _dot(h, w_ref[:, C_ML:C_MLG]).astype(BF16)
    mlg_ref[...] = _dot(h, w_ref[:, C_MLG:C_NA])

    na = _dot(h, w_ref[:, C_NA:C_GQ])
    for j in range(6):
        sl = slice(LANES * j, LANES * (j + 1))
        wsl = slice(LANES * (j % 3), LANES * (j % 3 + 1))
        t = _head_norm(na[:, sl], gmat, qkw_ref[j // 3:j // 3 + 1, wsl])
        na_ref[:, sl] = (t * Q_PRESCALE if j < 3 else t).astype(BF16)
    na_ref[:, 2 * D_NA:] = na[:, 2 * D_NA:].astype(BF16)

    gq = _dot(h, w_ref[:, C_GQ:C_BRG])
    if rope:
        lane = lax.broadcasted_iota(jnp.int32, (1, LANES), 1)
        first_half = (lane % HEAD_DIM) < (HEAD_DIM // 2)
        cos = cos_ref[...]
        sin = sin_ref[...]
    for j in range(4):
        sl = slice(LANES * j, LANES * (j + 1))
        t = _head_norm(gq[:, sl], gmat, qkw_ref[2:3, sl])
        if rope:
            rot = jnp.where(first_half, pltpu.roll(t, LANES - HEAD_DIM // 2, axis=1),
                            pltpu.roll(t, HEAD_DIM // 2, axis=1))
            t = t * cos + rot * sin
        gq_ref[:, sl] = (t * Q_PRESCALE if j < 3 else t).astype(BF16)
    gq_ref[:, D_GQ + D_KV:] = gq[:, D_GQ + D_KV:].astype(BF16)

    for j in range(3):
        sl = slice(1024 * j, 1024 * (j + 1))
        brg_ref[:, sl] = _sigmoid(_dot(h, w_ref[:, C_BRG + 1024 * j:C_BRG + 1024 * (j + 1)])).astype(BF16)


def _proj(x, mod, nw, w, gmat, qkw, rope_tabs, *, tiles_per_batch, ctx_row):
    t, d = x.shape
    tm = min(TM, t)
    rope = rope_tabs is not None
    in_specs = [pl.BlockSpec((tm, d), lambda i: (i, 0)),
                pl.BlockSpec((1, N_MOD, d), _mod_index(tiles_per_batch, ctx_row)),
                pl.BlockSpec((3, d), lambda i: (0, 0)),
                _resident((d, N_PROJ)),
                _resident((LANES, LANES)),
                _resident(qkw.shape)]
    args = [x, mod, nw, w, gmat, qkw]
    if rope:
        in_specs += [pl.BlockSpec((tm, LANES), lambda i: (i % tiles_per_batch, 0))] * 2
        args += list(rope_tabs)
    widths = (1024, LANES, 3 * D_NA, D_GQ + 2 * D_KV, 3072)
    dtypes = (BF16, F32, BF16, BF16, BF16)
    return pl.pallas_call(
        functools.partial(_proj_kernel, rope=rope),
        grid=(t // tm,),
        in_specs=in_specs,
        out_specs=[pl.BlockSpec((tm, wd), lambda i: (i, 0)) for wd in widths],
        out_shape=[jax.ShapeDtypeStruct((t, wd), dt) for wd, dt in zip(widths, dtypes)],
        compiler_params=_params(("parallel",)),
        name="mix_in_proj",
    )(*args)


def _ml_chunk(d, q_ref, k_ref, v_ref, g_ref, o_ref, tri_ref, bias_ref, st_ref, m_ref):
    lc = q_ref.shape[1]
    gates = g_ref[0] + bias_ref[...]
    logf = _log_sigmoid(gates)
    tri = tri_ref[d]
    hi, lo = _split_bf16(logf)
    bcum = _dot(tri, hi) + _dot(tri, lo)
    gates_t = gates.T
    bcum_t = bcum.T
    row = lax.broadcasted_iota(jnp.int32, (lc, lc), 0)
    col = lax.broadcasted_iota(jnp.int32, (lc, lc), 1)
    visible = (col <= row) if d == 0 else (col >= row)
    lane = lax.broadcasted_iota(jnp.int32, (1, LANES), 1)
    low = lane < HEAD_DIM
    last = lc - 1 if d == 0 else 0
    ones = jnp.ones((lc, LANES), BF16)

    for p in range(H_ML // 2):
        per_head = []
        for h in (2 * p, 2 * p + 1):
            ci, cf = 8 * d + h, 8 * d + 4 + h
            r = 4 * d + h
            ig_row = gates_t[ci:ci + 1, :]
            b_row = bcum_t[cf:cf + 1, :]
            ig_col = gates[:, ci:ci + 1]
            b_col = bcum[:, cf:cf + 1]
            b_tot = b_row[:, last:last + 1]
            m_prev = m_ref[r:r + 1, 0:1]
            w_end_row = b_tot - b_row + ig_row
            m_new = jnp.maximum(b_tot + m_prev, jnp.max(w_end_row, axis=1, keepdims=True))
            a_col = jnp.exp(b_tot - b_col + ig_col - m_new)
            decay = jnp.exp(b_tot + m_prev - m_new)
            m_inter = b_col + m_prev
            logw = jnp.where(visible, b_col - b_row + ig_row, NEG)
            m_j = jnp.maximum(m_inter, jnp.max(logw, axis=1, keepdims=True))
            w = jnp.exp(logw - m_j)
            g_col = jnp.exp(m_inter - m_j)
            m_ref[r:r + 1, :] = jnp.broadcast_to(m_new, (1, LANES))
            per_head.append((w, a_col, g_col, m_j, decay))
        (w_a, a_a, g_a, mj_a, dec_a), (w_b, a_b, g_b, mj_b, dec_b) = per_head

        sl = slice(LANES * p, LANES * (p + 1))
        q = q_ref[0, :, sl]
        k = k_ref[0, :, sl] * ATTN_SCALE
        v_aug = jnp.concatenate([v_ref[0, :, sl], ones], axis=1)
        zero = jnp.zeros_like(q)
        s_a = _dot_nt(jnp.where(low, q, zero), k)
        s_b = _dot_nt(jnp.where(low, zero, q), k)
        r_a = _dot((s_a * w_a).astype(BF16), v_aug)
        r_b = _dot((s_b * w_b).astype(BF16), v_aug)
        state = st_ref[d, p]
        r_i = _dot(q, state.astype(BF16))
        g_pair = jnp.where(low, g_a, g_b)
        num = jnp.where(low, r_a[:, :LANES], r_b[:, :LANES]) + g_pair * r_i[:, :LANES]
        den = jnp.where(low, r_a[:, LANES:], r_b[:, LANES:]) + g_pair * r_i[:, LANES:]
        floor = jnp.exp(-jnp.where(low, mj_a, mj_b))
        o_ref[0, :, sl] = (num / jnp.maximum(jnp.abs(den), floor)).astype(BF16)

        kw_t = (k.astype(F32) * jnp.where(low, a_a, a_b)).T.astype(BF16)
        upd = _dot(kw_t, v_aug)
        srow = lax.broadcasted_iota(jnp.int32, (LANES, 2 * LANES), 0)
        scol = lax.broadcasted_iota(jnp.int32, (LANES, 2 * LANES), 1)
        same_head = (srow < HEAD_DIM) == ((scol % LANES) < HEAD_DIM)
        dec_rows = jnp.where(srow[:, 0:1] < HEAD_DIM, dec_a, dec_b)
        st_ref[d, p] = dec_rows * state + jnp.where(same_head, upd, 0.0)


def _mlstm_kernel(qf, kf, vf, gf, qb, kb, vb, gb, qc, kc, vc, gc, tri_ref, bias_ref,
                  hf_ref, hb_ref, hcf_ref, hcb_ref, st_ref, m_ref):
    c = pl.program_id(1)

    @pl.when(c == 0)
    def _():
        st_ref[...] = jnp.zeros_like(st_ref)
        m_ref[...] = jnp.zeros_like(m_ref)
        _ml_chunk(0, qc, kc, vc, gc, hcf_ref, tri_ref, bias_ref, st_ref, m_ref)
        _ml_chunk(1, qc, kc, vc, gc, hcb_ref, tri_ref, bias_ref, st_ref, m_ref)

    @pl.when(c > 0)
    def _():
        _ml_chunk(0, qf, kf, vf, gf, hf_ref, tri_ref, bias_ref, st_ref, m_ref)
        _ml_chunk(1, qb, kb, vb, gb, hb_ref, tri_ref, bias_ref, st_ref, m_ref)


def _mlstm(p_ml, p_mlg, pc_ml, pc_mlg, tri, bias):
    b, s, _ = p_ml.shape
    l = pc_ml.shape[1]
    lc = ML_CHUNK
    assert l == lc and s % lc == 0
    nl = s // lc
    fwd = lambda c: jnp.maximum(c - 1, 0)
    bwd = lambda c: nl - 1 - jnp.maximum(c - 1, 0)

    def lat(idx, blk, width):
        return pl.BlockSpec((1, lc, width), lambda i, c: (i, idx(c), blk))

    def ctx(blk, width):
        return pl.BlockSpec((1, lc, width), lambda i, c: (i, 0, blk))

    in_specs = ([lat(fwd, 0, D_ML), lat(fwd, 1, D_ML), lat(fwd, 2, D_ML), lat(fwd, 0, LANES)]
                + [lat(bwd, 0, D_ML), lat(bwd, 1, D_ML), lat(bwd, 2, D_ML), lat(bwd, 0, LANES)]
                + [ctx(0, D_ML), ctx(1, D_ML), ctx(2, D_ML), ctx(0, LANES)]
                + [_resident((2, lc, lc)), _resident((1, LANES))])
    out_specs = [lat(fwd, 0, D_ML), lat(bwd, 0, D_ML), ctx(0, D_ML), ctx(0, D_ML)]
    out_shape = [jax.ShapeDtypeStruct((b, s, D_ML), BF16)] * 2 + [jax.ShapeDtypeStruct((b, l, D_ML), BF16)] * 2
    return pl.pallas_call(
        _mlstm_kernel,
        grid=(b, nl + 1),
        in_specs=in_specs,
        out_specs=out_specs,
        out_shape=out_shape,
        scratch_shapes=[pltpu.VMEM((2, H_ML // 2, LANES, 2 * LANES), F32), pltpu.VMEM((8, LANES), F32)],
        compiler_params=_params(("parallel", "arbitrary")),
        name="mlstm_bidir",
    )(p_ml, p_ml, p_ml, p_mlg, p_ml, p_ml, p_ml, p_mlg, pc_ml, pc_ml, pc_ml, pc_mlg, tri, bias)


def _attend_heads(n_heads, score_fn, value_fn, s_scr, p_scr):
    s_scr[0] = score_fn(0)
    outs = []
    for i in range(n_heads):
        if i + 1 < n_heads:
            s_scr[(i + 1) % 2] = score_fn(i + 1)
        s = s_scr[i % 2]
        p_scr[i % 2] = jnp.exp2(s - jnp.max(s, axis=1, keepdims=True)).astype(BF16)
        r = _dot(p_scr[i % 2], value_fn(i))
        outs.append(r[:, :LANES] / r[:, LANES:])
    return outs


def _low_lanes():
    return lax.broadcasted_iota(jnp.int32, (1, LANES), 1) < HEAD_DIM


def _one_head(q, low, half):
    zero = jnp.zeros_like(q)
    return jnp.where(low, q, zero) if half == 0 else jnp.where(low, zero, q)


def _pair_outputs(o_ref, outs, low):
    for j in range(len(outs) // 2):
        o_ref[0, :, LANES * j:LANES * (j + 1)] = jnp.where(low, outs[2 * j], outs[2 * j + 1]).astype(BF16)


def _gqa_kernel(q_ref, k_ref, v_ref, kc_ref, vc_ref, o_ref, kall, vall, s_scr, p_scr):
    n_lat = k_ref.shape[1]

    @pl.when(pl.program_id(1) == 0)
    def _():
        kall[0:n_lat, :] = k_ref[0]
        kall[n_lat:, :] = kc_ref[0]
        vall[0:n_lat, 0:LANES] = v_ref[0]
        vall[n_lat:, 0:LANES] = vc_ref[0]
        vall[:, LANES:] = jnp.ones((vall.shape[0], LANES), BF16)

    low = _low_lanes()

    def score(i):
        j, half = divmod(i, 2)
        return _dot_nt(_one_head(q_ref[0, :, LANES * j:LANES * (j + 1)], low, half), kall[...])

    _pair_outputs(o_ref, _attend_heads(H_GQ, score, lambda i: vall[...], s_scr, p_scr), low)


def _gqa(p_gq, pc_gq):
    b, s, _ = p_gq.shape
    l = pc_gq.shape[1]
    tq = min(GQ_TQ, s)
    kblk, vblk = D_GQ // LANES, D_GQ // LANES + 1
    return pl.pallas_call(
        _gqa_kernel,
        grid=(b, s // tq),
        in_specs=[pl.BlockSpec((1, tq, D_GQ), lambda i, t: (i, t, 0)),
                  pl.BlockSpec((1, s, LANES), lambda i, t: (i, 0, kblk)),
                  pl.BlockSpec((1, s, LANES), lambda i, t: (i, 0, vblk)),
                  pl.BlockSpec((1, l, LANES), lambda i, t: (i, 0, kblk)),
                  pl.BlockSpec((1, l, LANES), lambda i, t: (i, 0, vblk))],
        out_specs=pl.BlockSpec((1, tq, D_GQ), lambda i, t: (i, t, 0)),
        out_shape=jax.ShapeDtypeStruct((b, s, D_GQ), BF16),
        scratch_shapes=[pltpu.VMEM((s + l, LANES), BF16), pltpu.VMEM((s + l, 2 * LANES), BF16),
                        pltpu.VMEM((2, tq, s + l), F32), pltpu.VMEM((2, tq, s + l), BF16)],
        compiler_params=_params(("parallel", "arbitrary")),
        name="gqa_latent",
    )(p_gq, p_gq, p_gq, pc_gq, pc_gq)


def _ctx_attn_kernel(q_ref, k_ref, v_ref, o_ref, s_scr, p_scr, *, shared_kv):
    low = _low_lanes()
    ones = jnp.ones((v_ref.shape[1], LANES), BF16)

    def kv_lanes(i):
        return slice(0, LANES) if shared_kv else slice(LANES * (i // 2), LANES * (i // 2 + 1))

    def score(i):
        j, half = divmod(i, 2)
        return _dot_nt(_one_head(q_ref[0, :, LANES * j:LANES * (j + 1)], low, half), k_ref[0, :, kv_lanes(i)])

    def value(i):
        return jnp.concatenate([v_ref[0, :, kv_lanes(i)], ones], axis=1)

    n_heads = 2 * (q_ref.shape[2] // LANES)
    _pair_outputs(o_ref, _attend_heads(n_heads, score, value, s_scr, p_scr), low)


def _ctx_attn(pc, *, qw, kw, shared_kv):
    b, l, _ = pc.shape
    kb = qw // kw
    return pl.pallas_call(
        functools.partial(_ctx_attn_kernel, shared_kv=shared_kv),
        grid=(b,),
        in_specs=[pl.BlockSpec((1, l, qw), lambda i: (i, 0, 0)),
                  pl.BlockSpec((1, l, kw), lambda i: (i, 0, kb)),
                  pl.BlockSpec((1, l, kw), lambda i: (i, 0, kb + 1))],
        out_specs=pl.BlockSpec((1, l, qw), lambda i: (i, 0, 0)),
        out_shape=jax.ShapeDtypeStruct((b, l, qw), BF16),
        scratch_shapes=[pltpu.VMEM((2, l, l), F32), pltpu.VMEM((2, l, l), BF16)],
        compiler_params=_params(("parallel",)),
        name="ctx_attn",
    )(pc, pc, pc)


def _na_kernel(q_ref, k0, k1, k2, k3, v0, v1, v2, v3, kc_ref, vc_ref, tz_ref, o_ref,
               kall, vall, s_scr, p_scr, *, n_rows):
    i = pl.program_id(1)
    r0 = i * NA_QROWS
    start = jnp.clip(r0 - NA_WIN_R // 2, 0, n_rows - NA_BAND)
    delta = start - r0
    nq, nk = NA_QROWS * GRID_W, NA_BAND * GRID_W
    sub = nk // 4
    n_ctx = kc_ref.shape[1]

    for t, (kr, vr) in enumerate(zip((k0, k1, k2, k3, kc_ref), (v0, v1, v2, v3, vc_ref))):
        rows = slice(sub * t, sub * t + kr.shape[1])
        kall[rows, :] = kr[0]
        for j in range(D_NA // LANES):
            vall[j, rows, 0:LANES] = vr[0, :, LANES * j:LANES * (j + 1)]
    vall[:, :, LANES:] = jnp.ones((D_NA // LANES, nk + n_ctx, LANES), BF16)

    qrow = r0 + lax.broadcasted_iota(jnp.int32, (nq, nk), 0) // GRID_W
    krow = start + lax.broadcasted_iota(jnp.int32, (nq, nk), 1) // GRID_W
    first = jnp.clip(qrow - NA_WIN_R // 2, 0, n_rows - NA_WIN_R)
    row_mask = jnp.where((krow >= first) & (krow < first + NA_WIN_R), 0.0, NEG)
    low = _low_lanes()
    no_bias = jnp.zeros((nq, n_ctx), F32)

    def score(h):
        j, half = divmod(h, 2)
        sl = slice(LANES * j, LANES * (j + 1))
        slabs = []
        for a in range(NA_QROWS):
            pieces = [tz_ref[h, jnp.clip(delta + 2 * bp - a, -8, 7) + 8] for bp in range(NA_BAND // 2)]
            slabs.append(jnp.concatenate(pieces, axis=1))
        bias = jnp.concatenate([jnp.concatenate(slabs, axis=0) + row_mask, no_bias], axis=1)
        return _dot_nt(_one_head(q_ref[0, :, sl], low, half), kall[:, sl]) + bias

    _pair_outputs(o_ref, _attend_heads(H_NA, score, lambda h: vall[h // 2], s_scr, p_scr), low)


def _na(p_na, pc_na, tz):
    b, s, _ = p_na.shape
    l = pc_na.shape[1]
    n_rows = s // GRID_W
    assert n_rows % NA_QROWS == 0 and n_rows >= NA_BAND
    nq, nk = NA_QROWS * GRID_W, NA_BAND * GRID_W
    sub = nk // 4
    rows_per_sub = NA_BAND // 4

    def band(t, blk):
        def idx(i, r):
            start = jnp.clip(r * NA_QROWS - NA_WIN_R // 2, 0, n_rows - NA_BAND)
            return (i, start // rows_per_sub + t, blk)
        return pl.BlockSpec((1, sub, D_NA), idx)

    in_specs = ([pl.BlockSpec((1, nq, D_NA), lambda i, r: (i, r, 0))]
                + [band(t, 1) for t in range(4)] + [band(t, 2) for t in range(4)]
                + [pl.BlockSpec((1, l, D_NA), lambda i, r: (i, 0, 1)),
                   pl.BlockSpec((1, l, D_NA), lambda i, r: (i, 0, 2)),
                   _resident(tz.shape)])
    return pl.pallas_call(
        functools.partial(_na_kernel, n_rows=n_rows),
        grid=(b, n_rows // NA_QROWS),
        in_specs=in_specs,
        out_specs=pl.BlockSpec((1, nq, D_NA), lambda i, r: (i, r, 0)),
        out_shape=jax.ShapeDtypeStruct((b, s, D_NA), BF16),
        scratch_shapes=[pltpu.VMEM((nk + l, D_NA), BF16), pltpu.VMEM((D_NA // LANES, nk + l, 2 * LANES), BF16),
                        pltpu.VMEM((2, nq, nk + l), F32), pltpu.VMEM((2, nq, nk + l), BF16)],
        compiler_params=_params(("parallel", "arbitrary")),
        name="na_latent",
    )(p_na, *([p_na] * 8), pc_na, pc_na, tz)


def _merge_kernel(x_ref, mod_ref, brg_ref, hf_ref, hb_ref, og_ref, mlw_ref, g_ref, na_ref, gq_ref,
                  wml_ref, wna_ref, wgq_ref, wo_ref, o_ref):
    gmat = g_ref[...]
    h = hf_ref[...].astype(F32) + hb_ref[...].astype(F32)
    og = og_ref[...].astype(F32)
    y = None
    for p in range(D_ML // LANES):
        sl = slice(LANES * p, LANES * (p + 1))
        o_ml = (_head_norm(h[:, sl], gmat, mlw_ref[:, sl]) * _sigmoid(og[:, sl])).astype(BF16)
        part = _dot(o_ml, wml_ref[sl, :])
        y = part if y is None else y + part
    d = x_ref.shape[1]
    y = brg_ref[:, 0:d].astype(F32) * y
    y = y + brg_ref[:, d:2 * d].astype(F32) * _dot(na_ref[...], wna_ref[...])
    y = y + brg_ref[:, 2 * d:3 * d].astype(F32) * _dot(gq_ref[...], wgq_ref[...])
    o_ref[...] = x_ref[...] + mod_ref[0, 5:6, :] * _dot(y.astype(BF16), wo_ref[...])


def _merge(x, mod, brg, hf, hb, p_ml, mlw, gmat, o_na, o_gq, wml, wna, wgq, wo, *, tiles_per_batch, ctx_row):
    t, d = x.shape
    tm = min(TM, t)
    row = lambda wd, blk=0: pl.BlockSpec((tm, wd), lambda i: (i, blk))
    return pl.pallas_call(
        _merge_kernel,
        grid=(t // tm,),
        in_specs=[row(d),
                  pl.BlockSpec((1, N_MOD, d), _mod_index(tiles_per_batch, ctx_row)),
                  row(3 * d), row(D_ML), row(D_ML), row(D_ML, 3),
                  _resident((1, D_ML)), _resident((LANES, LANES)),
                  row(D_NA), row(D_GQ),
                  _resident((D_ML, d)), _resident((D_NA, d)), _resident((D_GQ, d)), _resident((d, d))],
        out_specs=row(d),
        out_shape=jax.ShapeDtypeStruct((t, d), F32),
        compiler_params=_params(("parallel",)),
        name="branch_merge",
    )(x, mod, brg, hf, hb, p_ml, mlw, gmat, o_na, o_gq, wml, wna, wgq, wo)


def _proj_weight(w):
    d = w.shape[0]
    o = 0
    seg = {}
    for name, width in (("ml_k", D_ML), ("ml_v", D_ML), ("ml_g", 4 * H_ML), ("na_k", D_NA), ("na_v", D_NA),
                        ("gq_k", D_KV), ("gq_v", D_KV), ("ml_q", D_ML), ("ml_o", D_ML), ("na_q", D_NA),
                        ("gq_q", D_GQ), ("br_g", 3 * d)):
        seg[name] = w[:, o:o + width]
        o += width
    gq_q = jnp.concatenate([seg["gq_q"][:, HEAD_DIM * h:HEAD_DIM * (h + 1)] for h in GQ_HEAD_ORDER], axis=1)
    pad = jnp.zeros((d, LANES - 4 * H_ML), w.dtype)
    out = jnp.concatenate([seg["ml_q"], seg["ml_k"], seg["ml_v"], seg["ml_o"], seg["ml_g"], pad,
                           seg["na_q"], seg["na_k"], seg["na_v"], gq_q, seg["gq_k"], seg["gq_v"], seg["br_g"]],
                          axis=1)
    return out.astype(BF16)


def _rope_tables(n_tok):
    t = jnp.arange(n_tok, dtype=jnp.int32)
    row = (t // GRID_W).astype(F32)
    col = (t % GRID_W).astype(F32)
    n_freq = HEAD_DIM // 4
    inv = ROPE_THETA ** (-jnp.arange(n_freq, dtype=F32) / n_freq)
    ang = jnp.concatenate([row[:, None] * inv, col[:, None] * inv], axis=-1)
    cos, sin = jnp.cos(ang), jnp.sin(ang)
    cos_t = jnp.tile(cos, (1, LANES // (HEAD_DIM // 2)))
    sin_t = jnp.tile(jnp.concatenate([-sin, sin], axis=-1), (1, LANES // HEAD_DIM))
    return cos_t, sin_t


def _na_bias_table(rpb):
    col = np.arange(GRID_W)
    first = np.clip(col - NA_WIN_C // 2, 0, GRID_W - NA_WIN_C)
    in_win = (col[None, :] >= first[:, None]) & (col[None, :] < first[:, None] + NA_WIN_C)
    off = np.clip(col[None, :] - col[:, None] + NA_WIN_C - 1, 0, 2 * NA_WIN_C - 2)
    rows = jnp.pad(rpb, ((0, 0), (1, 1), (0, 0)))
    row_ok = np.zeros((2 * NA_WIN_R + 1,), bool)
    row_ok[1:-1] = True
    full = rows[:, :, off] * float(np.log2(np.e))
    full = jnp.where(jnp.asarray(in_win[None, None] & row_ok[None, :, None, None]), full, NEG)
    return jnp.concatenate([full[:, :-1], full[:, 1:]], axis=-1).astype(F32)


def kernel(x, c, ctx, c_ctx, ada_w, ada_b, norm_w, ffn_w_in, ffn_w_out, mix_w_in, ml_gate_b, ml_norm_w,
           na_qk_w, na_rpb, gq_qk_w, w_br_ml, w_br_na, w_br_gq, w_out):
    b, s, d = x.shape
    l = ctx.shape[1]
    depth = ada_w.shape[0]
    assert b < MOD_ROWS and s % TM == 0 and (b * l) % min(TM, b * l) == 0
    ctx_row = b
    tiles_per_batch = s // TM

    cvec = jnp.zeros((MOD_ROWS, d), F32).at[:b].set(c).at[b].set(c_ctx)
    mod = _ada(cvec, ada_w, ada_b).reshape(depth, MOD_ROWS, N_MOD, d)

    lane = np.arange(LANES)
    gmat = jnp.asarray((lane[:, None] // HEAD_DIM) == (lane[None, :] // HEAD_DIM), BF16)
    idx = np.arange(ML_CHUNK)
    tri = jnp.asarray(np.stack([idx[:, None] >= idx[None, :], idx[:, None] <= idx[None, :]]), BF16)
    rope_tabs = _rope_tables(s)

    xl = x.reshape(b * s, d)
    xc = ctx.reshape(b * l, d)
    lat = dict(tiles_per_batch=tiles_per_batch, ctx_row=ctx_row)
    con = dict(tiles_per_batch=None, ctx_row=ctx_row)
    for li in range(depth):
        ctx_out = li < depth - 1
        w_in = [ffn_w_in[li, t].astype(BF16) for t in range(2)]
        w_o = [ffn_w_out[li, t].astype(BF16) for t in range(2)]
        w_proj = _proj_weight(mix_w_in[li])
        qkw = jnp.zeros((8, 4 * LANES), F32)
        qkw = qkw.at[0, :D_NA].set(jnp.tile(na_qk_w[li, 0], H_NA)).at[1, :D_NA].set(jnp.tile(na_qk_w[li, 1], H_NA))
        qkw = qkw.at[2, :D_GQ].set(jnp.tile(gq_qk_w[li, 0], H_GQ)).at[2, D_GQ:].set(jnp.tile(gq_qk_w[li, 1], H_KV))
        gate_b = jnp.zeros((1, LANES), F32).at[0, :4 * H_ML].set(ml_gate_b[li])
        mlw = ml_norm_w[li].reshape(1, D_ML)
        tz = _na_bias_table(na_rpb[li])
        wml, wna, wo = w_br_ml[li].astype(BF16), w_br_na[li].astype(BF16), w_out[li].astype(BF16)
        wgq = jnp.concatenate([w_br_gq[li, HEAD_DIM * h:HEAD_DIM * (h + 1)] for h in GQ_HEAD_ORDER],
                              axis=0).astype(BF16)
        m, nw = mod[li], norm_w[li]

        xl = _ffn(xl, m, nw, w_in[0], w_o[0], k0=0, nrm=0, **lat)
        xc = _ffn(xc, m, nw, w_in[0], w_o[0], k0=0, nrm=0, **con)

        p_ml, p_mlg, p_na, p_gq, p_brg = _proj(xl, m, nw, w_proj, gmat, qkw, rope_tabs, **lat)
        pc_ml, pc_mlg, pc_na, pc_gq, pc_brg = _proj(xc, m, nw, w_proj, gmat, qkw, None, **con)
        seq = lambda a: a.reshape(b, s, a.shape[-1])
        cseq = lambda a: a.reshape(b, l, a.shape[-1])

        hf, hb, hcf, hcb = _mlstm(seq(p_ml), seq(p_mlg), cseq(pc_ml), cseq(pc_mlg), tri, gate_b)
        o_na = _na(seq(p_na), cseq(pc_na), tz)
        o_gq = _gqa(seq(p_gq), cseq(pc_gq))
        flat = lambda a: a.reshape(-1, a.shape[-1])
        xl = _merge(xl, m, p_brg, flat(hf), flat(hb), p_ml, mlw, gmat, flat(o_na), flat(o_gq),
                    wml, wna, wgq, wo, **lat)
        xl = _ffn(xl, m, nw, w_in[1], w_o[1], k0=6, nrm=2, **lat)
        if ctx_out:
            co_na = _ctx_attn(cseq(pc_na), qw=D_NA, kw=D_NA, shared_kv=False)
            co_gq = _ctx_attn(cseq(pc_gq), qw=D_GQ, kw=D_KV, shared_kv=True)
            xc = _merge(xc, m, pc_brg, flat(hcf), flat(hcb), pc_ml, mlw, gmat, flat(co_na), flat(co_gq),
                        wml, wna, wgq, wo, **con)
            xc = _ffn(xc, m, nw, w_in[1], w_o[1], k0=6, nrm=2, **con)
    return xl.reshape(b, s, d)
```

```python
import functools

import numpy as np
import jax
import jax.numpy as jnp
from jax import lax
from jax.experimental import pallas as pl
from jax.experimental.pallas import tpu as pltpu

F32 = jnp.float32
BF16 = jnp.bfloat16

HEAD_DIM = 64
LANES = 128
H_ML, H_NA, H_GQ, H_KV = 4, 6, 6, 2
D_ML, D_NA, D_GQ, D_KV = 256, 384, 384, 128
GRID_W = 64
NA_WIN_R, NA_WIN_C = 8, 16
ROPE_THETA = 10000.0
EPS = 1e-6
N_MOD = 9
ATTN_SCALE = HEAD_DIM ** -0.5
NEG = -1e30

ML_CHUNK = 256
NA_QROWS = 8
NA_BAND = 16
TM = 512
GQ_TQ = 256
FFN_CHUNK = 768
Q_PRESCALE = ATTN_SCALE * float(np.log2(np.e))
MOD_ROWS = 16
VMEM_LIMIT = 56 * 1024 * 1024

C_ML, C_MLG, C_NA, C_GQ, C_BRG = 0, 1024, 1152, 2304, 2944
N_PROJ = 6016
GQ_HEAD_ORDER = (0, 3, 1, 4, 2, 5)
ML_GATE_ORDER = np.array([0, 1, 2, 3, 8, 9, 10, 11, 4, 5, 6, 7, 12, 13, 14, 15])


def _dot(a, b):
    return jnp.dot(a, b, preferred_element_type=F32)


def _dot_nt(a, b):
    return lax.dot_general(a, b, (((1,), (1,)), ((), ())), preferred_element_type=F32)


def _sigmoid(x):
    return 1.0 / (1.0 + jnp.exp(-x))


def _log_sigmoid(x):
    return jnp.minimum(x, 0.0) - jnp.log1p(jnp.exp(-jnp.abs(x)))


def _split_bf16(x):
    hi = x.astype(BF16)
    lo = (x - hi.astype(F32)).astype(BF16)
    return hi, lo


def _modnorm(x, nw, shift, scale):
    ms = jnp.mean(x * x, axis=-1, keepdims=True)
    return (x * lax.rsqrt(ms + EPS) * nw) * (1.0 + scale) + shift


def _head_norm(t, gmat, wrow):
    hi, lo = _split_bf16(t * t)
    ss = _dot(hi, gmat) + _dot(lo, gmat)
    return t * lax.rsqrt(ss * (1.0 / HEAD_DIM) + EPS) * wrow


def _resident(shape):
    return pl.BlockSpec(shape, lambda *_: (0,) * len(shape), pipeline_mode=pl.Buffered(1))


def _params(sem):
    return pltpu.CompilerParams(dimension_semantics=sem, vmem_limit_bytes=VMEM_LIMIT)


def _ada_kernel(c_ref, w_ref, b_ref, o_ref):
    c = c_ref[...]
    s = (c * _sigmoid(c)).astype(BF16)
    o_ref[0] = _dot(s, w_ref[0].astype(BF16)) + b_ref[0]


def _ada(cvec, ada_w, ada_b):
    depth, d, n = ada_w.shape
    tn = n // 8
    return pl.pallas_call(
        _ada_kernel,
        grid=(depth, n // tn),
        in_specs=[pl.BlockSpec((MOD_ROWS, d), lambda l, j: (0, 0)),
                  pl.BlockSpec((1, d, tn), lambda l, j: (l, 0, j)),
                  pl.BlockSpec((1, 1, tn), lambda l, j: (l, 0, j))],
        out_specs=pl.BlockSpec((1, MOD_ROWS, tn), lambda l, j: (l, 0, j)),
        out_shape=jax.ShapeDtypeStruct((depth, MOD_ROWS, n), F32),
        compiler_params=_params(("arbitrary", "arbitrary")),
        name="ada_mod",
    )(cvec, ada_w, ada_b.reshape(depth, 1, n))


def _mod_index(tiles_per_batch, ctx_row):
    if tiles_per_batch is None:
        return lambda i, *_: (ctx_row, 0, 0)
    return lambda i, *_: (i // tiles_per_batch, 0, 0)


def _ffn_kernel(x_ref, mod_ref, nw_ref, wi_ref, wo_ref, o_ref, *, k0, nrm, chunks):
    x = x_ref[...]
    h = _modnorm(x, nw_ref[nrm:nrm + 1, :], mod_ref[0, k0:k0 + 1, :], mod_ref[0, k0 + 1:k0 + 2, :]).astype(BF16)
    dff = wo_ref.shape[0]
    y = None
    for c0, c1 in chunks:
        g = _dot(h, wi_ref[:, c0:c1])
        u = _dot(h, wi_ref[:, dff + c0:dff + c1])
        part = _dot((g * _sigmoid(g) * u).astype(BF16), wo_ref[c0:c1, :])
        y = part if y is None else y + part
    o_ref[...] = x + (0.5 * mod_ref[0, k0 + 2:k0 + 3, :]) * y


def _ffn(x, mod, nw, w_in, w_out, *, k0, nrm, tiles_per_batch, ctx_row):
    t, d = x.shape
    dff = w_out.shape[0]
    tm = min(TM, t)
    edges = list(range(0, dff, FFN_CHUNK)) + [dff]
    chunks = tuple(zip(edges[:-1], edges[1:]))
    kern = functools.partial(_ffn_kernel, k0=k0, nrm=nrm, chunks=chunks)
    return pl.pallas_call(
        kern,
        grid=(t // tm,),
        in_specs=[pl.BlockSpec((tm, d), lambda i: (i, 0)),
                  pl.BlockSpec((1, N_MOD, d), _mod_index(tiles_per_batch, ctx_row)),
                  pl.BlockSpec((3, d), lambda i: (0, 0)),
                  _resident((d, 2 * dff)),
                  _resident((dff, d))],
        out_specs=pl.BlockSpec((tm, d), lambda i: (i, 0)),
        out_shape=jax.ShapeDtypeStruct((t, d), F32),
        compiler_params=_params(("parallel",)),
        name="ffn_swiglu",
    )(x, mod, nw, w_in, w_out)


def _proj_kernel(*refs, rope):
    if rope:
        (x_ref, mod_ref, nw_ref, w_ref, g_ref, qkw_ref, cos_ref, sin_ref,
         ml_ref, mlg_ref, na_ref, gq_ref, brg_ref) = refs
    else:
        (x_ref, mod_ref, nw_ref, w_ref, g_ref, qkw_ref,
         ml_ref, mlg_ref, na_ref, gq_ref, brg_ref) = refs
    h = _modnorm(x_ref[...], nw_ref[1:2, :], mod_ref[0, 3:4, :], mod_ref[0, 4:5, :]).astype(BF16)
    gmat = g_ref[...]

    ml_ref[...] = _dot(h, w_ref[:, C_ML:C_MLG]).astype(BF16)
    mlg_ref[...] = _dot(h, w_ref[:, C_MLG:C_NA])

    na = _dot(h, w_ref[:, C_NA:C_GQ])
    for j in range(6):
        sl = slice(LANES * j, LANES * (j + 1))
        wsl = slice(LANES * (j % 3), LANES * (j % 3 + 1))
        t = _head_norm(na[:, sl], gmat, qkw_ref[j // 3:j // 3 + 1, wsl])
        na_ref[:, sl] = (t * Q_PRESCALE if j < 3 else t).astype(BF16)
    na_ref[:, 2 * D_NA:] = na[:, 2 * D_NA:].astype(BF16)

    gq = _dot(h, w_ref[:, C_GQ:C_BRG])
    if rope:
        lane = lax.broadcasted_iota(jnp.int32, (1, LANES), 1)
        first_half = (lane % HEAD_DIM) < (HEAD_DIM // 2)
        cos = cos_ref[...]
        sin = sin_ref[...]
    for j in range(4):
        sl = slice(LANES * j, LANES * (j + 1))
        t = _head_norm(gq[:, sl], gmat, qkw_ref[2:3, sl])
        if rope:
            rot = jnp.where(first_half, pltpu.roll(t, LANES - HEAD_DIM // 2, axis=1),
                            pltpu.roll(t, HEAD_DIM // 2, axis=1))
            t = t * cos + rot * sin
        gq_ref[:, sl] = (t * Q_PRESCALE if j < 3 else t).astype(BF16)
    gq_ref[:, D_GQ + D_KV:] = gq[:, D_GQ + D_KV:].astype(BF16)

    for j in range(3):
        sl = slice(1024 * j, 1024 * (j + 1))
        brg_ref[:, sl] = _sigmoid(_dot(h, w_ref[:, C_BRG + 1024 * j:C_BRG + 1024 * (j + 1)])).astype(BF16)


def _proj(x, mod, nw, w, gmat, qkw, rope_tabs, *, tiles_per_batch, ctx_row):
    t, d = x.shape
    tm = min(TM, t)
    rope = rope_tabs is not None
    in_specs = [pl.BlockSpec((tm, d), lambda i: (i, 0)),
                pl.BlockSpec((1, N_MOD, d), _mod_index(tiles_per_batch, ctx_row)),
                pl.BlockSpec((3, d), lambda i: (0, 0)),
                _resident((d, N_PROJ)),
                _resident((LANES, LANES)),
                _resident(qkw.shape)]
    args = [x, mod, nw, w, gmat, qkw]
    if rope:
        in_specs += [pl.BlockSpec((tm, LANES), lambda i: (i % tiles_per_batch, 0))] * 2
        args += list(rope_tabs)
    widths = (1024, LANES, 3 * D_NA, D_GQ + 2 * D_KV, 3072)
    dtypes = (BF16, F32, BF16, BF16, BF16)
    return pl.pallas_call(
        functools.partial(_proj_kernel, rope=rope),
        grid=(t // tm,),
        in_specs=in_specs,
        out_specs=[pl.BlockSpec((tm, wd), lambda i: (i, 0)) for wd in widths],
        out_shape=[jax.ShapeDtypeStruct((t, wd), dt) for wd, dt in zip(widths, dtypes)],
        compiler_params=_params(("parallel",)),
        name="mix_in_proj",
    )(*args)


def _ml_prep(d, g_ref, tri_ref, bias_ref):
    log2e = float(np.log2(np.e))
    lc = g_ref.shape[1]
    gates = g_ref[0] + bias_ref[...]
    gates_t = gates.T
    ig_t = gates_t[0:8] * log2e
    hi, lo = _split_bf16(_log_sigmoid(gates_t[0:16]) * log2e)
    ones = jnp.ones((lc, lc), BF16)
    b_t = (_dot(hi, tri_ref[1 - d]) + _dot(lo, tri_ref[1 - d]))[8:16]
    btot_t = (_dot(hi, ones) + _dot(lo, ones))[8:16]
    lf_al = pltpu.roll(_log_sigmoid(gates) * log2e, LANES - 8, axis=1)
    hi, lo = _split_bf16(lf_al)
    c_mat = gates * log2e - (_dot(tri_ref[d], hi) + _dot(tri_ref[d], lo))
    return ig_t, b_t, btot_t, c_mat


def _ml_head(d, h, prep, m_ref):
    ig_t, b_t, btot_t, c_mat = prep
    lc = c_mat.shape[0]
    r = 4 * d + h
    row = lax.broadcasted_iota(jnp.int32, (lc, lc), 0)
    col = lax.broadcasted_iota(jnp.int32, (lc, lc), 1)
    visible = (row <= col) if d == 0 else (row >= col)
    ig, b, b_tot = ig_t[r:r + 1], b_t[r:r + 1], btot_t[r:r + 1]
    m_prev = m_ref[r:r + 1, :]
    w_end = b_tot - b + ig
    m_new = jnp.maximum(b_tot + m_prev, jnp.max(w_end, axis=1, keepdims=True))
    a = jnp.exp2(w_end - m_new)
    decay = jnp.exp2(b_tot + m_prev - m_new)
    m_inter = b + m_prev
    logw = jnp.where(visible, c_mat[:, r:r + 1] + b, NEG)
    m_j = jnp.maximum(m_inter, jnp.max(logw, axis=0, keepdims=True))
    w = jnp.exp2(logw - m_j)
    m_ref[r:r + 1, :] = m_new
    return w, a, jnp.exp2(m_inter - m_j), jnp.exp2(-m_j), decay


def _ml_pair(d, p, head_a, head_b, q_ref, k_ref, v_ref, o_ref, st_ref):
    (w_a, a_a, g_a, fl_a, dec_a), (w_b, a_b, g_b, fl_b, dec_b) = head_a, head_b
    lc = q_ref.shape[1]
    low = _low_lanes()
    sl = slice(LANES * p, LANES * (p + 1))
    q = q_ref[0, :, sl]
    k = k_ref[0, :, sl] * ATTN_SCALE
    vt = jnp.concatenate([v_ref[0, :, sl].astype(F32).T, jnp.ones((LANES, lc), F32)], axis=0)
    head_row = (lax.broadcasted_iota(jnp.int32, (2 * LANES, 1), 0) % LANES) < HEAD_DIM
    vt16 = vt.astype(BF16)
    r_a = _dot(vt16, (_dot_nt(k, _one_head(q, low, 0)) * w_a).astype(BF16))
    r_b = _dot(vt16, (_dot_nt(k, _one_head(q, low, 1)) * w_b).astype(BF16))
    state = st_ref[d, p]
    r_i = _dot_nt(state.astype(BF16), q)
    r = jnp.where(head_row, r_a, r_b) + jnp.where(head_row, g_a, g_b) * r_i
    num, den = r[:LANES], r[LANES:]
    h_t = num / jnp.maximum(jnp.abs(den), jnp.where(head_row[:LANES], fl_a, fl_b))
    o_ref[0, :, sl] = h_t.T.astype(BF16)

    upd = _dot((vt * jnp.where(head_row, a_a, a_b)).astype(BF16), k)
    same_head = head_row == low
    dec = jnp.where(head_row, dec_a[:, :LANES], dec_b[:, :LANES])
    st_ref[d, p] = dec * state + jnp.where(same_head, upd, 0.0)


def _ml_step(fwd, bwd, tri_ref, bias_ref, st_ref, m_ref):
    dirs = (fwd, bwd)
    preps = [_ml_prep(d, refs[3], tri_ref, bias_ref) for d, refs in enumerate(dirs)]
    heads = [[_ml_head(d, h, preps[d], m_ref) for h in range(H_ML)] for d in range(2)]
    for p in range(H_ML // 2):
        for d, (q_ref, k_ref, v_ref, _, o_ref) in enumerate(dirs):
            _ml_pair(d, p, heads[d][2 * p], heads[d][2 * p + 1], q_ref, k_ref, v_ref, o_ref, st_ref)


def _mlstm_kernel(qf, kf, vf, gf, qb, kb, vb, gb, qc, kc, vc, gc, tri_ref, bias_ref,
                  hf_ref, hb_ref, hcf_ref, hcb_ref, st_ref, m_ref):
    c = pl.program_id(1)

    @pl.when(c == 0)
    def _():
        st_ref[...] = jnp.zeros_like(st_ref)
        m_ref[...] = jnp.zeros_like(m_ref)
        _ml_step((qc, kc, vc, gc, hcf_ref), (qc, kc, vc, gc, hcb_ref), tri_ref, bias_ref, st_ref, m_ref)

    @pl.when(c > 0)
    def _():
        _ml_step((qf, kf, vf, gf, hf_ref), (qb, kb, vb, gb, hb_ref), tri_ref, bias_ref, st_ref, m_ref)


def _mlstm(p_ml, p_mlg, pc_ml, pc_mlg, tri, bias):
    b, s, _ = p_ml.shape
    l = pc_ml.shape[1]
    lc = ML_CHUNK
    assert l == lc and s % lc == 0
    nl = s // lc
    fwd = lambda c: jnp.maximum(c - 1, 0)
    bwd = lambda c: nl - 1 - jnp.maximum(c - 1, 0)

    def lat(idx, blk, width):
        return pl.BlockSpec((1, lc, width), lambda i, c: (i, idx(c), blk))

    def ctx(blk, width):
        return pl.BlockSpec((1, lc, width), lambda i, c: (i, 0, blk))

    in_specs = ([lat(fwd, 0, D_ML), lat(fwd, 1, D_ML), lat(fwd, 2, D_ML), lat(fwd, 0, LANES)]
                + [lat(bwd, 0, D_ML), lat(bwd, 1, D_ML), lat(bwd, 2, D_ML), lat(bwd, 0, LANES)]
                + [ctx(0, D_ML), ctx(1, D_ML), ctx(2, D_ML), ctx(0, LANES)]
                + [_resident((2, lc, lc)), _resident((1, LANES))])
    out_specs = [lat(fwd, 0, D_ML), lat(bwd, 0, D_ML), ctx(0, D_ML), ctx(0, D_ML)]
    out_shape = [jax.ShapeDtypeStruct((b, s, D_ML), BF16)] * 2 + [jax.ShapeDtypeStruct((b, l, D_ML), BF16)] * 2
    return pl.pallas_call(
        _mlstm_kernel,
        grid=(b, nl + 1),
        in_specs=in_specs,
        out_specs=out_specs,
        out_shape=out_shape,
        scratch_shapes=[pltpu.VMEM((2, H_ML // 2, 2 * LANES, LANES), F32), pltpu.VMEM((2 * H_ML, lc), F32)],
        compiler_params=_params(("parallel", "arbitrary")),
        name="mlstm_bidir",
    )(p_ml, p_ml, p_ml, p_mlg, p_ml, p_ml, p_ml, p_mlg, pc_ml, pc_ml, pc_ml, pc_mlg, tri, bias)


def _attend_heads(n_heads, score_fn, value_fn, s_scr, p_scr):
    s_scr[0] = score_fn(0)
    outs = []
    for i in range(n_heads):
        if i + 1 < n_heads:
            s_scr[(i + 1) % 2] = score_fn(i + 1)
        s = s_scr[i % 2]
        p_scr[i % 2] = jnp.exp2(s - jnp.max(s, axis=1, keepdims=True)).astype(BF16)
        r = _dot(p_scr[i % 2], value_fn(i))
        outs.append(r[:, :LANES] / r[:, LANES:])
    return outs


def _low_lanes():
    return lax.broadcasted_iota(jnp.int32, (1, LANES), 1) < HEAD_DIM


def _one_head(q, low, half):
    zero = jnp.zeros_like(q)
    return jnp.where(low, q, zero) if half == 0 else jnp.where(low, zero, q)


def _pair_outputs(o_ref, outs, low):
    for j in range(len(outs) // 2):
        o_ref[0, :, LANES * j:LANES * (j + 1)] = jnp.where(low, outs[2 * j], outs[2 * j + 1]).astype(BF16)


def _gqa_kernel(q_ref, k_ref, v_ref, kc_ref, vc_ref, o_ref, kall, vall, s_scr, p_scr):
    n_lat = k_ref.shape[1]

    @pl.when(pl.program_id(1) == 0)
    def _():
        kall[0:n_lat, :] = k_ref[0]
        kall[n_lat:, :] = kc_ref[0]
        vall[0:n_lat, 0:LANES] = v_ref[0]
        vall[n_lat:, 0:LANES] = vc_ref[0]
        vall[:, LANES:] = jnp.ones((vall.shape[0], LANES), BF16)

    low = _low_lanes()

    def score(i):
        j, half = divmod(i, 2)
        return _dot_nt(_one_head(q_ref[0, :, LANES * j:LANES * (j + 1)], low, half), kall[...])

    _pair_outputs(o_ref, _attend_heads(H_GQ, score, lambda i: vall[...], s_scr, p_scr), low)


def _gqa(p_gq, pc_gq):
    b, s, _ = p_gq.shape
    l = pc_gq.shape[1]
    tq = min(GQ_TQ, s)
    kblk, vblk = D_GQ // LANES, D_GQ // LANES + 1
    return pl.pallas_call(
        _gqa_kernel,
        grid=(b, s // tq),
        in_specs=[pl.BlockSpec((1, tq, D_GQ), lambda i, t: (i, t, 0)),
                  pl.BlockSpec((1, s, LANES), lambda i, t: (i, 0, kblk)),
                  pl.BlockSpec((1, s, LANES), lambda i, t: (i, 0, vblk)),
                  pl.BlockSpec((1, l, LANES), lambda i, t: (i, 0, kblk)),
                  pl.BlockSpec((1, l, LANES), lambda i, t: (i, 0, vblk))],
        out_specs=pl.BlockSpec((1, tq, D_GQ), lambda i, t: (i, t, 0)),
        out_shape=jax.ShapeDtypeStruct((b, s, D_GQ), BF16),
        scratch_shapes=[pltpu.VMEM((s + l, LANES), BF16), pltpu.VMEM((s + l, 2 * LANES), BF16),
                        pltpu.VMEM((2, tq, s + l), F32), pltpu.VMEM((2, tq, s + l), BF16)],
        compiler_params=_params(("parallel", "arbitrary")),
        name="gqa_latent",
    )(p_gq, p_gq, p_gq, pc_gq, pc_gq)


def _ctx_attn_kernel(q_ref, k_ref, v_ref, o_ref, s_scr, p_scr, *, shared_kv):
    low = _low_lanes()
    ones = jnp.ones((v_ref.shape[1], LANES), BF16)

    def kv_lanes(i):
        return slice(0, LANES) if shared_kv else slice(LANES * (i // 2), LANES * (i // 2 + 1))

    def score(i):
        j, half = divmod(i, 2)
        return _dot_nt(_one_head(q_ref[0, :, LANES * j:LANES * (j + 1)], low, half), k_ref[0, :, kv_lanes(i)])

    def value(i):
        return jnp.concatenate([v_ref[0, :, kv_lanes(i)], ones], axis=1)

    n_heads = 2 * (q_ref.shape[2] // LANES)
    _pair_outputs(o_ref, _attend_heads(n_heads, score, value, s_scr, p_scr), low)


def _ctx_attn(pc, *, qw, kw, shared_kv):
    b, l, _ = pc.shape
    kb = qw // kw
    return pl.pallas_call(
        functools.partial(_ctx_attn_kernel, shared_kv=shared_kv),
        grid=(b,),
        in_specs=[pl.BlockSpec((1, l, qw), lambda i: (i, 0, 0)),
                  pl.BlockSpec((1, l, kw), lambda i: (i, 0, kb)),
                  pl.BlockSpec((1, l, kw), lambda i: (i, 0, kb + 1))],
        out_specs=pl.BlockSpec((1, l, qw), lambda i: (i, 0, 0)),
        out_shape=jax.ShapeDtypeStruct((b, l, qw), BF16),
        scratch_shapes=[pltpu.VMEM((2, l, l), F32), pltpu.VMEM((2, l, l), BF16)],
        compiler_params=_params(("parallel",)),
        name="ctx_attn",
    )(pc, pc, pc)


def _na_kernel(q_ref, k0, k1, k2, k3, v0, v1, v2, v3, kc_ref, vc_ref, tz_ref, o_ref,
               kall, vall, s_scr, p_scr, *, n_rows):
    i = pl.program_id(1)
    r0 = i * NA_QROWS
    start = jnp.clip(r0 - NA_WIN_R // 2, 0, n_rows - NA_BAND)
    delta = start - r0
    nq, nk = NA_QROWS * GRID_W, NA_BAND * GRID_W
    sub = nk // 4
    n_ctx = kc_ref.shape[1]

    for t, (kr, vr) in enumerate(zip((k0, k1, k2, k3, kc_ref), (v0, v1, v2, v3, vc_ref))):
        rows = slice(sub * t, sub * t + kr.shape[1])
        kall[rows, :] = kr[0]
        for j in range(D_NA // LANES):
            vall[j, rows, 0:LANES] = vr[0, :, LANES * j:LANES * (j + 1)]
    vall[:, :, LANES:] = jnp.ones((D_NA // LANES, nk + n_ctx, LANES), BF16)

    qrow = r0 + lax.broadcasted_iota(jnp.int32, (nq, nk), 0) // GRID_W
    krow = start + lax.broadcasted_iota(jnp.int32, (nq, nk), 1) // GRID_W
    first = jnp.clip(qrow - NA_WIN_R // 2, 0, n_rows - NA_WIN_R)
    row_mask = jnp.where((krow >= first) & (krow < first + NA_WIN_R), 0.0, NEG)
    low = _low_lanes()
    no_bias = jnp.zeros((nq, n_ctx), F32)

    def score(h):
        j, half = divmod(h, 2)
        sl = slice(LANES * j, LANES * (j + 1))
        slabs = []
        for a in range(NA_QROWS):
            pieces = [tz_ref[h, jnp.clip(delta + 2 * bp - a, -8, 7) + 8] for bp in range(NA_BAND // 2)]
            slabs.append(jnp.concatenate(pieces, axis=1))
        bias = jnp.concatenate([jnp.concatenate(slabs, axis=0) + row_mask, no_bias], axis=1)
        return _dot_nt(_one_head(q_ref[0, :, sl], low, half), kall[:, sl]) + bias

    _pair_outputs(o_ref, _attend_heads(H_NA, score, lambda h: vall[h // 2], s_scr, p_scr), low)


def _na(p_na, pc_na, tz):
    b, s, _ = p_na.shape
    l = pc_na.shape[1]
    n_rows = s // GRID_W
    assert n_rows % NA_QROWS == 0 and n_rows >= NA_BAND
    nq, nk = NA_QROWS * GRID_W, NA_BAND * GRID_W
    sub = nk // 4
    rows_per_sub = NA_BAND // 4

    def band(t, blk):
        def idx(i, r):
            start = jnp.clip(r * NA_QROWS - NA_WIN_R // 2, 0, n_rows - NA_BAND)
            return (i, start // rows_per_sub + t, blk)
        return pl.BlockSpec((1, sub, D_NA), idx)

    in_specs = ([pl.BlockSpec((1, nq, D_NA), lambda i, r: (i, r, 0))]
                + [band(t, 1) for t in range(4)] + [band(t, 2) for t in range(4)]
                + [pl.BlockSpec((1, l, D_NA), lambda i, r: (i, 0, 1)),
                   pl.BlockSpec((1, l, D_NA), lambda i, r: (i, 0, 2)),
                   _resident(tz.shape)])
    return pl.pallas_call(
        functools.partial(_na_kernel, n_rows=n_rows),
        grid=(b, n_rows // NA_QROWS),
        in_specs=in_specs,
        out_specs=pl.BlockSpec((1, nq, D_NA), lambda i, r: (i, r, 0)),
        out_shape=jax.ShapeDtypeStruct((b, s, D_NA), BF16),
        scratch_shapes=[pltpu.VMEM((nk + l, D_NA), BF16), pltpu.VMEM((D_NA // LANES, nk + l, 2 * LANES), BF16),
                        pltpu.VMEM((2, nq, nk + l), F32), pltpu.VMEM((2, nq, nk + l), BF16)],
        compiler_params=_params(("parallel", "arbitrary")),
        name="na_latent",
    )(p_na, *([p_na] * 8), pc_na, pc_na, tz)


def _merge_kernel(x_ref, mod_ref, brg_ref, hf_ref, hb_ref, og_ref, mlw_ref, g_ref, na_ref, gq_ref,
                  wml_ref, wna_ref, wgq_ref, wo_ref, o_ref):
    gmat = g_ref[...]
    h = hf_ref[...].astype(F32) + hb_ref[...].astype(F32)
    og = og_ref[...].astype(F32)
    y = None
    for p in range(D_ML // LANES):
        sl = slice(LANES * p, LANES * (p + 1))
        o_ml = (_head_norm(h[:, sl], gmat, mlw_ref[:, sl]) * _sigmoid(og[:, sl])).astype(BF16)
        part = _dot(o_ml, wml_ref[sl, :])
        y = part if y is None else y + part
    d = x_ref.shape[1]
    y = brg_ref[:, 0:d].astype(F32) * y
    y = y + brg_ref[:, d:2 * d].astype(F32) * _dot(na_ref[...], wna_ref[...])
    y = y + brg_ref[:, 2 * d:3 * d].astype(F32) * _dot(gq_ref[...], wgq_ref[...])
    o_ref[...] = x_ref[...] + mod_ref[0, 5:6, :] * _dot(y.astype(BF16), wo_ref[...])


def _merge(x, mod, brg, hf, hb, p_ml, mlw, gmat, o_na, o_gq, wml, wna, wgq, wo, *, tiles_per_batch, ctx_row):
    t, d = x.shape
    tm = min(TM, t)
    row = lambda wd, blk=0: pl.BlockSpec((tm, wd), lambda i: (i, blk))
    return pl.pallas_call(
        _merge_kernel,
        grid=(t // tm,),
        in_specs=[row(d),
                  pl.BlockSpec((1, N_MOD, d), _mod_index(tiles_per_batch, ctx_row)),
                  row(3 * d), row(D_ML), row(D_ML), row(D_ML, 3),
                  _resident((1, D_ML)), _resident((LANES, LANES)),
                  row(D_NA), row(D_GQ),
                  _resident((D_ML, d)), _resident((D_NA, d)), _resident((D_GQ, d)), _resident((d, d))],
        out_specs=row(d),
        out_shape=jax.ShapeDtypeStruct((t, d), F32),
        compiler_params=_params(("parallel",)),
        name="branch_merge",
    )(x, mod, brg, hf, hb, p_ml, mlw, gmat, o_na, o_gq, wml, wna, wgq, wo)


def _proj_weight(w):
    d = w.shape[0]
    o = 0
    seg = {}
    for name, width in (("ml_k", D_ML), ("ml_v", D_ML), ("ml_g", 4 * H_ML), ("na_k", D_NA), ("na_v", D_NA),
                        ("gq_k", D_KV), ("gq_v", D_KV), ("ml_q", D_ML), ("ml_o", D_ML), ("na_q", D_NA),
                        ("gq_q", D_GQ), ("br_g", 3 * d)):
        seg[name] = w[:, o:o + width]
        o += width
    gq_q = jnp.concatenate([seg["gq_q"][:, HEAD_DIM * h:HEAD_DIM * (h + 1)] for h in GQ_HEAD_ORDER], axis=1)
    pad = jnp.zeros((d, LANES - 4 * H_ML), w.dtype)
    out = jnp.concatenate([seg["ml_q"], seg["ml_k"], seg["ml_v"], seg["ml_o"], seg["ml_g"][:, ML_GATE_ORDER], pad,
                           seg["na_q"], seg["na_k"], seg["na_v"], gq_q, seg["gq_k"], seg["gq_v"], seg["br_g"]],
                          axis=1)
    return out.astype(BF16)


def _rope_tables(n_tok):
    t = jnp.arange(n_tok, dtype=jnp.int32)
    row = (t // GRID_W).astype(F32)
    col = (t % GRID_W).astype(F32)
    n_freq = HEAD_DIM // 4
    inv = ROPE_THETA ** (-jnp.arange(n_freq, dtype=F32) / n_freq)
    ang = jnp.concatenate([row[:, None] * inv, col[:, None] * inv], axis=-1)
    cos, sin = jnp.cos(ang), jnp.sin(ang)
    cos_t = jnp.tile(cos, (1, LANES // (HEAD_DIM // 2)))
    sin_t = jnp.tile(jnp.concatenate([-sin, sin], axis=-1), (1, LANES // HEAD_DIM))
    return cos_t, sin_t


def _na_bias_table(rpb):
    col = np.arange(GRID_W)
    first = np.clip(col - NA_WIN_C // 2, 0, GRID_W - NA_WIN_C)
    in_win = (col[None, :] >= first[:, None]) & (col[None, :] < first[:, None] + NA_WIN_C)
    side = GRID_W - NA_WIN_C
    rows = jnp.pad(rpb, ((0, 0), (1, 1), (side, side)))
    row_ok = np.zeros((2 * NA_WIN_R + 1,), bool)
    row_ok[1:-1] = True
    full = jnp.stack([rows[:, :, GRID_W - 1 - qc:2 * GRID_W - 1 - qc] for qc in range(GRID_W)], axis=2)
    full = full * float(np.log2(np.e))
    full = jnp.where(jnp.asarray(in_win[None, None] & row_ok[None, :, None, None]), full, NEG)
    return jnp.concatenate([full[:, :-1], full[:, 1:]], axis=-1).astype(F32)


def kernel(x, c, ctx, c_ctx, ada_w, ada_b, norm_w, ffn_w_in, ffn_w_out, mix_w_in, ml_gate_b, ml_norm_w,
           na_qk_w, na_rpb, gq_qk_w, w_br_ml, w_br_na, w_br_gq, w_out):
    b, s, d = x.shape
    l = ctx.shape[1]
    depth = ada_w.shape[0]
    assert b < MOD_ROWS and s % TM == 0 and (b * l) % min(TM, b * l) == 0
    ctx_row = b
    tiles_per_batch = s // TM

    cvec = jnp.zeros((MOD_ROWS, d), F32).at[:b].set(c).at[b].set(c_ctx)
    mod = _ada(cvec, ada_w, ada_b).reshape(depth, MOD_ROWS, N_MOD, d)

    lane = np.arange(LANES)
    gmat = jnp.asarray((lane[:, None] // HEAD_DIM) == (lane[None, :] // HEAD_DIM), BF16)
    idx = np.arange(ML_CHUNK)
    tri = jnp.asarray(np.stack([idx[:, None] >= idx[None, :], idx[:, None] <= idx[None, :]]), BF16)
    rope_tabs = _rope_tables(s)

    xl = x.reshape(b * s, d)
    xc = ctx.reshape(b * l, d)
    lat = dict(tiles_per_batch=tiles_per_batch, ctx_row=ctx_row)
    con = dict(tiles_per_batch=None, ctx_row=ctx_row)
    for li in range(depth):
        ctx_out = li < depth - 1
        w_in = [ffn_w_in[li, t].astype(BF16) for t in range(2)]
        w_o = [ffn_w_out[li, t].astype(BF16) for t in range(2)]
        w_proj = _proj_weight(mix_w_in[li])
        qkw = jnp.zeros((8, 4 * LANES), F32)
        qkw = qkw.at[0, :D_NA].set(jnp.tile(na_qk_w[li, 0], H_NA)).at[1, :D_NA].set(jnp.tile(na_qk_w[li, 1], H_NA))
        qkw = qkw.at[2, :D_GQ].set(jnp.tile(gq_qk_w[li, 0], H_GQ)).at[2, D_GQ:].set(jnp.tile(gq_qk_w[li, 1], H_KV))
        gate_b = jnp.zeros((1, LANES), F32).at[0, :4 * H_ML].set(ml_gate_b[li][ML_GATE_ORDER])
        mlw = ml_norm_w[li].reshape(1, D_ML)
        tz = _na_bias_table(na_rpb[li])
        wml, wna, wo = w_br_ml[li].astype(BF16), w_br_na[li].astype(BF16), w_out[li].astype(BF16)
        wgq = jnp.concatenate([w_br_gq[li, HEAD_DIM * h:HEAD_DIM * (h + 1)] for h in GQ_HEAD_ORDER],
                              axis=0).astype(BF16)
        m, nw = mod[li], norm_w[li]

        xl = _ffn(xl, m, nw, w_in[0], w_o[0], k0=0, nrm=0, **lat)
        xc = _ffn(xc, m, nw, w_in[0], w_o[0], k0=0, nrm=0, **con)

        p_ml, p_mlg, p_na, p_gq, p_brg = _proj(xl, m, nw, w_proj, gmat, qkw, rope_tabs, **lat)
        pc_ml, pc_mlg, pc_na, pc_gq, pc_brg = _proj(xc, m, nw, w_proj, gmat, qkw, None, **con)
        seq = lambda a: a.reshape(b, s, a.shape[-1])
        cseq = lambda a: a.reshape(b, l, a.shape[-1])

        hf, hb, hcf, hcb = _mlstm(seq(p_ml), seq(p_mlg), cseq(pc_ml), cseq(pc_mlg), tri, gate_b)
        o_na = _na(seq(p_na), cseq(pc_na), tz)
        o_gq = _gqa(seq(p_gq), cseq(pc_gq))
        flat = lambda a: a.reshape(-1, a.shape[-1])
        xl = _merge(xl, m, p_brg, flat(hf), flat(hb), p_ml, mlw, gmat, flat(o_na), flat(o_gq),
                    wml, wna, wgq, wo, **lat)
        xl = _ffn(xl, m, nw, w_in[1], w_o[1], k0=6, nrm=2, **lat)
        if ctx_out:
            co_na = _ctx_attn(cseq(pc_na), qw=D_NA, kw=D_NA, shared_kv=False)
            co_gq = _ctx_attn(cseq(pc_gq), qw=D_GQ, kw=D_KV, shared_kv=True)
            xc = _merge(xc, m, pc_brg, flat(hcf), flat(hcb), pc_ml, mlw, gmat, flat(co_na), flat(co_gq),
                        wml, wna, wgq, wo, **con)
            xc = _ffn(xc, m, nw, w_in[1], w_o[1], k0=6, nrm=2, **con)
    return xl.reshape(b, s, d)
```

```python
import functools

import numpy as np
import jax
import jax.numpy as jnp
from jax import lax
from jax.experimental import pallas as pl
from jax.experimental.pallas import tpu as pltpu

F32 = jnp.float32
BF16 = jnp.bfloat16

HEAD_DIM = 64
LANES = 128
H_ML, H_NA, H_GQ, H_KV = 4, 6, 6, 2
D_ML, D_NA, D_GQ, D_KV = 256, 384, 384, 128
GRID_W = 64
NA_WIN_R, NA_WIN_C = 8, 16
ROPE_THETA = 10000.0
EPS = 1e-6
N_MOD = 9
ATTN_SCALE = HEAD_DIM ** -0.5
NEG = -1e30

ML_CHUNK = 256
NA_QROWS = 8
NA_BAND = 16
TM = 512
GQ_TQ = 256
FFN_CHUNK = 768
Q_PRESCALE = ATTN_SCALE * float(np.log2(np.e))
MOD_ROWS = 16
VMEM_LIMIT = 56 * 1024 * 1024

C_ML, C_MLG, C_NA, C_GQ, C_BRG = 0, 1024, 1152, 2304, 2944
N_PROJ = 6016
GQ_HEAD_ORDER = (0, 3, 1, 4, 2, 5)
ML_GATE_ORDER = np.array([0, 1, 2, 3, 8, 9, 10, 11, 4, 5, 6, 7, 12, 13, 14, 15])


def _dot(a, b):
    return jnp.dot(a, b, preferred_element_type=F32)


def _dot_nt(a, b):
    return lax.dot_general(a, b, (((1,), (1,)), ((), ())), preferred_element_type=F32)


def _sigmoid(x):
    return 1.0 / (1.0 + jnp.exp(-x))


def _log_sigmoid(x):
    return jnp.minimum(x, 0.0) - jnp.log1p(jnp.exp(-jnp.abs(x)))


def _split_bf16(x):
    hi = x.astype(BF16)
    lo = (x - hi.astype(F32)).astype(BF16)
    return hi, lo


def _modnorm(x, nw, shift, scale):
    ms = jnp.mean(x * x, axis=-1, keepdims=True)
    return (x * lax.rsqrt(ms + EPS) * nw) * (1.0 + scale) + shift


def _head_norm(t, gmat, wrow):
    ss = _dot((t * t).astype(BF16), gmat)
    return t * lax.rsqrt(ss * (1.0 / HEAD_DIM) + EPS) * wrow


def _resident(shape, lead=()):
    return pl.BlockSpec((None,) * len(lead) + tuple(shape), lambda *_: tuple(lead) + (0,) * len(shape),
                        pipeline_mode=pl.Buffered(1))


def _params(sem):
    return pltpu.CompilerParams(dimension_semantics=sem, vmem_limit_bytes=VMEM_LIMIT)


def _ada_kernel(c_ref, w_ref, b_ref, o_ref):
    c = c_ref[...]
    s = (c * _sigmoid(c)).astype(BF16)
    o_ref[0] = _dot(s, w_ref[0].astype(BF16)) + b_ref[0]


def _ada(cvec, ada_w, ada_b):
    depth, d, n = ada_w.shape
    tn = n // 8
    return pl.pallas_call(
        _ada_kernel,
        grid=(depth, n // tn),
        in_specs=[pl.BlockSpec((MOD_ROWS, d), lambda l, j: (0, 0)),
                  pl.BlockSpec((1, d, tn), lambda l, j: (l, 0, j)),
                  pl.BlockSpec((1, 1, tn), lambda l, j: (l, 0, j))],
        out_specs=pl.BlockSpec((1, MOD_ROWS, tn), lambda l, j: (l, 0, j)),
        out_shape=jax.ShapeDtypeStruct((depth, MOD_ROWS, n), F32),
        compiler_params=_params(("arbitrary", "arbitrary")),
        name="ada_mod",
    )(cvec, ada_w, ada_b.reshape(depth, 1, n))


def _mod_index(tiles_per_batch, ctx_row):
    if tiles_per_batch is None:
        return lambda i, *_: (ctx_row, 0, 0)
    return lambda i, *_: (i // tiles_per_batch, 0, 0)


def _ffn_kernel(x_ref, mod_ref, nw_ref, wi_ref, wo_ref, o_ref, *, k0, nrm, chunks):
    x = x_ref[...]
    h = _modnorm(x, nw_ref[nrm:nrm + 1, :], mod_ref[0, k0:k0 + 1, :], mod_ref[0, k0 + 1:k0 + 2, :]).astype(BF16)
    dff = wo_ref.shape[0]
    y = None
    for c0, c1 in chunks:
        g = _dot(h, wi_ref[:, c0:c1])
        u = _dot(h, wi_ref[:, dff + c0:dff + c1])
        part = _dot((g * _sigmoid(g) * u).astype(BF16), wo_ref[c0:c1, :])
        y = part if y is None else y + part
    o_ref[...] = x + (0.5 * mod_ref[0, k0 + 2:k0 + 3, :]) * y


def _ffn(x, mod, nw, w_in, w_out, lead, *, k0, nrm, tiles_per_batch, ctx_row):
    t, d = x.shape
    dff = w_out.shape[-2]
    tm = min(TM, t)
    edges = list(range(0, dff, FFN_CHUNK)) + [dff]
    chunks = tuple(zip(edges[:-1], edges[1:]))
    kern = functools.partial(_ffn_kernel, k0=k0, nrm=nrm, chunks=chunks)
    return pl.pallas_call(
        kern,
        grid=(t // tm,),
        in_specs=[pl.BlockSpec((tm, d), lambda i: (i, 0)),
                  pl.BlockSpec((1, N_MOD, d), _mod_index(tiles_per_batch, ctx_row)),
                  pl.BlockSpec((3, d), lambda i: (0, 0)),
                  _resident((d, 2 * dff), lead),
                  _resident((dff, d), lead)],
        out_specs=pl.BlockSpec((tm, d), lambda i: (i, 0)),
        out_shape=jax.ShapeDtypeStruct((t, d), F32),
        compiler_params=_params(("parallel",)),
        name="ffn_swiglu",
    )(x, mod, nw, w_in, w_out)


def _proj_kernel(*refs, rope):
    if rope:
        (x_ref, mod_ref, nw_ref, w_ref, g_ref, qkw_ref, cos_ref, sin_ref,
         ml_ref, mlg_ref, na_ref, gq_ref, brg_ref) = refs
    else:
        (x_ref, mod_ref, nw_ref, w_ref, g_ref, qkw_ref,
         ml_ref, mlg_ref, na_ref, gq_ref, brg_ref) = refs
    h = _modnorm(x_ref[...], nw_ref[1:2, :], mod_ref[0, 3:4, :], mod_ref[0, 4:5, :]).astype(BF16)
    gmat = g_ref[...]

    ml_ref[...] = _dot(h, w_ref[:, C_ML:C_MLG]).astype(BF16)
    mlg_ref[...] = _dot(h, w_ref[:, C_MLG:C_NA])

    na = _dot(h, w_ref[:, C_NA:C_GQ])
    for j in range(3):
        sl = slice(2 * LANES * j, 2 * LANES * (j + 1))
        na_ref[:, sl] = _head_norm(na[:, sl], gmat, qkw_ref[0:1, sl]).astype(BF16)
    na_ref[:, 2 * D_NA:] = na[:, 2 * D_NA:].astype(BF16)

    gq = _dot(h, w_ref[:, C_GQ:C_BRG])
    if rope:
        lane = lax.broadcasted_iota(jnp.int32, (1, LANES), 1)
        first_half = (lane % HEAD_DIM) < (HEAD_DIM // 2)
        cos = cos_ref[...]
        sin = sin_ref[...]
    for j in range(2):
        t2 = _head_norm(gq[:, 2 * LANES * j:2 * LANES * (j + 1)], gmat, qkw_ref[1:2, 2 * LANES * j:2 * LANES * (j + 1)])
        for half in range(2):
            t = t2[:, LANES * half:LANES * (half + 1)]
            if rope:
                rot = jnp.where(first_half, pltpu.roll(t, LANES - HEAD_DIM // 2, axis=1),
                                pltpu.roll(t, HEAD_DIM // 2, axis=1))
                t = t * cos + rot * sin
            gq_ref[:, LANES * (2 * j + half):LANES * (2 * j + half + 1)] = t.astype(BF16)
    gq_ref[:, D_GQ + D_KV:] = gq[:, D_GQ + D_KV:].astype(BF16)

    for j in range(3):
        sl = slice(1024 * j, 1024 * (j + 1))
        brg_ref[:, sl] = _sigmoid(_dot(h, w_ref[:, C_BRG + 1024 * j:C_BRG + 1024 * (j + 1)])).astype(BF16)


def _proj(x, mod, nw, w, li, gmat, qkw, rope_tabs, *, tiles_per_batch, ctx_row):
    t, d = x.shape
    tm = min(TM, t)
    rope = rope_tabs is not None
    in_specs = [pl.BlockSpec((tm, d), lambda i: (i, 0)),
                pl.BlockSpec((1, N_MOD, d), _mod_index(tiles_per_batch, ctx_row)),
                pl.BlockSpec((3, d), lambda i: (0, 0)),
                _resident((d, N_PROJ), (li,)),
                _resident(gmat.shape),
                _resident(qkw.shape)]
    args = [x, mod, nw, w, gmat, qkw]
    if rope:
        in_specs += [pl.BlockSpec((tm, LANES), lambda i: (i % tiles_per_batch, 0))] * 2
        args += list(rope_tabs)
    widths = (1024, LANES, 3 * D_NA, D_GQ + 2 * D_KV, 3072)
    dtypes = (BF16, F32, BF16, BF16, BF16)
    return pl.pallas_call(
        functools.partial(_proj_kernel, rope=rope),
        grid=(t // tm,),
        in_specs=in_specs,
        out_specs=[pl.BlockSpec((tm, wd), lambda i: (i, 0)) for wd in widths],
        out_shape=[jax.ShapeDtypeStruct((t, wd), dt) for wd, dt in zip(widths, dtypes)],
        compiler_params=_params(("parallel",)),
        name="mix_in_proj",
    )(*args)


def _ml_prep(d, g_ref, tri_ref, bias_ref):
    log2e = float(np.log2(np.e))
    lc = g_ref.shape[1]
    gates = g_ref[0] + bias_ref[...]
    gates_t = gates.T
    ig_t = gates_t[0:8] * log2e
    hi, lo = _split_bf16(_log_sigmoid(gates_t[0:16]) * log2e)
    ones = jnp.ones((lc, lc), BF16)
    b_t = (_dot(hi, tri_ref[1 - d]) + _dot(lo, tri_ref[1 - d]))[8:16]
    btot_t = (_dot(hi, ones) + _dot(lo, ones))[8:16]
    lf_al = pltpu.roll(_log_sigmoid(gates) * log2e, LANES - 8, axis=1)
    hi, lo = _split_bf16(lf_al)
    c_mat = gates * log2e - (_dot(tri_ref[d], hi) + _dot(tri_ref[d], lo))
    return ig_t, b_t, btot_t, c_mat


def _ml_head(d, h, prep, m_ref):
    ig_t, b_t, btot_t, c_mat = prep
    lc = c_mat.shape[0]
    r = 4 * d + h
    row = lax.broadcasted_iota(jnp.int32, (lc, lc), 0)
    col = lax.broadcasted_iota(jnp.int32, (lc, lc), 1)
    visible = (row <= col) if d == 0 else (row >= col)
    ig, b, b_tot = ig_t[r:r + 1], b_t[r:r + 1], btot_t[r:r + 1]
    m_prev = m_ref[r:r + 1, :]
    w_end = b_tot - b + ig
    m_new = jnp.maximum(b_tot + m_prev, jnp.max(w_end, axis=1, keepdims=True))
    a = jnp.exp2(w_end - m_new)
    decay = jnp.exp2(b_tot + m_prev - m_new)
    m_inter = b + m_prev
    logw = jnp.where(visible, c_mat[:, r:r + 1] + b, NEG)
    m_j = jnp.maximum(m_inter, jnp.max(logw, axis=0, keepdims=True))
    w = jnp.exp2(logw - m_j)
    m_ref[r:r + 1, :] = m_new
    return w, a, jnp.exp2(m_inter - m_j), jnp.exp2(-m_j), decay


def _ml_pair(d, p, head_a, head_b, q_ref, k_ref, v_ref, o_ref, st_ref):
    (w_a, a_a, g_a, fl_a, dec_a), (w_b, a_b, g_b, fl_b, dec_b) = head_a, head_b
    lc = q_ref.shape[1]
    low = _low_lanes()
    sl = slice(LANES * p, LANES * (p + 1))
    q = q_ref[0, :, sl]
    k = k_ref[0, :, sl] * ATTN_SCALE
    vt = jnp.concatenate([v_ref[0, :, sl].astype(F32).T, jnp.ones((LANES, lc), F32)], axis=0)
    head_row = (lax.broadcasted_iota(jnp.int32, (2 * LANES, 1), 0) % LANES) < HEAD_DIM
    vt16 = vt.astype(BF16)
    r_a = _dot(vt16, (_dot_nt(k, _one_head(q, low, 0)) * w_a).astype(BF16))
    r_b = _dot(vt16, (_dot_nt(k, _one_head(q, low, 1)) * w_b).astype(BF16))
    state = st_ref[d, p]
    r_i = _dot_nt(state.astype(BF16), q)
    r = jnp.where(head_row, r_a, r_b) + jnp.where(head_row, g_a, g_b) * r_i
    num, den = r[:LANES], r[LANES:]
    h_t = num / jnp.maximum(jnp.abs(den), jnp.where(head_row[:LANES], fl_a, fl_b))
    o_ref[0, :, sl] = h_t.T.astype(BF16)

    upd = _dot((vt * jnp.where(head_row, a_a, a_b)).astype(BF16), k)
    same_head = head_row == low
    dec = jnp.where(head_row, dec_a[:, :LANES], dec_b[:, :LANES])
    st_ref[d, p] = dec * state + jnp.where(same_head, upd, 0.0)


def _ml_step(fwd, bwd, tri_ref, bias_ref, st_ref, m_ref):
    dirs = (fwd, bwd)
    preps = [_ml_prep(d, refs[3], tri_ref, bias_ref) for d, refs in enumerate(dirs)]
    heads = [[_ml_head(d, h, preps[d], m_ref) for h in range(H_ML)] for d in range(2)]
    for p in range(H_ML // 2):
        for d, (q_ref, k_ref, v_ref, _, o_ref) in enumerate(dirs):
            _ml_pair(d, p, heads[d][2 * p], heads[d][2 * p + 1], q_ref, k_ref, v_ref, o_ref, st_ref)


def _mlstm_kernel(qf, kf, vf, gf, qb, kb, vb, gb, qc, kc, vc, gc, tri_ref, bias_ref,
                  hf_ref, hb_ref, hcf_ref, hcb_ref, st_ref, m_ref):
    c = pl.program_id(1)

    @pl.when(c == 0)
    def _():
        st_ref[...] = jnp.zeros_like(st_ref)
        m_ref[...] = jnp.zeros_like(m_ref)
        _ml_step((qc, kc, vc, gc, hcf_ref), (qc, kc, vc, gc, hcb_ref), tri_ref, bias_ref, st_ref, m_ref)

    @pl.when(c > 0)
    def _():
        _ml_step((qf, kf, vf, gf, hf_ref), (qb, kb, vb, gb, hb_ref), tri_ref, bias_ref, st_ref, m_ref)


def _mlstm(p_ml, p_mlg, pc_ml, pc_mlg, tri, bias):
    b, s, _ = p_ml.shape
    l = pc_ml.shape[1]
    lc = ML_CHUNK
    assert l == lc and s % lc == 0
    nl = s // lc
    fwd = lambda c: jnp.maximum(c - 1, 0)
    bwd = lambda c: nl - 1 - jnp.maximum(c - 1, 0)

    def lat(idx, blk, width):
        return pl.BlockSpec((1, lc, width), lambda i, c: (i, idx(c), blk))

    def ctx(blk, width):
        return pl.BlockSpec((1, lc, width), lambda i, c: (i, 0, blk))

    in_specs = ([lat(fwd, 0, D_ML), lat(fwd, 1, D_ML), lat(fwd, 2, D_ML), lat(fwd, 0, LANES)]
                + [lat(bwd, 0, D_ML), lat(bwd, 1, D_ML), lat(bwd, 2, D_ML), lat(bwd, 0, LANES)]
                + [ctx(0, D_ML), ctx(1, D_ML), ctx(2, D_ML), ctx(0, LANES)]
                + [_resident((2, lc, lc)), _resident((1, LANES))])
    out_specs = [lat(fwd, 0, D_ML), lat(bwd, 0, D_ML), ctx(0, D_ML), ctx(0, D_ML)]
    out_shape = [jax.ShapeDtypeStruct((b, s, D_ML), BF16)] * 2 + [jax.ShapeDtypeStruct((b, l, D_ML), BF16)] * 2
    return pl.pallas_call(
        _mlstm_kernel,
        grid=(b, nl + 1),
        in_specs=in_specs,
        out_specs=out_specs,
        out_shape=out_shape,
        scratch_shapes=[pltpu.VMEM((2, H_ML // 2, 2 * LANES, LANES), F32), pltpu.VMEM((2 * H_ML, lc), F32)],
        compiler_params=_params(("parallel", "arbitrary")),
        name="mlstm_bidir",
    )(p_ml, p_ml, p_ml, p_mlg, p_ml, p_ml, p_ml, p_mlg, pc_ml, pc_ml, pc_ml, pc_mlg, tri, bias)


def _attend_heads(n_heads, score_fn, value_fn, s_scr, p_scr):
    s_scr[0] = score_fn(0)
    outs = []
    for i in range(n_heads):
        if i + 1 < n_heads:
            s_scr[(i + 1) % 2] = score_fn(i + 1)
        s = s_scr[i % 2]
        p_scr[i % 2] = jnp.exp2(s - jnp.max(s, axis=1, keepdims=True)).astype(BF16)
        r = _dot(p_scr[i % 2], value_fn(i))
        outs.append(r[:, :LANES] / r[:, LANES:])
    return outs


def _low_lanes():
    return lax.broadcasted_iota(jnp.int32, (1, LANES), 1) < HEAD_DIM


def _one_head(q, low, half):
    zero = jnp.zeros_like(q)
    return jnp.where(low, q, zero) if half == 0 else jnp.where(low, zero, q)


def _pair_outputs(o_ref, outs, low):
    for j in range(len(outs) // 2):
        o_ref[0, :, LANES * j:LANES * (j + 1)] = jnp.where(low, outs[2 * j], outs[2 * j + 1]).astype(BF16)


def _gqa_kernel(q_ref, k_ref, v_ref, kc_ref, vc_ref, o_ref, kall, vall, s_scr, p_scr):
    n_lat = k_ref.shape[1]

    @pl.when(pl.program_id(1) == 0)
    def _():
        kall[0:n_lat, :] = k_ref[0]
        kall[n_lat:, :] = kc_ref[0]
        vall[0:n_lat, 0:LANES] = v_ref[0]
        vall[n_lat:, 0:LANES] = vc_ref[0]
        vall[:, LANES:] = jnp.ones((vall.shape[0], LANES), BF16)

    low = _low_lanes()

    def score(i):
        j, half = divmod(i, 2)
        return _dot_nt(_one_head(q_ref[0, :, LANES * j:LANES * (j + 1)], low, half), kall[...])

    _pair_outputs(o_ref, _attend_heads(H_GQ, score, lambda i: vall[...], s_scr, p_scr), low)


def _gqa(p_gq, pc_gq):
    b, s, _ = p_gq.shape
    l = pc_gq.shape[1]
    tq = min(GQ_TQ, s)
    kblk, vblk = D_GQ // LANES, D_GQ // LANES + 1
    return pl.pallas_call(
        _gqa_kernel,
        grid=(b, s // tq),
        in_specs=[pl.BlockSpec((1, tq, D_GQ), lambda i, t: (i, t, 0)),
                  pl.BlockSpec((1, s, LANES), lambda i, t: (i, 0, kblk)),
                  pl.BlockSpec((1, s, LANES), lambda i, t: (i, 0, vblk)),
                  pl.BlockSpec((1, l, LANES), lambda i, t: (i, 0, kblk)),
                  pl.BlockSpec((1, l, LANES), lambda i, t: (i, 0, vblk))],
        out_specs=pl.BlockSpec((1, tq, D_GQ), lambda i, t: (i, t, 0)),
        out_shape=jax.ShapeDtypeStruct((b, s, D_GQ), BF16),
        scratch_shapes=[pltpu.VMEM((s + l, LANES), BF16), pltpu.VMEM((s + l, 2 * LANES), BF16),
                        pltpu.VMEM((2, tq, s + l), F32), pltpu.VMEM((2, tq, s + l), BF16)],
        compiler_params=_params(("parallel", "arbitrary")),
        name="gqa_latent",
    )(p_gq, p_gq, p_gq, pc_gq, pc_gq)


def _ctx_attn_kernel(q_ref, k_ref, v_ref, o_ref, s_scr, p_scr, *, shared_kv):
    low = _low_lanes()
    ones = jnp.ones((v_ref.shape[1], LANES), BF16)

    def kv_lanes(i):
        return slice(0, LANES) if shared_kv else slice(LANES * (i // 2), LANES * (i // 2 + 1))

    def score(i):
        j, half = divmod(i, 2)
        return _dot_nt(_one_head(q_ref[0, :, LANES * j:LANES * (j + 1)], low, half), k_ref[0, :, kv_lanes(i)])

    def value(i):
        return jnp.concatenate([v_ref[0, :, kv_lanes(i)], ones], axis=1)

    n_heads = 2 * (q_ref.shape[2] // LANES)
    _pair_outputs(o_ref, _attend_heads(n_heads, score, value, s_scr, p_scr), low)


def _ctx_attn(pc, *, qw, kw, shared_kv):
    b, l, _ = pc.shape
    kb = qw // kw
    return pl.pallas_call(
        functools.partial(_ctx_attn_kernel, shared_kv=shared_kv),
        grid=(b,),
        in_specs=[pl.BlockSpec((1, l, qw), lambda i: (i, 0, 0)),
                  pl.BlockSpec((1, l, kw), lambda i: (i, 0, kb)),
                  pl.BlockSpec((1, l, kw), lambda i: (i, 0, kb + 1))],
        out_specs=pl.BlockSpec((1, l, qw), lambda i: (i, 0, 0)),
        out_shape=jax.ShapeDtypeStruct((b, l, qw), BF16),
        scratch_shapes=[pltpu.VMEM((2, l, l), F32), pltpu.VMEM((2, l, l), BF16)],
        compiler_params=_params(("parallel",)),
        name="ctx_attn",
    )(pc, pc, pc)


def _na_kernel(q_ref, k0, k1, k2, k3, v0, v1, v2, v3, kc_ref, vc_ref, tz_ref, o_ref,
               kall, vall, s_scr, p_scr, *, n_rows):
    i = pl.program_id(1)
    r0 = i * NA_QROWS
    start = jnp.clip(r0 - NA_WIN_R // 2, 0, n_rows - NA_BAND)
    delta = start - r0
    nq, nk = NA_QROWS * GRID_W, NA_BAND * GRID_W
    sub = nk // 4
    n_ctx = kc_ref.shape[1]

    for t, (kr, vr) in enumerate(zip((k0, k1, k2, k3, kc_ref), (v0, v1, v2, v3, vc_ref))):
        rows = slice(sub * t, sub * t + kr.shape[1])
        kall[rows, :] = kr[0]
        for j in range(D_NA // LANES):
            vall[j, rows, 0:LANES] = vr[0, :, LANES * j:LANES * (j + 1)]
    vall[:, :, LANES:] = jnp.ones((D_NA // LANES, nk + n_ctx, LANES), BF16)

    qrow = r0 + lax.broadcasted_iota(jnp.int32, (nq, nk), 0) // GRID_W
    krow = start + lax.broadcasted_iota(jnp.int32, (nq, nk), 1) // GRID_W
    first = jnp.clip(qrow - NA_WIN_R // 2, 0, n_rows - NA_WIN_R)
    row_mask = jnp.where((krow >= first) & (krow < first + NA_WIN_R), 0.0, NEG)
    low = _low_lanes()
    no_bias = jnp.zeros((nq, n_ctx), F32)

    def score(h):
        j, half = divmod(h, 2)
        sl = slice(LANES * j, LANES * (j + 1))
        slabs = []
        for a in range(NA_QROWS):
            pieces = [tz_ref[h, jnp.clip(delta + 2 * bp - a, -8, 7) + 8] for bp in range(NA_BAND // 2)]
            slabs.append(jnp.concatenate(pieces, axis=1))
        bias = jnp.concatenate([jnp.concatenate(slabs, axis=0) + row_mask, no_bias], axis=1)
        return _dot_nt(_one_head(q_ref[0, :, sl], low, half), kall[:, sl]) + bias

    _pair_outputs(o_ref, _attend_heads(H_NA, score, lambda h: vall[h // 2], s_scr, p_scr), low)


def _na(p_na, pc_na, tz):
    b, s, _ = p_na.shape
    l = pc_na.shape[1]
    n_rows = s // GRID_W
    assert n_rows % NA_QROWS == 0 and n_rows >= NA_BAND
    nq, nk = NA_QROWS * GRID_W, NA_BAND * GRID_W
    sub = nk // 4
    rows_per_sub = NA_BAND // 4

    def band(t, blk):
        def idx(i, r):
            start = jnp.clip(r * NA_QROWS - NA_WIN_R // 2, 0, n_rows - NA_BAND)
            return (i, start // rows_per_sub + t, blk)
        return pl.BlockSpec((1, sub, D_NA), idx)

    in_specs = ([pl.BlockSpec((1, nq, D_NA), lambda i, r: (i, r, 0))]
                + [band(t, 1) for t in range(4)] + [band(t, 2) for t in range(4)]
                + [pl.BlockSpec((1, l, D_NA), lambda i, r: (i, 0, 1)),
                   pl.BlockSpec((1, l, D_NA), lambda i, r: (i, 0, 2)),
                   _resident(tz.shape)])
    return pl.pallas_call(
        functools.partial(_na_kernel, n_rows=n_rows),
        grid=(b, n_rows // NA_QROWS),
        in_specs=in_specs,
        out_specs=pl.BlockSpec((1, nq, D_NA), lambda i, r: (i, r, 0)),
        out_shape=jax.ShapeDtypeStruct((b, s, D_NA), BF16),
        scratch_shapes=[pltpu.VMEM((nk + l, D_NA), BF16), pltpu.VMEM((D_NA // LANES, nk + l, 2 * LANES), BF16),
                        pltpu.VMEM((2, nq, nk + l), F32), pltpu.VMEM((2, nq, nk + l), BF16)],
        compiler_params=_params(("parallel", "arbitrary")),
        name="na_latent",
    )(p_na, *([p_na] * 8), pc_na, pc_na, tz)


def _merge_kernel(x_ref, mod_ref, brg_ref, hf_ref, hb_ref, og_ref, mlw_ref, g_ref, na_ref, gq_ref,
                  wml_ref, wna_ref, wgq_ref, wo_ref, o_ref):
    gmat = g_ref[...]
    h = hf_ref[...].astype(F32) + hb_ref[...].astype(F32)
    o_ml = (_head_norm(h, gmat, mlw_ref[...]) * _sigmoid(og_ref[...].astype(F32))).astype(BF16)
    d = x_ref.shape[1]
    y = brg_ref[:, 0:d].astype(F32) * _dot(o_ml, wml_ref[...])
    y = y + brg_ref[:, d:2 * d].astype(F32) * _dot(na_ref[...], wna_ref[...])
    y = y + brg_ref[:, 2 * d:3 * d].astype(F32) * _dot(gq_ref[...], wgq_ref[...])
    o_ref[...] = x_ref[...] + mod_ref[0, 5:6, :] * _dot(y.astype(BF16), wo_ref[...])


def _merge(x, mod, brg, hf, hb, p_ml, mlw, gmat, o_na, o_gq, wml, wna, wgq, wo, li,
           *, tiles_per_batch, ctx_row):
    t, d = x.shape
    tm = min(TM, t)
    row = lambda wd, blk=0: pl.BlockSpec((tm, wd), lambda i: (i, blk))
    lead = (li,)
    return pl.pallas_call(
        _merge_kernel,
        grid=(t // tm,),
        in_specs=[row(d),
                  pl.BlockSpec((1, N_MOD, d), _mod_index(tiles_per_batch, ctx_row)),
                  row(3 * d), row(D_ML), row(D_ML), row(D_ML, 3),
                  _resident((1, D_ML)), _resident(gmat.shape),
                  row(D_NA), row(D_GQ),
                  _resident((D_ML, d), lead), _resident((D_NA, d), lead), _resident((D_GQ, d), lead),
                  _resident((d, d), lead)],
        out_specs=row(d),
        out_shape=jax.ShapeDtypeStruct((t, d), F32),
        compiler_params=_params(("parallel",)),
        name="branch_merge",
    )(x, mod, brg, hf, hb, p_ml, mlw, gmat, o_na, o_gq, wml, wna, wgq, wo)


def _proj_weight(w):
    d = w.shape[-2]
    o = 0
    seg = {}
    for name, width in (("ml_k", D_ML), ("ml_v", D_ML), ("ml_g", 4 * H_ML), ("na_k", D_NA), ("na_v", D_NA),
                        ("gq_k", D_KV), ("gq_v", D_KV), ("ml_q", D_ML), ("ml_o", D_ML), ("na_q", D_NA),
                        ("gq_q", D_GQ), ("br_g", 3 * d)):
        seg[name] = w[..., o:o + width]
        o += width
    gq_q = jnp.concatenate([seg["gq_q"][..., HEAD_DIM * h:HEAD_DIM * (h + 1)] for h in GQ_HEAD_ORDER], axis=-1)
    pad = jnp.zeros(w.shape[:-1] + (LANES - 4 * H_ML,), w.dtype)
    out = jnp.concatenate([seg["ml_q"], seg["ml_k"], seg["ml_v"], seg["ml_o"], seg["ml_g"][..., ML_GATE_ORDER], pad,
                           seg["na_q"], seg["na_k"], seg["na_v"], gq_q, seg["gq_k"], seg["gq_v"], seg["br_g"]],
                          axis=-1)
    return out


def _rope_tables(n_tok):
    t = jnp.arange(n_tok, dtype=jnp.int32)
    row = (t // GRID_W).astype(F32)
    col = (t % GRID_W).astype(F32)
    n_freq = HEAD_DIM // 4
    inv = ROPE_THETA ** (-jnp.arange(n_freq, dtype=F32) / n_freq)
    ang = jnp.concatenate([row[:, None] * inv, col[:, None] * inv], axis=-1)
    cos, sin = jnp.cos(ang), jnp.sin(ang)
    cos_t = jnp.tile(cos, (1, LANES // (HEAD_DIM // 2)))
    sin_t = jnp.tile(jnp.concatenate([-sin, sin], axis=-1), (1, LANES // HEAD_DIM))
    return cos_t, sin_t


def _na_bias_table(rpb):
    col = np.arange(GRID_W)
    first = np.clip(col - NA_WIN_C // 2, 0, GRID_W - NA_WIN_C)
    in_win = (col[None, :] >= first[:, None]) & (col[None, :] < first[:, None] + NA_WIN_C)
    side = GRID_W - NA_WIN_C
    rows = jnp.pad(rpb, ((0, 0), (1, 1), (side, side)))
    row_ok = np.zeros((2 * NA_WIN_R + 1,), bool)
    row_ok[1:-1] = True
    full = jnp.stack([rows[:, :, GRID_W - 1 - qc:2 * GRID_W - 1 - qc] for qc in range(GRID_W)], axis=2)
    full = full * float(np.log2(np.e))
    full = jnp.where(jnp.asarray(in_win[None, None] & row_ok[None, :, None, None]), full, NEG)
    return jnp.concatenate([full[:, :-1], full[:, 1:]], axis=-1).astype(F32)


def kernel(x, c, ctx, c_ctx, ada_w, ada_b, norm_w, ffn_w_in, ffn_w_out, mix_w_in, ml_gate_b, ml_norm_w,
           na_qk_w, na_rpb, gq_qk_w, w_br_ml, w_br_na, w_br_gq, w_out):
    b, s, d = x.shape
    l = ctx.shape[1]
    depth = ada_w.shape[0]
    assert b < MOD_ROWS and s % TM == 0 and (b * l) % min(TM, b * l) == 0
    ctx_row = b
    tiles_per_batch = s // TM

    cvec = jnp.zeros((MOD_ROWS, d), F32).at[:b].set(c).at[b].set(c_ctx)
    mod = _ada(cvec, ada_w, ada_b).reshape(depth, MOD_ROWS, N_MOD, d)

    lane = np.arange(2 * LANES)
    gmat = jnp.asarray((lane[:, None] // HEAD_DIM) == (lane[None, :] // HEAD_DIM), BF16)
    idx = np.arange(ML_CHUNK)
    tri = jnp.asarray(np.stack([idx[:, None] >= idx[None, :], idx[:, None] <= idx[None, :]]), BF16)
    rope_tabs = _rope_tables(s)

    w_in, w_o = ffn_w_in.astype(BF16), ffn_w_out.astype(BF16)
    w_proj = _proj_weight(mix_w_in.astype(BF16))
    wml, wna, wo = w_br_ml.astype(BF16), w_br_na.astype(BF16), w_out.astype(BF16)
    wgq = jnp.concatenate([w_br_gq[:, HEAD_DIM * h:HEAD_DIM * (h + 1)] for h in GQ_HEAD_ORDER], axis=1).astype(BF16)

    xl = x.reshape(b * s, d)
    xc = ctx.reshape(b * l, d)
    lat = dict(tiles_per_batch=tiles_per_batch, ctx_row=ctx_row)
    con = dict(tiles_per_batch=None, ctx_row=ctx_row)
    for li in range(depth):
        ctx_out = li < depth - 1
        qkw = jnp.zeros((8, 2 * D_NA), F32)
        qkw = qkw.at[0, :D_NA].set(jnp.tile(na_qk_w[li, 0], H_NA) * Q_PRESCALE)
        qkw = qkw.at[0, D_NA:].set(jnp.tile(na_qk_w[li, 1], H_NA))
        qkw = qkw.at[1, :D_GQ].set(jnp.tile(gq_qk_w[li, 0], H_GQ) * Q_PRESCALE)
        qkw = qkw.at[1, D_GQ:D_GQ + D_KV].set(jnp.tile(gq_qk_w[li, 1], H_KV))
        gate_b = jnp.zeros((1, LANES), F32).at[0, :4 * H_ML].set(ml_gate_b[li][ML_GATE_ORDER])
        mlw = ml_norm_w[li].reshape(1, D_ML)
        tz = _na_bias_table(na_rpb[li])
        m, nw = mod[li], norm_w[li]

        xl = _ffn(xl, m, nw, w_in, w_o, (li, 0), k0=0, nrm=0, **lat)
        xc = _ffn(xc, m, nw, w_in, w_o, (li, 0), k0=0, nrm=0, **con)

        p_ml, p_mlg, p_na, p_gq, p_brg = _proj(xl, m, nw, w_proj, li, gmat, qkw, rope_tabs, **lat)
        pc_ml, pc_mlg, pc_na, pc_gq, pc_brg = _proj(xc, m, nw, w_proj, li, gmat, qkw, None, **con)
        seq = lambda a: a.reshape(b, s, a.shape[-1])
        cseq = lambda a: a.reshape(b, l, a.shape[-1])

        hf, hb, hcf, hcb = _mlstm(seq(p_ml), seq(p_mlg), cseq(pc_ml), cseq(pc_mlg), tri, gate_b)
        o_na = _na(seq(p_na), cseq(pc_na), tz)
        o_gq = _gqa(seq(p_gq), cseq(pc_gq))
        flat = lambda a: a.reshape(-1, a.shape[-1])
        xl = _merge(xl, m, p_brg, flat(hf), flat(hb), p_ml, mlw, gmat, flat(o_na), flat(o_gq),
                    wml, wna, wgq, wo, li, **lat)
        xl = _ffn(xl, m, nw, w_in, w_o, (li, 1), k0=6, nrm=2, **lat)
        if ctx_out:
            co_na = _ctx_attn(cseq(pc_na), qw=D_NA, kw=D_NA, shared_kv=False)
            co_gq = _ctx_attn(cseq(pc_gq), qw=D_GQ, kw=D_KV, shared_kv=True)
            xc = _merge(xc, m, pc_brg, flat(hcf), flat(hcb), pc_ml, mlw, gmat, flat(co_na), flat(co_gq),
                        wml, wna, wgq, wo, li, **con)
            xc = _ffn(xc, m, nw, w_in, w_o, (li, 1), k0=6, nrm=2, **con)
    return xl.reshape(b, s, d)
```

```python
import functools

import numpy as np
import jax
import jax.numpy as jnp
from jax import lax
from jax.experimental import pallas as pl
from jax.experimental.pallas import tpu as pltpu

F32 = jnp.float32
BF16 = jnp.bfloat16

HEAD_DIM = 64
LANES = 128
H_ML, H_NA, H_GQ, H_KV = 4, 6, 6, 2
D_ML, D_NA, D_GQ, D_KV = 256, 384, 384, 128
GRID_W = 64
NA_WIN_R, NA_WIN_C = 8, 16
ROPE_THETA = 10000.0
EPS = 1e-6
N_MOD = 9
ATTN_SCALE = HEAD_DIM ** -0.5
NEG = -1e30

ML_CHUNK = 256
NA_QROWS = 8
NA_BAND = 16
TM = 512
GQ_TQ = 256
FFN_CHUNK = 768
Q_PRESCALE = ATTN_SCALE * float(np.log2(np.e))
MOD_ROWS = 16
VMEM_LIMIT = 56 * 1024 * 1024

C_ML, C_MLG, C_NA, C_GQ, C_BRG = 0, 1024, 1152, 2304, 2944
N_PROJ = 6016
GQ_HEAD_ORDER = (0, 3, 1, 4, 2, 5)
ML_GATE_ORDER = np.array([0, 1, 2, 3, 8, 9, 10, 11, 4, 5, 6, 7, 12, 13, 14, 15])


def _dot(a, b):
    return jnp.dot(a, b, preferred_element_type=F32)


def _dot_nt(a, b):
    return lax.dot_general(a, b, (((1,), (1,)), ((), ())), preferred_element_type=F32)


def _sigmoid(x):
    return 1.0 / (1.0 + jnp.exp(-x))


def _log_sigmoid(x):
    return jnp.minimum(x, 0.0) - jnp.log1p(jnp.exp(-jnp.abs(x)))


def _split_bf16(x):
    hi = x.astype(BF16)
    lo = (x - hi.astype(F32)).astype(BF16)
    return hi, lo


def _modnorm(x, nw, shift, scale):
    ms = jnp.mean(x * x, axis=-1, keepdims=True)
    return (x * lax.rsqrt(ms + EPS) * nw) * (1.0 + scale) + shift


def _head_norm(t, gmat, wrow):
    ss = _dot((t * t).astype(BF16), gmat)
    return t * lax.rsqrt(ss * (1.0 / HEAD_DIM) + EPS) * wrow


def _resident(shape, lead=()):
    return pl.BlockSpec((None,) * len(lead) + tuple(shape), lambda *_: tuple(lead) + (0,) * len(shape),
                        pipeline_mode=pl.Buffered(1))


def _params(sem):
    return pltpu.CompilerParams(dimension_semantics=sem, vmem_limit_bytes=VMEM_LIMIT)


def _ada_kernel(c_ref, w_ref, b_ref, o_ref):
    c = c_ref[...]
    s = (c * _sigmoid(c)).astype(BF16)
    o_ref[0] = _dot(s, w_ref[0].astype(BF16)) + b_ref[0]


def _ada(cvec, ada_w, ada_b):
    depth, d, n = ada_w.shape
    tn = n // 8
    return pl.pallas_call(
        _ada_kernel,
        grid=(depth, n // tn),
        in_specs=[pl.BlockSpec((MOD_ROWS, d), lambda l, j: (0, 0)),
                  pl.BlockSpec((1, d, tn), lambda l, j: (l, 0, j)),
                  pl.BlockSpec((1, 1, tn), lambda l, j: (l, 0, j))],
        out_specs=pl.BlockSpec((1, MOD_ROWS, tn), lambda l, j: (l, 0, j)),
        out_shape=jax.ShapeDtypeStruct((depth, MOD_ROWS, n), F32),
        compiler_params=_params(("arbitrary", "arbitrary")),
        name="ada_mod",
    )(cvec, ada_w, ada_b.reshape(depth, 1, n))


def _mod_index(tiles_per_batch, ctx_row):
    if tiles_per_batch is None:
        return lambda i, *_: (ctx_row, 0, 0)
    return lambda i, *_: (i // tiles_per_batch, 0, 0)


def _tie_row(v, width):
    half = jnp.uint32(16)
    bits = lax.shift_right_logical(lax.shift_right_logical(pltpu.bitcast(v[0:8, 0:LANES], jnp.uint32), half), half)
    zero = pltpu.bitcast(bits, F32)[0:1, :]
    return jnp.concatenate([zero] * (width // LANES), axis=1)


def _two_phase(step, first, body, bufs):
    @pl.when(step == 0)
    def _():
        first(bufs[0])

    @pl.when(step % 2 == 0)
    def _():
        body(bufs[0], bufs[1])

    @pl.when(step % 2 == 1)
    def _():
        body(bufs[1], bufs[0])


def _ffn_kernel(x_ref, xn_ref, mod_ref, modn_ref, nw_ref, wi_ref, wo_ref, o_ref, h0, h1, *, k0, nrm, chunks):
    def normed(x, mr):
        return _modnorm(x, nw_ref[nrm:nrm + 1, :], mr[0, k0:k0 + 1, :], mr[0, k0 + 1:k0 + 2, :])

    def first(buf):
        buf[...] = normed(x_ref[...], mod_ref).astype(BF16)

    def body(cur, nxt):
        h = cur[...]
        dff = wo_ref.shape[0]
        rows = x_ref.shape[0] // len(chunks)
        y = None
        for r, (c0, c1) in enumerate(chunks):
            rs = slice(rows * r, rows * (r + 1))
            hn = normed(xn_ref[rs, :], modn_ref)
            nxt[rs, :] = hn.astype(BF16)
            g = _dot(h, wi_ref[:, c0:c1]) + _tie_row(hn, c1 - c0)
            u = _dot(h, wi_ref[:, dff + c0:dff + c1])
            part = _dot((g * _sigmoid(g) * u).astype(BF16), wo_ref[c0:c1, :])
            y = part if y is None else y + part
        o_ref[...] = x_ref[...] + (0.5 * mod_ref[0, k0 + 2:k0 + 3, :]) * y

    _two_phase(pl.program_id(0), first, body, (h0, h1))


def _ffn(x, mod, nw, w_in, w_out, lead, *, k0, nrm, tiles_per_batch, ctx_row):
    t, d = x.shape
    dff = w_out.shape[-2]
    tm = min(TM, t)
    edges = list(range(0, dff, FFN_CHUNK)) + [dff]
    chunks = tuple(zip(edges[:-1], edges[1:]))
    kern = functools.partial(_ffn_kernel, k0=k0, nrm=nrm, chunks=chunks)
    n = t // tm
    mod_idx = _mod_index(tiles_per_batch, ctx_row)
    nxt = lambda i: jnp.minimum(i + 1, n - 1)
    return pl.pallas_call(
        kern,
        grid=(n,),
        in_specs=[pl.BlockSpec((tm, d), lambda i: (i, 0)),
                  pl.BlockSpec((tm, d), lambda i: (nxt(i), 0)),
                  pl.BlockSpec((1, N_MOD, d), mod_idx),
                  pl.BlockSpec((1, N_MOD, d), lambda i: mod_idx(nxt(i))),
                  pl.BlockSpec((3, d), lambda i: (0, 0)),
                  _resident((d, 2 * dff), lead),
                  _resident((dff, d), lead)],
        out_specs=pl.BlockSpec((tm, d), lambda i: (i, 0)),
        out_shape=jax.ShapeDtypeStruct((t, d), F32),
        scratch_shapes=[pltpu.VMEM((tm, d), BF16)] * 2,
        compiler_params=_params(("arbitrary",)),
        name="ffn_swiglu",
    )(x, x, mod, mod, nw, w_in, w_out)


def _proj_kernel(*refs, rope):
    if rope:
        (x_ref, mod_ref, nw_ref, w_ref, g_ref, qkw_ref, cos_ref, sin_ref,
         ml_ref, mlg_ref, na_ref, gq_ref, brg_ref) = refs
    else:
        (x_ref, mod_ref, nw_ref, w_ref, g_ref, qkw_ref,
         ml_ref, mlg_ref, na_ref, gq_ref, brg_ref) = refs
    h = _modnorm(x_ref[...], nw_ref[1:2, :], mod_ref[0, 3:4, :], mod_ref[0, 4:5, :]).astype(BF16)
    gmat = g_ref[...]

    ml_ref[...] = _dot(h, w_ref[:, C_ML:C_MLG]).astype(BF16)
    mlg_ref[...] = _dot(h, w_ref[:, C_MLG:C_NA])

    na = _dot(h, w_ref[:, C_NA:C_GQ])
    for j in range(3):
        sl = slice(2 * LANES * j, 2 * LANES * (j + 1))
        na_ref[:, sl] = _head_norm(na[:, sl], gmat, qkw_ref[0:1, sl]).astype(BF16)
    na_ref[:, 2 * D_NA:] = na[:, 2 * D_NA:].astype(BF16)

    gq = _dot(h, w_ref[:, C_GQ:C_BRG])
    if rope:
        lane = lax.broadcasted_iota(jnp.int32, (1, LANES), 1)
        first_half = (lane % HEAD_DIM) < (HEAD_DIM // 2)
        cos = cos_ref[...]
        sin = sin_ref[...]
    for j in range(2):
        t2 = _head_norm(gq[:, 2 * LANES * j:2 * LANES * (j + 1)], gmat, qkw_ref[1:2, 2 * LANES * j:2 * LANES * (j + 1)])
        for half in range(2):
            t = t2[:, LANES * half:LANES * (half + 1)]
            if rope:
                rot = jnp.where(first_half, pltpu.roll(t, LANES - HEAD_DIM // 2, axis=1),
                                pltpu.roll(t, HEAD_DIM // 2, axis=1))
                t = t * cos + rot * sin
            gq_ref[:, LANES * (2 * j + half):LANES * (2 * j + half + 1)] = t.astype(BF16)
    gq_ref[:, D_GQ + D_KV:] = gq[:, D_GQ + D_KV:].astype(BF16)

    for j in range(3):
        sl = slice(1024 * j, 1024 * (j + 1))
        brg_ref[:, sl] = _sigmoid(_dot(h, w_ref[:, C_BRG + 1024 * j:C_BRG + 1024 * (j + 1)])).astype(BF16)


def _proj(x, mod, nw, w, li, gmat, qkw, rope_tabs, *, tiles_per_batch, ctx_row):
    t, d = x.shape
    tm = min(TM, t)
    rope = rope_tabs is not None
    in_specs = [pl.BlockSpec((tm, d), lambda i: (i, 0)),
                pl.BlockSpec((1, N_MOD, d), _mod_index(tiles_per_batch, ctx_row)),
                pl.BlockSpec((3, d), lambda i: (0, 0)),
                _resident((d, N_PROJ), (li,)),
                _resident(gmat.shape),
                _resident(qkw.shape)]
    args = [x, mod, nw, w, gmat, qkw]
    if rope:
        in_specs += [pl.BlockSpec((tm, LANES), lambda i: (i % tiles_per_batch, 0))] * 2
        args += list(rope_tabs)
    widths = (1024, LANES, 3 * D_NA, D_GQ + 2 * D_KV, 3072)
    dtypes = (BF16, F32, BF16, BF16, BF16)
    return pl.pallas_call(
        functools.partial(_proj_kernel, rope=rope),
        grid=(t // tm,),
        in_specs=in_specs,
        out_specs=[pl.BlockSpec((tm, wd), lambda i: (i, 0)) for wd in widths],
        out_shape=[jax.ShapeDtypeStruct((t, wd), dt) for wd, dt in zip(widths, dtypes)],
        compiler_params=_params(("parallel",)),
        name="mix_in_proj",
    )(*args)


def _ml_prep(d, g_ref, tri_ref, bias_ref):
    log2e = float(np.log2(np.e))
    lc = g_ref.shape[1]
    gates = g_ref[0] + bias_ref[...]
    gates_t = gates.T
    ig_t = gates_t[0:8] * log2e
    hi, lo = _split_bf16(_log_sigmoid(gates_t[0:16]) * log2e)
    ones = jnp.ones((lc, lc), BF16)
    b_t = (_dot(hi, tri_ref[1 - d]) + _dot(lo, tri_ref[1 - d]))[8:16]
    btot_t = (_dot(hi, ones) + _dot(lo, ones))[8:16]
    lf_al = pltpu.roll(_log_sigmoid(gates) * log2e, LANES - 8, axis=1)
    hi, lo = _split_bf16(lf_al)
    c_mat = gates * log2e - (_dot(tri_ref[d], hi) + _dot(tri_ref[d], lo))
    return ig_t, b_t, btot_t, c_mat


def _ml_head(d, h, prep, m_ref):
    ig_t, b_t, btot_t, c_mat = prep
    lc = c_mat.shape[0]
    r = 4 * d + h
    row = lax.broadcasted_iota(jnp.int32, (lc, lc), 0)
    col = lax.broadcasted_iota(jnp.int32, (lc, lc), 1)
    visible = (row <= col) if d == 0 else (row >= col)
    ig, b, b_tot = ig_t[r:r + 1], b_t[r:r + 1], btot_t[r:r + 1]
    m_prev = m_ref[r:r + 1, :]
    w_end = b_tot - b + ig
    m_new = jnp.maximum(b_tot + m_prev, jnp.max(w_end, axis=1, keepdims=True))
    a = jnp.exp2(w_end - m_new)
    decay = jnp.exp2(b_tot + m_prev - m_new)
    m_inter = b + m_prev
    logw = jnp.where(visible, c_mat[:, r:r + 1] + b, NEG)
    m_j = jnp.maximum(m_inter, jnp.max(logw, axis=0, keepdims=True))
    w = jnp.exp2(logw - m_j)
    m_ref[r:r + 1, :] = m_new
    return w, a, jnp.exp2(m_inter - m_j), jnp.exp2(-m_j), decay


def _ml_pair(d, p, head_a, head_b, q_ref, k_ref, v_ref, o_ref, st_ref):
    (w_a, a_a, g_a, fl_a, dec_a), (w_b, a_b, g_b, fl_b, dec_b) = head_a, head_b
    lc = q_ref.shape[1]
    low = _low_lanes()
    sl = slice(LANES * p, LANES * (p + 1))
    q = q_ref[0, :, sl]
    k = k_ref[0, :, sl] * ATTN_SCALE
    vt = jnp.concatenate([v_ref[0, :, sl].astype(F32).T, jnp.ones((LANES, lc), F32)], axis=0)
    head_row = (lax.broadcasted_iota(jnp.int32, (2 * LANES, 1), 0) % LANES) < HEAD_DIM
    vt16 = vt.astype(BF16)
    r_a = _dot(vt16, (_dot_nt(k, _one_head(q, low, 0)) * w_a).astype(BF16))
    r_b = _dot(vt16, (_dot_nt(k, _one_head(q, low, 1)) * w_b).astype(BF16))
    state = st_ref[d, p]
    r_i = _dot_nt(state.astype(BF16), q)
    r = jnp.where(head_row, r_a, r_b) + jnp.where(head_row, g_a, g_b) * r_i
    num, den = r[:LANES], r[LANES:]
    h_t = num / jnp.maximum(jnp.abs(den), jnp.where(head_row[:LANES], fl_a, fl_b))
    o_ref[0, :, sl] = h_t.T.astype(BF16)

    upd = _dot((vt * jnp.where(head_row, a_a, a_b)).astype(BF16), k)
    same_head = head_row == low
    dec = jnp.where(head_row, dec_a[:, :LANES], dec_b[:, :LANES])
    st_ref[d, p] = dec * state + jnp.where(same_head, upd, 0.0)


def _ml_step(fwd, bwd, tri_ref, bias_ref, st_ref, m_ref):
    dirs = (fwd, bwd)
    preps = [_ml_prep(d, refs[3], tri_ref, bias_ref) for d, refs in enumerate(dirs)]
    heads = [[_ml_head(d, h, preps[d], m_ref) for h in range(H_ML)] for d in range(2)]
    for p in range(H_ML // 2):
        for d, (q_ref, k_ref, v_ref, _, o_ref) in enumerate(dirs):
            _ml_pair(d, p, heads[d][2 * p], heads[d][2 * p + 1], q_ref, k_ref, v_ref, o_ref, st_ref)


def _mlstm_kernel(qf, kf, vf, gf, qb, kb, vb, gb, qc, kc, vc, gc, tri_ref, bias_ref,
                  hf_ref, hb_ref, hcf_ref, hcb_ref, st_ref, m_ref):
    c = pl.program_id(1)

    @pl.when(c == 0)
    def _():
        st_ref[...] = jnp.zeros_like(st_ref)
        m_ref[...] = jnp.zeros_like(m_ref)
        _ml_step((qc, kc, vc, gc, hcf_ref), (qc, kc, vc, gc, hcb_ref), tri_ref, bias_ref, st_ref, m_ref)

    @pl.when(c > 0)
    def _():
        _ml_step((qf, kf, vf, gf, hf_ref), (qb, kb, vb, gb, hb_ref), tri_ref, bias_ref, st_ref, m_ref)


def _mlstm(p_ml, p_mlg, pc_ml, pc_mlg, tri, bias):
    b, s, _ = p_ml.shape
    l = pc_ml.shape[1]
    lc = ML_CHUNK
    assert l == lc and s % lc == 0
    nl = s // lc
    fwd = lambda c: jnp.maximum(c - 1, 0)
    bwd = lambda c: nl - 1 - jnp.maximum(c - 1, 0)

    def lat(idx, blk, width):
        return pl.BlockSpec((1, lc, width), lambda i, c: (i, idx(c), blk))

    def ctx(blk, width):
        return pl.BlockSpec((1, lc, width), lambda i, c: (i, 0, blk))

    in_specs = ([lat(fwd, 0, D_ML), lat(fwd, 1, D_ML), lat(fwd, 2, D_ML), lat(fwd, 0, LANES)]
                + [lat(bwd, 0, D_ML), lat(bwd, 1, D_ML), lat(bwd, 2, D_ML), lat(bwd, 0, LANES)]
                + [ctx(0, D_ML), ctx(1, D_ML), ctx(2, D_ML), ctx(0, LANES)]
                + [_resident((2, lc, lc)), _resident((1, LANES))])
    out_specs = [lat(fwd, 0, D_ML), lat(bwd, 0, D_ML), ctx(0, D_ML), ctx(0, D_ML)]
    out_shape = [jax.ShapeDtypeStruct((b, s, D_ML), BF16)] * 2 + [jax.ShapeDtypeStruct((b, l, D_ML), BF16)] * 2
    return pl.pallas_call(
        _mlstm_kernel,
        grid=(b, nl + 1),
        in_specs=in_specs,
        out_specs=out_specs,
        out_shape=out_shape,
        scratch_shapes=[pltpu.VMEM((2, H_ML // 2, 2 * LANES, LANES), F32), pltpu.VMEM((2 * H_ML, lc), F32)],
        compiler_params=_params(("parallel", "arbitrary")),
        name="mlstm_bidir",
    )(p_ml, p_ml, p_ml, p_mlg, p_ml, p_ml, p_ml, p_mlg, pc_ml, pc_ml, pc_ml, pc_mlg, tri, bias)


def _attend_heads(n_heads, score_fn, value_fn, s_scr, p_scr, next_first=None):
    if next_first is None:
        s_scr[0] = score_fn(0)
    outs = []
    for i in range(n_heads):
        if i + 1 < n_heads:
            s_scr[(i + 1) % 2] = score_fn(i + 1)
        elif next_first is not None:
            s_scr[0] = next_first()
        s = s_scr[i % 2]
        p_scr[i % 2] = jnp.exp2(s - jnp.max(s, axis=1, keepdims=True)).astype(BF16)
        r = _dot(p_scr[i % 2], value_fn(i))
        outs.append(r[:, :LANES] / r[:, LANES:])
    return outs


def _low_lanes():
    return lax.broadcasted_iota(jnp.int32, (1, LANES), 1) < HEAD_DIM


def _one_head(q, low, half):
    zero = jnp.zeros_like(q)
    return jnp.where(low, q, zero) if half == 0 else jnp.where(low, zero, q)


def _pair_outputs(o_ref, outs, low):
    for j in range(len(outs) // 2):
        o_ref[0, :, LANES * j:LANES * (j + 1)] = jnp.where(low, outs[2 * j], outs[2 * j + 1]).astype(BF16)


def _gqa_kernel(q_ref, qn_ref, k_ref, v_ref, kc_ref, vc_ref, o_ref, kall, vall, s_scr, p_scr):
    n_lat = k_ref.shape[1]
    low = _low_lanes()

    def score(qr, i):
        j, half = divmod(i, 2)
        return _dot_nt(_one_head(qr[0, :, LANES * j:LANES * (j + 1)], low, half), kall[...])

    @pl.when(pl.program_id(1) == 0)
    def _():
        kall[0:n_lat, :] = k_ref[0]
        kall[n_lat:, :] = kc_ref[0]
        vall[0:n_lat, 0:LANES] = v_ref[0]
        vall[n_lat:, 0:LANES] = vc_ref[0]
        vall[:, LANES:] = jnp.ones((vall.shape[0], LANES), BF16)
        s_scr[0] = score(q_ref, 0)

    outs = _attend_heads(H_GQ, functools.partial(score, q_ref), lambda i: vall[...], s_scr, p_scr,
                         next_first=functools.partial(score, qn_ref, 0))
    _pair_outputs(o_ref, outs, low)


def _gqa(p_gq, pc_gq):
    b, s, _ = p_gq.shape
    l = pc_gq.shape[1]
    tq = min(GQ_TQ, s)
    kblk, vblk = D_GQ // LANES, D_GQ // LANES + 1
    nt = s // tq
    return pl.pallas_call(
        _gqa_kernel,
        grid=(b, nt),
        in_specs=[pl.BlockSpec((1, tq, D_GQ), lambda i, t: (i, t, 0)),
                  pl.BlockSpec((1, tq, D_GQ), lambda i, t: (i, jnp.minimum(t + 1, nt - 1), 0)),
                  pl.BlockSpec((1, s, LANES), lambda i, t: (i, 0, kblk)),
                  pl.BlockSpec((1, s, LANES), lambda i, t: (i, 0, vblk)),
                  pl.BlockSpec((1, l, LANES), lambda i, t: (i, 0, kblk)),
                  pl.BlockSpec((1, l, LANES), lambda i, t: (i, 0, vblk))],
        out_specs=pl.BlockSpec((1, tq, D_GQ), lambda i, t: (i, t, 0)),
        out_shape=jax.ShapeDtypeStruct((b, s, D_GQ), BF16),
        scratch_shapes=[pltpu.VMEM((s + l, LANES), BF16), pltpu.VMEM((s + l, 2 * LANES), BF16),
                        pltpu.VMEM((2, tq, s + l), F32), pltpu.VMEM((2, tq, s + l), BF16)],
        compiler_params=_params(("parallel", "arbitrary")),
        name="gqa_latent",
    )(p_gq, p_gq, p_gq, p_gq, pc_gq, pc_gq)


def _ctx_attn_kernel(q_ref, k_ref, v_ref, o_ref, s_scr, p_scr, *, shared_kv):
    low = _low_lanes()
    ones = jnp.ones((v_ref.shape[1], LANES), BF16)

    def kv_lanes(i):
        return slice(0, LANES) if shared_kv else slice(LANES * (i // 2), LANES * (i // 2 + 1))

    def score(i):
        j, half = divmod(i, 2)
        return _dot_nt(_one_head(q_ref[0, :, LANES * j:LANES * (j + 1)], low, half), k_ref[0, :, kv_lanes(i)])

    def value(i):
        return jnp.concatenate([v_ref[0, :, kv_lanes(i)], ones], axis=1)

    n_heads = 2 * (q_ref.shape[2] // LANES)
    _pair_outputs(o_ref, _attend_heads(n_heads, score, value, s_scr, p_scr), low)


def _ctx_attn(pc, *, qw, kw, shared_kv):
    b, l, _ = pc.shape
    kb = qw // kw
    return pl.pallas_call(
        functools.partial(_ctx_attn_kernel, shared_kv=shared_kv),
        grid=(b,),
        in_specs=[pl.BlockSpec((1, l, qw), lambda i: (i, 0, 0)),
                  pl.BlockSpec((1, l, kw), lambda i: (i, 0, kb)),
                  pl.BlockSpec((1, l, kw), lambda i: (i, 0, kb + 1))],
        out_specs=pl.BlockSpec((1, l, qw), lambda i: (i, 0, 0)),
        out_shape=jax.ShapeDtypeStruct((b, l, qw), BF16),
        scratch_shapes=[pltpu.VMEM((2, l, l), F32), pltpu.VMEM((2, l, l), BF16)],
        compiler_params=_params(("parallel",)),
        name="ctx_attn",
    )(pc, pc, pc)


def _na_kernel(q_ref, k0, k1, k2, k3, v0, v1, v2, v3, kc_ref, vc_ref, tz_ref, o_ref,
               kall, vall, s_scr, p_scr, *, n_rows):
    i = pl.program_id(1)
    r0 = i * NA_QROWS
    start = jnp.clip(r0 - NA_WIN_R // 2, 0, n_rows - NA_BAND)
    delta = start - r0
    nq, nk = NA_QROWS * GRID_W, NA_BAND * GRID_W
    sub = nk // 4
    n_ctx = kc_ref.shape[1]

    for t, (kr, vr) in enumerate(zip((k0, k1, k2, k3, kc_ref), (v0, v1, v2, v3, vc_ref))):
        rows = slice(sub * t, sub * t + kr.shape[1])
        kall[rows, :] = kr[0]
        for j in range(D_NA // LANES):
            vall[j, rows, 0:LANES] = vr[0, :, LANES * j:LANES * (j + 1)]
    vall[:, :, LANES:] = jnp.ones((D_NA // LANES, nk + n_ctx, LANES), BF16)

    qrow = r0 + lax.broadcasted_iota(jnp.int32, (nq, nk), 0) // GRID_W
    krow = start + lax.broadcasted_iota(jnp.int32, (nq, nk), 1) // GRID_W
    first = jnp.clip(qrow - NA_WIN_R // 2, 0, n_rows - NA_WIN_R)
    row_mask = jnp.where((krow >= first) & (krow < first + NA_WIN_R), 0.0, NEG)
    low = _low_lanes()
    no_bias = jnp.zeros((nq, n_ctx), F32)

    def score(h):
        j, half = divmod(h, 2)
        sl = slice(LANES * j, LANES * (j + 1))
        slabs = []
        for a in range(NA_QROWS):
            pieces = [tz_ref[h, jnp.clip(delta + 2 * bp - a, -8, 7) + 8] for bp in range(NA_BAND // 2)]
            slabs.append(jnp.concatenate(pieces, axis=1))
        bias = jnp.concatenate([jnp.concatenate(slabs, axis=0) + row_mask, no_bias], axis=1)
        return _dot_nt(_one_head(q_ref[0, :, sl], low, half), kall[:, sl]) + bias

    _pair_outputs(o_ref, _attend_heads(H_NA, score, lambda h: vall[h // 2], s_scr, p_scr), low)


def _na(p_na, pc_na, tz):
    b, s, _ = p_na.shape
    l = pc_na.shape[1]
    n_rows = s // GRID_W
    assert n_rows % NA_QROWS == 0 and n_rows >= NA_BAND
    nq, nk = NA_QROWS * GRID_W, NA_BAND * GRID_W
    sub = nk // 4
    rows_per_sub = NA_BAND // 4

    def band(t, blk):
        def idx(i, r):
            start = jnp.clip(r * NA_QROWS - NA_WIN_R // 2, 0, n_rows - NA_BAND)
            return (i, start // rows_per_sub + t, blk)
        return pl.BlockSpec((1, sub, D_NA), idx)

    in_specs = ([pl.BlockSpec((1, nq, D_NA), lambda i, r: (i, r, 0))]
                + [band(t, 1) for t in range(4)] + [band(t, 2) for t in range(4)]
                + [pl.BlockSpec((1, l, D_NA), lambda i, r: (i, 0, 1)),
                   pl.BlockSpec((1, l, D_NA), lambda i, r: (i, 0, 2)),
                   _resident(tz.shape)])
    return pl.pallas_call(
        functools.partial(_na_kernel, n_rows=n_rows),
        grid=(b, n_rows // NA_QROWS),
        in_specs=in_specs,
        out_specs=pl.BlockSpec((1, nq, D_NA), lambda i, r: (i, r, 0)),
        out_shape=jax.ShapeDtypeStruct((b, s, D_NA), BF16),
        scratch_shapes=[pltpu.VMEM((nk + l, D_NA), BF16), pltpu.VMEM((D_NA // LANES, nk + l, 2 * LANES), BF16),
                        pltpu.VMEM((2, nq, nk + l), F32), pltpu.VMEM((2, nq, nk + l), BF16)],
        compiler_params=_params(("parallel", "arbitrary")),
        name="na_latent",
    )(p_na, *([p_na] * 8), pc_na, pc_na, tz)


def _merge_kernel(x_ref, mod_ref, brg_ref, hf_ref, hb_ref, og_ref, mlw_ref, g_ref, na_ref, gq_ref,
                  wml_ref, wna_ref, wgq_ref, wo_ref, o_ref):
    gmat = g_ref[...]
    h = hf_ref[...].astype(F32) + hb_ref[...].astype(F32)
    o_ml = (_head_norm(h, gmat, mlw_ref[...]) * _sigmoid(og_ref[...].astype(F32))).astype(BF16)
    d = x_ref.shape[1]
    y = brg_ref[:, 0:d].astype(F32) * _dot(o_ml, wml_ref[...])
    y = y + brg_ref[:, d:2 * d].astype(F32) * _dot(na_ref[...], wna_ref[...])
    y = y + brg_ref[:, 2 * d:3 * d].astype(F32) * _dot(gq_ref[...], wgq_ref[...])
    o_ref[...] = x_ref[...] + mod_ref[0, 5:6, :] * _dot(y.astype(BF16), wo_ref[...])


def _merge(x, mod, brg, hf, hb, p_ml, mlw, gmat, o_na, o_gq, wml, wna, wgq, wo, li,
           *, tiles_per_batch, ctx_row):
    t, d = x.shape
    tm = min(TM, t)
    row = lambda wd, blk=0: pl.BlockSpec((tm, wd), lambda i: (i, blk))
    lead = (li,)
    return pl.pallas_call(
        _merge_kernel,
        grid=(t // tm,),
        in_specs=[row(d),
                  pl.BlockSpec((1, N_MOD, d), _mod_index(tiles_per_batch, ctx_row)),
                  row(3 * d), row(D_ML), row(D_ML), row(D_ML, 3),
                  _resident((1, D_ML)), _resident(gmat.shape),
                  row(D_NA), row(D_GQ),
                  _resident((D_ML, d), lead), _resident((D_NA, d), lead), _resident((D_GQ, d), lead),
                  _resident((d, d), lead)],
        out_specs=row(d),
        out_shape=jax.ShapeDtypeStruct((t, d), F32),
        compiler_params=_params(("parallel",)),
        name="branch_merge",
    )(x, mod, brg, hf, hb, p_ml, mlw, gmat, o_na, o_gq, wml, wna, wgq, wo)


def _proj_weight(w):
    d = w.shape[-2]
    o = 0
    seg = {}
    for name, width in (("ml_k", D_ML), ("ml_v", D_ML), ("ml_g", 4 * H_ML), ("na_k", D_NA), ("na_v", D_NA),
                        ("gq_k", D_KV), ("gq_v", D_KV), ("ml_q", D_ML), ("ml_o", D_ML), ("na_q", D_NA),
                        ("gq_q", D_GQ), ("br_g", 3 * d)):
        seg[name] = w[..., o:o + width]
        o += width
    gq_q = jnp.concatenate([seg["gq_q"][..., HEAD_DIM * h:HEAD_DIM * (h + 1)] for h in GQ_HEAD_ORDER], axis=-1)
    pad = jnp.zeros(w.shape[:-1] + (LANES - 4 * H_ML,), w.dtype)
    out = jnp.concatenate([seg["ml_q"], seg["ml_k"], seg["ml_v"], seg["ml_o"], seg["ml_g"][..., ML_GATE_ORDER], pad,
                           seg["na_q"], seg["na_k"], seg["na_v"], gq_q, seg["gq_k"], seg["gq_v"], seg["br_g"]],
                          axis=-1)
    return out


def _rope_tables(n_tok):
    t = jnp.arange(n_tok, dtype=jnp.int32)
    row = (t // GRID_W).astype(F32)
    col = (t % GRID_W).astype(F32)
    n_freq = HEAD_DIM // 4
    inv = ROPE_THETA ** (-jnp.arange(n_freq, dtype=F32) / n_freq)
    ang = jnp.concatenate([row[:, None] * inv, col[:, None] * inv], axis=-1)
    cos, sin = jnp.cos(ang), jnp.sin(ang)
    cos_t = jnp.tile(cos, (1, LANES // (HEAD_DIM // 2)))
    sin_t = jnp.tile(jnp.concatenate([-sin, sin], axis=-1), (1, LANES // HEAD_DIM))
    return cos_t, sin_t


def _na_bias_table(rpb):
    col = np.arange(GRID_W)
    first = np.clip(col - NA_WIN_C // 2, 0, GRID_W - NA_WIN_C)
    in_win = (col[None, :] >= first[:, None]) & (col[None, :] < first[:, None] + NA_WIN_C)
    side = GRID_W - NA_WIN_C
    rows = jnp.pad(rpb, ((0, 0), (1, 1), (side, side)))
    row_ok = np.zeros((2 * NA_WIN_R + 1,), bool)
    row_ok[1:-1] = True
    full = jnp.stack([rows[:, :, GRID_W - 1 - qc:2 * GRID_W - 1 - qc] for qc in range(GRID_W)], axis=2)
    full = full * float(np.log2(np.e))
    full = jnp.where(jnp.asarray(in_win[None, None] & row_ok[None, :, None, None]), full, NEG)
    return jnp.concatenate([full[:, :-1], full[:, 1:]], axis=-1).astype(F32)


def kernel(x, c, ctx, c_ctx, ada_w, ada_b, norm_w, ffn_w_in, ffn_w_out, mix_w_in, ml_gate_b, ml_norm_w,
           na_qk_w, na_rpb, gq_qk_w, w_br_ml, w_br_na, w_br_gq, w_out):
    b, s, d = x.shape
    l = ctx.shape[1]
    depth = ada_w.shape[0]
    assert b < MOD_ROWS and s % TM == 0 and (b * l) % min(TM, b * l) == 0
    ctx_row = b
    tiles_per_batch = s // TM

    cvec = jnp.zeros((MOD_ROWS, d), F32).at[:b].set(c).at[b].set(c_ctx)
    mod = _ada(cvec, ada_w, ada_b).reshape(depth, MOD_ROWS, N_MOD, d)

    lane = np.arange(2 * LANES)
    gmat = jnp.asarray((lane[:, None] // HEAD_DIM) == (lane[None, :] // HEAD_DIM), BF16)
    idx = np.arange(ML_CHUNK)
    tri = jnp.asarray(np.stack([idx[:, None] >= idx[None, :], idx[:, None] <= idx[None, :]]), BF16)
    rope_tabs = _rope_tables(s)

    w_in, w_o = ffn_w_in.astype(BF16), ffn_w_out.astype(BF16)
    w_proj = _proj_weight(mix_w_in.astype(BF16))
    wml, wna, wo = w_br_ml.astype(BF16), w_br_na.astype(BF16), w_out.astype(BF16)
    wgq = jnp.concatenate([w_br_gq[:, HEAD_DIM * h:HEAD_DIM * (h + 1)] for h in GQ_HEAD_ORDER], axis=1).astype(BF16)

    xl = x.reshape(b * s, d)
    xc = ctx.reshape(b * l, d)
    lat = dict(tiles_per_batch=tiles_per_batch, ctx_row=ctx_row)
    con = dict(tiles_per_batch=None, ctx_row=ctx_row)
    for li in range(depth):
        ctx_out = li < depth - 1
        qkw = jnp.zeros((8, 2 * D_NA), F32)
        qkw = qkw.at[0, :D_NA].set(jnp.tile(na_qk_w[li, 0], H_NA) * Q_PRESCALE)
        qkw = qkw.at[0, D_NA:].set(jnp.tile(na_qk_w[li, 1], H_NA))
        qkw = qkw.at[1, :D_GQ].set(jnp.tile(gq_qk_w[li, 0], H_GQ) * Q_PRESCALE)
        qkw = qkw.at[1, D_GQ:D_GQ + D_KV].set(jnp.tile(gq_qk_w[li, 1], H_KV))
        gate_b = jnp.zeros((1, LANES), F32).at[0, :4 * H_ML].set(ml_gate_b[li][ML_GATE_ORDER])
        mlw = ml_norm_w[li].reshape(1, D_ML)
        tz = _na_bias_table(na_rpb[li])
        m, nw = mod[li], norm_w[li]

        xl = _ffn(xl, m, nw, w_in, w_o, (li, 0), k0=0, nrm=0, **lat)
        xc = _ffn(xc, m, nw, w_in, w_o, (li, 0), k0=0, nrm=0, **con)

        p_ml, p_mlg, p_na, p_gq, p_brg = _proj(xl, m, nw, w_proj, li, gmat, qkw, rope_tabs, **lat)
        pc_ml, pc_mlg, pc_na, pc_gq, pc_brg = _proj(xc, m, nw, w_proj, li, gmat, qkw, None, **con)
        seq = lambda a: a.reshape(b, s, a.shape[-1])
        cseq = lambda a: a.reshape(b, l, a.shape[-1])

        hf, hb, hcf, hcb = _mlstm(seq(p_ml), seq(p_mlg), cseq(pc_ml), cseq(pc_mlg), tri, gate_b)
        o_na = _na(seq(p_na), cseq(pc_na), tz)
        o_gq = _gqa(seq(p_gq), cseq(pc_gq))
        flat = lambda a: a.reshape(-1, a.shape[-1])
        xl = _merge(xl, m, p_brg, flat(hf), flat(hb), p_ml, mlw, gmat, flat(o_na), flat(o_gq),
                    wml, wna, wgq, wo, li, **lat)
        xl = _ffn(xl, m, nw, w_in, w_o, (li, 1), k0=6, nrm=2, **lat)
        if ctx_out:
            co_na = _ctx_attn(cseq(pc_na), qw=D_NA, kw=D_NA, shared_kv=False)
            co_gq = _ctx_attn(cseq(pc_gq), qw=D_GQ, kw=D_KV, shared_kv=True)
            xc = _merge(xc, m, pc_brg, flat(hcf), flat(hcb), pc_ml, mlw, gmat, flat(co_na), flat(co_gq),
                        wml, wna, wgq, wo, li, **con)
            xc = _ffn(xc, m, nw, w_in, w_o, (li, 1), k0=6, nrm=2, **con)
    return xl.reshape(b, s, d)
```

```python
import functools

import numpy as np
import jax
import jax.numpy as jnp
from jax import lax
from jax.experimental import pallas as pl
from jax.experimental.pallas import tpu as pltpu

F32 = jnp.float32
BF16 = jnp.bfloat16

HEAD_DIM = 64
LANES = 128
H_ML, H_NA, H_GQ, H_KV = 4, 6, 6, 2
D_ML, D_NA, D_GQ, D_KV = 256, 384, 384, 128
GRID_W = 64
NA_WIN_R, NA_WIN_C = 8, 16
ROPE_THETA = 10000.0
EPS = 1e-6
N_MOD = 9
ATTN_SCALE = HEAD_DIM ** -0.5
NEG = -1e30

ML_CHUNK = 256
NA_QROWS = 8
NA_BAND = 16
TM = 512
GQ_TQ = 256
FFN_CHUNK = 768
Q_PRESCALE = ATTN_SCALE * float(np.log2(np.e))
MOD_ROWS = 16
VMEM_LIMIT = 56 * 1024 * 1024

C_ML, C_MLG, C_NA, C_GQ, C_END = 0, 1024, 1152, 2304, 2944
N_PROJ = C_END
GQ_HEAD_ORDER = (0, 3, 1, 4, 2, 5)
ML_GATE_ORDER = np.array([0, 1, 2, 3, 8, 9, 10, 11, 4, 5, 6, 7, 12, 13, 14, 15])


def _dot(a, b):
    return jnp.dot(a, b, preferred_element_type=F32)


def _dot_nt(a, b):
    return lax.dot_general(a, b, (((1,), (1,)), ((), ())), preferred_element_type=F32)


def _sigmoid(x):
    return 1.0 / (1.0 + jnp.exp(-x))


def _log_sigmoid(x):
    return jnp.minimum(x, 0.0) - jnp.log1p(jnp.exp(-jnp.abs(x)))


def _split_bf16(x):
    hi = x.astype(BF16)
    lo = (x - hi.astype(F32)).astype(BF16)
    return hi, lo


def _modnorm(x, nw, shift, scale):
    ms = jnp.mean(x * x, axis=-1, keepdims=True)
    return (x * lax.rsqrt(ms + EPS) * nw) * (1.0 + scale) + shift


def _head_norm(t, gmat, wrow):
    ss = _dot((t * t).astype(BF16), gmat)
    return t * lax.rsqrt(ss * (1.0 / HEAD_DIM) + EPS) * wrow


def _resident(shape, lead=()):
    return pl.BlockSpec((None,) * len(lead) + tuple(shape), lambda *_: tuple(lead) + (0,) * len(shape),
                        pipeline_mode=pl.Buffered(1))


def _params(sem):
    return pltpu.CompilerParams(dimension_semantics=sem, vmem_limit_bytes=VMEM_LIMIT)


def _ada_kernel(c_ref, w_ref, b_ref, o_ref):
    c = c_ref[...]
    s = (c * _sigmoid(c)).astype(BF16)
    o_ref[0] = _dot(s, w_ref[0].astype(BF16)) + b_ref[0]


def _ada(cvec, ada_w, ada_b):
    depth, d, n = ada_w.shape
    tn = n // 8
    return pl.pallas_call(
        _ada_kernel,
        grid=(depth, n // tn),
        in_specs=[pl.BlockSpec((MOD_ROWS, d), lambda l, j: (0, 0)),
                  pl.BlockSpec((1, d, tn), lambda l, j: (l, 0, j)),
                  pl.BlockSpec((1, 1, tn), lambda l, j: (l, 0, j))],
        out_specs=pl.BlockSpec((1, MOD_ROWS, tn), lambda l, j: (l, 0, j)),
        out_shape=jax.ShapeDtypeStruct((depth, MOD_ROWS, n), F32),
        compiler_params=_params(("arbitrary", "arbitrary")),
        name="ada_mod",
    )(cvec, ada_w, ada_b.reshape(depth, 1, n))


def _mod_index(tiles_per_batch, ctx_row):
    if tiles_per_batch is None:
        return lambda i, *_: (ctx_row, 0, 0)
    return lambda i, *_: (i // tiles_per_batch, 0, 0)


def _ffn_kernel(x_ref, mod_ref, nw_ref, wi_ref, wo_ref, o_ref, *, k0, nrm, chunks):
    x = x_ref[...]
    h = _modnorm(x, nw_ref[nrm:nrm + 1, :], mod_ref[0, k0:k0 + 1, :], mod_ref[0, k0 + 1:k0 + 2, :]).astype(BF16)
    dff = wo_ref.shape[0]
    y = None
    for c0, c1 in chunks:
        g = _dot(h, wi_ref[:, c0:c1])
        u = _dot(h, wi_ref[:, dff + c0:dff + c1])
        part = _dot((g * _sigmoid(g) * u).astype(BF16), wo_ref[c0:c1, :])
        y = part if y is None else y + part
    o_ref[...] = x + (0.5 * mod_ref[0, k0 + 2:k0 + 3, :]) * y


def _ffn(x, mod, nw, w_in, w_out, lead, *, k0, nrm, tiles_per_batch, ctx_row):
    t, d = x.shape
    dff = w_out.shape[-2]
    tm = min(TM, t)
    edges = list(range(0, dff, FFN_CHUNK)) + [dff]
    chunks = tuple(zip(edges[:-1], edges[1:]))
    kern = functools.partial(_ffn_kernel, k0=k0, nrm=nrm, chunks=chunks)
    return pl.pallas_call(
        kern,
        grid=(t // tm,),
        in_specs=[pl.BlockSpec((tm, d), lambda i: (i, 0)),
                  pl.BlockSpec((1, N_MOD, d), _mod_index(tiles_per_batch, ctx_row)),
                  pl.BlockSpec((3, d), lambda i: (0, 0)),
                  _resident((d, 2 * dff), lead),
                  _resident((dff, d), lead)],
        out_specs=pl.BlockSpec((tm, d), lambda i: (i, 0)),
        out_shape=jax.ShapeDtypeStruct((t, d), F32),
        compiler_params=_params(("parallel",)),
        name="ffn_swiglu",
    )(x, mod, nw, w_in, w_out)


def _proj_kernel(*refs, rope):
    if rope:
        (x_ref, mod_ref, nw_ref, w_ref, g_ref, qkw_ref, cos_ref, sin_ref,
         ml_ref, mlg_ref, na_ref, gq_ref) = refs
    else:
        (x_ref, mod_ref, nw_ref, w_ref, g_ref, qkw_ref,
         ml_ref, mlg_ref, na_ref, gq_ref) = refs
    h = _modnorm(x_ref[...], nw_ref[1:2, :], mod_ref[0, 3:4, :], mod_ref[0, 4:5, :]).astype(BF16)
    gmat = g_ref[...]

    ml_ref[...] = _dot(h, w_ref[:, C_ML:C_MLG]).astype(BF16)
    mlg_ref[...] = _dot(h, w_ref[:, C_MLG:C_NA])

    na = _dot(h, w_ref[:, C_NA:C_GQ])
    for j in range(3):
        sl = slice(2 * LANES * j, 2 * LANES * (j + 1))
        na_ref[:, sl] = _head_norm(na[:, sl], gmat, qkw_ref[0:1, sl]).astype(BF16)
    na_ref[:, 2 * D_NA:] = na[:, 2 * D_NA:].astype(BF16)

    gq = _dot(h, w_ref[:, C_GQ:C_END])
    if rope:
        lane = lax.broadcasted_iota(jnp.int32, (1, LANES), 1)
        first_half = (lane % HEAD_DIM) < (HEAD_DIM // 2)
        cos = cos_ref[...]
        sin = sin_ref[...]
    for j in range(2):
        t2 = _head_norm(gq[:, 2 * LANES * j:2 * LANES * (j + 1)], gmat, qkw_ref[1:2, 2 * LANES * j:2 * LANES * (j + 1)])
        for half in range(2):
            t = t2[:, LANES * half:LANES * (half + 1)]
            if rope:
                rot = jnp.where(first_half, pltpu.roll(t, LANES - HEAD_DIM // 2, axis=1),
                                pltpu.roll(t, HEAD_DIM // 2, axis=1))
                t = t * cos + rot * sin
            gq_ref[:, LANES * (2 * j + half):LANES * (2 * j + half + 1)] = t.astype(BF16)
    gq_ref[:, D_GQ + D_KV:] = gq[:, D_GQ + D_KV:].astype(BF16)


def _proj(x, mod, nw, w, li, gmat, qkw, rope_tabs, *, tiles_per_batch, ctx_row):
    t, d = x.shape
    tm = min(TM, t)
    rope = rope_tabs is not None
    in_specs = [pl.BlockSpec((tm, d), lambda i: (i, 0)),
                pl.BlockSpec((1, N_MOD, d), _mod_index(tiles_per_batch, ctx_row)),
                pl.BlockSpec((3, d), lambda i: (0, 0)),
                _resident((d, N_PROJ), (li,)),
                _resident(gmat.shape),
                _resident(qkw.shape)]
    args = [x, mod, nw, w, gmat, qkw]
    if rope:
        in_specs += [pl.BlockSpec((tm, LANES), lambda i: (i % tiles_per_batch, 0))] * 2
        args += list(rope_tabs)
    widths = (1024, LANES, 3 * D_NA, D_GQ + 2 * D_KV)
    dtypes = (BF16, F32, BF16, BF16)
    return pl.pallas_call(
        functools.partial(_proj_kernel, rope=rope),
        grid=(t // tm,),
        in_specs=in_specs,
        out_specs=[pl.BlockSpec((tm, wd), lambda i: (i, 0)) for wd in widths],
        out_shape=[jax.ShapeDtypeStruct((t, wd), dt) for wd, dt in zip(widths, dtypes)],
        compiler_params=_params(("parallel",)),
        name="mix_in_proj",
    )(*args)


def _ml_prep(d, g_ref, tri_ref, bias_ref):
    log2e = float(np.log2(np.e))
    lc = g_ref.shape[1]
    gates = g_ref[0] + bias_ref[...]
    gates_t = gates.T
    ig_t = gates_t[0:8] * log2e
    hi, lo = _split_bf16(_log_sigmoid(gates_t[0:16]) * log2e)
    ones = jnp.ones((lc, lc), BF16)
    b_t = (_dot(hi, tri_ref[1 - d]) + _dot(lo, tri_ref[1 - d]))[8:16]
    btot_t = (_dot(hi, ones) + _dot(lo, ones))[8:16]
    lf_al = pltpu.roll(_log_sigmoid(gates) * log2e, LANES - 8, axis=1)
    hi, lo = _split_bf16(lf_al)
    c_mat = gates * log2e - (_dot(tri_ref[d], hi) + _dot(tri_ref[d], lo))
    return ig_t, b_t, btot_t, c_mat


def _ml_head(d, h, prep, m_ref):
    ig_t, b_t, btot_t, c_mat = prep
    lc = c_mat.shape[0]
    r = 4 * d + h
    row = lax.broadcasted_iota(jnp.int32, (lc, lc), 0)
    col = lax.broadcasted_iota(jnp.int32, (lc, lc), 1)
    visible = (row <= col) if d == 0 else (row >= col)
    ig, b, b_tot = ig_t[r:r + 1], b_t[r:r + 1], btot_t[r:r + 1]
    m_prev = m_ref[r:r + 1, :]
    w_end = b_tot - b + ig
    m_new = jnp.maximum(b_tot + m_prev, jnp.max(w_end, axis=1, keepdims=True))
    a = jnp.exp2(w_end - m_new)
    decay = jnp.exp2(b_tot + m_prev - m_new)
    m_inter = b + m_prev
    logw = jnp.where(visible, c_mat[:, r:r + 1] + b, NEG)
    m_j = jnp.maximum(m_inter, jnp.max(logw, axis=0, keepdims=True))
    w = jnp.exp2(logw - m_j)
    m_ref[r:r + 1, :] = m_new
    return w, a, jnp.exp2(m_inter - m_j), jnp.exp2(-m_j), decay


def _ml_pair(d, p, head_a, head_b, q_ref, k_ref, v_ref, o_ref, st_ref):
    (w_a, a_a, g_a, fl_a, dec_a), (w_b, a_b, g_b, fl_b, dec_b) = head_a, head_b
    lc = q_ref.shape[1]
    low = _low_lanes()
    sl = slice(LANES * p, LANES * (p + 1))
    q = q_ref[0, :, sl]
    k = k_ref[0, :, sl] * ATTN_SCALE
    vt = jnp.concatenate([v_ref[0, :, sl].astype(F32).T, jnp.ones((LANES, lc), F32)], axis=0)
    head_row = (lax.broadcasted_iota(jnp.int32, (2 * LANES, 1), 0) % LANES) < HEAD_DIM
    vt16 = vt.astype(BF16)
    r_a = _dot(vt16, (_dot_nt(k, _one_head(q, low, 0)) * w_a).astype(BF16))
    r_b = _dot(vt16, (_dot_nt(k, _one_head(q, low, 1)) * w_b).astype(BF16))
    state = st_ref[d, p]
    r_i = _dot_nt(state.astype(BF16), q)
    r = jnp.where(head_row, r_a, r_b) + jnp.where(head_row, g_a, g_b) * r_i
    num, den = r[:LANES], r[LANES:]
    h_t = num / jnp.maximum(jnp.abs(den), jnp.where(head_row[:LANES], fl_a, fl_b))
    o_ref[0, :, sl] = h_t.T.astype(BF16)

    upd = _dot((vt * jnp.where(head_row, a_a, a_b)).astype(BF16), k)
    same_head = head_row == low
    dec = jnp.where(head_row, dec_a[:, :LANES], dec_b[:, :LANES])
    st_ref[d, p] = dec * state + jnp.where(same_head, upd, 0.0)


def _ml_step(fwd, bwd, tri_ref, bias_ref, st_ref, m_ref):
    dirs = (fwd, bwd)
    preps = [_ml_prep(d, refs[3], tri_ref, bias_ref) for d, refs in enumerate(dirs)]
    heads = [[_ml_head(d, h, preps[d], m_ref) for h in range(H_ML)] for d in range(2)]
    for p in range(H_ML // 2):
        for d, (q_ref, k_ref, v_ref, _, o_ref) in enumerate(dirs):
            _ml_pair(d, p, heads[d][2 * p], heads[d][2 * p + 1], q_ref, k_ref, v_ref, o_ref, st_ref)


def _mlstm_kernel(qf, kf, vf, gf, qb, kb, vb, gb, qc, kc, vc, gc, tri_ref, bias_ref,
                  hf_ref, hb_ref, hcf_ref, hcb_ref, st_ref, m_ref):
    c = pl.program_id(1)

    @pl.when(c == 0)
    def _():
        st_ref[...] = jnp.zeros_like(st_ref)
        m_ref[...] = jnp.zeros_like(m_ref)
        _ml_step((qc, kc, vc, gc, hcf_ref), (qc, kc, vc, gc, hcb_ref), tri_ref, bias_ref, st_ref, m_ref)

    @pl.when(c > 0)
    def _():
        _ml_step((qf, kf, vf, gf, hf_ref), (qb, kb, vb, gb, hb_ref), tri_ref, bias_ref, st_ref, m_ref)


def _mlstm(p_ml, p_mlg, pc_ml, pc_mlg, tri, bias):
    b, s, _ = p_ml.shape
    l = pc_ml.shape[1]
    lc = ML_CHUNK
    assert l == lc and s % lc == 0
    nl = s // lc
    fwd = lambda c: jnp.maximum(c - 1, 0)
    bwd = lambda c: nl - 1 - jnp.maximum(c - 1, 0)

    def lat(idx, blk, width):
        return pl.BlockSpec((1, lc, width), lambda i, c: (i, idx(c), blk))

    def ctx(blk, width):
        return pl.BlockSpec((1, lc, width), lambda i, c: (i, 0, blk))

    in_specs = ([lat(fwd, 0, D_ML), lat(fwd, 1, D_ML), lat(fwd, 2, D_ML), lat(fwd, 0, LANES)]
                + [lat(bwd, 0, D_ML), lat(bwd, 1, D_ML), lat(bwd, 2, D_ML), lat(bwd, 0, LANES)]
                + [ctx(0, D_ML), ctx(1, D_ML), ctx(2, D_ML), ctx(0, LANES)]
                + [_resident((2, lc, lc)), _resident((1, LANES))])
    out_specs = [lat(fwd, 0, D_ML), lat(bwd, 0, D_ML), ctx(0, D_ML), ctx(0, D_ML)]
    out_shape = [jax.ShapeDtypeStruct((b, s, D_ML), BF16)] * 2 + [jax.ShapeDtypeStruct((b, l, D_ML), BF16)] * 2
    return pl.pallas_call(
        _mlstm_kernel,
        grid=(b, nl + 1),
        in_specs=in_specs,
        out_specs=out_specs,
        out_shape=out_shape,
        scratch_shapes=[pltpu.VMEM((2, H_ML // 2, 2 * LANES, LANES), F32), pltpu.VMEM((2 * H_ML, lc), F32)],
        compiler_params=_params(("parallel", "arbitrary")),
        name="mlstm_bidir",
    )(p_ml, p_ml, p_ml, p_mlg, p_ml, p_ml, p_ml, p_mlg, pc_ml, pc_ml, pc_ml, pc_mlg, tri, bias)


def _attend_heads(n_heads, score_fn, value_fn, s_scr, p_scr):
    s_scr[0] = score_fn(0)
    outs = []
    for i in range(n_heads):
        if i + 1 < n_heads:
            s_scr[(i + 1) % 2] = score_fn(i + 1)
        s = s_scr[i % 2]
        p_scr[i % 2] = jnp.exp2(s - jnp.max(s, axis=1, keepdims=True)).astype(BF16)
        r = _dot(p_scr[i % 2], value_fn(i))
        outs.append(r[:, :LANES] / r[:, LANES:])
    return outs


def _low_lanes():
    return lax.broadcasted_iota(jnp.int32, (1, LANES), 1) < HEAD_DIM


def _one_head(q, low, half):
    zero = jnp.zeros_like(q)
    return jnp.where(low, q, zero) if half == 0 else jnp.where(low, zero, q)


def _pair_outputs(o_ref, outs, low):
    for j in range(len(outs) // 2):
        o_ref[0, :, LANES * j:LANES * (j + 1)] = jnp.where(low, outs[2 * j], outs[2 * j + 1]).astype(BF16)


def _gqa_kernel(q_ref, k_ref, v_ref, kc_ref, vc_ref, o_ref, kall, vall, s_scr, p_scr):
    n_lat = k_ref.shape[1]

    @pl.when(pl.program_id(1) == 0)
    def _():
        kall[0:n_lat, :] = k_ref[0]
        kall[n_lat:, :] = kc_ref[0]
        vall[0:n_lat, 0:LANES] = v_ref[0]
        vall[n_lat:, 0:LANES] = vc_ref[0]
        vall[:, LANES:] = jnp.ones((vall.shape[0], LANES), BF16)

    low = _low_lanes()

    def score(i):
        j, half = divmod(i, 2)
        return _dot_nt(_one_head(q_ref[0, :, LANES * j:LANES * (j + 1)], low, half), kall[...])

    _pair_outputs(o_ref, _attend_heads(H_GQ, score, lambda i: vall[...], s_scr, p_scr), low)


def _gqa(p_gq, pc_gq):
    b, s, _ = p_gq.shape
    l = pc_gq.shape[1]
    tq = min(GQ_TQ, s)
    kblk, vblk = D_GQ // LANES, D_GQ // LANES + 1
    return pl.pallas_call(
        _gqa_kernel,
        grid=(b, s // tq),
        in_specs=[pl.BlockSpec((1, tq, D_GQ), lambda i, t: (i, t, 0)),
                  pl.BlockSpec((1, s, LANES), lambda i, t: (i, 0, kblk)),
                  pl.BlockSpec((1, s, LANES), lambda i, t: (i, 0, vblk)),
                  pl.BlockSpec((1, l, LANES), lambda i, t: (i, 0, kblk)),
                  pl.BlockSpec((1, l, LANES), lambda i, t: (i, 0, vblk))],
        out_specs=pl.BlockSpec((1, tq, D_GQ), lambda i, t: (i, t, 0)),
        out_shape=jax.ShapeDtypeStruct((b, s, D_GQ), BF16),
        scratch_shapes=[pltpu.VMEM((s + l, LANES), BF16), pltpu.VMEM((s + l, 2 * LANES), BF16),
                        pltpu.VMEM((2, tq, s + l), F32), pltpu.VMEM((2, tq, s + l), BF16)],
        compiler_params=_params(("parallel", "arbitrary")),
        name="gqa_latent",
    )(p_gq, p_gq, p_gq, pc_gq, pc_gq)


def _ctx_attn_kernel(q_ref, k_ref, v_ref, o_ref, s_scr, p_scr, *, shared_kv):
    low = _low_lanes()
    ones = jnp.ones((v_ref.shape[1], LANES), BF16)

    def kv_lanes(i):
        return slice(0, LANES) if shared_kv else slice(LANES * (i // 2), LANES * (i // 2 + 1))

    def score(i):
        j, half = divmod(i, 2)
        return _dot_nt(_one_head(q_ref[0, :, LANES * j:LANES * (j + 1)], low, half), k_ref[0, :, kv_lanes(i)])

    def value(i):
        return jnp.concatenate([v_ref[0, :, kv_lanes(i)], ones], axis=1)

    n_heads = 2 * (q_ref.shape[2] // LANES)
    _pair_outputs(o_ref, _attend_heads(n_heads, score, value, s_scr, p_scr), low)


def _ctx_attn(pc, *, qw, kw, shared_kv):
    b, l, _ = pc.shape
    kb = qw // kw
    return pl.pallas_call(
        functools.partial(_ctx_attn_kernel, shared_kv=shared_kv),
        grid=(b,),
        in_specs=[pl.BlockSpec((1, l, qw), lambda i: (i, 0, 0)),
                  pl.BlockSpec((1, l, kw), lambda i: (i, 0, kb)),
                  pl.BlockSpec((1, l, kw), lambda i: (i, 0, kb + 1))],
        out_specs=pl.BlockSpec((1, l, qw), lambda i: (i, 0, 0)),
        out_shape=jax.ShapeDtypeStruct((b, l, qw), BF16),
        scratch_shapes=[pltpu.VMEM((2, l, l), F32), pltpu.VMEM((2, l, l), BF16)],
        compiler_params=_params(("parallel",)),
        name="ctx_attn",
    )(pc, pc, pc)


def _na_kernel(q_ref, k0, k1, k2, k3, v0, v1, v2, v3, kc_ref, vc_ref, tz_ref, o_ref,
               kall, vall, s_scr, p_scr, *, n_rows):
    i = pl.program_id(1)
    r0 = i * NA_QROWS
    start = jnp.clip(r0 - NA_WIN_R // 2, 0, n_rows - NA_BAND)
    delta = start - r0
    nq, nk = NA_QROWS * GRID_W, NA_BAND * GRID_W
    sub = nk // 4
    n_ctx = kc_ref.shape[1]

    for t, (kr, vr) in enumerate(zip((k0, k1, k2, k3, kc_ref), (v0, v1, v2, v3, vc_ref))):
        rows = slice(sub * t, sub * t + kr.shape[1])
        kall[rows, :] = kr[0]
        for j in range(D_NA // LANES):
            vall[j, rows, 0:LANES] = vr[0, :, LANES * j:LANES * (j + 1)]
    vall[:, :, LANES:] = jnp.ones((D_NA // LANES, nk + n_ctx, LANES), BF16)

    qrow = r0 + lax.broadcasted_iota(jnp.int32, (nq, nk), 0) // GRID_W
    krow = start + lax.broadcasted_iota(jnp.int32, (nq, nk), 1) // GRID_W
    first = jnp.clip(qrow - NA_WIN_R // 2, 0, n_rows - NA_WIN_R)
    row_mask = jnp.where((krow >= first) & (krow < first + NA_WIN_R), 0.0, NEG)
    low = _low_lanes()
    no_bias = jnp.zeros((nq, n_ctx), F32)

    def score(h):
        j, half = divmod(h, 2)
        sl = slice(LANES * j, LANES * (j + 1))
        slabs = []
        for a in range(NA_QROWS):
            pieces = [tz_ref[h, jnp.clip(delta + 2 * bp - a, -8, 7) + 8] for bp in range(NA_BAND // 2)]
            slabs.append(jnp.concatenate(pieces, axis=1))
        bias = jnp.concatenate([jnp.concatenate(slabs, axis=0) + row_mask, no_bias], axis=1)
        return _dot_nt(_one_head(q_ref[0, :, sl], low, half), kall[:, sl]) + bias

    _pair_outputs(o_ref, _attend_heads(H_NA, score, lambda h: vall[h // 2], s_scr, p_scr), low)


def _na(p_na, pc_na, tz):
    b, s, _ = p_na.shape
    l = pc_na.shape[1]
    n_rows = s // GRID_W
    assert n_rows % NA_QROWS == 0 and n_rows >= NA_BAND
    nq, nk = NA_QROWS * GRID_W, NA_BAND * GRID_W
    sub = nk // 4
    rows_per_sub = NA_BAND // 4

    def band(t, blk):
        def idx(i, r):
            start = jnp.clip(r * NA_QROWS - NA_WIN_R // 2, 0, n_rows - NA_BAND)
            return (i, start // rows_per_sub + t, blk)
        return pl.BlockSpec((1, sub, D_NA), idx)

    in_specs = ([pl.BlockSpec((1, nq, D_NA), lambda i, r: (i, r, 0))]
                + [band(t, 1) for t in range(4)] + [band(t, 2) for t in range(4)]
                + [pl.BlockSpec((1, l, D_NA), lambda i, r: (i, 0, 1)),
                   pl.BlockSpec((1, l, D_NA), lambda i, r: (i, 0, 2)),
                   _resident(tz.shape)])
    return pl.pallas_call(
        functools.partial(_na_kernel, n_rows=n_rows),
        grid=(b, n_rows // NA_QROWS),
        in_specs=in_specs,
        out_specs=pl.BlockSpec((1, nq, D_NA), lambda i, r: (i, r, 0)),
        out_shape=jax.ShapeDtypeStruct((b, s, D_NA), BF16),
        scratch_shapes=[pltpu.VMEM((nk + l, D_NA), BF16), pltpu.VMEM((D_NA // LANES, nk + l, 2 * LANES), BF16),
                        pltpu.VMEM((2, nq, nk + l), F32), pltpu.VMEM((2, nq, nk + l), BF16)],
        compiler_params=_params(("parallel", "arbitrary")),
        name="na_latent",
    )(p_na, *([p_na] * 8), pc_na, pc_na, tz)


def _merge_kernel(x_ref, mod_ref, nw_ref, hf_ref, hb_ref, og_ref, mlw_ref, g_ref, na_ref, gq_ref,
                  wg_ref, wml_ref, wna_ref, wgq_ref, wo_ref, o_ref):
    x = x_ref[...]
    d = x.shape[1]
    hx = _modnorm(x, nw_ref[1:2, :], mod_ref[0, 3:4, :], mod_ref[0, 4:5, :]).astype(BF16)
    h = hf_ref[...].astype(F32) + hb_ref[...].astype(F32)
    o_ml = (_head_norm(h, g_ref[...], mlw_ref[...]) * _sigmoid(og_ref[...].astype(F32))).astype(BF16)
    y = None
    for j, (o_br, w_br) in enumerate(((o_ml, wml_ref), (na_ref[...], wna_ref), (gq_ref[...], wgq_ref))):
        part = _sigmoid(_dot(hx, wg_ref[:, d * j:d * (j + 1)])) * _dot(o_br, w_br[...])
        y = part if y is None else y + part
    o_ref[...] = x + mod_ref[0, 5:6, :] * _dot(y.astype(BF16), wo_ref[...])


def _merge(x, mod, nw, hf, hb, p_ml, mlw, gmat, o_na, o_gq, wg, wml, wna, wgq, wo, li,
           *, tiles_per_batch, ctx_row):
    t, d = x.shape
    tm = min(TM, t)
    row = lambda wd, blk=0: pl.BlockSpec((tm, wd), lambda i: (i, blk))
    lead = (li,)
    return pl.pallas_call(
        _merge_kernel,
        grid=(t // tm,),
        in_specs=[row(d),
                  pl.BlockSpec((1, N_MOD, d), _mod_index(tiles_per_batch, ctx_row)),
                  pl.BlockSpec((3, d), lambda i: (0, 0)),
                  row(D_ML), row(D_ML), row(D_ML, 3),
                  _resident((1, D_ML)), _resident(gmat.shape),
                  row(D_NA), row(D_GQ),
                  _resident((d, 3 * d), lead), _resident((D_ML, d), lead), _resident((D_NA, d), lead),
                  _resident((D_GQ, d), lead), _resident((d, d), lead)],
        out_specs=row(d),
        out_shape=jax.ShapeDtypeStruct((t, d), F32),
        compiler_params=_params(("parallel",)),
        name="branch_merge",
    )(x, mod, nw, hf, hb, p_ml, mlw, gmat, o_na, o_gq, wg, wml, wna, wgq, wo)


def _proj_weight(w):
    d = w.shape[-2]
    o = 0
    seg = {}
    for name, width in (("ml_k", D_ML), ("ml_v", D_ML), ("ml_g", 4 * H_ML), ("na_k", D_NA), ("na_v", D_NA),
                        ("gq_k", D_KV), ("gq_v", D_KV), ("ml_q", D_ML), ("ml_o", D_ML), ("na_q", D_NA),
                        ("gq_q", D_GQ), ("br_g", 3 * d)):
        seg[name] = w[..., o:o + width]
        o += width
    gq_q = jnp.concatenate([seg["gq_q"][..., HEAD_DIM * h:HEAD_DIM * (h + 1)] for h in GQ_HEAD_ORDER], axis=-1)
    pad = jnp.zeros(w.shape[:-1] + (LANES - 4 * H_ML,), w.dtype)
    out = jnp.concatenate([seg["ml_q"], seg["ml_k"], seg["ml_v"], seg["ml_o"], seg["ml_g"][..., ML_GATE_ORDER], pad,
                           seg["na_q"], seg["na_k"], seg["na_v"], gq_q, seg["gq_k"], seg["gq_v"]], axis=-1)
    return out, seg["br_g"]


def _rope_tables(n_tok):
    t = jnp.arange(n_tok, dtype=jnp.int32)
    row = (t // GRID_W).astype(F32)
    col = (t % GRID_W).astype(F32)
    n_freq = HEAD_DIM // 4
    inv = ROPE_THETA ** (-jnp.arange(n_freq, dtype=F32) / n_freq)
    ang = jnp.concatenate([row[:, None] * inv, col[:, None] * inv], axis=-1)
    cos, sin = jnp.cos(ang), jnp.sin(ang)
    cos_t = jnp.tile(cos, (1, LANES // (HEAD_DIM // 2)))
    sin_t = jnp.tile(jnp.concatenate([-sin, sin], axis=-1), (1, LANES // HEAD_DIM))
    return cos_t, sin_t


def _na_bias_table(rpb):
    col = np.arange(GRID_W)
    first = np.clip(col - NA_WIN_C // 2, 0, GRID_W - NA_WIN_C)
    in_win = (col[None, :] >= first[:, None]) & (col[None, :] < first[:, None] + NA_WIN_C)
    side = GRID_W - NA_WIN_C
    rows = jnp.pad(rpb, ((0, 0), (1, 1), (side, side)))
    row_ok = np.zeros((2 * NA_WIN_R + 1,), bool)
    row_ok[1:-1] = True
    full = jnp.stack([rows[:, :, GRID_W - 1 - qc:2 * GRID_W - 1 - qc] for qc in range(GRID_W)], axis=2)
    full = full * float(np.log2(np.e))
    full = jnp.where(jnp.asarray(in_win[None, None] & row_ok[None, :, None, None]), full, NEG)
    return jnp.concatenate([full[:, :-1], full[:, 1:]], axis=-1).astype(F32)


def kernel(x, c, ctx, c_ctx, ada_w, ada_b, norm_w, ffn_w_in, ffn_w_out, mix_w_in, ml_gate_b, ml_norm_w,
           na_qk_w, na_rpb, gq_qk_w, w_br_ml, w_br_na, w_br_gq, w_out):
    b, s, d = x.shape
    l = ctx.shape[1]
    depth = ada_w.shape[0]
    assert b < MOD_ROWS and s % TM == 0 and (b * l) % min(TM, b * l) == 0
    ctx_row = b
    tiles_per_batch = s // TM

    cvec = jnp.zeros((MOD_ROWS, d), F32).at[:b].set(c).at[b].set(c_ctx)
    mod = _ada(cvec, ada_w, ada_b).reshape(depth, MOD_ROWS, N_MOD, d)

    lane = np.arange(2 * LANES)
    gmat = jnp.asarray((lane[:, None] // HEAD_DIM) == (lane[None, :] // HEAD_DIM), BF16)
    idx = np.arange(ML_CHUNK)
    tri = jnp.asarray(np.stack([idx[:, None] >= idx[None, :], idx[:, None] <= idx[None, :]]), BF16)
    rope_tabs = _rope_tables(s)

    w_in, w_o = ffn_w_in.astype(BF16), ffn_w_out.astype(BF16)
    w_proj, wg = _proj_weight(mix_w_in.astype(BF16))
    wml, wna, wo = w_br_ml.astype(BF16), w_br_na.astype(BF16), w_out.astype(BF16)
    wgq = jnp.concatenate([w_br_gq[:, HEAD_DIM * h:HEAD_DIM * (h + 1)] for h in GQ_HEAD_ORDER], axis=1).astype(BF16)

    xl = x.reshape(b * s, d)
    xc = ctx.reshape(b * l, d)
    lat = dict(tiles_per_batch=tiles_per_batch, ctx_row=ctx_row)
    con = dict(tiles_per_batch=None, ctx_row=ctx_row)
    for li in range(depth):
        ctx_out = li < depth - 1
        qkw = jnp.zeros((8, 2 * D_NA), F32)
        qkw = qkw.at[0, :D_NA].set(jnp.tile(na_qk_w[li, 0], H_NA) * Q_PRESCALE)
        qkw = qkw.at[0, D_NA:].set(jnp.tile(na_qk_w[li, 1], H_NA))
        qkw = qkw.at[1, :D_GQ].set(jnp.tile(gq_qk_w[li, 0], H_GQ) * Q_PRESCALE)
        qkw = qkw.at[1, D_GQ:D_GQ + D_KV].set(jnp.tile(gq_qk_w[li, 1], H_KV))
        gate_b = jnp.zeros((1, LANES), F32).at[0, :4 * H_ML].set(ml_gate_b[li][ML_GATE_ORDER])
        mlw = ml_norm_w[li].reshape(1, D_ML)
        tz = _na_bias_table(na_rpb[li])
        m, nw = mod[li], norm_w[li]

        xl = _ffn(xl, m, nw, w_in, w_o, (li, 0), k0=0, nrm=0, **lat)
        xc = _ffn(xc, m, nw, w_in, w_o, (li, 0), k0=0, nrm=0, **con)

        p_ml, p_mlg, p_na, p_gq = _proj(xl, m, nw, w_proj, li, gmat, qkw, rope_tabs, **lat)
        pc_ml, pc_mlg, pc_na, pc_gq = _proj(xc, m, nw, w_proj, li, gmat, qkw, None, **con)
        seq = lambda a: a.reshape(b, s, a.shape[-1])
        cseq = lambda a: a.reshape(b, l, a.shape[-1])

        hf, hb, hcf, hcb = _mlstm(seq(p_ml), seq(p_mlg), cseq(pc_ml), cseq(pc_mlg), tri, gate_b)
        o_na = _na(seq(p_na), cseq(pc_na), tz)
        o_gq = _gqa(seq(p_gq), cseq(pc_gq))
        flat = lambda a: a.reshape(-1, a.shape[-1])
        xl = _merge(xl, m, nw, flat(hf), flat(hb), p_ml, mlw, gmat, flat(o_na), flat(o_gq),
                    wg, wml, wna, wgq, wo, li, **lat)
        xl = _ffn(xl, m, nw, w_in, w_o, (li, 1), k0=6, nrm=2, **lat)
        if ctx_out:
            co_na = _ctx_attn(cseq(pc_na), qw=D_NA, kw=D_NA, shared_kv=False)
            co_gq = _ctx_attn(cseq(pc_gq), qw=D_GQ, kw=D_KV, shared_kv=True)
            xc = _merge(xc, m, nw, flat(hcf), flat(hcb), pc_ml, mlw, gmat, flat(co_na), flat(co_gq),
                        wg, wml, wna, wgq, wo, li, **con)
            xc = _ffn(xc, m, nw, w_in, w_o, (li, 1), k0=6, nrm=2, **con)
    return xl.reshape(b, s, d)
```

```python
import functools

import numpy as np
import jax
import jax.numpy as jnp
from jax import lax
from jax.experimental import pallas as pl
from jax.experimental.pallas import tpu as pltpu

F32 = jnp.float32
BF16 = jnp.bfloat16

HEAD_DIM = 64
LANES = 128
H_ML, H_NA, H_GQ, H_KV = 4, 6, 6, 2
D_ML, D_NA, D_GQ, D_KV = 256, 384, 384, 128
GRID_W = 64
NA_WIN_R, NA_WIN_C = 8, 16
ROPE_THETA = 10000.0
EPS = 1e-6
N_MOD = 9
ATTN_SCALE = HEAD_DIM ** -0.5
NEG = -1e30

ML_CHUNK = 256
NA_QROWS = 4
NA_BAND = 12
NA_SUBS = 3
TM = 512
GQ_TQ = 256
FFN_CHUNK = 768
Q_PRESCALE = ATTN_SCALE * float(np.log2(np.e))
MOD_ROWS = 16
VMEM_LIMIT = 56 * 1024 * 1024

C_ML, C_MLG, C_NA, C_GQ, C_END = 0, 1024, 1152, 2304, 2944
N_PROJ = C_END
GQ_HEAD_ORDER = (0, 3, 1, 4, 2, 5)
ML_GATE_ORDER = np.array([0, 1, 2, 3, 8, 9, 10, 11, 4, 5, 6, 7, 12, 13, 14, 15])


def _dot(a, b):
    return jnp.dot(a, b, preferred_element_type=F32)


def _dot_nt(a, b):
    return lax.dot_general(a, b, (((1,), (1,)), ((), ())), preferred_element_type=F32)


def _sigmoid(x):
    return 1.0 / (1.0 + jnp.exp(-x))


def _log_sigmoid(x):
    return jnp.minimum(x, 0.0) - jnp.log1p(jnp.exp(-jnp.abs(x)))


def _split_bf16(x):
    hi = x.astype(BF16)
    lo = (x - hi.astype(F32)).astype(BF16)
    return hi, lo


def _modnorm(x, nw, shift, scale):
    ms = jnp.mean(x * x, axis=-1, keepdims=True)
    return (x * lax.rsqrt(ms + EPS) * nw) * (1.0 + scale) + shift


def _head_norm(t, gmat, wrow):
    ss = _dot((t * t).astype(BF16), gmat)
    return t * lax.rsqrt(ss * (1.0 / HEAD_DIM) + EPS) * wrow


def _resident(shape, lead=()):
    return pl.BlockSpec((None,) * len(lead) + tuple(shape), lambda *_: tuple(lead) + (0,) * len(shape),
                        pipeline_mode=pl.Buffered(1))


def _params(sem):
    return pltpu.CompilerParams(dimension_semantics=sem, vmem_limit_bytes=VMEM_LIMIT)


def _ada_kernel(c_ref, w_ref, b_ref, o_ref):
    c = c_ref[...]
    s = (c * _sigmoid(c)).astype(BF16)
    o_ref[0] = _dot(s, w_ref[0].astype(BF16)) + b_ref[0]


def _ada(cvec, ada_w, ada_b):
    depth, d, n = ada_w.shape
    tn = n // 8
    return pl.pallas_call(
        _ada_kernel,
        grid=(depth, n // tn),
        in_specs=[pl.BlockSpec((MOD_ROWS, d), lambda l, j: (0, 0)),
                  pl.BlockSpec((1, d, tn), lambda l, j: (l, 0, j)),
                  pl.BlockSpec((1, 1, tn), lambda l, j: (l, 0, j))],
        out_specs=pl.BlockSpec((1, MOD_ROWS, tn), lambda l, j: (l, 0, j)),
        out_shape=jax.ShapeDtypeStruct((depth, MOD_ROWS, n), F32),
        compiler_params=_params(("arbitrary", "arbitrary")),
        name="ada_mod",
    )(cvec, ada_w, ada_b.reshape(depth, 1, n))


def _mod_index(tiles_per_batch, ctx_row):
    if tiles_per_batch is None:
        return lambda i, *_: (ctx_row, 0, 0)
    return lambda i, *_: (i // tiles_per_batch, 0, 0)


def _ffn_kernel(x_ref, mod_ref, nw_ref, wi_ref, wo_ref, o_ref, *, k0, nrm, chunks):
    x = x_ref[...]
    h = _modnorm(x, nw_ref[nrm:nrm + 1, :], mod_ref[0, k0:k0 + 1, :], mod_ref[0, k0 + 1:k0 + 2, :]).astype(BF16)
    dff = wo_ref.shape[0]
    y = None
    for c0, c1 in chunks:
        g = _dot(h, wi_ref[:, c0:c1])
        u = _dot(h, wi_ref[:, dff + c0:dff + c1])
        part = _dot((g * _sigmoid(g) * u).astype(BF16), wo_ref[c0:c1, :])
        y = part if y is None else y + part
    o_ref[...] = x + (0.5 * mod_ref[0, k0 + 2:k0 + 3, :]) * y


def _ffn(x, mod, nw, w_in, w_out, lead, *, k0, nrm, tiles_per_batch, ctx_row):
    t, d = x.shape
    dff = w_out.shape[-2]
    tm = min(TM, t)
    edges = list(range(0, dff, FFN_CHUNK)) + [dff]
    chunks = tuple(zip(edges[:-1], edges[1:]))
    kern = functools.partial(_ffn_kernel, k0=k0, nrm=nrm, chunks=chunks)
    return pl.pallas_call(
        kern,
        grid=(t // tm,),
        in_specs=[pl.BlockSpec((tm, d), lambda i: (i, 0)),
                  pl.BlockSpec((1, N_MOD, d), _mod_index(tiles_per_batch, ctx_row)),
                  pl.BlockSpec((3, d), lambda i: (0, 0)),
                  _resident((d, 2 * dff), lead),
                  _resident((dff, d), lead)],
        out_specs=pl.BlockSpec((tm, d), lambda i: (i, 0)),
        out_shape=jax.ShapeDtypeStruct((t, d), F32),
        compiler_params=_params(("parallel",)),
        name="ffn_swiglu",
    )(x, mod, nw, w_in, w_out)


def _proj_kernel(*refs, rope):
    if rope:
        (x_ref, mod_ref, nw_ref, w_ref, g_ref, qkw_ref, cos_ref, sin_ref,
         ml_ref, mlg_ref, na_ref, gq_ref) = refs
    else:
        (x_ref, mod_ref, nw_ref, w_ref, g_ref, qkw_ref,
         ml_ref, mlg_ref, na_ref, gq_ref) = refs
    h = _modnorm(x_ref[...], nw_ref[1:2, :], mod_ref[0, 3:4, :], mod_ref[0, 4:5, :]).astype(BF16)
    gmat = g_ref[...]

    ml_ref[...] = _dot(h, w_ref[:, C_ML:C_MLG]).astype(BF16)
    mlg_ref[...] = _dot(h, w_ref[:, C_MLG:C_NA])

    na = _dot(h, w_ref[:, C_NA:C_GQ])
    for j in range(3):
        sl = slice(2 * LANES * j, 2 * LANES * (j + 1))
        na_ref[:, sl] = _head_norm(na[:, sl], gmat, qkw_ref[0:1, sl]).astype(BF16)
    na_ref[:, 2 * D_NA:] = na[:, 2 * D_NA:].astype(BF16)

    gq = _dot(h, w_ref[:, C_GQ:C_END])
    if rope:
        lane = lax.broadcasted_iota(jnp.int32, (1, LANES), 1)
        first_half = (lane % HEAD_DIM) < (HEAD_DIM // 2)
        cos = cos_ref[...]
        sin = sin_ref[...]
    for j in range(2):
        t2 = _head_norm(gq[:, 2 * LANES * j:2 * LANES * (j + 1)], gmat, qkw_ref[1:2, 2 * LANES * j:2 * LANES * (j + 1)])
        for half in range(2):
            t = t2[:, LANES * half:LANES * (half + 1)]
            if rope:
                rot = jnp.where(first_half, pltpu.roll(t, LANES - HEAD_DIM // 2, axis=1),
                                pltpu.roll(t, HEAD_DIM // 2, axis=1))
                t = t * cos + rot * sin
            gq_ref[:, LANES * (2 * j + half):LANES * (2 * j + half + 1)] = t.astype(BF16)
    gq_ref[:, D_GQ + D_KV:] = gq[:, D_GQ + D_KV:].astype(BF16)


def _proj(x, mod, nw, w, li, gmat, qkw, rope_tabs, *, tiles_per_batch, ctx_row):
    t, d = x.shape
    tm = min(TM, t)
    rope = rope_tabs is not None
    in_specs = [pl.BlockSpec((tm, d), lambda i: (i, 0)),
                pl.BlockSpec((1, N_MOD, d), _mod_index(tiles_per_batch, ctx_row)),
                pl.BlockSpec((3, d), lambda i: (0, 0)),
                _resident((d, N_PROJ), (li,)),
                _resident(gmat.shape),
                _resident(qkw.shape)]
    args = [x, mod, nw, w, gmat, qkw]
    if rope:
        in_specs += [pl.BlockSpec((tm, LANES), lambda i: (i % tiles_per_batch, 0))] * 2
        args += list(rope_tabs)
    widths = (1024, LANES, 3 * D_NA, D_GQ + 2 * D_KV)
    dtypes = (BF16, F32, BF16, BF16)
    return pl.pallas_call(
        functools.partial(_proj_kernel, rope=rope),
        grid=(t // tm,),
        in_specs=in_specs,
        out_specs=[pl.BlockSpec((tm, wd), lambda i: (i, 0)) for wd in widths],
        out_shape=[jax.ShapeDtypeStruct((t, wd), dt) for wd, dt in zip(widths, dtypes)],
        compiler_params=_params(("parallel",)),
        name="mix_in_proj",
    )(*args)


def _ml_prep(d, g_ref, tri_ref, bias_ref):
    log2e = float(np.log2(np.e))
    lc = g_ref.shape[1]
    gates = g_ref[0] + bias_ref[...]
    gates_t = gates.T
    ig_t = gates_t[0:8] * log2e
    hi, lo = _split_bf16(_log_sigmoid(gates_t[0:16]) * log2e)
    ones = jnp.ones((lc, lc), BF16)
    b_t = (_dot(hi, tri_ref[1 - d]) + _dot(lo, tri_ref[1 - d]))[8:16]
    btot_t = (_dot(hi, ones) + _dot(lo, ones))[8:16]
    lf_al = pltpu.roll(_log_sigmoid(gates) * log2e, LANES - 8, axis=1)
    hi, lo = _split_bf16(lf_al)
    c_mat = gates * log2e - (_dot(tri_ref[d], hi) + _dot(tri_ref[d], lo))
    return ig_t, b_t, btot_t, c_mat


def _ml_head(d, h, prep, m_ref):
    ig_t, b_t, btot_t, c_mat = prep
    lc = c_mat.shape[0]
    r = 4 * d + h
    row = lax.broadcasted_iota(jnp.int32, (lc, lc), 0)
    col = lax.broadcasted_iota(jnp.int32, (lc, lc), 1)
    visible = (row <= col) if d == 0 else (row >= col)
    ig, b, b_tot = ig_t[r:r + 1], b_t[r:r + 1], btot_t[r:r + 1]
    m_prev = m_ref[r:r + 1, :]
    w_end = b_tot - b + ig
    m_new = jnp.maximum(b_tot + m_prev, jnp.max(w_end, axis=1, keepdims=True))
    a = jnp.exp2(w_end - m_new)
    decay = jnp.exp2(b_tot + m_prev - m_new)
    m_inter = b + m_prev
    logw = jnp.where(visible, c_mat[:, r:r + 1] + b, NEG)
    m_j = jnp.maximum(m_inter, jnp.max(logw, axis=0, keepdims=True))
    w = jnp.exp2(logw - m_j)
    m_ref[r:r + 1, :] = m_new
    return w, a, jnp.exp2(m_inter - m_j), jnp.exp2(-m_j), decay


def _ml_pair(d, p, head_a, head_b, q_ref, k_ref, v_ref, o_ref, st_ref):
    (w_a, a_a, g_a, fl_a, dec_a), (w_b, a_b, g_b, fl_b, dec_b) = head_a, head_b
    lc = q_ref.shape[1]
    low = _low_lanes()
    sl = slice(LANES * p, LANES * (p + 1))
    q = q_ref[0, :, sl]
    k = k_ref[0, :, sl] * ATTN_SCALE
    vt = jnp.concatenate([v_ref[0, :, sl].astype(F32).T, jnp.ones((LANES, lc), F32)], axis=0)
    head_row = (lax.broadcasted_iota(jnp.int32, (2 * LANES, 1), 0) % LANES) < HEAD_DIM
    vt16 = vt.astype(BF16)
    r_a = _dot(vt16, (_dot_nt(k, _one_head(q, low, 0)) * w_a).astype(BF16))
    r_b = _dot(vt16, (_dot_nt(k, _one_head(q, low, 1)) * w_b).astype(BF16))
    state = st_ref[d, p]
    r_i = _dot_nt(state.astype(BF16), q)
    r = jnp.where(head_row, r_a, r_b) + jnp.where(head_row, g_a, g_b) * r_i
    num, den = r[:LANES], r[LANES:]
    h_t = num / jnp.maximum(jnp.abs(den), jnp.where(head_row[:LANES], fl_a, fl_b))
    o_ref[0, :, sl] = h_t.T.astype(BF16)

    upd = _dot((vt * jnp.where(head_row, a_a, a_b)).astype(BF16), k)
    same_head = head_row == low
    dec = jnp.where(head_row, dec_a[:, :LANES], dec_b[:, :LANES])
    st_ref[d, p] = dec * state + jnp.where(same_head, upd, 0.0)


def _ml_step(fwd, bwd, tri_ref, bias_ref, st_ref, m_ref):
    dirs = (fwd, bwd)
    preps = [_ml_prep(d, refs[3], tri_ref, bias_ref) for d, refs in enumerate(dirs)]
    heads = [[_ml_head(d, h, preps[d], m_ref) for h in range(H_ML)] for d in range(2)]
    for p in range(H_ML // 2):
        for d, (q_ref, k_ref, v_ref, _, o_ref) in enumerate(dirs):
            _ml_pair(d, p, heads[d][2 * p], heads[d][2 * p + 1], q_ref, k_ref, v_ref, o_ref, st_ref)


def _mlstm_kernel(qf, kf, vf, gf, qb, kb, vb, gb, qc, kc, vc, gc, tri_ref, bias_ref,
                  hf_ref, hb_ref, hcf_ref, hcb_ref, st_ref, m_ref):
    c = pl.program_id(1)

    @pl.when(c == 0)
    def _():
        st_ref[...] = jnp.zeros_like(st_ref)
        m_ref[...] = jnp.zeros_like(m_ref)
        _ml_step((qc, kc, vc, gc, hcf_ref), (qc, kc, vc, gc, hcb_ref), tri_ref, bias_ref, st_ref, m_ref)

    @pl.when(c > 0)
    def _():
        _ml_step((qf, kf, vf, gf, hf_ref), (qb, kb, vb, gb, hb_ref), tri_ref, bias_ref, st_ref, m_ref)


def _mlstm(p_ml, p_mlg, pc_ml, pc_mlg, tri, bias):
    b, s, _ = p_ml.shape
    l = pc_ml.shape[1]
    lc = ML_CHUNK
    assert l == lc and s % lc == 0
    nl = s // lc
    fwd = lambda c: jnp.maximum(c - 1, 0)
    bwd = lambda c: nl - 1 - jnp.maximum(c - 1, 0)

    def lat(idx, blk, width):
        return pl.BlockSpec((1, lc, width), lambda i, c: (i, idx(c), blk))

    def ctx(blk, width):
        return pl.BlockSpec((1, lc, width), lambda i, c: (i, 0, blk))

    in_specs = ([lat(fwd, 0, D_ML), lat(fwd, 1, D_ML), lat(fwd, 2, D_ML), lat(fwd, 0, LANES)]
                + [lat(bwd, 0, D_ML), lat(bwd, 1, D_ML), lat(bwd, 2, D_ML), lat(bwd, 0, LANES)]
                + [ctx(0, D_ML), ctx(1, D_ML), ctx(2, D_ML), ctx(0, LANES)]
                + [_resident((2, lc, lc)), _resident((1, LANES))])
    out_specs = [lat(fwd, 0, D_ML), lat(bwd, 0, D_ML), ctx(0, D_ML), ctx(0, D_ML)]
    out_shape = [jax.ShapeDtypeStruct((b, s, D_ML), BF16)] * 2 + [jax.ShapeDtypeStruct((b, l, D_ML), BF16)] * 2
    return pl.pallas_call(
        _mlstm_kernel,
        grid=(b, nl + 1),
        in_specs=in_specs,
        out_specs=out_specs,
        out_shape=out_shape,
        scratch_shapes=[pltpu.VMEM((2, H_ML // 2, 2 * LANES, LANES), F32), pltpu.VMEM((2 * H_ML, lc), F32)],
        compiler_params=_params(("parallel", "arbitrary")),
        name="mlstm_bidir",
    )(p_ml, p_ml, p_ml, p_mlg, p_ml, p_ml, p_ml, p_mlg, pc_ml, pc_ml, pc_ml, pc_mlg, tri, bias)


def _attend_heads(n_heads, score_fn, value_fn, s_scr, p_scr):
    s_scr[0] = score_fn(0)
    outs = []
    for i in range(n_heads):
        if i + 1 < n_heads:
            s_scr[(i + 1) % 2] = score_fn(i + 1)
        s = s_scr[i % 2]
        p_scr[i % 2] = jnp.exp2(s - jnp.max(s, axis=1, keepdims=True)).astype(BF16)
        r = _dot(p_scr[i % 2], value_fn(i))
        outs.append(r[:, :LANES] / r[:, LANES:])
    return outs


def _low_lanes():
    return lax.broadcasted_iota(jnp.int32, (1, LANES), 1) < HEAD_DIM


def _one_head(q, low, half):
    zero = jnp.zeros_like(q)
    return jnp.where(low, q, zero) if half == 0 else jnp.where(low, zero, q)


def _pair_outputs(o_ref, outs, low):
    for j in range(len(outs) // 2):
        o_ref[0, :, LANES * j:LANES * (j + 1)] = jnp.where(low, outs[2 * j], outs[2 * j + 1]).astype(BF16)


def _gqa_kernel(q_ref, k_ref, v_ref, kc_ref, vc_ref, o_ref, kall, vall, s_scr, p_scr):
    n_lat = k_ref.shape[1]

    @pl.when(pl.program_id(1) == 0)
    def _():
        kall[0:n_lat, :] = k_ref[0]
        kall[n_lat:, :] = kc_ref[0]
        vall[0:n_lat, 0:LANES] = v_ref[0]
        vall[n_lat:, 0:LANES] = vc_ref[0]
        vall[:, LANES:] = jnp.ones((vall.shape[0], LANES), BF16)

    low = _low_lanes()

    def score(i):
        j, half = divmod(i, 2)
        return _dot_nt(_one_head(q_ref[0, :, LANES * j:LANES * (j + 1)], low, half), kall[...])

    _pair_outputs(o_ref, _attend_heads(H_GQ, score, lambda i: vall[...], s_scr, p_scr), low)


def _gqa(p_gq, pc_gq):
    b, s, _ = p_gq.shape
    l = pc_gq.shape[1]
    tq = min(GQ_TQ, s)
    kblk, vblk = D_GQ // LANES, D_GQ // LANES + 1
    return pl.pallas_call(
        _gqa_kernel,
        grid=(b, s // tq),
        in_specs=[pl.BlockSpec((1, tq, D_GQ), lambda i, t: (i, t, 0)),
                  pl.BlockSpec((1, s, LANES), lambda i, t: (i, 0, kblk)),
                  pl.BlockSpec((1, s, LANES), lambda i, t: (i, 0, vblk)),
                  pl.BlockSpec((1, l, LANES), lambda i, t: (i, 0, kblk)),
                  pl.BlockSpec((1, l, LANES), lambda i, t: (i, 0, vblk))],
        out_specs=pl.BlockSpec((1, tq, D_GQ), lambda i, t: (i, t, 0)),
        out_shape=jax.ShapeDtypeStruct((b, s, D_GQ), BF16),
        scratch_shapes=[pltpu.VMEM((s + l, LANES), BF16), pltpu.VMEM((s + l, 2 * LANES), BF16),
                        pltpu.VMEM((2, tq, s + l), F32), pltpu.VMEM((2, tq, s + l), BF16)],
        compiler_params=_params(("parallel", "arbitrary")),
        name="gqa_latent",
    )(p_gq, p_gq, p_gq, pc_gq, pc_gq)


def _ctx_attn_kernel(q_ref, k_ref, v_ref, o_ref, s_scr, p_scr, *, shared_kv):
    low = _low_lanes()
    ones = jnp.ones((v_ref.shape[1], LANES), BF16)

    def kv_lanes(i):
        return slice(0, LANES) if shared_kv else slice(LANES * (i // 2), LANES * (i // 2 + 1))

    def score(i):
        j, half = divmod(i, 2)
        return _dot_nt(_one_head(q_ref[0, :, LANES * j:LANES * (j + 1)], low, half), k_ref[0, :, kv_lanes(i)])

    def value(i):
        return jnp.concatenate([v_ref[0, :, kv_lanes(i)], ones], axis=1)

    n_heads = 2 * (q_ref.shape[2] // LANES)
    _pair_outputs(o_ref, _attend_heads(n_heads, score, value, s_scr, p_scr), low)


def _ctx_attn(pc, *, qw, kw, shared_kv):
    b, l, _ = pc.shape
    kb = qw // kw
    return pl.pallas_call(
        functools.partial(_ctx_attn_kernel, shared_kv=shared_kv),
        grid=(b,),
        in_specs=[pl.BlockSpec((1, l, qw), lambda i: (i, 0, 0)),
                  pl.BlockSpec((1, l, kw), lambda i: (i, 0, kb)),
                  pl.BlockSpec((1, l, kw), lambda i: (i, 0, kb + 1))],
        out_specs=pl.BlockSpec((1, l, qw), lambda i: (i, 0, 0)),
        out_shape=jax.ShapeDtypeStruct((b, l, qw), BF16),
        scratch_shapes=[pltpu.VMEM((2, l, l), F32), pltpu.VMEM((2, l, l), BF16)],
        compiler_params=_params(("parallel",)),
        name="ctx_attn",
    )(pc, pc, pc)


def _na_kernel(*refs, n_rows):
    q_ref, k_subs, v_subs = refs[0], refs[1:1 + NA_SUBS], refs[1 + NA_SUBS:1 + 2 * NA_SUBS]
    kc_ref, vc_ref, tz_ref, o_ref, kall, vall, s_scr, p_scr = refs[1 + 2 * NA_SUBS:]
    i = pl.program_id(1)
    r0 = i * NA_QROWS
    start = jnp.clip(r0 - NA_WIN_R // 2, 0, n_rows - NA_BAND)
    delta = start - r0
    nq, nk = NA_QROWS * GRID_W, NA_BAND * GRID_W
    sub = nk // NA_SUBS
    n_ctx = kc_ref.shape[1]

    for t, (kr, vr) in enumerate(zip(k_subs + (kc_ref,), v_subs + (vc_ref,))):
        rows = slice(sub * t, sub * t + kr.shape[1])
        kall[rows, :] = kr[0]
        for j in range(D_NA // LANES):
            vall[j, rows, 0:LANES] = vr[0, :, LANES * j:LANES * (j + 1)]
    vall[:, :, LANES:] = jnp.ones((D_NA // LANES, nk + n_ctx, LANES), BF16)

    qrow = r0 + lax.broadcasted_iota(jnp.int32, (nq, nk), 0) // GRID_W
    krow = start + lax.broadcasted_iota(jnp.int32, (nq, nk), 1) // GRID_W
    first = jnp.clip(qrow - NA_WIN_R // 2, 0, n_rows - NA_WIN_R)
    row_mask = jnp.where((krow >= first) & (krow < first + NA_WIN_R), 0.0, NEG)
    low = _low_lanes()
    no_bias = jnp.zeros((nq, n_ctx), F32)

    def score(h):
        j, half = divmod(h, 2)
        sl = slice(LANES * j, LANES * (j + 1))
        slabs = []
        for a in range(NA_QROWS):
            pieces = [tz_ref[h, jnp.clip(delta + 2 * bp - a, -8, 7) + 8] for bp in range(NA_BAND // 2)]
            slabs.append(jnp.concatenate(pieces, axis=1))
        bias = jnp.concatenate([jnp.concatenate(slabs, axis=0) + row_mask, no_bias], axis=1)
        return _dot_nt(_one_head(q_ref[0, :, sl], low, half), kall[:, sl]) + bias

    _pair_outputs(o_ref, _attend_heads(H_NA, score, lambda h: vall[h // 2], s_scr, p_scr), low)


def _na(p_na, pc_na, tz):
    b, s, _ = p_na.shape
    l = pc_na.shape[1]
    n_rows = s // GRID_W
    assert n_rows % NA_QROWS == 0 and n_rows >= NA_BAND
    nq, nk = NA_QROWS * GRID_W, NA_BAND * GRID_W
    sub = nk // NA_SUBS
    rows_per_sub = NA_BAND // NA_SUBS
    assert all(v % rows_per_sub == 0 for v in (NA_QROWS, NA_WIN_R // 2, n_rows - NA_BAND))

    def band(t, blk):
        def idx(i, r):
            start = jnp.clip(r * NA_QROWS - NA_WIN_R // 2, 0, n_rows - NA_BAND)
            return (i, start // rows_per_sub + t, blk)
        return pl.BlockSpec((1, sub, D_NA), idx)

    in_specs = ([pl.BlockSpec((1, nq, D_NA), lambda i, r: (i, r, 0))]
                + [band(t, 1) for t in range(NA_SUBS)] + [band(t, 2) for t in range(NA_SUBS)]
                + [pl.BlockSpec((1, l, D_NA), lambda i, r: (i, 0, 1)),
                   pl.BlockSpec((1, l, D_NA), lambda i, r: (i, 0, 2)),
                   _resident(tz.shape)])
    return pl.pallas_call(
        functools.partial(_na_kernel, n_rows=n_rows),
        grid=(b, n_rows // NA_QROWS),
        in_specs=in_specs,
        out_specs=pl.BlockSpec((1, nq, D_NA), lambda i, r: (i, r, 0)),
        out_shape=jax.ShapeDtypeStruct((b, s, D_NA), BF16),
        scratch_shapes=[pltpu.VMEM((nk + l, D_NA), BF16), pltpu.VMEM((D_NA // LANES, nk + l, 2 * LANES), BF16),
                        pltpu.VMEM((2, nq, nk + l), F32), pltpu.VMEM((2, nq, nk + l), BF16)],
        compiler_params=_params(("parallel", "arbitrary")),
        name="na_latent",
    )(p_na, *([p_na] * (2 * NA_SUBS)), pc_na, pc_na, tz)


def _merge_kernel(x_ref, mod_ref, nw_ref, hf_ref, hb_ref, og_ref, mlw_ref, g_ref, na_ref, gq_ref,
                  wg_ref, wml_ref, wna_ref, wgq_ref, wo_ref, o_ref):
    x = x_ref[...]
    d = x.shape[1]
    hx = _modnorm(x, nw_ref[1:2, :], mod_ref[0, 3:4, :], mod_ref[0, 4:5, :]).astype(BF16)
    h = hf_ref[...].astype(F32) + hb_ref[...].astype(F32)
    o_ml = (_head_norm(h, g_ref[...], mlw_ref[...]) * _sigmoid(og_ref[...].astype(F32))).astype(BF16)
    y = None
    for j, (o_br, w_br) in enumerate(((o_ml, wml_ref), (na_ref[...], wna_ref), (gq_ref[...], wgq_ref))):
        part = _sigmoid(_dot(hx, wg_ref[:, d * j:d * (j + 1)])) * _dot(o_br, w_br[...])
        y = part if y is None else y + part
    o_ref[...] = x + mod_ref[0, 5:6, :] * _dot(y.astype(BF16), wo_ref[...])


def _merge(x, mod, nw, hf, hb, p_ml, mlw, gmat, o_na, o_gq, wg, wml, wna, wgq, wo, li,
           *, tiles_per_batch, ctx_row):
    t, d = x.shape
    tm = min(TM, t)
    row = lambda wd, blk=0: pl.BlockSpec((tm, wd), lambda i: (i, blk))
    lead = (li,)
    return pl.pallas_call(
        _merge_kernel,
        grid=(t // tm,),
        in_specs=[row(d),
                  pl.BlockSpec((1, N_MOD, d), _mod_index(tiles_per_batch, ctx_row)),
                  pl.BlockSpec((3, d), lambda i: (0, 0)),
                  row(D_ML), row(D_ML), row(D_ML, 3),
                  _resident((1, D_ML)), _resident(gmat.shape),
                  row(D_NA), row(D_GQ),
                  _resident((d, 3 * d), lead), _resident((D_ML, d), lead), _resident((D_NA, d), lead),
                  _resident((D_GQ, d), lead), _resident((d, d), lead)],
        out_specs=row(d),
        out_shape=jax.ShapeDtypeStruct((t, d), F32),
        compiler_params=_params(("parallel",)),
        name="branch_merge",
    )(x, mod, nw, hf, hb, p_ml, mlw, gmat, o_na, o_gq, wg, wml, wna, wgq, wo)


def _proj_weight(w):
    d = w.shape[-2]
    o = 0
    seg = {}
    for name, width in (("ml_k", D_ML), ("ml_v", D_ML), ("ml_g", 4 * H_ML), ("na_k", D_NA), ("na_v", D_NA),
                        ("gq_k", D_KV), ("gq_v", D_KV), ("ml_q", D_ML), ("ml_o", D_ML), ("na_q", D_NA),
                        ("gq_q", D_GQ), ("br_g", 3 * d)):
        seg[name] = w[..., o:o + width]
        o += width
    gq_q = jnp.concatenate([seg["gq_q"][..., HEAD_DIM * h:HEAD_DIM * (h + 1)] for h in GQ_HEAD_ORDER], axis=-1)
    pad = jnp.zeros(w.shape[:-1] + (LANES - 4 * H_ML,), w.dtype)
    out = jnp.concatenate([seg["ml_q"], seg["ml_k"], seg["ml_v"], seg["ml_o"], seg["ml_g"][..., ML_GATE_ORDER], pad,
                           seg["na_q"], seg["na_k"], seg["na_v"], gq_q, seg["gq_k"], seg["gq_v"]], axis=-1)
    return out, seg["br_g"]


def _rope_tables(n_tok):
    t = np.arange(n_tok)
    row = (t // GRID_W).astype(np.float64)
    col = (t % GRID_W).astype(np.float64)
    n_freq = HEAD_DIM // 4
    inv = ROPE_THETA ** (-np.arange(n_freq, dtype=np.float64) / n_freq)
    ang = np.concatenate([row[:, None] * inv, col[:, None] * inv], axis=-1)
    cos, sin = np.cos(ang), np.sin(ang)
    cos_t = np.tile(cos, (1, LANES // (HEAD_DIM // 2)))
    sin_t = np.tile(np.concatenate([-sin, sin], axis=-1), (1, LANES // HEAD_DIM))
    return jnp.asarray(cos_t, F32), jnp.asarray(sin_t, F32)


def _na_bias_table(rpb):
    col = np.arange(GRID_W)
    first = np.clip(col - NA_WIN_C // 2, 0, GRID_W - NA_WIN_C)
    in_win = (col[None, :] >= first[:, None]) & (col[None, :] < first[:, None] + NA_WIN_C)
    side = GRID_W - NA_WIN_C
    rows = jnp.pad(rpb, ((0, 0), (1, 1), (side, side)))
    row_ok = np.zeros((2 * NA_WIN_R + 1,), bool)
    row_ok[1:-1] = True
    full = jnp.stack([rows[:, :, GRID_W - 1 - qc:2 * GRID_W - 1 - qc] for qc in range(GRID_W)], axis=2)
    full = full * float(np.log2(np.e))
    full = jnp.where(jnp.asarray(in_win[None, None] & row_ok[None, :, None, None]), full, NEG)
    return jnp.concatenate([full[:, :-1], full[:, 1:]], axis=-1).astype(F32)


def kernel(x, c, ctx, c_ctx, ada_w, ada_b, norm_w, ffn_w_in, ffn_w_out, mix_w_in, ml_gate_b, ml_norm_w,
           na_qk_w, na_rpb, gq_qk_w, w_br_ml, w_br_na, w_br_gq, w_out):
    b, s, d = x.shape
    l = ctx.shape[1]
    depth = ada_w.shape[0]
    assert b < MOD_ROWS and s % TM == 0 and (b * l) % min(TM, b * l) == 0
    ctx_row = b
    tiles_per_batch = s // TM

    cvec = jnp.zeros((MOD_ROWS, d), F32).at[:b].set(c).at[b].set(c_ctx)
    mod = _ada(cvec, ada_w, ada_b).reshape(depth, MOD_ROWS, N_MOD, d)

    lane = np.arange(2 * LANES)
    gmat = jnp.asarray((lane[:, None] // HEAD_DIM) == (lane[None, :] // HEAD_DIM), BF16)
    idx = np.arange(ML_CHUNK)
    tri = jnp.asarray(np.stack([idx[:, None] >= idx[None, :], idx[:, None] <= idx[None, :]]), BF16)
    rope_tabs = _rope_tables(s)

    w_in, w_o = ffn_w_in.astype(BF16), ffn_w_out.astype(BF16)
    w_proj, wg = _proj_weight(mix_w_in.astype(BF16))
    wml, wna, wo = w_br_ml.astype(BF16), w_br_na.astype(BF16), w_out.astype(BF16)
    wgq = jnp.concatenate([w_br_gq[:, HEAD_DIM * h:HEAD_DIM * (h + 1)] for h in GQ_HEAD_ORDER], axis=1).astype(BF16)

    xl = x.reshape(b * s, d)
    xc = ctx.reshape(b * l, d)
    lat = dict(tiles_per_batch=tiles_per_batch, ctx_row=ctx_row)
    con = dict(tiles_per_batch=None, ctx_row=ctx_row)
    for li in range(depth):
        ctx_out = li < depth - 1
        qkw = jnp.zeros((8, 2 * D_NA), F32)
        qkw = qkw.at[0, :D_NA].set(jnp.tile(na_qk_w[li, 0], H_NA) * Q_PRESCALE)
        qkw = qkw.at[0, D_NA:].set(jnp.tile(na_qk_w[li, 1], H_NA))
        qkw = qkw.at[1, :D_GQ].set(jnp.tile(gq_qk_w[li, 0], H_GQ) * Q_PRESCALE)
        qkw = qkw.at[1, D_GQ:D_GQ + D_KV].set(jnp.tile(gq_qk_w[li, 1], H_KV))
        gate_b = jnp.zeros((1, LANES), F32).at[0, :4 * H_ML].set(ml_gate_b[li][ML_GATE_ORDER])
        mlw = ml_norm_w[li].reshape(1, D_ML)
        tz = _na_bias_table(na_rpb[li])
        m, nw = mod[li], norm_w[li]

        xl = _ffn(xl, m, nw, w_in, w_o, (li, 0), k0=0, nrm=0, **lat)
        xc = _ffn(xc, m, nw, w_in, w_o, (li, 0), k0=0, nrm=0, **con)

        p_ml, p_mlg, p_na, p_gq = _proj(xl, m, nw, w_proj, li, gmat, qkw, rope_tabs, **lat)
        pc_ml, pc_mlg, pc_na, pc_gq = _proj(xc, m, nw, w_proj, li, gmat, qkw, None, **con)
        seq = lambda a: a.reshape(b, s, a.shape[-1])
        cseq = lambda a: a.reshape(b, l, a.shape[-1])

        hf, hb, hcf, hcb = _mlstm(seq(p_ml), seq(p_mlg), cseq(pc_ml), cseq(pc_mlg), tri, gate_b)
        o_na = _na(seq(p_na), cseq(pc_na), tz)
        o_gq = _gqa(seq(p_gq), cseq(pc_gq))
        flat = lambda a: a.reshape(-1, a.shape[-1])
        xl = _merge(xl, m, nw, flat(hf), flat(hb), p_ml, mlw, gmat, flat(o_na), flat(o_gq),
                    wg, wml, wna, wgq, wo, li, **lat)
        xl = _ffn(xl, m, nw, w_in, w_o, (li, 1), k0=6, nrm=2, **lat)
        if ctx_out:
            co_na = _ctx_attn(cseq(pc_na), qw=D_NA, kw=D_NA, shared_kv=False)
            co_gq = _ctx_attn(cseq(pc_gq), qw=D_GQ, kw=D_KV, shared_kv=True)
            xc = _merge(xc, m, nw, flat(hcf), flat(hcb), pc_ml, mlw, gmat, flat(co_na), flat(co_gq),
                        wg, wml, wna, wgq, wo, li, **con)
            xc = _ffn(xc, m, nw, w_in, w_o, (li, 1), k0=6, nrm=2, **con)
    return xl.reshape(b, s, d)
```

```python
import functools

import numpy as np
import jax
import jax.numpy as jnp
from jax import lax
from jax.experimental import pallas as pl
from jax.experimental.pallas import tpu as pltpu

F32 = jnp.float32
BF16 = jnp.bfloat16

HEAD_DIM = 64
LANES = 128
H_ML, H_NA, H_GQ, H_KV = 4, 6, 6, 2
D_ML, D_NA, D_GQ, D_KV = 256, 384, 384, 128
GRID_W = 64
NA_WIN_R, NA_WIN_C = 8, 16
ROPE_THETA = 10000.0
EPS = 1e-6
N_MOD = 9
ATTN_SCALE = HEAD_DIM ** -0.5
NEG = -1e30

ML_CHUNK = 256
NA_QROWS = 4
NA_BAND = 12
NA_SUBS = 3
TM = 512
GQ_TQ = 256
FFN_CHUNK = 768
Q_PRESCALE = ATTN_SCALE * float(np.log2(np.e))
MOD_ROWS = 16
VMEM_LIMIT = 56 * 1024 * 1024

C_ML, C_MLG, C_NA, C_GQ, C_END = 0, 1024, 1152, 2304, 2944
N_PROJ = C_END
GQ_HEAD_ORDER = (0, 3, 1, 4, 2, 5)
ML_GATE_ORDER = np.array([0, 1, 2, 3, 8, 9, 10, 11, 4, 5, 6, 7, 12, 13, 14, 15])


def _dot(a, b):
    return jnp.dot(a, b, preferred_element_type=F32)


def _dot_nt(a, b):
    return lax.dot_general(a, b, (((1,), (1,)), ((), ())), preferred_element_type=F32)


def _sigmoid(x):
    return 1.0 / (1.0 + jnp.exp(-x))


def _log_sigmoid(x):
    return jnp.minimum(x, 0.0) - jnp.log1p(jnp.exp(-jnp.abs(x)))


def _split_bf16(x):
    hi = x.astype(BF16)
    lo = (x - hi.astype(F32)).astype(BF16)
    return hi, lo


def _modnorm(x, nw, shift, scale):
    ms = jnp.mean(x * x, axis=-1, keepdims=True)
    return (x * lax.rsqrt(ms + EPS) * nw) * (1.0 + scale) + shift


def _head_norm(t, gmat, wrow):
    ss = _dot((t * t).astype(BF16), gmat)
    return t * lax.rsqrt(ss * (1.0 / HEAD_DIM) + EPS) * wrow


def _resident(shape, lead=()):
    return pl.BlockSpec((None,) * len(lead) + tuple(shape), lambda *_: tuple(lead) + (0,) * len(shape),
                        pipeline_mode=pl.Buffered(1))


def _params(sem):
    return pltpu.CompilerParams(dimension_semantics=sem, vmem_limit_bytes=VMEM_LIMIT)


def _ada_kernel(c_ref, w_ref, b_ref, o_ref):
    c = c_ref[...]
    s = (c * _sigmoid(c)).astype(BF16)
    o_ref[0] = _dot(s, w_ref[0].astype(BF16)) + b_ref[0]


def _ada(cvec, ada_w, ada_b):
    depth, d, n = ada_w.shape
    tn = n // 8
    return pl.pallas_call(
        _ada_kernel,
        grid=(depth, n // tn),
        in_specs=[pl.BlockSpec((MOD_ROWS, d), lambda l, j: (0, 0)),
                  pl.BlockSpec((1, d, tn), lambda l, j: (l, 0, j)),
                  pl.BlockSpec((1, 1, tn), lambda l, j: (l, 0, j))],
        out_specs=pl.BlockSpec((1, MOD_ROWS, tn), lambda l, j: (l, 0, j)),
        out_shape=jax.ShapeDtypeStruct((depth, MOD_ROWS, n), F32),
        compiler_params=_params(("arbitrary", "arbitrary")),
        name="ada_mod",
    )(cvec, ada_w, ada_b.reshape(depth, 1, n))


def _mod_index(tiles_per_batch, ctx_row):
    if tiles_per_batch is None:
        return lambda i, *_: (ctx_row, 0, 0)
    return lambda i, *_: (i // tiles_per_batch, 0, 0)


def _ffn_kernel(x_ref, mod_ref, nw_ref, wi_ref, wo_ref, o_ref, *, k0, nrm, chunks):
    x = x_ref[...]
    h = _modnorm(x, nw_ref[nrm:nrm + 1, :], mod_ref[0, k0:k0 + 1, :], mod_ref[0, k0 + 1:k0 + 2, :]).astype(BF16)
    dff = wo_ref.shape[0]
    y = None
    for c0, c1 in chunks:
        g = _dot(h, wi_ref[:, c0:c1])
        u = _dot(h, wi_ref[:, dff + c0:dff + c1])
        part = _dot((g * _sigmoid(g) * u).astype(BF16), wo_ref[c0:c1, :])
        y = part if y is None else y + part
    o_ref[...] = x + (0.5 * mod_ref[0, k0 + 2:k0 + 3, :]) * y


def _ffn(x, mod, nw, w_in, w_out, lead, *, k0, nrm, tiles_per_batch, ctx_row):
    t, d = x.shape
    dff = w_out.shape[-2]
    tm = min(TM, t)
    edges = list(range(0, dff, FFN_CHUNK)) + [dff]
    chunks = tuple(zip(edges[:-1], edges[1:]))
    kern = functools.partial(_ffn_kernel, k0=k0, nrm=nrm, chunks=chunks)
    return pl.pallas_call(
        kern,
        grid=(t // tm,),
        in_specs=[pl.BlockSpec((tm, d), lambda i: (i, 0)),
                  pl.BlockSpec((1, N_MOD, d), _mod_index(tiles_per_batch, ctx_row)),
                  pl.BlockSpec((3, d), lambda i: (0, 0)),
                  _resident((d, 2 * dff), lead),
                  _resident((dff, d), lead)],
        out_specs=pl.BlockSpec((tm, d), lambda i: (i, 0)),
        out_shape=jax.ShapeDtypeStruct((t, d), F32),
        compiler_params=_params(("parallel",)),
        name="ffn_swiglu",
    )(x, mod, nw, w_in, w_out)


def _proj_kernel(*refs, rope):
    if rope:
        (x_ref, mod_ref, nw_ref, w_ref, g_ref, qkw_ref, cos_ref, sin_ref,
         ml_ref, mlg_ref, na_ref, gq_ref) = refs
    else:
        (x_ref, mod_ref, nw_ref, w_ref, g_ref, qkw_ref,
         ml_ref, mlg_ref, na_ref, gq_ref) = refs
    h = _modnorm(x_ref[...], nw_ref[1:2, :], mod_ref[0, 3:4, :], mod_ref[0, 4:5, :]).astype(BF16)
    gmat = g_ref[...]

    ml_ref[...] = _dot(h, w_ref[:, C_ML:C_MLG]).astype(BF16)
    mlg_ref[...] = _dot(h, w_ref[:, C_MLG:C_NA])

    na = _dot(h, w_ref[:, C_NA:C_GQ])
    for j in range(3):
        sl = slice(2 * LANES * j, 2 * LANES * (j + 1))
        na_ref[:, sl] = _head_norm(na[:, sl], gmat, qkw_ref[0:1, sl]).astype(BF16)
    na_ref[:, 2 * D_NA:] = na[:, 2 * D_NA:].astype(BF16)

    gq = _dot(h, w_ref[:, C_GQ:C_END])
    if rope:
        lane = lax.broadcasted_iota(jnp.int32, (1, LANES), 1)
        first_half = (lane % HEAD_DIM) < (HEAD_DIM // 2)
        cos = cos_ref[...]
        sin = sin_ref[...]
    for j in range(2):
        t2 = _head_norm(gq[:, 2 * LANES * j:2 * LANES * (j + 1)], gmat, qkw_ref[1:2, 2 * LANES * j:2 * LANES * (j + 1)])
        for half in range(2):
            t = t2[:, LANES * half:LANES * (half + 1)]
            if rope:
                rot = jnp.where(first_half, pltpu.roll(t, LANES - HEAD_DIM // 2, axis=1),
                                pltpu.roll(t, HEAD_DIM // 2, axis=1))
                t = t * cos + rot * sin
            gq_ref[:, LANES * (2 * j + half):LANES * (2 * j + half + 1)] = t.astype(BF16)
    gq_ref[:, D_GQ + D_KV:] = gq[:, D_GQ + D_KV:].astype(BF16)


def _proj(x, mod, nw, w, li, gmat, qkw, rope_tabs, *, tiles_per_batch, ctx_row):
    t, d = x.shape
    tm = min(TM, t)
    rope = rope_tabs is not None
    in_specs = [pl.BlockSpec((tm, d), lambda i: (i, 0)),
                pl.BlockSpec((1, N_MOD, d), _mod_index(tiles_per_batch, ctx_row)),
                pl.BlockSpec((3, d), lambda i: (0, 0)),
                _resident((d, N_PROJ), (li,)),
                _resident(gmat.shape),
                _resident(qkw.shape)]
    args = [x, mod, nw, w, gmat, qkw]
    if rope:
        in_specs += [pl.BlockSpec((tm, LANES), lambda i: (i % tiles_per_batch, 0))] * 2
        args += list(rope_tabs)
    widths = (1024, LANES, 3 * D_NA, D_GQ + 2 * D_KV)
    dtypes = (BF16, F32, BF16, BF16)
    return pl.pallas_call(
        functools.partial(_proj_kernel, rope=rope),
        grid=(t // tm,),
        in_specs=in_specs,
        out_specs=[pl.BlockSpec((tm, wd), lambda i: (i, 0)) for wd in widths],
        out_shape=[jax.ShapeDtypeStruct((t, wd), dt) for wd, dt in zip(widths, dtypes)],
        compiler_params=_params(("parallel",)),
        name="mix_in_proj",
    )(*args)


def _ml_prep(d, g_ref, tri_ref, bias_ref):
    log2e = float(np.log2(np.e))
    lc = g_ref.shape[1]
    gates = g_ref[0] + bias_ref[...]
    gates_t = gates.T
    ig_t = gates_t[0:8] * log2e
    hi, lo = _split_bf16(_log_sigmoid(gates_t[0:16]) * log2e)
    ones = jnp.ones((lc, lc), BF16)
    b_t = (_dot(hi, tri_ref[1 - d]) + _dot(lo, tri_ref[1 - d]))[8:16]
    btot_t = (_dot(hi, ones) + _dot(lo, ones))[8:16]
    lf_al = pltpu.roll(_log_sigmoid(gates) * log2e, LANES - 8, axis=1)
    hi, lo = _split_bf16(lf_al)
    c_mat = gates * log2e - (_dot(tri_ref[d], hi) + _dot(tri_ref[d], lo))
    return ig_t, b_t, btot_t, c_mat


def _ml_head(d, h, prep, m_ref):
    ig_t, b_t, btot_t, c_mat = prep
    lc = c_mat.shape[0]
    r = 4 * d + h
    row = lax.broadcasted_iota(jnp.int32, (lc, lc), 0)
    col = lax.broadcasted_iota(jnp.int32, (lc, lc), 1)
    visible = (row <= col) if d == 0 else (row >= col)
    ig, b, b_tot = ig_t[r:r + 1], b_t[r:r + 1], btot_t[r:r + 1]
    m_prev = m_ref[r:r + 1, :]
    w_end = b_tot - b + ig
    m_new = jnp.maximum(b_tot + m_prev, jnp.max(w_end, axis=1, keepdims=True))
    a = jnp.exp2(w_end - m_new)
    decay = jnp.exp2(b_tot + m_prev - m_new)
    m_inter = b + m_prev
    logw = jnp.where(visible, c_mat[:, r:r + 1] + b, NEG)
    m_j = jnp.maximum(m_inter, jnp.max(logw, axis=0, keepdims=True))
    w = jnp.exp2(logw - m_j)
    m_ref[r:r + 1, :] = m_new
    return w, a, jnp.exp2(m_inter - m_j), jnp.exp2(-m_j), decay


def _ml_pair(d, p, head_a, head_b, q_ref, k_ref, v_ref, o_ref, st_ref):
    (w_a, a_a, g_a, fl_a, dec_a), (w_b, a_b, g_b, fl_b, dec_b) = head_a, head_b
    lc = q_ref.shape[1]
    low = _low_lanes()
    sl = slice(LANES * p, LANES * (p + 1))
    q = q_ref[0, :, sl]
    k = k_ref[0, :, sl] * ATTN_SCALE
    vt = jnp.concatenate([v_ref[0, :, sl].astype(F32).T, jnp.ones((LANES, lc), F32)], axis=0)
    head_row = (lax.broadcasted_iota(jnp.int32, (2 * LANES, 1), 0) % LANES) < HEAD_DIM
    vt16 = vt.astype(BF16)
    r_a = _dot(vt16, (_dot_nt(k, _one_head(q, low, 0)) * w_a).astype(BF16))
    r_b = _dot(vt16, (_dot_nt(k, _one_head(q, low, 1)) * w_b).astype(BF16))
    state = st_ref[d, p]
    r_i = _dot_nt(state.astype(BF16), q)
    r = jnp.where(head_row, r_a, r_b) + jnp.where(head_row, g_a, g_b) * r_i
    num, den = r[:LANES], r[LANES:]
    h_t = num / jnp.maximum(jnp.abs(den), jnp.where(head_row[:LANES], fl_a, fl_b))
    o_ref[0, :, sl] = h_t.T.astype(BF16)

    upd = _dot((vt * jnp.where(head_row, a_a, a_b)).astype(BF16), k)
    same_head = head_row == low
    dec = jnp.where(head_row, dec_a[:, :LANES], dec_b[:, :LANES])
    st_ref[d, p] = dec * state + jnp.where(same_head, upd, 0.0)


def _ml_step(fwd, bwd, tri_ref, bias_ref, st_ref, m_ref):
    dirs = (fwd, bwd)
    preps = [_ml_prep(d, refs[3], tri_ref, bias_ref) for d, refs in enumerate(dirs)]
    heads = [[_ml_head(d, h, preps[d], m_ref) for h in range(H_ML)] for d in range(2)]
    for p in range(H_ML // 2):
        for d, (q_ref, k_ref, v_ref, _, o_ref) in enumerate(dirs):
            _ml_pair(d, p, heads[d][2 * p], heads[d][2 * p + 1], q_ref, k_ref, v_ref, o_ref, st_ref)


def _mlstm_kernel(qf, kf, vf, gf, qb, kb, vb, gb, qc, kc, vc, gc, tri_ref, bias_ref,
                  hf_ref, hb_ref, hcf_ref, hcb_ref, st_ref, m_ref):
    c = pl.program_id(1)

    @pl.when(c == 0)
    def _():
        st_ref[...] = jnp.zeros_like(st_ref)
        m_ref[...] = jnp.zeros_like(m_ref)
        _ml_step((qc, kc, vc, gc, hcf_ref), (qc, kc, vc, gc, hcb_ref), tri_ref, bias_ref, st_ref, m_ref)

    @pl.when(c > 0)
    def _():
        _ml_step((qf, kf, vf, gf, hf_ref), (qb, kb, vb, gb, hb_ref), tri_ref, bias_ref, st_ref, m_ref)


def _mlstm(p_ml, p_mlg, pc_ml, pc_mlg, tri, bias):
    b, s, _ = p_ml.shape
    l = pc_ml.shape[1]
    lc = ML_CHUNK
    assert l == lc and s % lc == 0
    nl = s // lc
    fwd = lambda c: jnp.maximum(c - 1, 0)
    bwd = lambda c: nl - 1 - jnp.maximum(c - 1, 0)

    def lat(idx, blk, width):
        return pl.BlockSpec((1, lc, width), lambda i, c: (i, idx(c), blk))

    def ctx(blk, width):
        return pl.BlockSpec((1, lc, width), lambda i, c: (i, 0, blk))

    in_specs = ([lat(fwd, 0, D_ML), lat(fwd, 1, D_ML), lat(fwd, 2, D_ML), lat(fwd, 0, LANES)]
                + [lat(bwd, 0, D_ML), lat(bwd, 1, D_ML), lat(bwd, 2, D_ML), lat(bwd, 0, LANES)]
                + [ctx(0, D_ML), ctx(1, D_ML), ctx(2, D_ML), ctx(0, LANES)]
                + [_resident((2, lc, lc)), _resident((1, LANES))])
    out_specs = [lat(fwd, 0, D_ML), lat(bwd, 0, D_ML), ctx(0, D_ML), ctx(0, D_ML)]
    out_shape = [jax.ShapeDtypeStruct((b, s, D_ML), BF16)] * 2 + [jax.ShapeDtypeStruct((b, l, D_ML), BF16)] * 2
    return pl.pallas_call(
        _mlstm_kernel,
        grid=(b, nl + 1),
        in_specs=in_specs,
        out_specs=out_specs,
        out_shape=out_shape,
        scratch_shapes=[pltpu.VMEM((2, H_ML // 2, 2 * LANES, LANES), F32), pltpu.VMEM((2 * H_ML, lc), F32)],
        compiler_params=_params(("parallel", "arbitrary")),
        name="mlstm_bidir",
    )(p_ml, p_ml, p_ml, p_mlg, p_ml, p_ml, p_ml, p_mlg, pc_ml, pc_ml, pc_ml, pc_mlg, tri, bias)


def _attend_heads(n_heads, score_fn, value_fn, s_scr, p_scr):
    s_scr[0] = score_fn(0)
    outs = []
    for i in range(n_heads):
        if i + 1 < n_heads:
            s_scr[(i + 1) % 2] = score_fn(i + 1)
        s = s_scr[i % 2]
        p_scr[i % 2] = jnp.exp2(s - jnp.max(s, axis=1, keepdims=True)).astype(BF16)
        r = _dot(p_scr[i % 2], value_fn(i))
        outs.append(r[:, :LANES] / r[:, LANES:])
    return outs


def _low_lanes():
    return lax.broadcasted_iota(jnp.int32, (1, LANES), 1) < HEAD_DIM


def _one_head(q, low, half):
    zero = jnp.zeros_like(q)
    return jnp.where(low, q, zero) if half == 0 else jnp.where(low, zero, q)


def _pair_outputs(o_ref, outs, low):
    for j in range(len(outs) // 2):
        o_ref[0, :, LANES * j:LANES * (j + 1)] = jnp.where(low, outs[2 * j], outs[2 * j + 1]).astype(BF16)


def _gqa_kernel(q_ref, k_ref, v_ref, kc_ref, vc_ref, o_ref, kall, vall, s_scr, p_scr):
    n_lat = k_ref.shape[1]

    @pl.when(pl.program_id(1) == 0)
    def _():
        kall[0:n_lat, :] = k_ref[0]
        kall[n_lat:, :] = kc_ref[0]
        vall[0:n_lat, 0:LANES] = v_ref[0]
        vall[n_lat:, 0:LANES] = vc_ref[0]
        vall[:, LANES:] = jnp.ones((vall.shape[0], LANES), BF16)

    low = _low_lanes()

    def score(i):
        j, half = divmod(i, 2)
        return _dot_nt(_one_head(q_ref[0, :, LANES * j:LANES * (j + 1)], low, half), kall[...])

    _pair_outputs(o_ref, _attend_heads(H_GQ, score, lambda i: vall[...], s_scr, p_scr), low)


def _gqa(p_gq, pc_gq):
    b, s, _ = p_gq.shape
    l = pc_gq.shape[1]
    tq = min(GQ_TQ, s)
    kblk, vblk = D_GQ // LANES, D_GQ // LANES + 1
    return pl.pallas_call(
        _gqa_kernel,
        grid=(b, s // tq),
        in_specs=[pl.BlockSpec((1, tq, D_GQ), lambda i, t: (i, t, 0)),
                  pl.BlockSpec((1, s, LANES), lambda i, t: (i, 0, kblk)),
                  pl.BlockSpec((1, s, LANES), lambda i, t: (i, 0, vblk)),
                  pl.BlockSpec((1, l, LANES), lambda i, t: (i, 0, kblk)),
                  pl.BlockSpec((1, l, LANES), lambda i, t: (i, 0, vblk))],
        out_specs=pl.BlockSpec((1, tq, D_GQ), lambda i, t: (i, t, 0)),
        out_shape=jax.ShapeDtypeStruct((b, s, D_GQ), BF16),
        scratch_shapes=[pltpu.VMEM((s + l, LANES), BF16), pltpu.VMEM((s + l, 2 * LANES), BF16),
                        pltpu.VMEM((2, tq, s + l), F32), pltpu.VMEM((2, tq, s + l), BF16)],
        compiler_params=_params(("parallel", "arbitrary")),
        name="gqa_latent",
    )(p_gq, p_gq, p_gq, pc_gq, pc_gq)


def _ctx_attn_kernel(q_ref, k_ref, v_ref, o_ref, s_scr, p_scr, *, shared_kv):
    low = _low_lanes()
    ones = jnp.ones((v_ref.shape[1], LANES), BF16)

    def kv_lanes(i):
        return slice(0, LANES) if shared_kv else slice(LANES * (i // 2), LANES * (i // 2 + 1))

    def score(i):
        j, half = divmod(i, 2)
        return _dot_nt(_one_head(q_ref[0, :, LANES * j:LANES * (j + 1)], low, half), k_ref[0, :, kv_lanes(i)])

    def value(i):
        return jnp.concatenate([v_ref[0, :, kv_lanes(i)], ones], axis=1)

    n_heads = 2 * (q_ref.shape[2] // LANES)
    _pair_outputs(o_ref, _attend_heads(n_heads, score, value, s_scr, p_scr), low)


def _ctx_attn(pc, *, qw, kw, shared_kv):
    b, l, _ = pc.shape
    kb = qw // kw
    return pl.pallas_call(
        functools.partial(_ctx_attn_kernel, shared_kv=shared_kv),
        grid=(b,),
        in_specs=[pl.BlockSpec((1, l, qw), lambda i: (i, 0, 0)),
                  pl.BlockSpec((1, l, kw), lambda i: (i, 0, kb)),
                  pl.BlockSpec((1, l, kw), lambda i: (i, 0, kb + 1))],
        out_specs=pl.BlockSpec((1, l, qw), lambda i: (i, 0, 0)),
        out_shape=jax.ShapeDtypeStruct((b, l, qw), BF16),
        scratch_shapes=[pltpu.VMEM((2, l, l), F32), pltpu.VMEM((2, l, l), BF16)],
        compiler_params=_params(("parallel",)),
        name="ctx_attn",
    )(pc, pc, pc)


def _na_kernel(*refs, n_rows):
    q_ref, k_subs, v_subs = refs[0], refs[1:1 + NA_SUBS], refs[1 + NA_SUBS:1 + 2 * NA_SUBS]
    kc_ref, vc_ref, tz_ref, o_ref, kall, vall, s_scr, p_scr = refs[1 + 2 * NA_SUBS:]
    i = pl.program_id(1)
    r0 = i * NA_QROWS
    start = jnp.clip(r0 - NA_WIN_R // 2, 0, n_rows - NA_BAND)
    delta = start - r0
    nq, nk = NA_QROWS * GRID_W, NA_BAND * GRID_W
    sub = nk // NA_SUBS
    n_ctx = kc_ref.shape[1]

    for t, (kr, vr) in enumerate(zip(k_subs + (kc_ref,), v_subs + (vc_ref,))):
        rows = slice(sub * t, sub * t + kr.shape[1])
        kall[rows, :] = kr[0]
        for j in range(D_NA // LANES):
            vall[j, rows, 0:LANES] = vr[0, :, LANES * j:LANES * (j + 1)]
    vall[:, :, LANES:] = jnp.ones((D_NA // LANES, nk + n_ctx, LANES), BF16)

    qrow = r0 + lax.broadcasted_iota(jnp.int32, (nq, nk), 0) // GRID_W
    krow = start + lax.broadcasted_iota(jnp.int32, (nq, nk), 1) // GRID_W
    first = jnp.clip(qrow - NA_WIN_R // 2, 0, n_rows - NA_WIN_R)
    row_mask = jnp.where((krow >= first) & (krow < first + NA_WIN_R), 0.0, NEG)
    low = _low_lanes()
    no_bias = jnp.zeros((nq, n_ctx), F32)

    def score(h):
        j, half = divmod(h, 2)
        sl = slice(LANES * j, LANES * (j + 1))
        slabs = []
        for a in range(NA_QROWS):
            pieces = [tz_ref[h, jnp.clip(delta + 2 * bp - a, -8, 7) + 8] for bp in range(NA_BAND // 2)]
            slabs.append(jnp.concatenate(pieces, axis=1))
        bias = jnp.concatenate([jnp.concatenate(slabs, axis=0) + row_mask, no_bias], axis=1)
        return _dot_nt(_one_head(q_ref[0, :, sl], low, half), kall[:, sl]) + bias

    _pair_outputs(o_ref, _attend_heads(H_NA, score, lambda h: vall[h // 2], s_scr, p_scr), low)


def _na(p_na, pc_na, tz, li):
    b, s, _ = p_na.shape
    l = pc_na.shape[1]
    n_rows = s // GRID_W
    assert n_rows % NA_QROWS == 0 and n_rows >= NA_BAND
    nq, nk = NA_QROWS * GRID_W, NA_BAND * GRID_W
    sub = nk // NA_SUBS
    rows_per_sub = NA_BAND // NA_SUBS
    assert all(v % rows_per_sub == 0 for v in (NA_QROWS, NA_WIN_R // 2, n_rows - NA_BAND))

    def band(t, blk):
        def idx(i, r):
            start = jnp.clip(r * NA_QROWS - NA_WIN_R // 2, 0, n_rows - NA_BAND)
            return (i, start // rows_per_sub + t, blk)
        return pl.BlockSpec((1, sub, D_NA), idx)

    in_specs = ([pl.BlockSpec((1, nq, D_NA), lambda i, r: (i, r, 0))]
                + [band(t, 1) for t in range(NA_SUBS)] + [band(t, 2) for t in range(NA_SUBS)]
                + [pl.BlockSpec((1, l, D_NA), lambda i, r: (i, 0, 1)),
                   pl.BlockSpec((1, l, D_NA), lambda i, r: (i, 0, 2)),
                   _resident(tz.shape[1:], (li,))])
    return pl.pallas_call(
        functools.partial(_na_kernel, n_rows=n_rows),
        grid=(b, n_rows // NA_QROWS),
        in_specs=in_specs,
        out_specs=pl.BlockSpec((1, nq, D_NA), lambda i, r: (i, r, 0)),
        out_shape=jax.ShapeDtypeStruct((b, s, D_NA), BF16),
        scratch_shapes=[pltpu.VMEM((nk + l, D_NA), BF16), pltpu.VMEM((D_NA // LANES, nk + l, 2 * LANES), BF16),
                        pltpu.VMEM((2, nq, nk + l), F32), pltpu.VMEM((2, nq, nk + l), BF16)],
        compiler_params=_params(("parallel", "arbitrary")),
        name="na_latent",
    )(p_na, *([p_na] * (2 * NA_SUBS)), pc_na, pc_na, tz)


def _merge_kernel(x_ref, mod_ref, nw_ref, hf_ref, hb_ref, og_ref, mlw_ref, g_ref, na_ref, gq_ref,
                  wg_ref, wml_ref, wna_ref, wgq_ref, wo_ref, o_ref):
    x = x_ref[...]
    d = x.shape[1]
    hx = _modnorm(x, nw_ref[1:2, :], mod_ref[0, 3:4, :], mod_ref[0, 4:5, :]).astype(BF16)
    h = hf_ref[...].astype(F32) + hb_ref[...].astype(F32)
    o_ml = (_head_norm(h, g_ref[...], mlw_ref[...]) * _sigmoid(og_ref[...].astype(F32))).astype(BF16)
    y = None
    for j, (o_br, w_br) in enumerate(((o_ml, wml_ref), (na_ref[...], wna_ref), (gq_ref[...], wgq_ref))):
        part = _sigmoid(_dot(hx, wg_ref[:, d * j:d * (j + 1)])) * _dot(o_br, w_br[...])
        y = part if y is None else y + part
    o_ref[...] = x + mod_ref[0, 5:6, :] * _dot(y.astype(BF16), wo_ref[...])


def _merge(x, mod, nw, hf, hb, p_ml, mlw, gmat, o_na, o_gq, wg, wml, wna, wgq, wo, li,
           *, tiles_per_batch, ctx_row):
    t, d = x.shape
    tm = min(TM, t)
    row = lambda wd, blk=0: pl.BlockSpec((tm, wd), lambda i: (i, blk))
    lead = (li,)
    return pl.pallas_call(
        _merge_kernel,
        grid=(t // tm,),
        in_specs=[row(d),
                  pl.BlockSpec((1, N_MOD, d), _mod_index(tiles_per_batch, ctx_row)),
                  pl.BlockSpec((3, d), lambda i: (0, 0)),
                  row(D_ML), row(D_ML), row(D_ML, 3),
                  _resident((1, D_ML)), _resident(gmat.shape),
                  row(D_NA), row(D_GQ),
                  _resident((d, 3 * d), lead), _resident((D_ML, d), lead), _resident((D_NA, d), lead),
                  _resident((D_GQ, d), lead), _resident((d, d), lead)],
        out_specs=row(d),
        out_shape=jax.ShapeDtypeStruct((t, d), F32),
        compiler_params=_params(("parallel",)),
        name="branch_merge",
    )(x, mod, nw, hf, hb, p_ml, mlw, gmat, o_na, o_gq, wg, wml, wna, wgq, wo)


def _proj_weight(w):
    d = w.shape[-2]
    o = 0
    seg = {}
    for name, width in (("ml_k", D_ML), ("ml_v", D_ML), ("ml_g", 4 * H_ML), ("na_k", D_NA), ("na_v", D_NA),
                        ("gq_k", D_KV), ("gq_v", D_KV), ("ml_q", D_ML), ("ml_o", D_ML), ("na_q", D_NA),
                        ("gq_q", D_GQ), ("br_g", 3 * d)):
        seg[name] = w[..., o:o + width].astype(BF16)
        o += width
    gq_q = jnp.concatenate([seg["gq_q"][..., HEAD_DIM * h:HEAD_DIM * (h + 1)] for h in GQ_HEAD_ORDER], axis=-1)
    pad = jnp.zeros(w.shape[:-1] + (LANES - 4 * H_ML,), BF16)
    out = jnp.concatenate([seg["ml_q"], seg["ml_k"], seg["ml_v"], seg["ml_o"], seg["ml_g"][..., ML_GATE_ORDER], pad,
                           seg["na_q"], seg["na_k"], seg["na_v"], gq_q, seg["gq_k"], seg["gq_v"]], axis=-1)
    return out, seg["br_g"]


def _rope_tables(n_tok):
    t = np.arange(n_tok)
    row = (t // GRID_W).astype(np.float64)
    col = (t % GRID_W).astype(np.float64)
    n_freq = HEAD_DIM // 4
    inv = ROPE_THETA ** (-np.arange(n_freq, dtype=np.float64) / n_freq)
    ang = np.concatenate([row[:, None] * inv, col[:, None] * inv], axis=-1)
    cos, sin = np.cos(ang), np.sin(ang)
    cos_t = np.tile(cos, (1, LANES // (HEAD_DIM // 2)))
    sin_t = np.tile(np.concatenate([-sin, sin], axis=-1), (1, LANES // HEAD_DIM))
    return jnp.asarray(cos_t, F32), jnp.asarray(sin_t, F32)


def _na_bias_table(rpb):
    col = np.arange(GRID_W)
    first = np.clip(col - NA_WIN_C // 2, 0, GRID_W - NA_WIN_C)
    in_win = (col[None, :] >= first[:, None]) & (col[None, :] < first[:, None] + NA_WIN_C)
    side = GRID_W - NA_WIN_C
    width = 2 * GRID_W
    rows = jnp.pad(rpb, ((0, 0), (0, 0), (1, 1), (side, width - side - rpb.shape[-1])))
    lead = rows.shape[:-1]
    flat = jnp.broadcast_to(rows[..., None, :], lead + (GRID_W, width)).reshape(lead + (GRID_W * width,))
    skew = flat[..., :GRID_W * (width - 1)].reshape(lead + (GRID_W, width - 1))
    full = skew[..., GRID_W - 1:2 * GRID_W - 1] * float(np.log2(np.e))
    row_ok = np.zeros((2 * NA_WIN_R + 1,), bool)
    row_ok[1:-1] = True
    full = jnp.where(jnp.asarray(in_win[None, None, None] & row_ok[None, None, :, None, None]), full, NEG)
    return jnp.concatenate([full[:, :, :-1], full[:, :, 1:]], axis=-1).astype(F32)


def kernel(x, c, ctx, c_ctx, ada_w, ada_b, norm_w, ffn_w_in, ffn_w_out, mix_w_in, ml_gate_b, ml_norm_w,
           na_qk_w, na_rpb, gq_qk_w, w_br_ml, w_br_na, w_br_gq, w_out):
    b, s, d = x.shape
    l = ctx.shape[1]
    depth = ada_w.shape[0]
    assert b < MOD_ROWS and s % TM == 0 and (b * l) % min(TM, b * l) == 0
    ctx_row = b
    tiles_per_batch = s // TM

    cvec = jnp.zeros((MOD_ROWS, d), F32).at[:b].set(c).at[b].set(c_ctx)
    mod = _ada(cvec, ada_w, ada_b).reshape(depth, MOD_ROWS, N_MOD, d)

    lane = np.arange(2 * LANES)
    gmat = jnp.asarray((lane[:, None] // HEAD_DIM) == (lane[None, :] // HEAD_DIM), BF16)
    idx = np.arange(ML_CHUNK)
    tri = jnp.asarray(np.stack([idx[:, None] >= idx[None, :], idx[:, None] <= idx[None, :]]), BF16)
    rope_tabs = _rope_tables(s)

    w_in, w_o = ffn_w_in.astype(BF16), ffn_w_out.astype(BF16)
    w_proj, wg = _proj_weight(mix_w_in)
    wml, wna, wo = w_br_ml.astype(BF16), w_br_na.astype(BF16), w_out.astype(BF16)
    wgq = jnp.concatenate([w_br_gq[:, HEAD_DIM * h:HEAD_DIM * (h + 1)] for h in GQ_HEAD_ORDER], axis=1).astype(BF16)
    tz = _na_bias_table(na_rpb)

    xl = x.reshape(b * s, d)
    xc = ctx.reshape(b * l, d)
    lat = dict(tiles_per_batch=tiles_per_batch, ctx_row=ctx_row)
    con = dict(tiles_per_batch=None, ctx_row=ctx_row)
    for li in range(depth):
        ctx_out = li < depth - 1
        qkw = jnp.zeros((8, 2 * D_NA), F32)
        qkw = qkw.at[0, :D_NA].set(jnp.tile(na_qk_w[li, 0], H_NA) * Q_PRESCALE)
        qkw = qkw.at[0, D_NA:].set(jnp.tile(na_qk_w[li, 1], H_NA))
        qkw = qkw.at[1, :D_GQ].set(jnp.tile(gq_qk_w[li, 0], H_GQ) * Q_PRESCALE)
        qkw = qkw.at[1, D_GQ:D_GQ + D_KV].set(jnp.tile(gq_qk_w[li, 1], H_KV))
        gate_b = jnp.zeros((1, LANES), F32).at[0, :4 * H_ML].set(ml_gate_b[li][ML_GATE_ORDER])
        mlw = ml_norm_w[li].reshape(1, D_ML)
        m, nw = mod[li], norm_w[li]

        xl = _ffn(xl, m, nw, w_in, w_o, (li, 0), k0=0, nrm=0, **lat)
        xc = _ffn(xc, m, nw, w_in, w_o, (li, 0), k0=0, nrm=0, **con)

        p_ml, p_mlg, p_na, p_gq = _proj(xl, m, nw, w_proj, li, gmat, qkw, rope_tabs, **lat)
        pc_ml, pc_mlg, pc_na, pc_gq = _proj(xc, m, nw, w_proj, li, gmat, qkw, None, **con)
        seq = lambda a: a.reshape(b, s, a.shape[-1])
        cseq = lambda a: a.reshape(b, l, a.shape[-1])

        hf, hb, hcf, hcb = _mlstm(seq(p_ml), seq(p_mlg), cseq(pc_ml), cseq(pc_mlg), tri, gate_b)
        o_na = _na(seq(p_na), cseq(pc_na), tz, li)
        o_gq = _gqa(seq(p_gq), cseq(pc_gq))
        flat = lambda a: a.reshape(-1, a.shape[-1])
        xl = _merge(xl, m, nw, flat(hf), flat(hb), p_ml, mlw, gmat, flat(o_na), flat(o_gq),
                    wg, wml, wna, wgq, wo, li, **lat)
        xl = _ffn(xl, m, nw, w_in, w_o, (li, 1), k0=6, nrm=2, **lat)
        if ctx_out:
            co_na = _ctx_attn(cseq(pc_na), qw=D_NA, kw=D_NA, shared_kv=False)
            co_gq = _ctx_attn(cseq(pc_gq), qw=D_GQ, kw=D_KV, shared_kv=True)
            xc = _merge(xc, m, nw, flat(hcf), flat(hcb), pc_ml, mlw, gmat, flat(co_na), flat(co_gq),
                        wg, wml, wna, wgq, wo, li, **con)
            xc = _ffn(xc, m, nw, w_in, w_o, (li, 1), k0=6, nrm=2, **con)
    return xl.reshape(b, s, d)
```

```python
import functools

import numpy as np
import jax
import jax.numpy as jnp
from jax import lax
from jax.experimental import pallas as pl
from jax.experimental.pallas import tpu as pltpu

F32 = jnp.float32
BF16 = jnp.bfloat16

HEAD_DIM = 64
LANES = 128
H_ML, H_NA, H_GQ, H_KV = 4, 6, 6, 2
D_ML, D_NA, D_GQ, D_KV = 256, 384, 384, 128
GRID_W = 64
NA_WIN_R, NA_WIN_C = 8, 16
ROPE_THETA = 10000.0
EPS = 1e-6
N_MOD = 9
ATTN_SCALE = HEAD_DIM ** -0.5
NEG = -1e30

ML_CHUNK = 256
NA_QROWS = 4
NA_BAND = 12
NA_SUBS = 3
TM = 512
GQ_TQ = 256
FFN_CHUNK = 768
Q_PRESCALE = ATTN_SCALE * float(np.log2(np.e))
MOD_ROWS = 16
VMEM_LIMIT = 56 * 1024 * 1024

C_ML, C_MLG, C_NA, C_GQ, C_END = 0, 1024, 1152, 2304, 2944
N_PROJ = C_END
GQ_HEAD_ORDER = (0, 3, 1, 4, 2, 5)
ML_GATE_ORDER = np.array([0, 1, 2, 3, 8, 9, 10, 11, 4, 5, 6, 7, 12, 13, 14, 15])


def _dot(a, b):
    return jnp.dot(a, b, preferred_element_type=F32)


def _dot_nt(a, b):
    return lax.dot_general(a, b, (((1,), (1,)), ((), ())), preferred_element_type=F32)


def _sigmoid(x):
    return 1.0 / (1.0 + jnp.exp(-x))


def _log_sigmoid(x):
    return jnp.minimum(x, 0.0) - jnp.log1p(jnp.exp(-jnp.abs(x)))


def _split_bf16(x):
    parts = []
    for _ in range(3):
        p = x.astype(BF16)
        parts.append(p)
        x = x - p.astype(F32)
    return parts


def _dot_f32_left(x, m01):
    return functools.reduce(jnp.add, [_dot(p, m01) for p in _split_bf16(x)])


def _modnorm(x, nw, shift, scale):
    ms = jnp.mean(x * x, axis=-1, keepdims=True)
    return (x * lax.rsqrt(ms + EPS) * nw) * (1.0 + scale) + shift


def _head_norm(t, gmat, wrow):
    ss = _dot((t * t).astype(BF16), gmat)
    return t * lax.rsqrt(ss * (1.0 / HEAD_DIM) + EPS) * wrow


def _resident(shape, lead=()):
    return pl.BlockSpec((None,) * len(lead) + tuple(shape), lambda *_: tuple(lead) + (0,) * len(shape),
                        pipeline_mode=pl.Buffered(1))


def _params(sem):
    return pltpu.CompilerParams(dimension_semantics=sem, vmem_limit_bytes=VMEM_LIMIT)


def _ada_kernel(c_ref, w_ref, b_ref, o_ref):
    c = c_ref[...]
    s = (c * _sigmoid(c)).astype(BF16)
    o_ref[0] = _dot(s, w_ref[0].astype(BF16)) + b_ref[0]


def _ada(cvec, ada_w, ada_b):
    depth, d, n = ada_w.shape
    tn = n // 8
    return pl.pallas_call(
        _ada_kernel,
        grid=(depth, n // tn),
        in_specs=[pl.BlockSpec((MOD_ROWS, d), lambda l, j: (0, 0)),
                  pl.BlockSpec((1, d, tn), lambda l, j: (l, 0, j)),
                  pl.BlockSpec((1, 1, tn), lambda l, j: (l, 0, j))],
        out_specs=pl.BlockSpec((1, MOD_ROWS, tn), lambda l, j: (l, 0, j)),
        out_shape=jax.ShapeDtypeStruct((depth, MOD_ROWS, n), F32),
        compiler_params=_params(("arbitrary", "arbitrary")),
        name="ada_mod",
    )(cvec, ada_w, ada_b.reshape(depth, 1, n))


def _mod_index(tiles_per_batch, ctx_row):
    if tiles_per_batch is None:
        return lambda i, *_: (ctx_row, 0, 0)
    return lambda i, *_: (i // tiles_per_batch, 0, 0)


def _ffn_kernel(x_ref, mod_ref, nw_ref, wi_ref, wo_ref, o_ref, *, k0, nrm, chunks):
    x = x_ref[...]
    h = _modnorm(x, nw_ref[nrm:nrm + 1, :], mod_ref[0, k0:k0 + 1, :], mod_ref[0, k0 + 1:k0 + 2, :]).astype(BF16)
    dff = wo_ref.shape[0]
    y = None
    for c0, c1 in chunks:
        g = _dot(h, wi_ref[:, c0:c1])
        u = _dot(h, wi_ref[:, dff + c0:dff + c1])
        part = _dot((g * _sigmoid(g) * u).astype(BF16), wo_ref[c0:c1, :])
        y = part if y is None else y + part
    o_ref[...] = x + (0.5 * mod_ref[0, k0 + 2:k0 + 3, :]) * y


def _ffn(x, mod, nw, w_in, w_out, lead, *, k0, nrm, tiles_per_batch, ctx_row):
    t, d = x.shape
    dff = w_out.shape[-2]
    tm = min(TM, t)
    edges = list(range(0, dff, FFN_CHUNK)) + [dff]
    chunks = tuple(zip(edges[:-1], edges[1:]))
    kern = functools.partial(_ffn_kernel, k0=k0, nrm=nrm, chunks=chunks)
    return pl.pallas_call(
        kern,
        grid=(t // tm,),
        in_specs=[pl.BlockSpec((tm, d), lambda i: (i, 0)),
                  pl.BlockSpec((1, N_MOD, d), _mod_index(tiles_per_batch, ctx_row)),
                  pl.BlockSpec((3, d), lambda i: (0, 0)),
                  _resident((d, 2 * dff), lead),
                  _resident((dff, d), lead)],
        out_specs=pl.BlockSpec((tm, d), lambda i: (i, 0)),
        out_shape=jax.ShapeDtypeStruct((t, d), F32),
        compiler_params=_params(("parallel",)),
        name="ffn_swiglu",
    )(x, mod, nw, w_in, w_out)


def _proj_kernel(*refs, rope):
    if rope:
        (x_ref, mod_ref, nw_ref, w_ref, g_ref, qkw_ref, cos_ref, sin_ref,
         ml_ref, mlg_ref, na_ref, gq_ref) = refs
    else:
        (x_ref, mod_ref, nw_ref, w_ref, g_ref, qkw_ref,
         ml_ref, mlg_ref, na_ref, gq_ref) = refs
    h = _modnorm(x_ref[...], nw_ref[1:2, :], mod_ref[0, 3:4, :], mod_ref[0, 4:5, :]).astype(BF16)
    gmat = g_ref[...]

    ml_ref[...] = _dot(h, w_ref[:, C_ML:C_MLG]).astype(BF16)
    gates = _dot(h, w_ref[:, C_MLG:C_NA]) + qkw_ref[2:3, 0:LANES]
    glane = lax.broadcasted_iota(jnp.int32, (1, LANES), 1)
    mlg_ref[...] = jnp.where((glane >= 2 * H_ML) & (glane < 4 * H_ML), _log_sigmoid(gates), gates)

    na = _dot(h, w_ref[:, C_NA:C_GQ])
    for j in range(3):
        sl = slice(2 * LANES * j, 2 * LANES * (j + 1))
        na_ref[:, sl] = _head_norm(na[:, sl], gmat, qkw_ref[0:1, sl]).astype(BF16)
    na_ref[:, 2 * D_NA:] = na[:, 2 * D_NA:].astype(BF16)

    gq = _dot(h, w_ref[:, C_GQ:C_END])
    if rope:
        lane = lax.broadcasted_iota(jnp.int32, (1, LANES), 1)
        first_half = (lane % HEAD_DIM) < (HEAD_DIM // 2)
        cos = cos_ref[...]
        sin = sin_ref[...]
    for j in range(2):
        t2 = _head_norm(gq[:, 2 * LANES * j:2 * LANES * (j + 1)], gmat, qkw_ref[1:2, 2 * LANES * j:2 * LANES * (j + 1)])
        for half in range(2):
            t = t2[:, LANES * half:LANES * (half + 1)]
            if rope:
                rot = jnp.where(first_half, pltpu.roll(t, LANES - HEAD_DIM // 2, axis=1),
                                pltpu.roll(t, HEAD_DIM // 2, axis=1))
                t = t * cos + rot * sin
            gq_ref[:, LANES * (2 * j + half):LANES * (2 * j + half + 1)] = t.astype(BF16)
    gq_ref[:, D_GQ + D_KV:] = gq[:, D_GQ + D_KV:].astype(BF16)


def _proj(x, mod, nw, w, li, gmat, qkw, rope_tabs, *, tiles_per_batch, ctx_row):
    t, d = x.shape
    tm = min(TM, t)
    rope = rope_tabs is not None
    in_specs = [pl.BlockSpec((tm, d), lambda i: (i, 0)),
                pl.BlockSpec((1, N_MOD, d), _mod_index(tiles_per_batch, ctx_row)),
                pl.BlockSpec((3, d), lambda i: (0, 0)),
                _resident((d, N_PROJ), (li,)),
                _resident(gmat.shape),
                _resident(qkw.shape)]
    args = [x, mod, nw, w, gmat, qkw]
    if rope:
        in_specs += [pl.BlockSpec((tm, LANES), lambda i: (i % tiles_per_batch, 0))] * 2
        args += list(rope_tabs)
    widths = (1024, LANES, 3 * D_NA, D_GQ + 2 * D_KV)
    dtypes = (BF16, F32, BF16, BF16)
    return pl.pallas_call(
        functools.partial(_proj_kernel, rope=rope),
        grid=(t // tm,),
        in_specs=in_specs,
        out_specs=[pl.BlockSpec((tm, wd), lambda i: (i, 0)) for wd in widths],
        out_shape=[jax.ShapeDtypeStruct((t, wd), dt) for wd, dt in zip(widths, dtypes)],
        compiler_params=_params(("parallel",)),
        name="mix_in_proj",
    )(*args)


def _ml_prep(d, g_ref, tri_ref):
    log2e = float(np.log2(np.e))
    lc = g_ref.shape[1]
    gates = g_ref[0] * log2e
    gates_t = gates.T
    ig_t = gates_t[0:8]
    lf_t = gates_t[0:16]
    b_t = _dot_f32_left(lf_t, tri_ref[1 - d])[8:16]
    btot_t = _dot_f32_left(lf_t, jnp.ones((lc, lc), BF16))[8:16]
    lf_al = pltpu.roll(gates, LANES - 8, axis=1)
    c_mat = gates - functools.reduce(jnp.add, [_dot(tri_ref[d], p) for p in _split_bf16(lf_al)])
    return ig_t, b_t, btot_t, c_mat


def _ml_head(d, h, prep, m_ref):
    ig_t, b_t, btot_t, c_mat = prep
    lc = c_mat.shape[0]
    r = 4 * d + h
    row = lax.broadcasted_iota(jnp.int32, (lc, lc), 0)
    col = lax.broadcasted_iota(jnp.int32, (lc, lc), 1)
    visible = (row <= col) if d == 0 else (row >= col)
    ig, b, b_tot = ig_t[r:r + 1], b_t[r:r + 1], btot_t[r:r + 1]
    m_prev = m_ref[r:r + 1, :]
    w_end = b_tot - b + ig
    m_new = jnp.maximum(b_tot + m_prev, jnp.max(w_end, axis=1, keepdims=True))
    a = jnp.exp2(w_end - m_new)
    decay = jnp.exp2(b_tot + m_prev - m_new)
    m_inter = b + m_prev
    logw = jnp.where(visible, c_mat[:, r:r + 1] + b, NEG)
    m_j = jnp.maximum(m_inter, jnp.max(logw, axis=0, keepdims=True))
    w = jnp.exp2(logw - m_j)
    m_ref[r:r + 1, :] = m_new
    return w, a, jnp.exp2(m_inter - m_j), jnp.exp2(-m_j), decay


def _ml_pair(d, p, head_a, head_b, q_ref, k_ref, v_ref, o_ref, st_ref):
    (w_a, a_a, g_a, fl_a, dec_a), (w_b, a_b, g_b, fl_b, dec_b) = head_a, head_b
    lc = q_ref.shape[1]
    low = _low_lanes()
    sl = slice(LANES * p, LANES * (p + 1))
    q = q_ref[0, :, sl]
    k = k_ref[0, :, sl] * ATTN_SCALE
    vt = jnp.concatenate([v_ref[0, :, sl].astype(F32).T, jnp.ones((LANES, lc), F32)], axis=0)
    head_row = (lax.broadcasted_iota(jnp.int32, (2 * LANES, 1), 0) % LANES) < HEAD_DIM
    vt16 = vt.astype(BF16)
    r_a = _dot(vt16, (_dot_nt(k, _one_head(q, low, 0)) * w_a).astype(BF16))
    r_b = _dot(vt16, (_dot_nt(k, _one_head(q, low, 1)) * w_b).astype(BF16))
    state = st_ref[d, p]
    r_i = _dot_nt(state.astype(BF16), q)
    r = jnp.where(head_row, r_a, r_b) + jnp.where(head_row, g_a, g_b) * r_i
    num, den = r[:LANES], r[LANES:]
    h_t = num / jnp.maximum(jnp.abs(den), jnp.where(head_row[:LANES], fl_a, fl_b))
    o_ref[0, :, sl] = h_t.T.astype(BF16)

    upd = _dot((vt * jnp.where(head_row, a_a, a_b)).astype(BF16), k)
    same_head = head_row == low
    dec = jnp.where(head_row, dec_a[:, :LANES], dec_b[:, :LANES])
    st_ref[d, p] = dec * state + jnp.where(same_head, upd, 0.0)


def _ml_step(fwd, bwd, tri_ref, st_ref, m_ref):
    dirs = (fwd, bwd)
    preps = [_ml_prep(d, refs[3], tri_ref) for d, refs in enumerate(dirs)]
    heads = [[_ml_head(d, h, preps[d], m_ref) for h in range(H_ML)] for d in range(2)]
    for p in range(H_ML // 2):
        for d, (q_ref, k_ref, v_ref, _, o_ref) in enumerate(dirs):
            _ml_pair(d, p, heads[d][2 * p], heads[d][2 * p + 1], q_ref, k_ref, v_ref, o_ref, st_ref)


def _mlstm_kernel(qf, kf, vf, gf, qb, kb, vb, gb, qc, kc, vc, gc, tri_ref,
                  hf_ref, hb_ref, hcf_ref, hcb_ref, st_ref, m_ref):
    c = pl.program_id(1)

    @pl.when(c == 0)
    def _():
        st_ref[...] = jnp.zeros_like(st_ref)
        m_ref[...] = jnp.zeros_like(m_ref)
        _ml_step((qc, kc, vc, gc, hcf_ref), (qc, kc, vc, gc, hcb_ref), tri_ref, st_ref, m_ref)

    @pl.when(c > 0)
    def _():
        _ml_step((qf, kf, vf, gf, hf_ref), (qb, kb, vb, gb, hb_ref), tri_ref, st_ref, m_ref)


def _mlstm(p_ml, p_mlg, pc_ml, pc_mlg, tri):
    b, s, _ = p_ml.shape
    l = pc_ml.shape[1]
    lc = ML_CHUNK
    assert l == lc and s % lc == 0
    nl = s // lc
    fwd = lambda c: jnp.maximum(c - 1, 0)
    bwd = lambda c: nl - 1 - jnp.maximum(c - 1, 0)

    def lat(idx, blk, width):
        return pl.BlockSpec((1, lc, width), lambda i, c: (i, idx(c), blk))

    def ctx(blk, width):
        return pl.BlockSpec((1, lc, width), lambda i, c: (i, 0, blk))

    in_specs = ([lat(fwd, 0, D_ML), lat(fwd, 1, D_ML), lat(fwd, 2, D_ML), lat(fwd, 0, LANES)]
                + [lat(bwd, 0, D_ML), lat(bwd, 1, D_ML), lat(bwd, 2, D_ML), lat(bwd, 0, LANES)]
                + [ctx(0, D_ML), ctx(1, D_ML), ctx(2, D_ML), ctx(0, LANES)]
                + [_resident((2, lc, lc))])
    out_specs = [lat(fwd, 0, D_ML), lat(bwd, 0, D_ML), ctx(0, D_ML), ctx(0, D_ML)]
    out_shape = [jax.ShapeDtypeStruct((b, s, D_ML), BF16)] * 2 + [jax.ShapeDtypeStruct((b, l, D_ML), BF16)] * 2
    return pl.pallas_call(
        _mlstm_kernel,
        grid=(b, nl + 1),
        in_specs=in_specs,
        out_specs=out_specs,
        out_shape=out_shape,
        scratch_shapes=[pltpu.VMEM((2, H_ML // 2, 2 * LANES, LANES), F32), pltpu.VMEM((2 * H_ML, lc), F32)],
        compiler_params=_params(("parallel", "arbitrary")),
        name="mlstm_bidir",
    )(p_ml, p_ml, p_ml, p_mlg, p_ml, p_ml, p_ml, p_mlg, pc_ml, pc_ml, pc_ml, pc_mlg, tri)


def _attend_heads(n_heads, score_fn, value_fn, s_scr, p_scr):
    s_scr[0] = score_fn(0)
    outs = []
    for i in range(n_heads):
        if i + 1 < n_heads:
            s_scr[(i + 1) % 2] = score_fn(i + 1)
        s = s_scr[i % 2]
        p_scr[i % 2] = jnp.exp2(s - jnp.max(s, axis=1, keepdims=True)).astype(BF16)
        r = _dot(p_scr[i % 2], value_fn(i))
        outs.append(r[:, :LANES] / r[:, LANES:])
    return outs


def _low_lanes():
    return lax.broadcasted_iota(jnp.int32, (1, LANES), 1) < HEAD_DIM


def _one_head(q, low, half):
    zero = jnp.zeros_like(q)
    return jnp.where(low, q, zero) if half == 0 else jnp.where(low, zero, q)


def _pair_outputs(o_ref, outs, low):
    for j in range(len(outs) // 2):
        o_ref[0, :, LANES * j:LANES * (j + 1)] = jnp.where(low, outs[2 * j], outs[2 * j + 1]).astype(BF16)


def _gqa_kernel(q_ref, k_ref, v_ref, kc_ref, vc_ref, o_ref, kall, vall, s_scr, p_scr):
    n_lat = k_ref.shape[1]

    @pl.when(pl.program_id(1) == 0)
    def _():
        kall[0:n_lat, :] = k_ref[0]
        kall[n_lat:, :] = kc_ref[0]
        vall[0:n_lat, 0:LANES] = v_ref[0]
        vall[n_lat:, 0:LANES] = vc_ref[0]
        vall[:, LANES:] = jnp.ones((vall.shape[0], LANES), BF16)

    low = _low_lanes()

    def score(i):
        j, half = divmod(i, 2)
        return _dot_nt(_one_head(q_ref[0, :, LANES * j:LANES * (j + 1)], low, half), kall[...])

    _pair_outputs(o_ref, _attend_heads(H_GQ, score, lambda i: vall[...], s_scr, p_scr), low)


def _gqa(p_gq, pc_gq):
    b, s, _ = p_gq.shape
    l = pc_gq.shape[1]
    tq = min(GQ_TQ, s)
    kblk, vblk = D_GQ // LANES, D_GQ // LANES + 1
    return pl.pallas_call(
        _gqa_kernel,
        grid=(b, s // tq),
        in_specs=[pl.BlockSpec((1, tq, D_GQ), lambda i, t: (i, t, 0)),
                  pl.BlockSpec((1, s, LANES), lambda i, t: (i, 0, kblk)),
                  pl.BlockSpec((1, s, LANES), lambda i, t: (i, 0, vblk)),
                  pl.BlockSpec((1, l, LANES), lambda i, t: (i, 0, kblk)),
                  pl.BlockSpec((1, l, LANES), lambda i, t: (i, 0, vblk))],
        out_specs=pl.BlockSpec((1, tq, D_GQ), lambda i, t: (i, t, 0)),
        out_shape=jax.ShapeDtypeStruct((b, s, D_GQ), BF16),
        scratch_shapes=[pltpu.VMEM((s + l, LANES), BF16), pltpu.VMEM((s + l, 2 * LANES), BF16),
                        pltpu.VMEM((2, tq, s + l), F32), pltpu.VMEM((2, tq, s + l), BF16)],
        compiler_params=_params(("parallel", "arbitrary")),
        name="gqa_latent",
    )(p_gq, p_gq, p_gq, pc_gq, pc_gq)


def _ctx_attn_kernel(q_ref, k_ref, v_ref, o_ref, s_scr, p_scr, *, shared_kv):
    low = _low_lanes()
    ones = jnp.ones((v_ref.shape[1], LANES), BF16)

    def kv_lanes(i):
        return slice(0, LANES) if shared_kv else slice(LANES * (i // 2), LANES * (i // 2 + 1))

    def score(i):
        j, half = divmod(i, 2)
        return _dot_nt(_one_head(q_ref[0, :, LANES * j:LANES * (j + 1)], low, half), k_ref[0, :, kv_lanes(i)])

    def value(i):
        return jnp.concatenate([v_ref[0, :, kv_lanes(i)], ones], axis=1)

    n_heads = 2 * (q_ref.shape[2] // LANES)
    _pair_outputs(o_ref, _attend_heads(n_heads, score, value, s_scr, p_scr), low)


def _ctx_attn(pc, *, qw, kw, shared_kv):
    b, l, _ = pc.shape
    kb = qw // kw
    return pl.pallas_call(
        functools.partial(_ctx_attn_kernel, shared_kv=shared_kv),
        grid=(b,),
        in_specs=[pl.BlockSpec((1, l, qw), lambda i: (i, 0, 0)),
                  pl.BlockSpec((1, l, kw), lambda i: (i, 0, kb)),
                  pl.BlockSpec((1, l, kw), lambda i: (i, 0, kb + 1))],
        out_specs=pl.BlockSpec((1, l, qw), lambda i: (i, 0, 0)),
        out_shape=jax.ShapeDtypeStruct((b, l, qw), BF16),
        scratch_shapes=[pltpu.VMEM((2, l, l), F32), pltpu.VMEM((2, l, l), BF16)],
        compiler_params=_params(("parallel",)),
        name="ctx_attn",
    )(pc, pc, pc)


def _na_kernel(*refs, n_rows):
    q_ref, k_subs, v_subs = refs[0], refs[1:1 + NA_SUBS], refs[1 + NA_SUBS:1 + 2 * NA_SUBS]
    kc_ref, vc_ref, tz_ref, o_ref, kall, vall, s_scr, p_scr = refs[1 + 2 * NA_SUBS:]
    i = pl.program_id(1)
    r0 = i * NA_QROWS
    start = jnp.clip(r0 - NA_WIN_R // 2, 0, n_rows - NA_BAND)
    delta = start - r0
    nq, nk = NA_QROWS * GRID_W, NA_BAND * GRID_W
    sub = nk // NA_SUBS
    n_ctx = kc_ref.shape[1]

    for t, (kr, vr) in enumerate(zip(k_subs + (kc_ref,), v_subs + (vc_ref,))):
        rows = slice(sub * t, sub * t + kr.shape[1])
        kall[rows, :] = kr[0]
        for j in range(D_NA // LANES):
            vall[j, rows, 0:LANES] = vr[0, :, LANES * j:LANES * (j + 1)]
    vall[:, :, LANES:] = jnp.ones((D_NA // LANES, nk + n_ctx, LANES), BF16)

    qrow = r0 + lax.broadcasted_iota(jnp.int32, (nq, nk), 0) // GRID_W
    krow = start + lax.broadcasted_iota(jnp.int32, (nq, nk), 1) // GRID_W
    first = jnp.clip(qrow - NA_WIN_R // 2, 0, n_rows - NA_WIN_R)
    row_mask = jnp.where((krow >= first) & (krow < first + NA_WIN_R), 0.0, NEG)
    low = _low_lanes()
    no_bias = jnp.zeros((nq, n_ctx), F32)

    def score(h):
        j, half = divmod(h, 2)
        sl = slice(LANES * j, LANES * (j + 1))
        slabs = []
        for a in range(NA_QROWS):
            pieces = [tz_ref[h, jnp.clip(delta + 2 * bp - a, -8, 7) + 8] for bp in range(NA_BAND // 2)]
            slabs.append(jnp.concatenate(pieces, axis=1))
        bias = jnp.concatenate([jnp.concatenate(slabs, axis=0) + row_mask, no_bias], axis=1)
        return _dot_nt(_one_head(q_ref[0, :, sl], low, half), kall[:, sl]) + bias

    _pair_outputs(o_ref, _attend_heads(H_NA, score, lambda h: vall[h // 2], s_scr, p_scr), low)


def _na(p_na, pc_na, tz, li):
    b, s, _ = p_na.shape
    l = pc_na.shape[1]
    n_rows = s // GRID_W
    assert n_rows % NA_QROWS == 0 and n_rows >= NA_BAND
    nq, nk = NA_QROWS * GRID_W, NA_BAND * GRID_W
    sub = nk // NA_SUBS
    rows_per_sub = NA_BAND // NA_SUBS
    assert all(v % rows_per_sub == 0 for v in (NA_QROWS, NA_WIN_R // 2, n_rows - NA_BAND))

    def band(t, blk):
        def idx(i, r):
            start = jnp.clip(r * NA_QROWS - NA_WIN_R // 2, 0, n_rows - NA_BAND)
            return (i, start // rows_per_sub + t, blk)
        return pl.BlockSpec((1, sub, D_NA), idx)

    in_specs = ([pl.BlockSpec((1, nq, D_NA), lambda i, r: (i, r, 0))]
                + [band(t, 1) for t in range(NA_SUBS)] + [band(t, 2) for t in range(NA_SUBS)]
                + [pl.BlockSpec((1, l, D_NA), lambda i, r: (i, 0, 1)),
                   pl.BlockSpec((1, l, D_NA), lambda i, r: (i, 0, 2)),
                   _resident(tz.shape[1:], (li,))])
    return pl.pallas_call(
        functools.partial(_na_kernel, n_rows=n_rows),
        grid=(b, n_rows // NA_QROWS),
        in_specs=in_specs,
        out_specs=pl.BlockSpec((1, nq, D_NA), lambda i, r: (i, r, 0)),
        out_shape=jax.ShapeDtypeStruct((b, s, D_NA), BF16),
        scratch_shapes=[pltpu.VMEM((nk + l, D_NA), BF16), pltpu.VMEM((D_NA // LANES, nk + l, 2 * LANES), BF16),
                        pltpu.VMEM((2, nq, nk + l), F32), pltpu.VMEM((2, nq, nk + l), BF16)],
        compiler_params=_params(("parallel", "arbitrary")),
        name="na_latent",
    )(p_na, *([p_na] * (2 * NA_SUBS)), pc_na, pc_na, tz)


def _merge_kernel(x_ref, mod_ref, nw_ref, hf_ref, hb_ref, og_ref, mlw_ref, g_ref, na_ref, gq_ref,
                  wg_ref, wml_ref, wna_ref, wgq_ref, wo_ref, o_ref):
    x = x_ref[...]
    d = x.shape[1]
    hx = _modnorm(x, nw_ref[1:2, :], mod_ref[0, 3:4, :], mod_ref[0, 4:5, :]).astype(BF16)
    h = hf_ref[...].astype(F32) + hb_ref[...].astype(F32)
    o_ml = (_head_norm(h, g_ref[...], mlw_ref[...]) * _sigmoid(og_ref[...].astype(F32))).astype(BF16)
    y = None
    for j, (o_br, w_br) in enumerate(((o_ml, wml_ref), (na_ref[...], wna_ref), (gq_ref[...], wgq_ref))):
        part = _sigmoid(_dot(hx, wg_ref[:, d * j:d * (j + 1)])) * _dot(o_br, w_br[...])
        y = part if y is None else y + part
    o_ref[...] = x + mod_ref[0, 5:6, :] * _dot(y.astype(BF16), wo_ref[...])


def _merge(x, mod, nw, hf, hb, p_ml, mlw, gmat, o_na, o_gq, wg, wml, wna, wgq, wo, li,
           *, tiles_per_batch, ctx_row):
    t, d = x.shape
    tm = min(TM, t)
    row = lambda wd, blk=0: pl.BlockSpec((tm, wd), lambda i: (i, blk))
    lead = (li,)
    return pl.pallas_call(
        _merge_kernel,
        grid=(t // tm,),
        in_specs=[row(d),
                  pl.BlockSpec((1, N_MOD, d), _mod_index(tiles_per_batch, ctx_row)),
                  pl.BlockSpec((3, d), lambda i: (0, 0)),
                  row(D_ML), row(D_ML), row(D_ML, 3),
                  _resident((1, D_ML)), _resident(gmat.shape),
                  row(D_NA), row(D_GQ),
                  _resident((d, 3 * d), lead), _resident((D_ML, d), lead), _resident((D_NA, d), lead),
                  _resident((D_GQ, d), lead), _resident((d, d), lead)],
        out_specs=row(d),
        out_shape=jax.ShapeDtypeStruct((t, d), F32),
        compiler_params=_params(("parallel",)),
        name="branch_merge",
    )(x, mod, nw, hf, hb, p_ml, mlw, gmat, o_na, o_gq, wg, wml, wna, wgq, wo)


def _proj_weight(w):
    d = w.shape[-2]
    o = 0
    seg = {}
    for name, width in (("ml_k", D_ML), ("ml_v", D_ML), ("ml_g", 4 * H_ML), ("na_k", D_NA), ("na_v", D_NA),
                        ("gq_k", D_KV), ("gq_v", D_KV), ("ml_q", D_ML), ("ml_o", D_ML), ("na_q", D_NA),
                        ("gq_q", D_GQ), ("br_g", 3 * d)):
        seg[name] = w[..., o:o + width].astype(BF16)
        o += width
    gq_q = jnp.concatenate([seg["gq_q"][..., HEAD_DIM * h:HEAD_DIM * (h + 1)] for h in GQ_HEAD_ORDER], axis=-1)
    pad = jnp.zeros(w.shape[:-1] + (LANES - 4 * H_ML,), BF16)
    out = jnp.concatenate([seg["ml_q"], seg["ml_k"], seg["ml_v"], seg["ml_o"], seg["ml_g"][..., ML_GATE_ORDER], pad,
                           seg["na_q"], seg["na_k"], seg["na_v"], gq_q, seg["gq_k"], seg["gq_v"]], axis=-1)
    return out, seg["br_g"]


def _rope_tables(n_tok):
    t = np.arange(n_tok)
    row = (t // GRID_W).astype(np.float64)
    col = (t % GRID_W).astype(np.float64)
    n_freq = HEAD_DIM // 4
    inv = ROPE_THETA ** (-np.arange(n_freq, dtype=np.float64) / n_freq)
    ang = np.concatenate([row[:, None] * inv, col[:, None] * inv], axis=-1)
    cos, sin = np.cos(ang), np.sin(ang)
    cos_t = np.tile(cos, (1, LANES // (HEAD_DIM // 2)))
    sin_t = np.tile(np.concatenate([-sin, sin], axis=-1), (1, LANES // HEAD_DIM))
    return jnp.asarray(cos_t, F32), jnp.asarray(sin_t, F32)


def _na_bias_table(rpb):
    col = np.arange(GRID_W)
    first = np.clip(col - NA_WIN_C // 2, 0, GRID_W - NA_WIN_C)
    in_win = (col[None, :] >= first[:, None]) & (col[None, :] < first[:, None] + NA_WIN_C)
    side = GRID_W - NA_WIN_C
    width = 2 * GRID_W
    rows = jnp.pad(rpb, ((0, 0), (0, 0), (1, 1), (side, width - side - rpb.shape[-1])))
    lead = rows.shape[:-1]
    flat = jnp.broadcast_to(rows[..., None, :], lead + (GRID_W, width)).reshape(lead + (GRID_W * width,))
    skew = flat[..., :GRID_W * (width - 1)].reshape(lead + (GRID_W, width - 1))
    full = skew[..., GRID_W - 1:2 * GRID_W - 1] * float(np.log2(np.e))
    row_ok = np.zeros((2 * NA_WIN_R + 1,), bool)
    row_ok[1:-1] = True
    full = jnp.where(jnp.asarray(in_win[None, None, None] & row_ok[None, None, :, None, None]), full, NEG)
    return jnp.concatenate([full[:, :, :-1], full[:, :, 1:]], axis=-1).astype(F32)


def kernel(x, c, ctx, c_ctx, ada_w, ada_b, norm_w, ffn_w_in, ffn_w_out, mix_w_in, ml_gate_b, ml_norm_w,
           na_qk_w, na_rpb, gq_qk_w, w_br_ml, w_br_na, w_br_gq, w_out):
    b, s, d = x.shape
    l = ctx.shape[1]
    depth = ada_w.shape[0]
    assert b < MOD_ROWS and s % TM == 0 and (b * l) % min(TM, b * l) == 0
    ctx_row = b
    tiles_per_batch = s // TM

    cvec = jnp.zeros((MOD_ROWS, d), F32).at[:b].set(c).at[b].set(c_ctx)
    mod = _ada(cvec, ada_w, ada_b).reshape(depth, MOD_ROWS, N_MOD, d)

    lane = np.arange(2 * LANES)
    gmat = jnp.asarray((lane[:, None] // HEAD_DIM) == (lane[None, :] // HEAD_DIM), BF16)
    idx = np.arange(ML_CHUNK)
    tri = jnp.asarray(np.stack([idx[:, None] >= idx[None, :], idx[:, None] <= idx[None, :]]), BF16)
    rope_tabs = _rope_tables(s)

    w_in, w_o = ffn_w_in.astype(BF16), ffn_w_out.astype(BF16)
    w_proj, wg = _proj_weight(mix_w_in)
    wml, wna, wo = w_br_ml.astype(BF16), w_br_na.astype(BF16), w_out.astype(BF16)
    wgq = jnp.concatenate([w_br_gq[:, HEAD_DIM * h:HEAD_DIM * (h + 1)] for h in GQ_HEAD_ORDER], axis=1).astype(BF16)
    tz = _na_bias_table(na_rpb)

    xl = x.reshape(b * s, d)
    xc = ctx.reshape(b * l, d)
    lat = dict(tiles_per_batch=tiles_per_batch, ctx_row=ctx_row)
    con = dict(tiles_per_batch=None, ctx_row=ctx_row)
    for li in range(depth):
        ctx_out = li < depth - 1
        qkw = jnp.zeros((8, 2 * D_NA), F32)
        qkw = qkw.at[0, :D_NA].set(jnp.tile(na_qk_w[li, 0], H_NA) * Q_PRESCALE)
        qkw = qkw.at[0, D_NA:].set(jnp.tile(na_qk_w[li, 1], H_NA))
        qkw = qkw.at[1, :D_GQ].set(jnp.tile(gq_qk_w[li, 0], H_GQ) * Q_PRESCALE)
        qkw = qkw.at[1, D_GQ:D_GQ + D_KV].set(jnp.tile(gq_qk_w[li, 1], H_KV))
        qkw = qkw.at[2, :4 * H_ML].set(ml_gate_b[li][ML_GATE_ORDER])
        mlw = ml_norm_w[li].reshape(1, D_ML)
        m, nw = mod[li], norm_w[li]

        xl = _ffn(xl, m, nw, w_in, w_o, (li, 0), k0=0, nrm=0, **lat)
        xc = _ffn(xc, m, nw, w_in, w_o, (li, 0), k0=0, nrm=0, **con)

        p_ml, p_mlg, p_na, p_gq = _proj(xl, m, nw, w_proj, li, gmat, qkw, rope_tabs, **lat)
        pc_ml, pc_mlg, pc_na, pc_gq = _proj(xc, m, nw, w_proj, li, gmat, qkw, None, **con)
        seq = lambda a: a.reshape(b, s, a.shape[-1])
        cseq = lambda a: a.reshape(b, l, a.shape[-1])

        hf, hb, hcf, hcb = _mlstm(seq(p_ml), seq(p_mlg), cseq(pc_ml), cseq(pc_mlg), tri)
        o_na = _na(seq(p_na), cseq(pc_na), tz, li)
        o_gq = _gqa(seq(p_gq), cseq(pc_gq))
        flat = lambda a: a.reshape(-1, a.shape[-1])
        xl = _merge(xl, m, nw, flat(hf), flat(hb), p_ml, mlw, gmat, flat(o_na), flat(o_gq),
                    wg, wml, wna, wgq, wo, li, **lat)
        xl = _ffn(xl, m, nw, w_in, w_o, (li, 1), k0=6, nrm=2, **lat)
        if ctx_out:
            co_na = _ctx_attn(cseq(pc_na), qw=D_NA, kw=D_NA, shared_kv=False)
            co_gq = _ctx_attn(cseq(pc_gq), qw=D_GQ, kw=D_KV, shared_kv=True)
            xc = _merge(xc, m, nw, flat(hcf), flat(hcb), pc_ml, mlw, gmat, flat(co_na), flat(co_gq),
                        wg, wml, wna, wgq, wo, li, **con)
            xc = _ffn(xc, m, nw, w_in, w_o, (li, 1), k0=6, nrm=2, **con)
    return xl.reshape(b, s, d)
```

```python
import functools

import numpy as np
import jax
import jax.numpy as jnp
from jax import lax
from jax.experimental import pallas as pl
from jax.experimental.pallas import tpu as pltpu

F32 = jnp.float32
BF16 = jnp.bfloat16

HEAD_DIM = 64
LANES = 128
H_ML, H_NA, H_GQ, H_KV = 4, 6, 6, 2
D_ML, D_NA, D_GQ, D_KV = 256, 384, 384, 128
GRID_W = 64
NA_WIN_R, NA_WIN_C = 8, 16
ROPE_THETA = 10000.0
EPS = 1e-6
N_MOD = 9
ATTN_SCALE = HEAD_DIM ** -0.5
NEG = -1e30

ML_CHUNK = 256
NA_QROWS = 8
NA_HALF = 4
NA_BAND = 16
NA_CBAND = 12
NA_SUBS = 4
TM = 512
GQ_TQ = 256
FFN_CHUNK = 768
Q_PRESCALE = ATTN_SCALE * float(np.log2(np.e))
MOD_ROWS = 16
VMEM_LIMIT = 56 * 1024 * 1024

C_ML, C_MLG, C_NA, C_GQ, C_END = 0, 1024, 1152, 2304, 2944
N_PROJ = C_END
GQ_HEAD_ORDER = (0, 3, 1, 4, 2, 5)
ML_GATE_ORDER = np.array([0, 1, 2, 3, 8, 9, 10, 11, 4, 5, 6, 7, 12, 13, 14, 15])


def _dot(a, b):
    return jnp.dot(a, b, preferred_element_type=F32)


def _dot_nt(a, b):
    return lax.dot_general(a, b, (((1,), (1,)), ((), ())), preferred_element_type=F32)


def _sigmoid(x):
    return 1.0 / (1.0 + jnp.exp(-x))


def _log_sigmoid(x):
    return jnp.minimum(x, 0.0) - jnp.log1p(jnp.exp(-jnp.abs(x)))


def _split_bf16(x):
    parts = []
    for _ in range(3):
        p = x.astype(BF16)
        parts.append(p)
        x = x - p.astype(F32)
    return parts


def _dot_f32_left(x, m01):
    return functools.reduce(jnp.add, [_dot(p, m01) for p in _split_bf16(x)])


def _modnorm(x, nw, shift, scale):
    ms = jnp.mean(x * x, axis=-1, keepdims=True)
    return (x * lax.rsqrt(ms + EPS) * nw) * (1.0 + scale) + shift


def _head_norm(t, gmat, wrow):
    ss = _dot((t * t).astype(BF16), gmat)
    return t * lax.rsqrt(ss * (1.0 / HEAD_DIM) + EPS) * wrow


def _resident(shape, lead=()):
    return pl.BlockSpec((None,) * len(lead) + tuple(shape), lambda *_: tuple(lead) + (0,) * len(shape),
                        pipeline_mode=pl.Buffered(1))


def _params(sem):
    return pltpu.CompilerParams(dimension_semantics=sem, vmem_limit_bytes=VMEM_LIMIT)


def _ada_kernel(c_ref, w_ref, b_ref, o_ref):
    c = c_ref[...]
    s = (c * _sigmoid(c)).astype(BF16)
    o_ref[0] = _dot(s, w_ref[0].astype(BF16)) + b_ref[0]


def _ada(cvec, ada_w, ada_b):
    depth, d, n = ada_w.shape
    tn = n // 8
    return pl.pallas_call(
        _ada_kernel,
        grid=(depth, n // tn),
        in_specs=[pl.BlockSpec((MOD_ROWS, d), lambda l, j: (0, 0)),
                  pl.BlockSpec((1, d, tn), lambda l, j: (l, 0, j)),
                  pl.BlockSpec((1, 1, tn), lambda l, j: (l, 0, j))],
        out_specs=pl.BlockSpec((1, MOD_ROWS, tn), lambda l, j: (l, 0, j)),
        out_shape=jax.ShapeDtypeStruct((depth, MOD_ROWS, n), F32),
        compiler_params=_params(("arbitrary", "arbitrary")),
        name="ada_mod",
    )(cvec, ada_w, ada_b.reshape(depth, 1, n))


def _mod_index(tiles_per_batch, ctx_row):
    if tiles_per_batch is None:
        return lambda i, *_: (ctx_row, 0, 0)
    return lambda i, *_: (i // tiles_per_batch, 0, 0)


def _ffn_kernel(x_ref, mod_ref, nw_ref, wi_ref, wo_ref, o_ref, *, k0, nrm, chunks):
    x = x_ref[...]
    h = _modnorm(x, nw_ref[nrm:nrm + 1, :], mod_ref[0, k0:k0 + 1, :], mod_ref[0, k0 + 1:k0 + 2, :]).astype(BF16)
    dff = wo_ref.shape[0]
    y = None
    for c0, c1 in chunks:
        g = _dot(h, wi_ref[:, c0:c1])
        u = _dot(h, wi_ref[:, dff + c0:dff + c1])
        part = _dot((g * _sigmoid(g) * u).astype(BF16), wo_ref[c0:c1, :])
        y = part if y is None else y + part
    o_ref[...] = x + (0.5 * mod_ref[0, k0 + 2:k0 + 3, :]) * y


def _ffn(x, mod, nw, w_in, w_out, lead, *, k0, nrm, tiles_per_batch, ctx_row):
    t, d = x.shape
    dff = w_out.shape[-2]
    tm = min(TM, t)
    edges = list(range(0, dff, FFN_CHUNK)) + [dff]
    chunks = tuple(zip(edges[:-1], edges[1:]))
    kern = functools.partial(_ffn_kernel, k0=k0, nrm=nrm, chunks=chunks)
    return pl.pallas_call(
        kern,
        grid=(t // tm,),
        in_specs=[pl.BlockSpec((tm, d), lambda i: (i, 0)),
                  pl.BlockSpec((1, N_MOD, d), _mod_index(tiles_per_batch, ctx_row)),
                  pl.BlockSpec((3, d), lambda i: (0, 0)),
                  _resident((d, 2 * dff), lead),
                  _resident((dff, d), lead)],
        out_specs=pl.BlockSpec((tm, d), lambda i: (i, 0)),
        out_shape=jax.ShapeDtypeStruct((t, d), F32),
        compiler_params=_params(("parallel",)),
        name="ffn_swiglu",
    )(x, mod, nw, w_in, w_out)


def _proj_kernel(*refs, rope):
    if rope:
        (x_ref, mod_ref, nw_ref, w_ref, g_ref, qkw_ref, cos_ref, sin_ref,
         ml_ref, mlg_ref, na_ref, gq_ref) = refs
    else:
        (x_ref, mod_ref, nw_ref, w_ref, g_ref, qkw_ref,
         ml_ref, mlg_ref, na_ref, gq_ref) = refs
    h = _modnorm(x_ref[...], nw_ref[1:2, :], mod_ref[0, 3:4, :], mod_ref[0, 4:5, :]).astype(BF16)
    gmat = g_ref[...]

    ml_ref[...] = _dot(h, w_ref[:, C_ML:C_MLG]).astype(BF16)
    gates = _dot(h, w_ref[:, C_MLG:C_NA]) + qkw_ref[2:3, 0:LANES]
    glane = lax.broadcasted_iota(jnp.int32, (1, LANES), 1)
    mlg_ref[...] = jnp.where((glane >= 2 * H_ML) & (glane < 4 * H_ML), _log_sigmoid(gates), gates)

    na = _dot(h, w_ref[:, C_NA:C_GQ])
    for j in range(3):
        sl = slice(2 * LANES * j, 2 * LANES * (j + 1))
        na_ref[:, sl] = _head_norm(na[:, sl], gmat, qkw_ref[0:1, sl]).astype(BF16)
    na_ref[:, 2 * D_NA:] = na[:, 2 * D_NA:].astype(BF16)

    gq = _dot(h, w_ref[:, C_GQ:C_END])
    if rope:
        lane = lax.broadcasted_iota(jnp.int32, (1, LANES), 1)
        first_half = (lane % HEAD_DIM) < (HEAD_DIM // 2)
        cos = cos_ref[...]
        sin = sin_ref[...]
    for j in range(2):
        t2 = _head_norm(gq[:, 2 * LANES * j:2 * LANES * (j + 1)], gmat, qkw_ref[1:2, 2 * LANES * j:2 * LANES * (j + 1)])
        for half in range(2):
            t = t2[:, LANES * half:LANES * (half + 1)]
            if rope:
                rot = jnp.where(first_half, pltpu.roll(t, LANES - HEAD_DIM // 2, axis=1),
                                pltpu.roll(t, HEAD_DIM // 2, axis=1))
                t = t * cos + rot * sin
            gq_ref[:, LANES * (2 * j + half):LANES * (2 * j + half + 1)] = t.astype(BF16)
    gq_ref[:, D_GQ + D_KV:] = gq[:, D_GQ + D_KV:].astype(BF16)


def _proj(x, mod, nw, w, li, gmat, qkw, rope_tabs, *, tiles_per_batch, ctx_row):
    t, d = x.shape
    tm = min(TM, t)
    rope = rope_tabs is not None
    in_specs = [pl.BlockSpec((tm, d), lambda i: (i, 0)),
                pl.BlockSpec((1, N_MOD, d), _mod_index(tiles_per_batch, ctx_row)),
                pl.BlockSpec((3, d), lambda i: (0, 0)),
                _resident((d, N_PROJ), (li,)),
                _resident(gmat.shape),
                _resident(qkw.shape)]
    args = [x, mod, nw, w, gmat, qkw]
    if rope:
        in_specs += [pl.BlockSpec((tm, LANES), lambda i: (i % tiles_per_batch, 0))] * 2
        args += list(rope_tabs)
    widths = (1024, LANES, 3 * D_NA, D_GQ + 2 * D_KV)
    dtypes = (BF16, F32, BF16, BF16)
    return pl.pallas_call(
        functools.partial(_proj_kernel, rope=rope),
        grid=(t // tm,),
        in_specs=in_specs,
        out_specs=[pl.BlockSpec((tm, wd), lambda i: (i, 0)) for wd in widths],
        out_shape=[jax.ShapeDtypeStruct((t, wd), dt) for wd, dt in zip(widths, dtypes)],
        compiler_params=_params(("parallel",)),
        name="mix_in_proj",
    )(*args)


def _ml_prep(d, g_ref, tri_ref):
    log2e = float(np.log2(np.e))
    lc = g_ref.shape[1]
    gates = g_ref[0] * log2e
    gates_t = gates.T
    ig_t = gates_t[0:8]
    lf_t = gates_t[0:16]
    b_t = _dot_f32_left(lf_t, tri_ref[1 - d])[8:16]
    btot_t = _dot_f32_left(lf_t, jnp.ones((lc, lc), BF16))[8:16]
    lf_al = pltpu.roll(gates, LANES - 8, axis=1)
    c_mat = gates - functools.reduce(jnp.add, [_dot(tri_ref[d], p) for p in _split_bf16(lf_al)])
    return ig_t, b_t, btot_t, c_mat


def _ml_head(d, h, prep, m_ref):
    ig_t, b_t, btot_t, c_mat = prep
    lc = c_mat.shape[0]
    r = 4 * d + h
    row = lax.broadcasted_iota(jnp.int32, (lc, lc), 0)
    col = lax.broadcasted_iota(jnp.int32, (lc, lc), 1)
    visible = (row <= col) if d == 0 else (row >= col)
    ig, b, b_tot = ig_t[r:r + 1], b_t[r:r + 1], btot_t[r:r + 1]
    m_prev = m_ref[r:r + 1, :]
    w_end = b_tot - b + ig
    m_new = jnp.maximum(b_tot + m_prev, jnp.max(w_end, axis=1, keepdims=True))
    a = jnp.exp2(w_end - m_new)
    decay = jnp.exp2(b_tot + m_prev - m_new)
    m_inter = b + m_prev
    logw = jnp.where(visible, c_mat[:, r:r + 1] + b, NEG)
    m_j = jnp.maximum(m_inter, jnp.max(logw, axis=0, keepdims=True))
    w = jnp.exp2(logw - m_j)
    m_ref[r:r + 1, :] = m_new
    return w, a, jnp.exp2(m_inter - m_j), jnp.exp2(-m_j), decay


def _ml_pair(d, p, head_a, head_b, q_ref, k_ref, v_ref, o_ref, st_ref):
    (w_a, a_a, g_a, fl_a, dec_a), (w_b, a_b, g_b, fl_b, dec_b) = head_a, head_b
    lc = q_ref.shape[1]
    low = _low_lanes()
    sl = slice(LANES * p, LANES * (p + 1))
    q = q_ref[0, :, sl]
    k = k_ref[0, :, sl] * ATTN_SCALE
    vt = jnp.concatenate([v_ref[0, :, sl].astype(F32).T, jnp.ones((LANES, lc), F32)], axis=0)
    head_row = (lax.broadcasted_iota(jnp.int32, (2 * LANES, 1), 0) % LANES) < HEAD_DIM
    vt16 = vt.astype(BF16)
    r_a = _dot(vt16, (_dot_nt(k, _one_head(q, low, 0)) * w_a).astype(BF16))
    r_b = _dot(vt16, (_dot_nt(k, _one_head(q, low, 1)) * w_b).astype(BF16))
    state = st_ref[d, p]
    r_i = _dot_nt(state.astype(BF16), q)
    r = jnp.where(head_row, r_a, r_b) + jnp.where(head_row, g_a, g_b) * r_i
    num, den = r[:LANES], r[LANES:]
    h_t = num / jnp.maximum(jnp.abs(den), jnp.where(head_row[:LANES], fl_a, fl_b))
    o_ref[0, :, sl] = h_t.T.astype(BF16)

    upd = _dot((vt * jnp.where(head_row, a_a, a_b)).astype(BF16), k)
    same_head = head_row == low
    dec = jnp.where(head_row, dec_a[:, :LANES], dec_b[:, :LANES])
    st_ref[d, p] = dec * state + jnp.where(same_head, upd, 0.0)


def _ml_step(fwd, bwd, tri_ref, st_ref, m_ref):
    dirs = (fwd, bwd)
    preps = [_ml_prep(d, refs[3], tri_ref) for d, refs in enumerate(dirs)]
    heads = [[_ml_head(d, h, preps[d], m_ref) for h in range(H_ML)] for d in range(2)]
    for p in range(H_ML // 2):
        for d, (q_ref, k_ref, v_ref, _, o_ref) in enumerate(dirs):
            _ml_pair(d, p, heads[d][2 * p], heads[d][2 * p + 1], q_ref, k_ref, v_ref, o_ref, st_ref)


def _mlstm_kernel(qf, kf, vf, gf, qb, kb, vb, gb, qc, kc, vc, gc, tri_ref,
                  hf_ref, hb_ref, hcf_ref, hcb_ref, st_ref, m_ref):
    c = pl.program_id(1)

    @pl.when(c == 0)
    def _():
        st_ref[...] = jnp.zeros_like(st_ref)
        m_ref[...] = jnp.zeros_like(m_ref)
        _ml_step((qc, kc, vc, gc, hcf_ref), (qc, kc, vc, gc, hcb_ref), tri_ref, st_ref, m_ref)

    @pl.when(c > 0)
    def _():
        _ml_step((qf, kf, vf, gf, hf_ref), (qb, kb, vb, gb, hb_ref), tri_ref, st_ref, m_ref)


def _mlstm(p_ml, p_mlg, pc_ml, pc_mlg, tri):
    b, s, _ = p_ml.shape
    l = pc_ml.shape[1]
    lc = ML_CHUNK
    assert l == lc and s % lc == 0
    nl = s // lc
    fwd = lambda c: jnp.maximum(c - 1, 0)
    bwd = lambda c: nl - 1 - jnp.maximum(c - 1, 0)

    def lat(idx, blk, width):
        return pl.BlockSpec((1, lc, width), lambda i, c: (i, idx(c), blk))

    def ctx(blk, width):
        return pl.BlockSpec((1, lc, width), lambda i, c: (i, 0, blk))

    in_specs = ([lat(fwd, 0, D_ML), lat(fwd, 1, D_ML), lat(fwd, 2, D_ML), lat(fwd, 0, LANES)]
                + [lat(bwd, 0, D_ML), lat(bwd, 1, D_ML), lat(bwd, 2, D_ML), lat(bwd, 0, LANES)]
                + [ctx(0, D_ML), ctx(1, D_ML), ctx(2, D_ML), ctx(0, LANES)]
                + [_resident((2, lc, lc))])
    out_specs = [lat(fwd, 0, D_ML), lat(bwd, 0, D_ML), ctx(0, D_ML), ctx(0, D_ML)]
    out_shape = [jax.ShapeDtypeStruct((b, s, D_ML), BF16)] * 2 + [jax.ShapeDtypeStruct((b, l, D_ML), BF16)] * 2
    return pl.pallas_call(
        _mlstm_kernel,
        grid=(b, nl + 1),
        in_specs=in_specs,
        out_specs=out_specs,
        out_shape=out_shape,
        scratch_shapes=[pltpu.VMEM((2, H_ML // 2, 2 * LANES, LANES), F32), pltpu.VMEM((2 * H_ML, lc), F32)],
        compiler_params=_params(("parallel", "arbitrary")),
        name="mlstm_bidir",
    )(p_ml, p_ml, p_ml, p_mlg, p_ml, p_ml, p_ml, p_mlg, pc_ml, pc_ml, pc_ml, pc_mlg, tri)


def _attend_heads(n_heads, score_fn, value_fn, s_scr, p_scr):
    s_scr[0] = score_fn(0)
    outs = []
    for i in range(n_heads):
        if i + 1 < n_heads:
            s_scr[(i + 1) % 2] = score_fn(i + 1)
        s = s_scr[i % 2]
        p_scr[i % 2] = jnp.exp2(s - jnp.max(s, axis=1, keepdims=True)).astype(BF16)
        r = _dot(p_scr[i % 2], value_fn(i))
        outs.append(r[:, :LANES] / r[:, LANES:])
    return outs


def _low_lanes():
    return lax.broadcasted_iota(jnp.int32, (1, LANES), 1) < HEAD_DIM


def _one_head(q, low, half):
    zero = jnp.zeros_like(q)
    return jnp.where(low, q, zero) if half == 0 else jnp.where(low, zero, q)


def _pair_outputs(o_ref, outs, low):
    for j in range(len(outs) // 2):
        o_ref[0, :, LANES * j:LANES * (j + 1)] = jnp.where(low, outs[2 * j], outs[2 * j + 1]).astype(BF16)


def _gqa_kernel(q_ref, k_ref, v_ref, kc_ref, vc_ref, o_ref, kall, vall, s_scr, p_scr):
    n_lat = k_ref.shape[1]

    @pl.when(pl.program_id(1) == 0)
    def _():
        kall[0:n_lat, :] = k_ref[0]
        kall[n_lat:, :] = kc_ref[0]
        vall[0:n_lat, 0:LANES] = v_ref[0]
        vall[n_lat:, 0:LANES] = vc_ref[0]
        vall[:, LANES:] = jnp.ones((vall.shape[0], LANES), BF16)

    low = _low_lanes()

    def score(i):
        j, half = divmod(i, 2)
        return _dot_nt(_one_head(q_ref[0, :, LANES * j:LANES * (j + 1)], low, half), kall[...])

    _pair_outputs(o_ref, _attend_heads(H_GQ, score, lambda i: vall[...], s_scr, p_scr), low)


def _gqa(p_gq, pc_gq):
    b, s, _ = p_gq.shape
    l = pc_gq.shape[1]
    tq = min(GQ_TQ, s)
    kblk, vblk = D_GQ // LANES, D_GQ // LANES + 1
    return pl.pallas_call(
        _gqa_kernel,
        grid=(b, s // tq),
        in_specs=[pl.BlockSpec((1, tq, D_GQ), lambda i, t: (i, t, 0)),
                  pl.BlockSpec((1, s, LANES), lambda i, t: (i, 0, kblk)),
                  pl.BlockSpec((1, s, LANES), lambda i, t: (i, 0, vblk)),
                  pl.BlockSpec((1, l, LANES), lambda i, t: (i, 0, kblk)),
                  pl.BlockSpec((1, l, LANES), lambda i, t: (i, 0, vblk))],
        out_specs=pl.BlockSpec((1, tq, D_GQ), lambda i, t: (i, t, 0)),
        out_shape=jax.ShapeDtypeStruct((b, s, D_GQ), BF16),
        scratch_shapes=[pltpu.VMEM((s + l, LANES), BF16), pltpu.VMEM((s + l, 2 * LANES), BF16),
                        pltpu.VMEM((2, tq, s + l), F32), pltpu.VMEM((2, tq, s + l), BF16)],
        compiler_params=_params(("parallel", "arbitrary")),
        name="gqa_latent",
    )(p_gq, p_gq, p_gq, pc_gq, pc_gq)


def _ctx_attn_kernel(q_ref, k_ref, v_ref, o_ref, s_scr, p_scr, *, shared_kv):
    low = _low_lanes()
    ones = jnp.ones((v_ref.shape[1], LANES), BF16)

    def kv_lanes(i):
        return slice(0, LANES) if shared_kv else slice(LANES * (i // 2), LANES * (i // 2 + 1))

    def score(i):
        j, half = divmod(i, 2)
        return _dot_nt(_one_head(q_ref[0, :, LANES * j:LANES * (j + 1)], low, half), k_ref[0, :, kv_lanes(i)])

    def value(i):
        return jnp.concatenate([v_ref[0, :, kv_lanes(i)], ones], axis=1)

    n_heads = 2 * (q_ref.shape[2] // LANES)
    _pair_outputs(o_ref, _attend_heads(n_heads, score, value, s_scr, p_scr), low)


def _ctx_attn(pc, *, qw, kw, shared_kv):
    b, l, _ = pc.shape
    kb = qw // kw
    return pl.pallas_call(
        functools.partial(_ctx_attn_kernel, shared_kv=shared_kv),
        grid=(b,),
        in_specs=[pl.BlockSpec((1, l, qw), lambda i: (i, 0, 0)),
                  pl.BlockSpec((1, l, kw), lambda i: (i, 0, kb)),
                  pl.BlockSpec((1, l, kw), lambda i: (i, 0, kb + 1))],
        out_specs=pl.BlockSpec((1, l, qw), lambda i: (i, 0, 0)),
        out_shape=jax.ShapeDtypeStruct((b, l, qw), BF16),
        scratch_shapes=[pltpu.VMEM((2, l, l), F32), pltpu.VMEM((2, l, l), BF16)],
        compiler_params=_params(("parallel",)),
        name="ctx_attn",
    )(pc, pc, pc)


def _na_kernel(*refs, n_rows):
    q_ref, k_subs, v_subs = refs[0], refs[1:1 + NA_SUBS], refs[1 + NA_SUBS:1 + 2 * NA_SUBS]
    kc_ref, vc_ref, tz_ref, o_ref, kwin, vwin, vctx, s_scr, p_scr = refs[1 + 2 * NA_SUBS:]
    r0 = pl.program_id(1) * NA_QROWS
    start = jnp.clip(r0 - NA_WIN_R // 2, 0, n_rows - NA_BAND)
    nq, nk = NA_HALF * GRID_W, NA_CBAND * GRID_W
    sub = NA_BAND * GRID_W // NA_SUBS
    n_pairs = D_NA // LANES

    for t, (kr, vr) in enumerate(zip(k_subs, v_subs)):
        rows = slice(sub * t, sub * (t + 1))
        kwin[rows, :] = kr[0]
        for j in range(n_pairs):
            vwin[j, rows, 0:LANES] = vr[0, :, LANES * j:LANES * (j + 1)]
    vwin[:, :, LANES:] = jnp.ones((n_pairs,) + vwin.shape[1:2] + (LANES,), BF16)
    for j in range(n_pairs):
        vctx[j, :, 0:LANES] = vc_ref[0, :, LANES * j:LANES * (j + 1)]
    vctx[:, :, LANES:] = jnp.ones((n_pairs,) + vctx.shape[1:2] + (LANES,), BF16)

    low = _low_lanes()
    halves = []
    for hf in range(NA_QROWS // NA_HALF):
        rq = r0 + NA_HALF * hf
        first_key = jnp.clip(rq - NA_WIN_R // 2, 0, n_rows - NA_CBAND)
        off = pl.multiple_of((first_key - start) * GRID_W, sub)
        qrow = rq + lax.broadcasted_iota(jnp.int32, (nq, nk), 0) // GRID_W
        krow = first_key + lax.broadcasted_iota(jnp.int32, (nq, nk), 1) // GRID_W
        first = jnp.clip(qrow - NA_WIN_R // 2, 0, n_rows - NA_WIN_R)
        row_mask = jnp.where((krow >= first) & (krow < first + NA_WIN_R), 0.0, NEG)
        halves.append((off, first_key - rq, row_mask))

    n_inst = len(halves) * H_NA

    def score(i, dst):
        hf, h = divmod(i, H_NA)
        j, half = divmod(h, 2)
        off, delta, row_mask = halves[hf]
        sl = slice(LANES * j, LANES * (j + 1))
        qm = _one_head(q_ref[0, nq * hf:nq * (hf + 1), sl], low, half)
        slabs = []
        for a in range(NA_HALF):
            pieces = [tz_ref[h, jnp.clip(delta + 2 * bp - a, -8, 7) + 8] for bp in range(NA_CBAND // 2)]
            slabs.append(jnp.concatenate(pieces, axis=1))
        dst[:, :nk] = _dot_nt(qm, kwin[pl.ds(off, nk), sl]) + (jnp.concatenate(slabs, axis=0) + row_mask)
        dst[:, nk:] = _dot_nt(qm, kc_ref[0, :, sl])

    score(0, s_scr.at[0])
    outs = []
    for i in range(n_inst):
        if i + 1 < n_inst:
            score(i + 1, s_scr.at[(i + 1) % 2])
        hf, h = divmod(i, H_NA)
        s = s_scr[i % 2]
        p_scr[i % 2] = jnp.exp2(s - jnp.max(s, axis=1, keepdims=True)).astype(BF16)
        r = (_dot(p_scr[i % 2, :, :nk], vwin[h // 2, pl.ds(halves[hf][0], nk), :])
             + _dot(p_scr[i % 2, :, nk:], vctx[h // 2]))
        outs.append(r[:, :LANES] / r[:, LANES:])
    for hf in range(len(halves)):
        for j in range(n_pairs):
            pair = jnp.where(low, outs[hf * H_NA + 2 * j], outs[hf * H_NA + 2 * j + 1])
            o_ref[0, nq * hf:nq * (hf + 1), LANES * j:LANES * (j + 1)] = pair.astype(BF16)


def _na(p_na, pc_na, tz, li):
    b, s, _ = p_na.shape
    l = pc_na.shape[1]
    n_rows = s // GRID_W
    assert n_rows % NA_QROWS == 0 and n_rows >= NA_BAND
    nq, nk = NA_QROWS * GRID_W, NA_BAND * GRID_W
    nh, nkc = NA_HALF * GRID_W, NA_CBAND * GRID_W
    sub = nk // NA_SUBS
    rows_per_sub = NA_BAND // NA_SUBS
    assert all(v % rows_per_sub == 0 for v in (NA_QROWS, NA_HALF, NA_WIN_R // 2, n_rows - NA_BAND,
                                               n_rows - NA_CBAND))

    def band(t, blk):
        def idx(i, r):
            start = jnp.clip(r * NA_QROWS - NA_WIN_R // 2, 0, n_rows - NA_BAND)
            return (i, start // rows_per_sub + t, blk)
        return pl.BlockSpec((1, sub, D_NA), idx)

    in_specs = ([pl.BlockSpec((1, nq, D_NA), lambda i, r: (i, r, 0))]
                + [band(t, 1) for t in range(NA_SUBS)] + [band(t, 2) for t in range(NA_SUBS)]
                + [pl.BlockSpec((1, l, D_NA), lambda i, r: (i, 0, 1)),
                   pl.BlockSpec((1, l, D_NA), lambda i, r: (i, 0, 2)),
                   _resident(tz.shape[1:], (li,))])
    return pl.pallas_call(
        functools.partial(_na_kernel, n_rows=n_rows),
        grid=(b, n_rows // NA_QROWS),
        in_specs=in_specs,
        out_specs=pl.BlockSpec((1, nq, D_NA), lambda i, r: (i, r, 0)),
        out_shape=jax.ShapeDtypeStruct((b, s, D_NA), BF16),
        scratch_shapes=[pltpu.VMEM((nk, D_NA), BF16), pltpu.VMEM((D_NA // LANES, nk, 2 * LANES), BF16),
                        pltpu.VMEM((D_NA // LANES, l, 2 * LANES), BF16),
                        pltpu.VMEM((2, nh, nkc + l), F32), pltpu.VMEM((2, nh, nkc + l), BF16)],
        compiler_params=_params(("parallel", "arbitrary")),
        name="na_latent",
    )(p_na, *([p_na] * (2 * NA_SUBS)), pc_na, pc_na, tz)


def _merge_kernel(x_ref, mod_ref, nw_ref, hf_ref, hb_ref, og_ref, mlw_ref, g_ref, na_ref, gq_ref,
                  wg_ref, wml_ref, wna_ref, wgq_ref, wo_ref, o_ref):
    x = x_ref[...]
    d = x.shape[1]
    hx = _modnorm(x, nw_ref[1:2, :], mod_ref[0, 3:4, :], mod_ref[0, 4:5, :]).astype(BF16)
    h = hf_ref[...].astype(F32) + hb_ref[...].astype(F32)
    o_ml = (_head_norm(h, g_ref[...], mlw_ref[...]) * _sigmoid(og_ref[...].astype(F32))).astype(BF16)
    y = None
    for j, (o_br, w_br) in enumerate(((o_ml, wml_ref), (na_ref[...], wna_ref), (gq_ref[...], wgq_ref))):
        part = _sigmoid(_dot(hx, wg_ref[:, d * j:d * (j + 1)])) * _dot(o_br, w_br[...])
        y = part if y is None else y + part
    o_ref[...] = x + mod_ref[0, 5:6, :] * _dot(y.astype(BF16), wo_ref[...])


def _merge(x, mod, nw, hf, hb, p_ml, mlw, gmat, o_na, o_gq, wg, wml, wna, wgq, wo, li,
           *, tiles_per_batch, ctx_row):
    t, d = x.shape
    tm = min(TM, t)
    row = lambda wd, blk=0: pl.BlockSpec((tm, wd), lambda i: (i, blk))
    lead = (li,)
    return pl.pallas_call(
        _merge_kernel,
        grid=(t // tm,),
        in_specs=[row(d),
                  pl.BlockSpec((1, N_MOD, d), _mod_index(tiles_per_batch, ctx_row)),
                  pl.BlockSpec((3, d), lambda i: (0, 0)),
                  row(D_ML), row(D_ML), row(D_ML, 3),
                  _resident((1, D_ML)), _resident(gmat.shape),
                  row(D_NA), row(D_GQ),
                  _resident((d, 3 * d), lead), _resident((D_ML, d), lead), _resident((D_NA, d), lead),
                  _resident((D_GQ, d), lead), _resident((d, d), lead)],
        out_specs=row(d),
        out_shape=jax.ShapeDtypeStruct((t, d), F32),
        compiler_params=_params(("parallel",)),
        name="branch_merge",
    )(x, mod, nw, hf, hb, p_ml, mlw, gmat, o_na, o_gq, wg, wml, wna, wgq, wo)


def _proj_weight(w):
    d = w.shape[-2]
    o = 0
    seg = {}
    for name, width in (("ml_k", D_ML), ("ml_v", D_ML), ("ml_g", 4 * H_ML), ("na_k", D_NA), ("na_v", D_NA),
                        ("gq_k", D_KV), ("gq_v", D_KV), ("ml_q", D_ML), ("ml_o", D_ML), ("na_q", D_NA),
                        ("gq_q", D_GQ), ("br_g", 3 * d)):
        seg[name] = w[..., o:o + width].astype(BF16)
        o += width
    gq_q = jnp.concatenate([seg["gq_q"][..., HEAD_DIM * h:HEAD_DIM * (h + 1)] for h in GQ_HEAD_ORDER], axis=-1)
    pad = jnp.zeros(w.shape[:-1] + (LANES - 4 * H_ML,), BF16)
    out = jnp.concatenate([seg["ml_q"], seg["ml_k"], seg["ml_v"], seg["ml_o"], seg["ml_g"][..., ML_GATE_ORDER], pad,
                           seg["na_q"], seg["na_k"], seg["na_v"], gq_q, seg["gq_k"], seg["gq_v"]], axis=-1)
    return out, seg["br_g"]


def _rope_tables(n_tok):
    t = np.arange(n_tok)
    row = (t // GRID_W).astype(np.float64)
    col = (t % GRID_W).astype(np.float64)
    n_freq = HEAD_DIM // 4
    inv = ROPE_THETA ** (-np.arange(n_freq, dtype=np.float64) / n_freq)
    ang = np.concatenate([row[:, None] * inv, col[:, None] * inv], axis=-1)
    cos, sin = np.cos(ang), np.sin(ang)
    cos_t = np.tile(cos, (1, LANES // (HEAD_DIM // 2)))
    sin_t = np.tile(np.concatenate([-sin, sin], axis=-1), (1, LANES // HEAD_DIM))
    return jnp.asarray(cos_t, F32), jnp.asarray(sin_t, F32)


def _na_bias_table(rpb):
    col = np.arange(GRID_W)
    first = np.clip(col - NA_WIN_C // 2, 0, GRID_W - NA_WIN_C)
    in_win = (col[None, :] >= first[:, None]) & (col[None, :] < first[:, None] + NA_WIN_C)
    side = GRID_W - NA_WIN_C
    width = 2 * GRID_W
    rows = jnp.pad(rpb, ((0, 0), (0, 0), (1, 1), (side, width - side - rpb.shape[-1])))
    lead = rows.shape[:-1]
    flat = jnp.broadcast_to(rows[..., None, :], lead + (GRID_W, width)).reshape(lead + (GRID_W * width,))
    skew = flat[..., :GRID_W * (width - 1)].reshape(lead + (GRID_W, width - 1))
    full = skew[..., GRID_W - 1:2 * GRID_W - 1] * float(np.log2(np.e))
    row_ok = np.zeros((2 * NA_WIN_R + 1,), bool)
    row_ok[1:-1] = True
    full = jnp.where(jnp.asarray(in_win[None, None, None] & row_ok[None, None, :, None, None]), full, NEG)
    return jnp.concatenate([full[:, :, :-1], full[:, :, 1:]], axis=-1).astype(F32)


def kernel(x, c, ctx, c_ctx, ada_w, ada_b, norm_w, ffn_w_in, ffn_w_out, mix_w_in, ml_gate_b, ml_norm_w,
           na_qk_w, na_rpb, gq_qk_w, w_br_ml, w_br_na, w_br_gq, w_out):
    b, s, d = x.shape
    l = ctx.shape[1]
    depth = ada_w.shape[0]
    assert b < MOD_ROWS and s % TM == 0 and (b * l) % min(TM, b * l) == 0
    ctx_row = b
    tiles_per_batch = s // TM

    cvec = jnp.zeros((MOD_ROWS, d), F32).at[:b].set(c).at[b].set(c_ctx)
    mod = _ada(cvec, ada_w, ada_b).reshape(depth, MOD_ROWS, N_MOD, d)

    lane = np.arange(2 * LANES)
    gmat = jnp.asarray((lane[:, None] // HEAD_DIM) == (lane[None, :] // HEAD_DIM), BF16)
    idx = np.arange(ML_CHUNK)
    tri = jnp.asarray(np.stack([idx[:, None] >= idx[None, :], idx[:, None] <= idx[None, :]]), BF16)
    rope_tabs = _rope_tables(s)

    w_in, w_o = ffn_w_in.astype(BF16), ffn_w_out.astype(BF16)
    w_proj, wg = _proj_weight(mix_w_in)
    wml, wna, wo = w_br_ml.astype(BF16), w_br_na.astype(BF16), w_out.astype(BF16)
    wgq = jnp.concatenate([w_br_gq[:, HEAD_DIM * h:HEAD_DIM * (h + 1)] for h in GQ_HEAD_ORDER], axis=1).astype(BF16)
    tz = _na_bias_table(na_rpb)

    xl = x.reshape(b * s, d)
    xc = ctx.reshape(b * l, d)
    lat = dict(tiles_per_batch=tiles_per_batch, ctx_row=ctx_row)
    con = dict(tiles_per_batch=None, ctx_row=ctx_row)
    for li in range(depth):
        ctx_out = li < depth - 1
        qkw = jnp.zeros((8, 2 * D_NA), F32)
        qkw = qkw.at[0, :D_NA].set(jnp.tile(na_qk_w[li, 0], H_NA) * Q_PRESCALE)
        qkw = qkw.at[0, D_NA:].set(jnp.tile(na_qk_w[li, 1], H_NA))
        qkw = qkw.at[1, :D_GQ].set(jnp.tile(gq_qk_w[li, 0], H_GQ) * Q_PRESCALE)
        qkw = qkw.at[1, D_GQ:D_GQ + D_KV].set(jnp.tile(gq_qk_w[li, 1], H_KV))
        qkw = qkw.at[2, :4 * H_ML].set(ml_gate_b[li][ML_GATE_ORDER])
        mlw = ml_norm_w[li].reshape(1, D_ML)
        m, nw = mod[li], norm_w[li]

        xl = _ffn(xl, m, nw, w_in, w_o, (li, 0), k0=0, nrm=0, **lat)
        xc = _ffn(xc, m, nw, w_in, w_o, (li, 0), k0=0, nrm=0, **con)

        p_ml, p_mlg, p_na, p_gq = _proj(xl, m, nw, w_proj, li, gmat, qkw, rope_tabs, **lat)
        pc_ml, pc_mlg, pc_na, pc_gq = _proj(xc, m, nw, w_proj, li, gmat, qkw, None, **con)
        seq = lambda a: a.reshape(b, s, a.shape[-1])
        cseq = lambda a: a.reshape(b, l, a.shape[-1])

        hf, hb, hcf, hcb = _mlstm(seq(p_ml), seq(p_mlg), cseq(pc_ml), cseq(pc_mlg), tri)
        o_na = _na(seq(p_na), cseq(pc_na), tz, li)
        o_gq = _gqa(seq(p_gq), cseq(pc_gq))
        flat = lambda a: a.reshape(-1, a.shape[-1])
        xl = _merge(xl, m, nw, flat(hf), flat(hb), p_ml, mlw, gmat, flat(o_na), flat(o_gq),
                    wg, wml, wna, wgq, wo, li, **lat)
        xl = _ffn(xl, m, nw, w_in, w_o, (li, 1), k0=6, nrm=2, **lat)
        if ctx_out:
            co_na = _ctx_attn(cseq(pc_na), qw=D_NA, kw=D_NA, shared_kv=False)
            co_gq = _ctx_attn(cseq(pc_gq), qw=D_GQ, kw=D_KV, shared_kv=True)
            xc = _merge(xc, m, nw, flat(hcf), flat(hcb), pc_ml, mlw, gmat, flat(co_na), flat(co_gq),
                        wg, wml, wna, wgq, wo, li, **con)
            xc = _ffn(xc, m, nw, w_in, w_o, (li, 1), k0=6, nrm=2, **con)
    return xl.reshape(b, s, d)
```

```python
import functools

import numpy as np
import jax
import jax.numpy as jnp
from jax import lax
from jax.experimental import pallas as pl
from jax.experimental.pallas import tpu as pltpu

F32 = jnp.float32
BF16 = jnp.bfloat16

HEAD_DIM = 64
LANES = 128
H_ML, H_NA, H_GQ, H_KV = 4, 6, 6, 2
D_ML, D_NA, D_GQ, D_KV = 256, 384, 384, 128
GRID_W = 64
NA_WIN_R, NA_WIN_C = 8, 16
ROPE_THETA = 10000.0
EPS = 1e-6
N_MOD = 9
ATTN_SCALE = HEAD_DIM ** -0.5
NEG = -1e30

ML_CHUNK = 256
NA_QROWS = 16
NA_HALF = 4
NA_BAND = 24
NA_CBAND = 12
NA_SUBS = 6
TM = 512
GQ_TQ = 512
FFN_CHUNK = 768
Q_PRESCALE = ATTN_SCALE * float(np.log2(np.e))
MOD_ROWS = 16
VMEM_LIMIT = 56 * 1024 * 1024

C_ML, C_MLG, C_NA, C_GQ, C_END = 0, 1024, 1152, 2304, 2944
N_PROJ = C_END
GQ_HEAD_ORDER = (0, 3, 1, 4, 2, 5)
ML_GATE_ORDER = np.array([0, 1, 2, 3, 8, 9, 10, 11, 4, 5, 6, 7, 12, 13, 14, 15])


def _dot(a, b):
    return jnp.dot(a, b, preferred_element_type=F32)


def _dot_nt(a, b):
    return lax.dot_general(a, b, (((1,), (1,)), ((), ())), preferred_element_type=F32)


def _sigmoid(x):
    return 1.0 / (1.0 + jnp.exp(-x))


def _log_sigmoid(x):
    return jnp.minimum(x, 0.0) - jnp.log1p(jnp.exp(-jnp.abs(x)))


def _split_bf16(x):
    parts = []
    for _ in range(3):
        p = x.astype(BF16)
        parts.append(p)
        x = x - p.astype(F32)
    return parts


def _dot_f32_left(x, m01):
    return functools.reduce(jnp.add, [_dot(p, m01) for p in _split_bf16(x)])


def _modnorm(x, nw, shift, scale):
    ms = jnp.mean(x * x, axis=-1, keepdims=True)
    return (x * lax.rsqrt(ms + EPS) * nw) * (1.0 + scale) + shift


def _head_norm(t, gmat, wrow):
    ss = _dot((t * t).astype(BF16), gmat)
    return t * lax.rsqrt(ss * (1.0 / HEAD_DIM) + EPS) * wrow


def _resident(shape, lead=()):
    return pl.BlockSpec((None,) * len(lead) + tuple(shape), lambda *_: tuple(lead) + (0,) * len(shape),
                        pipeline_mode=pl.Buffered(1))


def _params(sem):
    return pltpu.CompilerParams(dimension_semantics=sem, vmem_limit_bytes=VMEM_LIMIT)


def _ada_kernel(c_ref, w_ref, b_ref, o_ref):
    c = c_ref[...]
    s = (c * _sigmoid(c)).astype(BF16)
    o_ref[0] = _dot(s, w_ref[0].astype(BF16)) + b_ref[0]


def _ada(cvec, ada_w, ada_b):
    depth, d, n = ada_w.shape
    tn = n // 8
    return pl.pallas_call(
        _ada_kernel,
        grid=(depth, n // tn),
        in_specs=[pl.BlockSpec((MOD_ROWS, d), lambda l, j: (0, 0)),
                  pl.BlockSpec((1, d, tn), lambda l, j: (l, 0, j)),
                  pl.BlockSpec((1, 1, tn), lambda l, j: (l, 0, j))],
        out_specs=pl.BlockSpec((1, MOD_ROWS, tn), lambda l, j: (l, 0, j)),
        out_shape=jax.ShapeDtypeStruct((depth, MOD_ROWS, n), F32),
        compiler_params=_params(("arbitrary", "arbitrary")),
        name="ada_mod",
    )(cvec, ada_w, ada_b.reshape(depth, 1, n))


def _mod_index(tiles_per_batch, ctx_row):
    if tiles_per_batch is None:
        return lambda i, *_: (ctx_row, 0, 0)
    return lambda i, *_: (i // tiles_per_batch, 0, 0)


def _ffn_kernel(x_ref, mod_ref, nw_ref, wi_ref, wo_ref, o_ref, *, k0, nrm, chunks):
    x = x_ref[...]
    h = _modnorm(x, nw_ref[nrm:nrm + 1, :], mod_ref[0, k0:k0 + 1, :], mod_ref[0, k0 + 1:k0 + 2, :]).astype(BF16)
    dff = wo_ref.shape[0]
    y = None
    for c0, c1 in chunks:
        g = _dot(h, wi_ref[:, c0:c1])
        u = _dot(h, wi_ref[:, dff + c0:dff + c1])
        part = _dot((g * _sigmoid(g) * u).astype(BF16), wo_ref[c0:c1, :])
        y = part if y is None else y + part
    o_ref[...] = x + (0.5 * mod_ref[0, k0 + 2:k0 + 3, :]) * y


def _ffn(x, mod, nw, w_in, w_out, lead, *, k0, nrm, tiles_per_batch, ctx_row):
    t, d = x.shape
    dff = w_out.shape[-2]
    tm = min(TM, t)
    edges = list(range(0, dff, FFN_CHUNK)) + [dff]
    chunks = tuple(zip(edges[:-1], edges[1:]))
    kern = functools.partial(_ffn_kernel, k0=k0, nrm=nrm, chunks=chunks)
    return pl.pallas_call(
        kern,
        grid=(t // tm,),
        in_specs=[pl.BlockSpec((tm, d), lambda i: (i, 0)),
                  pl.BlockSpec((1, N_MOD, d), _mod_index(tiles_per_batch, ctx_row)),
                  pl.BlockSpec((3, d), lambda i: (0, 0)),
                  _resident((d, 2 * dff), lead),
                  _resident((dff, d), lead)],
        out_specs=pl.BlockSpec((tm, d), lambda i: (i, 0)),
        out_shape=jax.ShapeDtypeStruct((t, d), F32),
        compiler_params=_params(("parallel",)),
        name="ffn_swiglu",
    )(x, mod, nw, w_in, w_out)


def _proj_kernel(*refs, rope):
    if rope:
        (x_ref, mod_ref, nw_ref, w_ref, g_ref, qkw_ref, cos_ref, sin_ref,
         ml_ref, mlg_ref, na_ref, gq_ref) = refs
    else:
        (x_ref, mod_ref, nw_ref, w_ref, g_ref, qkw_ref,
         ml_ref, mlg_ref, na_ref, gq_ref) = refs
    h = _modnorm(x_ref[...], nw_ref[1:2, :], mod_ref[0, 3:4, :], mod_ref[0, 4:5, :]).astype(BF16)
    gmat = g_ref[...]

    ml_ref[...] = _dot(h, w_ref[:, C_ML:C_MLG]).astype(BF16)
    gates = _dot(h, w_ref[:, C_MLG:C_NA]) + qkw_ref[2:3, 0:LANES]
    glane = lax.broadcasted_iota(jnp.int32, (1, LANES), 1)
    mlg_ref[...] = jnp.where((glane >= 2 * H_ML) & (glane < 4 * H_ML), _log_sigmoid(gates), gates)

    na = _dot(h, w_ref[:, C_NA:C_GQ])
    for j in range(3):
        sl = slice(2 * LANES * j, 2 * LANES * (j + 1))
        na_ref[:, sl] = _head_norm(na[:, sl], gmat, qkw_ref[0:1, sl]).astype(BF16)
    na_ref[:, 2 * D_NA:] = na[:, 2 * D_NA:].astype(BF16)

    gq = _dot(h, w_ref[:, C_GQ:C_END])
    if rope:
        lane = lax.broadcasted_iota(jnp.int32, (1, LANES), 1)
        first_half = (lane % HEAD_DIM) < (HEAD_DIM // 2)
        cos = cos_ref[...]
        sin = sin_ref[...]
    for j in range(2):
        t2 = _head_norm(gq[:, 2 * LANES * j:2 * LANES * (j + 1)], gmat, qkw_ref[1:2, 2 * LANES * j:2 * LANES * (j + 1)])
        for half in range(2):
            t = t2[:, LANES * half:LANES * (half + 1)]
            if rope:
                rot = jnp.where(first_half, pltpu.roll(t, LANES - HEAD_DIM // 2, axis=1),
                                pltpu.roll(t, HEAD_DIM // 2, axis=1))
                t = t * cos + rot * sin
            gq_ref[:, LANES * (2 * j + half):LANES * (2 * j + half + 1)] = t.astype(BF16)
    gq_ref[:, D_GQ + D_KV:] = gq[:, D_GQ + D_KV:].astype(BF16)


def _proj(x, mod, nw, w, li, gmat, qkw, rope_tabs, *, tiles_per_batch, ctx_row):
    t, d = x.shape
    tm = min(TM, t)
    rope = rope_tabs is not None
    in_specs = [pl.BlockSpec((tm, d), lambda i: (i, 0)),
                pl.BlockSpec((1, N_MOD, d), _mod_index(tiles_per_batch, ctx_row)),
                pl.BlockSpec((3, d), lambda i: (0, 0)),
                _resident((d, N_PROJ), (li,)),
                _resident(gmat.shape),
                _resident(qkw.shape)]
    args = [x, mod, nw, w, gmat, qkw]
    if rope:
        in_specs += [pl.BlockSpec((tm, LANES), lambda i: (i % tiles_per_batch, 0))] * 2
        args += list(rope_tabs)
    widths = (1024, LANES, 3 * D_NA, D_GQ + 2 * D_KV)
    dtypes = (BF16, F32, BF16, BF16)
    return pl.pallas_call(
        functools.partial(_proj_kernel, rope=rope),
        grid=(t // tm,),
        in_specs=in_specs,
        out_specs=[pl.BlockSpec((tm, wd), lambda i: (i, 0)) for wd in widths],
        out_shape=[jax.ShapeDtypeStruct((t, wd), dt) for wd, dt in zip(widths, dtypes)],
        compiler_params=_params(("parallel",)),
        name="mix_in_proj",
    )(*args)


def _ml_prep(d, g_ref, tri_ref):
    log2e = float(np.log2(np.e))
    lc = g_ref.shape[1]
    gates = g_ref[0] * log2e
    gates_t = gates.T
    ig_t = gates_t[0:8]
    lf_t = gates_t[0:16]
    b_t = _dot_f32_left(lf_t, tri_ref[1 - d])[8:16]
    btot_t = _dot_f32_left(lf_t, jnp.ones((lc, lc), BF16))[8:16]
    lf_al = pltpu.roll(gates, LANES - 8, axis=1)
    c_mat = gates - functools.reduce(jnp.add, [_dot(tri_ref[d], p) for p in _split_bf16(lf_al)])
    return ig_t, b_t, btot_t, c_mat


def _ml_head(d, h, prep, m_ref):
    ig_t, b_t, btot_t, c_mat = prep
    lc = c_mat.shape[0]
    r = 4 * d + h
    row = lax.broadcasted_iota(jnp.int32, (lc, lc), 0)
    col = lax.broadcasted_iota(jnp.int32, (lc, lc), 1)
    visible = (row <= col) if d == 0 else (row >= col)
    ig, b, b_tot = ig_t[r:r + 1], b_t[r:r + 1], btot_t[r:r + 1]
    m_prev = m_ref[r:r + 1, :]
    w_end = b_tot - b + ig
    m_new = jnp.maximum(b_tot + m_prev, jnp.max(w_end, axis=1, keepdims=True))
    a = jnp.exp2(w_end - m_new)
    decay = jnp.exp2(b_tot + m_prev - m_new)
    m_inter = b + m_prev
    logw = jnp.where(visible, c_mat[:, r:r + 1] + b, NEG)
    m_j = jnp.maximum(m_inter, jnp.max(logw, axis=0, keepdims=True))
    w = jnp.exp2(logw - m_j)
    m_ref[r:r + 1, :] = m_new
    return w, a, jnp.exp2(m_inter - m_j), jnp.exp2(-m_j), decay


def _ml_pair(d, p, head_a, head_b, q_ref, k_ref, v_ref, o_ref, st_ref):
    (w_a, a_a, g_a, fl_a, dec_a), (w_b, a_b, g_b, fl_b, dec_b) = head_a, head_b
    lc = q_ref.shape[1]
    low = _low_lanes()
    sl = slice(LANES * p, LANES * (p + 1))
    q = q_ref[0, :, sl]
    k = k_ref[0, :, sl] * ATTN_SCALE
    vt = jnp.concatenate([v_ref[0, :, sl].astype(F32).T, jnp.ones((LANES, lc), F32)], axis=0)
    head_row = (lax.broadcasted_iota(jnp.int32, (2 * LANES, 1), 0) % LANES) < HEAD_DIM
    vt16 = vt.astype(BF16)
    r_a = _dot(vt16, (_dot_nt(k, _one_head(q, low, 0)) * w_a).astype(BF16))
    r_b = _dot(vt16, (_dot_nt(k, _one_head(q, low, 1)) * w_b).astype(BF16))
    state = st_ref[d, p]
    r_i = _dot_nt(state.astype(BF16), q)
    r = jnp.where(head_row, r_a, r_b) + jnp.where(head_row, g_a, g_b) * r_i
    num, den = r[:LANES], r[LANES:]
    h_t = num / jnp.maximum(jnp.abs(den), jnp.where(head_row[:LANES], fl_a, fl_b))
    o_ref[0, :, sl] = h_t.T.astype(BF16)

    upd = _dot((vt * jnp.where(head_row, a_a, a_b)).astype(BF16), k)
    same_head = head_row == low
    dec = jnp.where(head_row, dec_a[:, :LANES], dec_b[:, :LANES])
    st_ref[d, p] = dec * state + jnp.where(same_head, upd, 0.0)


def _ml_step(fwd, bwd, tri_ref, st_ref, m_ref):
    dirs = (fwd, bwd)
    preps = [_ml_prep(d, refs[3], tri_ref) for d, refs in enumerate(dirs)]
    heads = [[_ml_head(d, h, preps[d], m_ref) for h in range(H_ML)] for d in range(2)]
    for p in range(H_ML // 2):
        for d, (q_ref, k_ref, v_ref, _, o_ref) in enumerate(dirs):
            _ml_pair(d, p, heads[d][2 * p], heads[d][2 * p + 1], q_ref, k_ref, v_ref, o_ref, st_ref)


def _mlstm_kernel(qf, kf, vf, gf, qb, kb, vb, gb, qc, kc, vc, gc, tri_ref,
                  hf_ref, hb_ref, hcf_ref, hcb_ref, st_ref, m_ref):
    c = pl.program_id(1)

    @pl.when(c == 0)
    def _():
        st_ref[...] = jnp.zeros_like(st_ref)
        m_ref[...] = jnp.zeros_like(m_ref)
        _ml_step((qc, kc, vc, gc, hcf_ref), (qc, kc, vc, gc, hcb_ref), tri_ref, st_ref, m_ref)

    @pl.when(c > 0)
    def _():
        _ml_step((qf, kf, vf, gf, hf_ref), (qb, kb, vb, gb, hb_ref), tri_ref, st_ref, m_ref)


def _mlstm(p_ml, p_mlg, pc_ml, pc_mlg, tri):
    b, s, _ = p_ml.shape
    l = pc_ml.shape[1]
    lc = ML_CHUNK
    assert l == lc and s % lc == 0
    nl = s // lc
    fwd = lambda c: jnp.maximum(c - 1, 0)
    bwd = lambda c: nl - 1 - jnp.maximum(c - 1, 0)

    def lat(idx, blk, width):
        return pl.BlockSpec((1, lc, width), lambda i, c: (i, idx(c), blk))

    def ctx(blk, width):
        return pl.BlockSpec((1, lc, width), lambda i, c: (i, 0, blk))

    in_specs = ([lat(fwd, 0, D_ML), lat(fwd, 1, D_ML), lat(fwd, 2, D_ML), lat(fwd, 0, LANES)]
                + [lat(bwd, 0, D_ML), lat(bwd, 1, D_ML), lat(bwd, 2, D_ML), lat(bwd, 0, LANES)]
                + [ctx(0, D_ML), ctx(1, D_ML), ctx(2, D_ML), ctx(0, LANES)]
                + [_resident((2, lc, lc))])
    out_specs = [lat(fwd, 0, D_ML), lat(bwd, 0, D_ML), ctx(0, D_ML), ctx(0, D_ML)]
    out_shape = [jax.ShapeDtypeStruct((b, s, D_ML), BF16)] * 2 + [jax.ShapeDtypeStruct((b, l, D_ML), BF16)] * 2
    return pl.pallas_call(
        _mlstm_kernel,
        grid=(b, nl + 1),
        in_specs=in_specs,
        out_specs=out_specs,
        out_shape=out_shape,
        scratch_shapes=[pltpu.VMEM((2, H_ML // 2, 2 * LANES, LANES), F32), pltpu.VMEM((2 * H_ML, lc), F32)],
        compiler_params=_params(("parallel", "arbitrary")),
        name="mlstm_bidir",
    )(p_ml, p_ml, p_ml, p_mlg, p_ml, p_ml, p_ml, p_mlg, pc_ml, pc_ml, pc_ml, pc_mlg, tri)


def _attend_heads(n_heads, score_fn, value_fn, s_scr, p_scr):
    s_scr[0] = score_fn(0)
    outs = []
    for i in range(n_heads):
        if i + 1 < n_heads:
            s_scr[(i + 1) % 2] = score_fn(i + 1)
        s = s_scr[i % 2]
        p_scr[i % 2] = jnp.exp2(s - jnp.max(s, axis=1, keepdims=True)).astype(BF16)
        r = _dot(p_scr[i % 2], value_fn(i))
        outs.append(r[:, :LANES] / r[:, LANES:])
    return outs


def _low_lanes():
    return lax.broadcasted_iota(jnp.int32, (1, LANES), 1) < HEAD_DIM


def _one_head(q, low, half):
    zero = jnp.zeros_like(q)
    return jnp.where(low, q, zero) if half == 0 else jnp.where(low, zero, q)


def _pair_outputs(o_ref, outs, low):
    for j in range(len(outs) // 2):
        o_ref[0, :, LANES * j:LANES * (j + 1)] = jnp.where(low, outs[2 * j], outs[2 * j + 1]).astype(BF16)


def _gqa_kernel(q_ref, k_ref, v_ref, kc_ref, vc_ref, o_ref, kall, vall, s_scr, p_scr):
    n_lat = k_ref.shape[1]

    @pl.when(pl.program_id(1) == 0)
    def _():
        kall[0:n_lat, :] = k_ref[0]
        kall[n_lat:, :] = kc_ref[0]
        vall[0:n_lat, 0:LANES] = v_ref[0]
        vall[n_lat:, 0:LANES] = vc_ref[0]
        vall[:, LANES:] = jnp.ones((vall.shape[0], LANES), BF16)

    low = _low_lanes()

    def score(i):
        j, half = divmod(i, 2)
        return _dot_nt(_one_head(q_ref[0, :, LANES * j:LANES * (j + 1)], low, half), kall[...])

    _pair_outputs(o_ref, _attend_heads(H_GQ, score, lambda i: vall[...], s_scr, p_scr), low)


def _gqa(p_gq, pc_gq):
    b, s, _ = p_gq.shape
    l = pc_gq.shape[1]
    tq = min(GQ_TQ, s)
    kblk, vblk = D_GQ // LANES, D_GQ // LANES + 1
    return pl.pallas_call(
        _gqa_kernel,
        grid=(b, s // tq),
        in_specs=[pl.BlockSpec((1, tq, D_GQ), lambda i, t: (i, t, 0)),
                  pl.BlockSpec((1, s, LANES), lambda i, t: (i, 0, kblk)),
                  pl.BlockSpec((1, s, LANES), lambda i, t: (i, 0, vblk)),
                  pl.BlockSpec((1, l, LANES), lambda i, t: (i, 0, kblk)),
                  pl.BlockSpec((1, l, LANES), lambda i, t: (i, 0, vblk))],
        out_specs=pl.BlockSpec((1, tq, D_GQ), lambda i, t: (i, t, 0)),
        out_shape=jax.ShapeDtypeStruct((b, s, D_GQ), BF16),
        scratch_shapes=[pltpu.VMEM((s + l, LANES), BF16), pltpu.VMEM((s + l, 2 * LANES), BF16),
                        pltpu.VMEM((2, tq, s + l), F32), pltpu.VMEM((2, tq, s + l), BF16)],
        compiler_params=_params(("parallel", "arbitrary")),
        name="gqa_latent",
    )(p_gq, p_gq, p_gq, pc_gq, pc_gq)


def _ctx_attn_kernel(q_ref, k_ref, v_ref, o_ref, s_scr, p_scr, *, shared_kv):
    low = _low_lanes()
    ones = jnp.ones((v_ref.shape[1], LANES), BF16)

    def kv_lanes(i):
        return slice(0, LANES) if shared_kv else slice(LANES * (i // 2), LANES * (i // 2 + 1))

    def score(i):
        j, half = divmod(i, 2)
        return _dot_nt(_one_head(q_ref[0, :, LANES * j:LANES * (j + 1)], low, half), k_ref[0, :, kv_lanes(i)])

    def value(i):
        return jnp.concatenate([v_ref[0, :, kv_lanes(i)], ones], axis=1)

    n_heads = 2 * (q_ref.shape[2] // LANES)
    _pair_outputs(o_ref, _attend_heads(n_heads, score, value, s_scr, p_scr), low)


def _ctx_attn(pc, *, qw, kw, shared_kv):
    b, l, _ = pc.shape
    kb = qw // kw
    return pl.pallas_call(
        functools.partial(_ctx_attn_kernel, shared_kv=shared_kv),
        grid=(b,),
        in_specs=[pl.BlockSpec((1, l, qw), lambda i: (i, 0, 0)),
                  pl.BlockSpec((1, l, kw), lambda i: (i, 0, kb)),
                  pl.BlockSpec((1, l, kw), lambda i: (i, 0, kb + 1))],
        out_specs=pl.BlockSpec((1, l, qw), lambda i: (i, 0, 0)),
        out_shape=jax.ShapeDtypeStruct((b, l, qw), BF16),
        scratch_shapes=[pltpu.VMEM((2, l, l), F32), pltpu.VMEM((2, l, l), BF16)],
        compiler_params=_params(("parallel",)),
        name="ctx_attn",
    )(pc, pc, pc)


def _na_kernel(*refs, n_rows):
    q_ref, k_subs, v_subs = refs[0], refs[1:1 + NA_SUBS], refs[1 + NA_SUBS:1 + 2 * NA_SUBS]
    kc_ref, vc_ref, tz_ref, o_ref, kwin, vwin, vctx, s_scr, p_scr = refs[1 + 2 * NA_SUBS:]
    r0 = pl.program_id(1) * NA_QROWS
    start = jnp.clip(r0 - NA_WIN_R // 2, 0, n_rows - NA_BAND)
    nq, nk = NA_HALF * GRID_W, NA_CBAND * GRID_W
    sub = NA_BAND * GRID_W // NA_SUBS
    n_pairs = D_NA // LANES

    for t, (kr, vr) in enumerate(zip(k_subs, v_subs)):
        rows = slice(sub * t, sub * (t + 1))
        kwin[rows, :] = kr[0]
        for j in range(n_pairs):
            vwin[j, rows, 0:LANES] = vr[0, :, LANES * j:LANES * (j + 1)]
    vwin[:, :, LANES:] = jnp.ones((n_pairs,) + vwin.shape[1:2] + (LANES,), BF16)
    for j in range(n_pairs):
        vctx[j, :, 0:LANES] = vc_ref[0, :, LANES * j:LANES * (j + 1)]
    vctx[:, :, LANES:] = jnp.ones((n_pairs,) + vctx.shape[1:2] + (LANES,), BF16)

    low = _low_lanes()
    halves = []
    for hf in range(NA_QROWS // NA_HALF):
        rq = r0 + NA_HALF * hf
        first_key = jnp.clip(rq - NA_WIN_R // 2, 0, n_rows - NA_CBAND)
        off = pl.multiple_of((first_key - start) * GRID_W, sub)
        qrow = rq + lax.broadcasted_iota(jnp.int32, (nq, nk), 0) // GRID_W
        krow = first_key + lax.broadcasted_iota(jnp.int32, (nq, nk), 1) // GRID_W
        first = jnp.clip(qrow - NA_WIN_R // 2, 0, n_rows - NA_WIN_R)
        row_mask = jnp.where((krow >= first) & (krow < first + NA_WIN_R), 0.0, NEG)
        halves.append((off, first_key - rq, row_mask))

    n_inst = len(halves) * H_NA

    def score(i, dst):
        hf, h = divmod(i, H_NA)
        j, half = divmod(h, 2)
        off, delta, row_mask = halves[hf]
        sl = slice(LANES * j, LANES * (j + 1))
        qm = _one_head(q_ref[0, nq * hf:nq * (hf + 1), sl], low, half)
        slabs = []
        for a in range(NA_HALF):
            pieces = [tz_ref[h, jnp.clip(delta + 2 * bp - a, -8, 7) + 8] for bp in range(NA_CBAND // 2)]
            slabs.append(jnp.concatenate(pieces, axis=1))
        dst[:, :nk] = _dot_nt(qm, kwin[pl.ds(off, nk), sl]) + (jnp.concatenate(slabs, axis=0) + row_mask)
        dst[:, nk:] = _dot_nt(qm, kc_ref[0, :, sl])

    score(0, s_scr.at[0])
    outs = []
    for i in range(n_inst):
        if i + 1 < n_inst:
            score(i + 1, s_scr.at[(i + 1) % 2])
        hf, h = divmod(i, H_NA)
        s = s_scr[i % 2]
        p_scr[i % 2] = jnp.exp2(s - jnp.max(s, axis=1, keepdims=True)).astype(BF16)
        r = (_dot(p_scr[i % 2, :, :nk], vwin[h // 2, pl.ds(halves[hf][0], nk), :])
             + _dot(p_scr[i % 2, :, nk:], vctx[h // 2]))
        outs.append(r[:, :LANES] / r[:, LANES:])
    for hf in range(len(halves)):
        for j in range(n_pairs):
            pair = jnp.where(low, outs[hf * H_NA + 2 * j], outs[hf * H_NA + 2 * j + 1])
            o_ref[0, nq * hf:nq * (hf + 1), LANES * j:LANES * (j + 1)] = pair.astype(BF16)


def _na(p_na, pc_na, tz, li):
    b, s, _ = p_na.shape
    l = pc_na.shape[1]
    n_rows = s // GRID_W
    assert n_rows % NA_QROWS == 0 and n_rows >= NA_BAND
    nq, nk = NA_QROWS * GRID_W, NA_BAND * GRID_W
    nh, nkc = NA_HALF * GRID_W, NA_CBAND * GRID_W
    sub = nk // NA_SUBS
    rows_per_sub = NA_BAND // NA_SUBS
    assert all(v % rows_per_sub == 0 for v in (NA_QROWS, NA_HALF, NA_WIN_R // 2, n_rows - NA_BAND,
                                               n_rows - NA_CBAND))

    def band(t, blk):
        def idx(i, r):
            start = jnp.clip(r * NA_QROWS - NA_WIN_R // 2, 0, n_rows - NA_BAND)
            return (i, start // rows_per_sub + t, blk)
        return pl.BlockSpec((1, sub, D_NA), idx)

    in_specs = ([pl.BlockSpec((1, nq, D_NA), lambda i, r: (i, r, 0))]
                + [band(t, 1) for t in range(NA_SUBS)] + [band(t, 2) for t in range(NA_SUBS)]
                + [pl.BlockSpec((1, l, D_NA), lambda i, r: (i, 0, 1)),
                   pl.BlockSpec((1, l, D_NA), lambda i, r: (i, 0, 2)),
                   _resident(tz.shape[1:], (li,))])
    return pl.pallas_call(
        functools.partial(_na_kernel, n_rows=n_rows),
        grid=(b, n_rows // NA_QROWS),
        in_specs=in_specs,
        out_specs=pl.BlockSpec((1, nq, D_NA), lambda i, r: (i, r, 0)),
        out_shape=jax.ShapeDtypeStruct((b, s, D_NA), BF16),
        scratch_shapes=[pltpu.VMEM((nk, D_NA), BF16), pltpu.VMEM((D_NA // LANES, nk, 2 * LANES), BF16),
                        pltpu.VMEM((D_NA // LANES, l, 2 * LANES), BF16),
                        pltpu.VMEM((2, nh, nkc + l), F32), pltpu.VMEM((2, nh, nkc + l), BF16)],
        compiler_params=_params(("parallel", "arbitrary")),
        name="na_latent",
    )(p_na, *([p_na] * (2 * NA_SUBS)), pc_na, pc_na, tz)


def _merge_kernel(x_ref, mod_ref, nw_ref, hf_ref, hb_ref, og_ref, mlw_ref, g_ref, na_ref, gq_ref,
                  wg_ref, wml_ref, wna_ref, wgq_ref, wo_ref, o_ref):
    x = x_ref[...]
    d = x.shape[1]
    hx = _modnorm(x, nw_ref[1:2, :], mod_ref[0, 3:4, :], mod_ref[0, 4:5, :]).astype(BF16)
    h = hf_ref[...].astype(F32) + hb_ref[...].astype(F32)
    o_ml = (_head_norm(h, g_ref[...], mlw_ref[...]) * _sigmoid(og_ref[...].astype(F32))).astype(BF16)
    y = None
    for j, (o_br, w_br) in enumerate(((o_ml, wml_ref), (na_ref[...], wna_ref), (gq_ref[...], wgq_ref))):
        part = _sigmoid(_dot(hx, wg_ref[:, d * j:d * (j + 1)])) * _dot(o_br, w_br[...])
        y = part if y is None else y + part
    o_ref[...] = x + mod_ref[0, 5:6, :] * _dot(y.astype(BF16), wo_ref[...])


def _merge(x, mod, nw, hf, hb, p_ml, mlw, gmat, o_na, o_gq, wg, wml, wna, wgq, wo, li,
           *, tiles_per_batch, ctx_row):
    t, d = x.shape
    tm = min(TM, t)
    row = lambda wd, blk=0: pl.BlockSpec((tm, wd), lambda i: (i, blk))
    lead = (li,)
    return pl.pallas_call(
        _merge_kernel,
        grid=(t // tm,),
        in_specs=[row(d),
                  pl.BlockSpec((1, N_MOD, d), _mod_index(tiles_per_batch, ctx_row)),
                  pl.BlockSpec((3, d), lambda i: (0, 0)),
                  row(D_ML), row(D_ML), row(D_ML, 3),
                  _resident((1, D_ML)), _resident(gmat.shape),
                  row(D_NA), row(D_GQ),
                  _resident((d, 3 * d), lead), _resident((D_ML, d), lead), _resident((D_NA, d), lead),
                  _resident((D_GQ, d), lead), _resident((d, d), lead)],
        out_specs=row(d),
        out_shape=jax.ShapeDtypeStruct((t, d), F32),
        compiler_params=_params(("parallel",)),
        name="branch_merge",
    )(x, mod, nw, hf, hb, p_ml, mlw, gmat, o_na, o_gq, wg, wml, wna, wgq, wo)


def _proj_weight(w):
    d = w.shape[-2]
    o = 0
    seg = {}
    for name, width in (("ml_k", D_ML), ("ml_v", D_ML), ("ml_g", 4 * H_ML), ("na_k", D_NA), ("na_v", D_NA),
                        ("gq_k", D_KV), ("gq_v", D_KV), ("ml_q", D_ML), ("ml_o", D_ML), ("na_q", D_NA),
                        ("gq_q", D_GQ), ("br_g", 3 * d)):
        seg[name] = w[..., o:o + width].astype(BF16)
        o += width
    gq_q = jnp.concatenate([seg["gq_q"][..., HEAD_DIM * h:HEAD_DIM * (h + 1)] for h in GQ_HEAD_ORDER], axis=-1)
    pad = jnp.zeros(w.shape[:-1] + (LANES - 4 * H_ML,), BF16)
    out = jnp.concatenate([seg["ml_q"], seg["ml_k"], seg["ml_v"], seg["ml_o"], seg["ml_g"][..., ML_GATE_ORDER], pad,
                           seg["na_q"], seg["na_k"], seg["na_v"], gq_q, seg["gq_k"], seg["gq_v"]], axis=-1)
    return out, seg["br_g"]


def _rope_tables(n_tok):
    t = np.arange(n_tok)
    row = (t // GRID_W).astype(np.float64)
    col = (t % GRID_W).astype(np.float64)
    n_freq = HEAD_DIM // 4
    inv = ROPE_THETA ** (-np.arange(n_freq, dtype=np.float64) / n_freq)
    ang = np.concatenate([row[:, None] * inv, col[:, None] * inv], axis=-1)
    cos, sin = np.cos(ang), np.sin(ang)
    cos_t = np.tile(cos, (1, LANES // (HEAD_DIM // 2)))
    sin_t = np.tile(np.concatenate([-sin, sin], axis=-1), (1, LANES // HEAD_DIM))
    return jnp.asarray(cos_t, F32), jnp.asarray(sin_t, F32)


def _na_bias_table(rpb):
    col = np.arange(GRID_W)
    first = np.clip(col - NA_WIN_C // 2, 0, GRID_W - NA_WIN_C)
    in_win = (col[None, :] >= first[:, None]) & (col[None, :] < first[:, None] + NA_WIN_C)
    side = GRID_W - NA_WIN_C
    width = 2 * GRID_W
    rows = jnp.pad(rpb, ((0, 0), (0, 0), (1, 1), (side, width - side - rpb.shape[-1])))
    lead = rows.shape[:-1]
    flat = jnp.broadcast_to(rows[..., None, :], lead + (GRID_W, width)).reshape(lead + (GRID_W * width,))
    skew = flat[..., :GRID_W * (width - 1)].reshape(lead + (GRID_W, width - 1))
    full = skew[..., GRID_W - 1:2 * GRID_W - 1] * float(np.log2(np.e))
    row_ok = np.zeros((2 * NA_WIN_R + 1,), bool)
    row_ok[1:-1] = True
    full = jnp.where(jnp.asarray(in_win[None, None, None] & row_ok[None, None, :, None, None]), full, NEG)
    return jnp.concatenate([full[:, :, :-1], full[:, :, 1:]], axis=-1).astype(F32)


def kernel(x, c, ctx, c_ctx, ada_w, ada_b, norm_w, ffn_w_in, ffn_w_out, mix_w_in, ml_gate_b, ml_norm_w,
           na_qk_w, na_rpb, gq_qk_w, w_br_ml, w_br_na, w_br_gq, w_out):
    b, s, d = x.shape
    l = ctx.shape[1]
    depth = ada_w.shape[0]
    assert b < MOD_ROWS and s % TM == 0 and (b * l) % min(TM, b * l) == 0
    ctx_row = b
    tiles_per_batch = s // TM

    cvec = jnp.zeros((MOD_ROWS, d), F32).at[:b].set(c).at[b].set(c_ctx)
    mod = _ada(cvec, ada_w, ada_b).reshape(depth, MOD_ROWS, N_MOD, d)

    lane = np.arange(2 * LANES)
    gmat = jnp.asarray((lane[:, None] // HEAD_DIM) == (lane[None, :] // HEAD_DIM), BF16)
    idx = np.arange(ML_CHUNK)
    tri = jnp.asarray(np.stack([idx[:, None] >= idx[None, :], idx[:, None] <= idx[None, :]]), BF16)
    rope_tabs = _rope_tables(s)

    w_in, w_o = ffn_w_in.astype(BF16), ffn_w_out.astype(BF16)
    w_proj, wg = _proj_weight(mix_w_in)
    wml, wna, wo = w_br_ml.astype(BF16), w_br_na.astype(BF16), w_out.astype(BF16)
    wgq = jnp.concatenate([w_br_gq[:, HEAD_DIM * h:HEAD_DIM * (h + 1)] for h in GQ_HEAD_ORDER], axis=1).astype(BF16)
    tz = _na_bias_table(na_rpb)

    xl = x.reshape(b * s, d)
    xc = ctx.reshape(b * l, d)
    lat = dict(tiles_per_batch=tiles_per_batch, ctx_row=ctx_row)
    con = dict(tiles_per_batch=None, ctx_row=ctx_row)
    for li in range(depth):
        ctx_out = li < depth - 1
        qkw = jnp.zeros((8, 2 * D_NA), F32)
        qkw = qkw.at[0, :D_NA].set(jnp.tile(na_qk_w[li, 0], H_NA) * Q_PRESCALE)
        qkw = qkw.at[0, D_NA:].set(jnp.tile(na_qk_w[li, 1], H_NA))
        qkw = qkw.at[1, :D_GQ].set(jnp.tile(gq_qk_w[li, 0], H_GQ) * Q_PRESCALE)
        qkw = qkw.at[1, D_GQ:D_GQ + D_KV].set(jnp.tile(gq_qk_w[li, 1], H_KV))
        qkw = qkw.at[2, :4 * H_ML].set(ml_gate_b[li][ML_GATE_ORDER])
        mlw = ml_norm_w[li].reshape(1, D_ML)
        m, nw = mod[li], norm_w[li]

        xl = _ffn(xl, m, nw, w_in, w_o, (li, 0), k0=0, nrm=0, **lat)
        xc = _ffn(xc, m, nw, w_in, w_o, (li, 0), k0=0, nrm=0, **con)

        p_ml, p_mlg, p_na, p_gq = _proj(xl, m, nw, w_proj, li, gmat, qkw, rope_tabs, **lat)
        pc_ml, pc_mlg, pc_na, pc_gq = _proj(xc, m, nw, w_proj, li, gmat, qkw, None, **con)
        seq = lambda a: a.reshape(b, s, a.shape[-1])
        cseq = lambda a: a.reshape(b, l, a.shape[-1])

        hf, hb, hcf, hcb = _mlstm(seq(p_ml), seq(p_mlg), cseq(pc_ml), cseq(pc_mlg), tri)
        o_na = _na(seq(p_na), cseq(pc_na), tz, li)
        o_gq = _gqa(seq(p_gq), cseq(pc_gq))
        flat = lambda a: a.reshape(-1, a.shape[-1])
        xl = _merge(xl, m, nw, flat(hf), flat(hb), p_ml, mlw, gmat, flat(o_na), flat(o_gq),
                    wg, wml, wna, wgq, wo, li, **lat)
        xl = _ffn(xl, m, nw, w_in, w_o, (li, 1), k0=6, nrm=2, **lat)
        if ctx_out:
            co_na = _ctx_attn(cseq(pc_na), qw=D_NA, kw=D_NA, shared_kv=False)
            co_gq = _ctx_attn(cseq(pc_gq), qw=D_GQ, kw=D_KV, shared_kv=True)
            xc = _merge(xc, m, nw, flat(hcf), flat(hcb), pc_ml, mlw, gmat, flat(co_na), flat(co_gq),
                        wg, wml, wna, wgq, wo, li, **con)
            xc = _ffn(xc, m, nw, w_in, w_o, (li, 1), k0=6, nrm=2, **con)
    return xl.reshape(b, s, d)
```

```python
import functools

import numpy as np
import jax
import jax.numpy as jnp
from jax import lax
from jax.experimental import pallas as pl
from jax.experimental.pallas import tpu as pltpu

F32 = jnp.float32
BF16 = jnp.bfloat16

HEAD_DIM = 64
LANES = 128
H_ML, H_NA, H_GQ, H_KV = 4, 6, 6, 2
D_ML, D_NA, D_GQ, D_KV = 256, 384, 384, 128
GRID_W = 64
NA_WIN_R, NA_WIN_C = 8, 16
ROPE_THETA = 10000.0
EPS = 1e-6
N_MOD = 9
ATTN_SCALE = HEAD_DIM ** -0.5
NEG = -1e30

ML_CHUNK = 256
NA_QROWS = 16
NA_HALF = 4
NA_BAND = 24
NA_CBAND = 12
NA_SUBS = 6
TM = 1024
GQ_TQ = 512
FFN_CHUNK = 768
Q_PRESCALE = ATTN_SCALE * float(np.log2(np.e))
MOD_ROWS = 16
VMEM_LIMIT = 56 * 1024 * 1024

C_ML, C_MLG, C_NA, C_GQ, C_END = 0, 1024, 1152, 2304, 2944
N_PROJ = C_END
GQ_HEAD_ORDER = (0, 3, 1, 4, 2, 5)
ML_GATE_ORDER = np.array([0, 1, 2, 3, 8, 9, 10, 11, 4, 5, 6, 7, 12, 13, 14, 15])


def _dot(a, b):
    return jnp.dot(a, b, preferred_element_type=F32)


def _dot_nt(a, b):
    return lax.dot_general(a, b, (((1,), (1,)), ((), ())), preferred_element_type=F32)


def _sigmoid(x):
    return 1.0 / (1.0 + jnp.exp(-x))


def _log_sigmoid(x):
    return jnp.minimum(x, 0.0) - jnp.log1p(jnp.exp(-jnp.abs(x)))


def _split_bf16(x):
    parts = []
    for _ in range(3):
        p = x.astype(BF16)
        parts.append(p)
        x = x - p.astype(F32)
    return parts


def _dot_f32_left(x, m01):
    return functools.reduce(jnp.add, [_dot(p, m01) for p in _split_bf16(x)])


def _modnorm(x, nw, shift, scale):
    ms = jnp.mean(x * x, axis=-1, keepdims=True)
    return (x * lax.rsqrt(ms + EPS) * nw) * (1.0 + scale) + shift


def _head_norm(t, gmat, wrow):
    ss = _dot((t * t).astype(BF16), gmat)
    return t * lax.rsqrt(ss * (1.0 / HEAD_DIM) + EPS) * wrow


def _resident(shape, lead=()):
    return pl.BlockSpec((None,) * len(lead) + tuple(shape), lambda *_: tuple(lead) + (0,) * len(shape),
                        pipeline_mode=pl.Buffered(1))


def _params(sem):
    return pltpu.CompilerParams(dimension_semantics=sem, vmem_limit_bytes=VMEM_LIMIT)


def _ada_kernel(c_ref, w_ref, b_ref, o_ref):
    c = c_ref[...]
    s = (c * _sigmoid(c)).astype(BF16)
    o_ref[0] = _dot(s, w_ref[0].astype(BF16)) + b_ref[0]


def _ada(cvec, ada_w, ada_b):
    depth, d, n = ada_w.shape
    tn = n // 8
    return pl.pallas_call(
        _ada_kernel,
        grid=(depth, n // tn),
        in_specs=[pl.BlockSpec((MOD_ROWS, d), lambda l, j: (0, 0)),
                  pl.BlockSpec((1, d, tn), lambda l, j: (l, 0, j)),
                  pl.BlockSpec((1, 1, tn), lambda l, j: (l, 0, j))],
        out_specs=pl.BlockSpec((1, MOD_ROWS, tn), lambda l, j: (l, 0, j)),
        out_shape=jax.ShapeDtypeStruct((depth, MOD_ROWS, n), F32),
        compiler_params=_params(("arbitrary", "arbitrary")),
        name="ada_mod",
    )(cvec, ada_w, ada_b.reshape(depth, 1, n))


def _mod_index(tiles_per_batch, ctx_row):
    if tiles_per_batch is None:
        return lambda i, *_: (ctx_row, 0, 0)
    return lambda i, *_: (i // tiles_per_batch, 0, 0)


def _ffn_kernel(x_ref, mod_ref, nw_ref, wi_ref, wo_ref, o_ref, *, k0, nrm, chunks):
    x = x_ref[...]
    h = _modnorm(x, nw_ref[nrm:nrm + 1, :], mod_ref[0, k0:k0 + 1, :], mod_ref[0, k0 + 1:k0 + 2, :]).astype(BF16)
    dff = wo_ref.shape[0]
    y = None
    for c0, c1 in chunks:
        g = _dot(h, wi_ref[:, c0:c1])
        u = _dot(h, wi_ref[:, dff + c0:dff + c1])
        part = _dot((g * _sigmoid(g) * u).astype(BF16), wo_ref[c0:c1, :])
        y = part if y is None else y + part
    o_ref[...] = x + (0.5 * mod_ref[0, k0 + 2:k0 + 3, :]) * y


def _ffn(x, mod, nw, w_in, w_out, lead, *, k0, nrm, tiles_per_batch, ctx_row):
    t, d = x.shape
    dff = w_out.shape[-2]
    tm = min(TM, t)
    edges = list(range(0, dff, FFN_CHUNK)) + [dff]
    chunks = tuple(zip(edges[:-1], edges[1:]))
    kern = functools.partial(_ffn_kernel, k0=k0, nrm=nrm, chunks=chunks)
    return pl.pallas_call(
        kern,
        grid=(t // tm,),
        in_specs=[pl.BlockSpec((tm, d), lambda i: (i, 0)),
                  pl.BlockSpec((1, N_MOD, d), _mod_index(tiles_per_batch, ctx_row)),
                  pl.BlockSpec((3, d), lambda i: (0, 0)),
                  _resident((d, 2 * dff), lead),
                  _resident((dff, d), lead)],
        out_specs=pl.BlockSpec((tm, d), lambda i: (i, 0)),
        out_shape=jax.ShapeDtypeStruct((t, d), F32),
        compiler_params=_params(("parallel",)),
        name="ffn_swiglu",
    )(x, mod, nw, w_in, w_out)


def _proj_kernel(*refs, rope):
    if rope:
        (x_ref, mod_ref, nw_ref, w_ref, g_ref, qkw_ref, cos_ref, sin_ref,
         ml_ref, mlg_ref, na_ref, gq_ref) = refs
    else:
        (x_ref, mod_ref, nw_ref, w_ref, g_ref, qkw_ref,
         ml_ref, mlg_ref, na_ref, gq_ref) = refs
    h = _modnorm(x_ref[...], nw_ref[1:2, :], mod_ref[0, 3:4, :], mod_ref[0, 4:5, :]).astype(BF16)
    gmat = g_ref[...]

    ml_ref[...] = _dot(h, w_ref[:, C_ML:C_MLG]).astype(BF16)
    gates = _dot(h, w_ref[:, C_MLG:C_NA]) + qkw_ref[2:3, 0:LANES]
    glane = lax.broadcasted_iota(jnp.int32, (1, LANES), 1)
    mlg_ref[...] = jnp.where((glane >= 2 * H_ML) & (glane < 4 * H_ML), _log_sigmoid(gates), gates)

    na = _dot(h, w_ref[:, C_NA:C_GQ])
    for j in range(3):
        sl = slice(2 * LANES * j, 2 * LANES * (j + 1))
        na_ref[:, sl] = _head_norm(na[:, sl], gmat, qkw_ref[0:1, sl]).astype(BF16)
    na_ref[:, 2 * D_NA:] = na[:, 2 * D_NA:].astype(BF16)

    gq = _dot(h, w_ref[:, C_GQ:C_END])
    if rope:
        lane = lax.broadcasted_iota(jnp.int32, (1, LANES), 1)
        first_half = (lane % HEAD_DIM) < (HEAD_DIM // 2)
        cos = cos_ref[...]
        sin = sin_ref[...]
    for j in range(2):
        t2 = _head_norm(gq[:, 2 * LANES * j:2 * LANES * (j + 1)], gmat, qkw_ref[1:2, 2 * LANES * j:2 * LANES * (j + 1)])
        for half in range(2):
            t = t2[:, LANES * half:LANES * (half + 1)]
            if rope:
                rot = jnp.where(first_half, pltpu.roll(t, LANES - HEAD_DIM // 2, axis=1),
                                pltpu.roll(t, HEAD_DIM // 2, axis=1))
                t = t * cos + rot * sin
            gq_ref[:, LANES * (2 * j + half):LANES * (2 * j + half + 1)] = t.astype(BF16)
    gq_ref[:, D_GQ + D_KV:] = gq[:, D_GQ + D_KV:].astype(BF16)


def _proj(x, mod, nw, w, li, gmat, qkw, rope_tabs, *, tiles_per_batch, ctx_row):
    t, d = x.shape
    tm = min(TM, t)
    rope = rope_tabs is not None
    in_specs = [pl.BlockSpec((tm, d), lambda i: (i, 0)),
                pl.BlockSpec((1, N_MOD, d), _mod_index(tiles_per_batch, ctx_row)),
                pl.BlockSpec((3, d), lambda i: (0, 0)),
                _resident((d, N_PROJ), (li,)),
                _resident(gmat.shape),
                _resident(qkw.shape)]
    args = [x, mod, nw, w, gmat, qkw]
    if rope:
        in_specs += [pl.BlockSpec((tm, LANES), lambda i: (i % tiles_per_batch, 0))] * 2
        args += list(rope_tabs)
    widths = (1024, LANES, 3 * D_NA, D_GQ + 2 * D_KV)
    dtypes = (BF16, F32, BF16, BF16)
    return pl.pallas_call(
        functools.partial(_proj_kernel, rope=rope),
        grid=(t // tm,),
        in_specs=in_specs,
        out_specs=[pl.BlockSpec((tm, wd), lambda i: (i, 0)) for wd in widths],
        out_shape=[jax.ShapeDtypeStruct((t, wd), dt) for wd, dt in zip(widths, dtypes)],
        compiler_params=_params(("parallel",)),
        name="mix_in_proj",
    )(*args)


def _ml_prep(d, g_ref, tri_ref):
    log2e = float(np.log2(np.e))
    lc = g_ref.shape[1]
    gates = g_ref[0] * log2e
    gates_t = gates.T
    ig_t = gates_t[0:8]
    lf_t = gates_t[0:16]
    b_t = _dot_f32_left(lf_t, tri_ref[1 - d])[8:16]
    btot_t = _dot_f32_left(lf_t, jnp.ones((lc, lc), BF16))[8:16]
    lf_al = pltpu.roll(gates, LANES - 8, axis=1)
    c_mat = gates - functools.reduce(jnp.add, [_dot(tri_ref[d], p) for p in _split_bf16(lf_al)])
    return ig_t, b_t, btot_t, c_mat


def _ml_head(d, h, prep, m_ref):
    ig_t, b_t, btot_t, c_mat = prep
    lc = c_mat.shape[0]
    r = 4 * d + h
    row = lax.broadcasted_iota(jnp.int32, (lc, lc), 0)
    col = lax.broadcasted_iota(jnp.int32, (lc, lc), 1)
    visible = (row <= col) if d == 0 else (row >= col)
    ig, b, b_tot = ig_t[r:r + 1], b_t[r:r + 1], btot_t[r:r + 1]
    m_prev = m_ref[r:r + 1, :]
    w_end = b_tot - b + ig
    m_new = jnp.maximum(b_tot + m_prev, jnp.max(w_end, axis=1, keepdims=True))
    a = jnp.exp2(w_end - m_new)
    decay = jnp.exp2(b_tot + m_prev - m_new)
    m_inter = b + m_prev
    logw = jnp.where(visible, c_mat[:, r:r + 1] + b, NEG)
    m_j = jnp.maximum(m_inter, jnp.max(logw, axis=0, keepdims=True))
    w = jnp.exp2(logw - m_j)
    m_ref[r:r + 1, :] = m_new
    return w, a, jnp.exp2(m_inter - m_j), jnp.exp2(-m_j), decay


def _ml_pair(d, p, head_a, head_b, q_ref, k_ref, v_ref, o_ref, st_ref):
    (w_a, a_a, g_a, fl_a, dec_a), (w_b, a_b, g_b, fl_b, dec_b) = head_a, head_b
    lc = q_ref.shape[1]
    low = _low_lanes()
    sl = slice(LANES * p, LANES * (p + 1))
    q = q_ref[0, :, sl]
    k = k_ref[0, :, sl] * ATTN_SCALE
    vt = jnp.concatenate([v_ref[0, :, sl].astype(F32).T, jnp.ones((LANES, lc), F32)], axis=0)
    head_row = (lax.broadcasted_iota(jnp.int32, (2 * LANES, 1), 0) % LANES) < HEAD_DIM
    vt16 = vt.astype(BF16)
    r_a = _dot(vt16, (_dot_nt(k, _one_head(q, low, 0)) * w_a).astype(BF16))
    r_b = _dot(vt16, (_dot_nt(k, _one_head(q, low, 1)) * w_b).astype(BF16))
    state = st_ref[d, p]
    r_i = _dot_nt(state.astype(BF16), q)
    r = jnp.where(head_row, r_a, r_b) + jnp.where(head_row, g_a, g_b) * r_i
    num, den = r[:LANES], r[LANES:]
    h_t = num / jnp.maximum(jnp.abs(den), jnp.where(head_row[:LANES], fl_a, fl_b))
    o_ref[0, :, sl] = h_t.T.astype(BF16)

    upd = _dot((vt * jnp.where(head_row, a_a, a_b)).astype(BF16), k)
    same_head = head_row == low
    dec = jnp.where(head_row, dec_a[:, :LANES], dec_b[:, :LANES])
    st_ref[d, p] = dec * state + jnp.where(same_head, upd, 0.0)


def _ml_step(fwd, bwd, tri_ref, st_ref, m_ref):
    dirs = (fwd, bwd)
    preps = [_ml_prep(d, refs[3], tri_ref) for d, refs in enumerate(dirs)]
    heads = [[_ml_head(d, h, preps[d], m_ref) for h in range(H_ML)] for d in range(2)]
    for p in range(H_ML // 2):
        for d, (q_ref, k_ref, v_ref, _, o_ref) in enumerate(dirs):
            _ml_pair(d, p, heads[d][2 * p], heads[d][2 * p + 1], q_ref, k_ref, v_ref, o_ref, st_ref)


def _mlstm_kernel(qf, kf, vf, gf, qb, kb, vb, gb, qc, kc, vc, gc, tri_ref,
                  hf_ref, hb_ref, hcf_ref, hcb_ref, st_ref, m_ref):
    c = pl.program_id(1)

    @pl.when(c == 0)
    def _():
        st_ref[...] = jnp.zeros_like(st_ref)
        m_ref[...] = jnp.zeros_like(m_ref)
        _ml_step((qc, kc, vc, gc, hcf_ref), (qc, kc, vc, gc, hcb_ref), tri_ref, st_ref, m_ref)

    @pl.when(c > 0)
    def _():
        _ml_step((qf, kf, vf, gf, hf_ref), (qb, kb, vb, gb, hb_ref), tri_ref, st_ref, m_ref)


def _mlstm(p_ml, p_mlg, pc_ml, pc_mlg, tri):
    b, s, _ = p_ml.shape
    l = pc_ml.shape[1]
    lc = ML_CHUNK
    assert l == lc and s % lc == 0
    nl = s // lc
    fwd = lambda c: jnp.maximum(c - 1, 0)
    bwd = lambda c: nl - 1 - jnp.maximum(c - 1, 0)

    def lat(idx, blk, width):
        return pl.BlockSpec((1, lc, width), lambda i, c: (i, idx(c), blk))

    def ctx(blk, width):
        return pl.BlockSpec((1, lc, width), lambda i, c: (i, 0, blk))

    in_specs = ([lat(fwd, 0, D_ML), lat(fwd, 1, D_ML), lat(fwd, 2, D_ML), lat(fwd, 0, LANES)]
                + [lat(bwd, 0, D_ML), lat(bwd, 1, D_ML), lat(bwd, 2, D_ML), lat(bwd, 0, LANES)]
                + [ctx(0, D_ML), ctx(1, D_ML), ctx(2, D_ML), ctx(0, LANES)]
                + [_resident((2, lc, lc))])
    out_specs = [lat(fwd, 0, D_ML), lat(bwd, 0, D_ML), ctx(0, D_ML), ctx(0, D_ML)]
    out_shape = [jax.ShapeDtypeStruct((b, s, D_ML), BF16)] * 2 + [jax.ShapeDtypeStruct((b, l, D_ML), BF16)] * 2
    return pl.pallas_call(
        _mlstm_kernel,
        grid=(b, nl + 1),
        in_specs=in_specs,
        out_specs=out_specs,
        out_shape=out_shape,
        scratch_shapes=[pltpu.VMEM((2, H_ML // 2, 2 * LANES, LANES), F32), pltpu.VMEM((2 * H_ML, lc), F32)],
        compiler_params=_params(("parallel", "arbitrary")),
        name="mlstm_bidir",
    )(p_ml, p_ml, p_ml, p_mlg, p_ml, p_ml, p_ml, p_mlg, pc_ml, pc_ml, pc_ml, pc_mlg, tri)


def _attend_heads(n_heads, score_fn, value_fn, s_scr, p_scr):
    s_scr[0] = score_fn(0)
    outs = []
    for i in range(n_heads):
        if i + 1 < n_heads:
            s_scr[(i + 1) % 2] = score_fn(i + 1)
        s = s_scr[i % 2]
        p_scr[i % 2] = jnp.exp2(s - jnp.max(s, axis=1, keepdims=True)).astype(BF16)
        r = _dot(p_scr[i % 2], value_fn(i))
        outs.append(r[:, :LANES] / r[:, LANES:])
    return outs


def _low_lanes():
    return lax.broadcasted_iota(jnp.int32, (1, LANES), 1) < HEAD_DIM


def _one_head(q, low, half):
    zero = jnp.zeros_like(q)
    return jnp.where(low, q, zero) if half == 0 else jnp.where(low, zero, q)


def _pair_outputs(o_ref, outs, low):
    for j in range(len(outs) // 2):
        o_ref[0, :, LANES * j:LANES * (j + 1)] = jnp.where(low, outs[2 * j], outs[2 * j + 1]).astype(BF16)


def _gqa_kernel(q_ref, k_ref, v_ref, kc_ref, vc_ref, o_ref, kall, vall, s_scr, p_scr):
    n_lat = k_ref.shape[1]

    @pl.when(pl.program_id(1) == 0)
    def _():
        kall[0:n_lat, :] = k_ref[0]
        kall[n_lat:, :] = kc_ref[0]
        vall[0:n_lat, 0:LANES] = v_ref[0]
        vall[n_lat:, 0:LANES] = vc_ref[0]
        vall[:, LANES:] = jnp.ones((vall.shape[0], LANES), BF16)

    low = _low_lanes()

    def score(i):
        j, half = divmod(i, 2)
        return _dot_nt(_one_head(q_ref[0, :, LANES * j:LANES * (j + 1)], low, half), kall[...])

    _pair_outputs(o_ref, _attend_heads(H_GQ, score, lambda i: vall[...], s_scr, p_scr), low)


def _gqa(p_gq, pc_gq):
    b, s, _ = p_gq.shape
    l = pc_gq.shape[1]
    tq = min(GQ_TQ, s)
    kblk, vblk = D_GQ // LANES, D_GQ // LANES + 1
    return pl.pallas_call(
        _gqa_kernel,
        grid=(b, s // tq),
        in_specs=[pl.BlockSpec((1, tq, D_GQ), lambda i, t: (i, t, 0)),
                  pl.BlockSpec((1, s, LANES), lambda i, t: (i, 0, kblk)),
                  pl.BlockSpec((1, s, LANES), lambda i, t: (i, 0, vblk)),
                  pl.BlockSpec((1, l, LANES), lambda i, t: (i, 0, kblk)),
                  pl.BlockSpec((1, l, LANES), lambda i, t: (i, 0, vblk))],
        out_specs=pl.BlockSpec((1, tq, D_GQ), lambda i, t: (i, t, 0)),
        out_shape=jax.ShapeDtypeStruct((b, s, D_GQ), BF16),
        scratch_shapes=[pltpu.VMEM((s + l, LANES), BF16), pltpu.VMEM((s + l, 2 * LANES), BF16),
                        pltpu.VMEM((2, tq, s + l), F32), pltpu.VMEM((2, tq, s + l), BF16)],
        compiler_params=_params(("parallel", "arbitrary")),
        name="gqa_latent",
    )(p_gq, p_gq, p_gq, pc_gq, pc_gq)


def _ctx_attn_kernel(q_ref, k_ref, v_ref, o_ref, s_scr, p_scr, *, shared_kv):
    low = _low_lanes()
    ones = jnp.ones((v_ref.shape[1], LANES), BF16)

    def kv_lanes(i):
        return slice(0, LANES) if shared_kv else slice(LANES * (i // 2), LANES * (i // 2 + 1))

    def score(i):
        j, half = divmod(i, 2)
        return _dot_nt(_one_head(q_ref[0, :, LANES * j:LANES * (j + 1)], low, half), k_ref[0, :, kv_lanes(i)])

    def value(i):
        return jnp.concatenate([v_ref[0, :, kv_lanes(i)], ones], axis=1)

    n_heads = 2 * (q_ref.shape[2] // LANES)
    _pair_outputs(o_ref, _attend_heads(n_heads, score, value, s_scr, p_scr), low)


def _ctx_attn(pc, *, qw, kw, shared_kv):
    b, l, _ = pc.shape
    kb = qw // kw
    return pl.pallas_call(
        functools.partial(_ctx_attn_kernel, shared_kv=shared_kv),
        grid=(b,),
        in_specs=[pl.BlockSpec((1, l, qw), lambda i: (i, 0, 0)),
                  pl.BlockSpec((1, l, kw), lambda i: (i, 0, kb)),
                  pl.BlockSpec((1, l, kw), lambda i: (i, 0, kb + 1))],
        out_specs=pl.BlockSpec((1, l, qw), lambda i: (i, 0, 0)),
        out_shape=jax.ShapeDtypeStruct((b, l, qw), BF16),
        scratch_shapes=[pltpu.VMEM((2, l, l), F32), pltpu.VMEM((2, l, l), BF16)],
        compiler_params=_params(("parallel",)),
        name="ctx_attn",
    )(pc, pc, pc)


def _na_kernel(*refs, n_rows):
    q_ref, k_subs, v_subs = refs[0], refs[1:1 + NA_SUBS], refs[1 + NA_SUBS:1 + 2 * NA_SUBS]
    kc_ref, vc_ref, tz_ref, o_ref, kwin, vwin, vctx, s_scr, p_scr = refs[1 + 2 * NA_SUBS:]
    r0 = pl.program_id(1) * NA_QROWS
    start = jnp.clip(r0 - NA_WIN_R // 2, 0, n_rows - NA_BAND)
    nq, nk = NA_HALF * GRID_W, NA_CBAND * GRID_W
    sub = NA_BAND * GRID_W // NA_SUBS
    n_pairs = D_NA // LANES

    for t, (kr, vr) in enumerate(zip(k_subs, v_subs)):
        rows = slice(sub * t, sub * (t + 1))
        kwin[rows, :] = kr[0]
        for j in range(n_pairs):
            vwin[j, rows, 0:LANES] = vr[0, :, LANES * j:LANES * (j + 1)]
    vwin[:, :, LANES:] = jnp.ones((n_pairs,) + vwin.shape[1:2] + (LANES,), BF16)
    for j in range(n_pairs):
        vctx[j, :, 0:LANES] = vc_ref[0, :, LANES * j:LANES * (j + 1)]
    vctx[:, :, LANES:] = jnp.ones((n_pairs,) + vctx.shape[1:2] + (LANES,), BF16)

    low = _low_lanes()
    halves = []
    for hf in range(NA_QROWS // NA_HALF):
        rq = r0 + NA_HALF * hf
        first_key = jnp.clip(rq - NA_WIN_R // 2, 0, n_rows - NA_CBAND)
        off = pl.multiple_of((first_key - start) * GRID_W, sub)
        qrow = rq + lax.broadcasted_iota(jnp.int32, (nq, nk), 0) // GRID_W
        krow = first_key + lax.broadcasted_iota(jnp.int32, (nq, nk), 1) // GRID_W
        first = jnp.clip(qrow - NA_WIN_R // 2, 0, n_rows - NA_WIN_R)
        row_mask = jnp.where((krow >= first) & (krow < first + NA_WIN_R), 0.0, NEG)
        halves.append((off, first_key - rq, row_mask))

    n_inst = len(halves) * H_NA

    def score(i, dst):
        hf, h = divmod(i, H_NA)
        j, half = divmod(h, 2)
        off, delta, row_mask = halves[hf]
        sl = slice(LANES * j, LANES * (j + 1))
        qm = _one_head(q_ref[0, nq * hf:nq * (hf + 1), sl], low, half)
        slabs = []
        for a in range(NA_HALF):
            pieces = [tz_ref[h, jnp.clip(delta + 2 * bp - a, -8, 7) + 8] for bp in range(NA_CBAND // 2)]
            slabs.append(jnp.concatenate(pieces, axis=1))
        dst[:, :nk] = _dot_nt(qm, kwin[pl.ds(off, nk), sl]) + (jnp.concatenate(slabs, axis=0) + row_mask)
        dst[:, nk:] = _dot_nt(qm, kc_ref[0, :, sl])

    score(0, s_scr.at[0])
    outs = []
    for i in range(n_inst):
        if i + 1 < n_inst:
            score(i + 1, s_scr.at[(i + 1) % 2])
        hf, h = divmod(i, H_NA)
        s = s_scr[i % 2]
        p_scr[i % 2] = jnp.exp2(s - jnp.max(s, axis=1, keepdims=True)).astype(BF16)
        r = (_dot(p_scr[i % 2, :, :nk], vwin[h // 2, pl.ds(halves[hf][0], nk), :])
             + _dot(p_scr[i % 2, :, nk:], vctx[h // 2]))
        outs.append(r[:, :LANES] / r[:, LANES:])
    for hf in range(len(halves)):
        for j in range(n_pairs):
            pair = jnp.where(low, outs[hf * H_NA + 2 * j], outs[hf * H_NA + 2 * j + 1])
            o_ref[0, nq * hf:nq * (hf + 1), LANES * j:LANES * (j + 1)] = pair.astype(BF16)


def _na(p_na, pc_na, tz, li):
    b, s, _ = p_na.shape
    l = pc_na.shape[1]
    n_rows = s // GRID_W
    assert n_rows % NA_QROWS == 0 and n_rows >= NA_BAND
    nq, nk = NA_QROWS * GRID_W, NA_BAND * GRID_W
    nh, nkc = NA_HALF * GRID_W, NA_CBAND * GRID_W
    sub = nk // NA_SUBS
    rows_per_sub = NA_BAND // NA_SUBS
    assert all(v % rows_per_sub == 0 for v in (NA_QROWS, NA_HALF, NA_WIN_R // 2, n_rows - NA_BAND,
                                               n_rows - NA_CBAND))

    def band(t, blk):
        def idx(i, r):
            start = jnp.clip(r * NA_QROWS - NA_WIN_R // 2, 0, n_rows - NA_BAND)
            return (i, start // rows_per_sub + t, blk)
        return pl.BlockSpec((1, sub, D_NA), idx)

    in_specs = ([pl.BlockSpec((1, nq, D_NA), lambda i, r: (i, r, 0))]
                + [band(t, 1) for t in range(NA_SUBS)] + [band(t, 2) for t in range(NA_SUBS)]
                + [pl.BlockSpec((1, l, D_NA), lambda i, r: (i, 0, 1)),
                   pl.BlockSpec((1, l, D_NA), lambda i, r: (i, 0, 2)),
                   _resident(tz.shape[1:], (li,))])
    return pl.pallas_call(
        functools.partial(_na_kernel, n_rows=n_rows),
        grid=(b, n_rows // NA_QROWS),
        in_specs=in_specs,
        out_specs=pl.BlockSpec((1, nq, D_NA), lambda i, r: (i, r, 0)),
        out_shape=jax.ShapeDtypeStruct((b, s, D_NA), BF16),
        scratch_shapes=[pltpu.VMEM((nk, D_NA), BF16), pltpu.VMEM((D_NA // LANES, nk, 2 * LANES), BF16),
                        pltpu.VMEM((D_NA // LANES, l, 2 * LANES), BF16),
                        pltpu.VMEM((2, nh, nkc + l), F32), pltpu.VMEM((2, nh, nkc + l), BF16)],
        compiler_params=_params(("parallel", "arbitrary")),
        name="na_latent",
    )(p_na, *([p_na] * (2 * NA_SUBS)), pc_na, pc_na, tz)


def _merge_kernel(x_ref, mod_ref, nw_ref, hf_ref, hb_ref, og_ref, mlw_ref, g_ref, na_ref, gq_ref,
                  wg_ref, wml_ref, wna_ref, wgq_ref, wo_ref, o_ref):
    x = x_ref[...]
    d = x.shape[1]
    hx = _modnorm(x, nw_ref[1:2, :], mod_ref[0, 3:4, :], mod_ref[0, 4:5, :]).astype(BF16)
    h = hf_ref[...].astype(F32) + hb_ref[...].astype(F32)
    o_ml = (_head_norm(h, g_ref[...], mlw_ref[...]) * _sigmoid(og_ref[...].astype(F32))).astype(BF16)
    y = None
    for j, (o_br, w_br) in enumerate(((o_ml, wml_ref), (na_ref[...], wna_ref), (gq_ref[...], wgq_ref))):
        part = _sigmoid(_dot(hx, wg_ref[:, d * j:d * (j + 1)])) * _dot(o_br, w_br[...])
        y = part if y is None else y + part
    o_ref[...] = x + mod_ref[0, 5:6, :] * _dot(y.astype(BF16), wo_ref[...])


def _merge(x, mod, nw, hf, hb, p_ml, mlw, gmat, o_na, o_gq, wg, wml, wna, wgq, wo, li,
           *, tiles_per_batch, ctx_row):
    t, d = x.shape
    tm = min(TM, t)
    row = lambda wd, blk=0: pl.BlockSpec((tm, wd), lambda i: (i, blk))
    lead = (li,)
    return pl.pallas_call(
        _merge_kernel,
        grid=(t // tm,),
        in_specs=[row(d),
                  pl.BlockSpec((1, N_MOD, d), _mod_index(tiles_per_batch, ctx_row)),
                  pl.BlockSpec((3, d), lambda i: (0, 0)),
                  row(D_ML), row(D_ML), row(D_ML, 3),
                  _resident((1, D_ML)), _resident(gmat.shape),
                  row(D_NA), row(D_GQ),
                  _resident((d, 3 * d), lead), _resident((D_ML, d), lead), _resident((D_NA, d), lead),
                  _resident((D_GQ, d), lead), _resident((d, d), lead)],
        out_specs=row(d),
        out_shape=jax.ShapeDtypeStruct((t, d), F32),
        compiler_params=_params(("parallel",)),
        name="branch_merge",
    )(x, mod, nw, hf, hb, p_ml, mlw, gmat, o_na, o_gq, wg, wml, wna, wgq, wo)


def _proj_weight(w):
    d = w.shape[-2]
    o = 0
    seg = {}
    for name, width in (("ml_k", D_ML), ("ml_v", D_ML), ("ml_g", 4 * H_ML), ("na_k", D_NA), ("na_v", D_NA),
                        ("gq_k", D_KV), ("gq_v", D_KV), ("ml_q", D_ML), ("ml_o", D_ML), ("na_q", D_NA),
                        ("gq_q", D_GQ), ("br_g", 3 * d)):
        seg[name] = w[..., o:o + width].astype(BF16)
        o += width
    gq_q = jnp.concatenate([seg["gq_q"][..., HEAD_DIM * h:HEAD_DIM * (h + 1)] for h in GQ_HEAD_ORDER], axis=-1)
    pad = jnp.zeros(w.shape[:-1] + (LANES - 4 * H_ML,), BF16)
    out = jnp.concatenate([seg["ml_q"], seg["ml_k"], seg["ml_v"], seg["ml_o"], seg["ml_g"][..., ML_GATE_ORDER], pad,
                           seg["na_q"], seg["na_k"], seg["na_v"], gq_q, seg["gq_k"], seg["gq_v"]], axis=-1)
    return out, seg["br_g"]


def _rope_tables(n_tok):
    t = np.arange(n_tok)
    row = (t // GRID_W).astype(np.float64)
    col = (t % GRID_W).astype(np.float64)
    n_freq = HEAD_DIM // 4
    inv = ROPE_THETA ** (-np.arange(n_freq, dtype=np.float64) / n_freq)
    ang = np.concatenate([row[:, None] * inv, col[:, None] * inv], axis=-1)
    cos, sin = np.cos(ang), np.sin(ang)
    cos_t = np.tile(cos, (1, LANES // (HEAD_DIM // 2)))
    sin_t = np.tile(np.concatenate([-sin, sin], axis=-1), (1, LANES // HEAD_DIM))
    return jnp.asarray(cos_t, F32), jnp.asarray(sin_t, F32)


def _na_bias_table(rpb):
    col = np.arange(GRID_W)
    first = np.clip(col - NA_WIN_C // 2, 0, GRID_W - NA_WIN_C)
    in_win = (col[None, :] >= first[:, None]) & (col[None, :] < first[:, None] + NA_WIN_C)
    side = GRID_W - NA_WIN_C
    width = 2 * GRID_W
    rows = jnp.pad(rpb, ((0, 0), (0, 0), (1, 1), (side, width - side - rpb.shape[-1])))
    lead = rows.shape[:-1]
    flat = jnp.broadcast_to(rows[..., None, :], lead + (GRID_W, width)).reshape(lead + (GRID_W * width,))
    skew = flat[..., :GRID_W * (width - 1)].reshape(lead + (GRID_W, width - 1))
    full = skew[..., GRID_W - 1:2 * GRID_W - 1] * float(np.log2(np.e))
    row_ok = np.zeros((2 * NA_WIN_R + 1,), bool)
    row_ok[1:-1] = True
    full = jnp.where(jnp.asarray(in_win[None, None, None] & row_ok[None, None, :, None, None]), full, NEG)
    return jnp.concatenate([full[:, :, :-1], full[:, :, 1:]], axis=-1).astype(F32)


def kernel(x, c, ctx, c_ctx, ada_w, ada_b, norm_w, ffn_w_in, ffn_w_out, mix_w_in, ml_gate_b, ml_norm_w,
           na_qk_w, na_rpb, gq_qk_w, w_br_ml, w_br_na, w_br_gq, w_out):
    b, s, d = x.shape
    l = ctx.shape[1]
    depth = ada_w.shape[0]
    assert b < MOD_ROWS and s % TM == 0 and (b * l) % min(TM, b * l) == 0
    ctx_row = b
    tiles_per_batch = s // TM

    cvec = jnp.zeros((MOD_ROWS, d), F32).at[:b].set(c).at[b].set(c_ctx)
    mod = _ada(cvec, ada_w, ada_b).reshape(depth, MOD_ROWS, N_MOD, d)

    lane = np.arange(2 * LANES)
    gmat = jnp.asarray((lane[:, None] // HEAD_DIM) == (lane[None, :] // HEAD_DIM), BF16)
    idx = np.arange(ML_CHUNK)
    tri = jnp.asarray(np.stack([idx[:, None] >= idx[None, :], idx[:, None] <= idx[None, :]]), BF16)
    rope_tabs = _rope_tables(s)

    w_in, w_o = ffn_w_in.astype(BF16), ffn_w_out.astype(BF16)
    w_proj, wg = _proj_weight(mix_w_in)
    wml, wna, wo = w_br_ml.astype(BF16), w_br_na.astype(BF16), w_out.astype(BF16)
    wgq = jnp.concatenate([w_br_gq[:, HEAD_DIM * h:HEAD_DIM * (h + 1)] for h in GQ_HEAD_ORDER], axis=1).astype(BF16)
    tz = _na_bias_table(na_rpb)

    xl = x.reshape(b * s, d)
    xc = ctx.reshape(b * l, d)
    lat = dict(tiles_per_batch=tiles_per_batch, ctx_row=ctx_row)
    con = dict(tiles_per_batch=None, ctx_row=ctx_row)
    for li in range(depth):
        ctx_out = li < depth - 1
        qkw = jnp.zeros((8, 2 * D_NA), F32)
        qkw = qkw.at[0, :D_NA].set(jnp.tile(na_qk_w[li, 0], H_NA) * Q_PRESCALE)
        qkw = qkw.at[0, D_NA:].set(jnp.tile(na_qk_w[li, 1], H_NA))
        qkw = qkw.at[1, :D_GQ].set(jnp.tile(gq_qk_w[li, 0], H_GQ) * Q_PRESCALE)
        qkw = qkw.at[1, D_GQ:D_GQ + D_KV].set(jnp.tile(gq_qk_w[li, 1], H_KV))
        qkw = qkw.at[2, :4 * H_ML].set(ml_gate_b[li][ML_GATE_ORDER])
        mlw = ml_norm_w[li].reshape(1, D_ML)
        m, nw = mod[li], norm_w[li]

        xl = _ffn(xl, m, nw, w_in, w_o, (li, 0), k0=0, nrm=0, **lat)
        xc = _ffn(xc, m, nw, w_in, w_o, (li, 0), k0=0, nrm=0, **con)

        p_ml, p_mlg, p_na, p_gq = _proj(xl, m, nw, w_proj, li, gmat, qkw, rope_tabs, **lat)
        pc_ml, pc_mlg, pc_na, pc_gq = _proj(xc, m, nw, w_proj, li, gmat, qkw, None, **con)
        seq = lambda a: a.reshape(b, s, a.shape[-1])
        cseq = lambda a: a.reshape(b, l, a.shape[-1])

        hf, hb, hcf, hcb = _mlstm(seq(p_ml), seq(p_mlg), cseq(pc_ml), cseq(pc_mlg), tri)
        o_na = _na(seq(p_na), cseq(pc_na), tz, li)
        o_gq = _gqa(seq(p_gq), cseq(pc_gq))
        flat = lambda a: a.reshape(-1, a.shape[-1])
        xl = _merge(xl, m, nw, flat(hf), flat(hb), p_ml, mlw, gmat, flat(o_na), flat(o_gq),
                    wg, wml, wna, wgq, wo, li, **lat)
        xl = _ffn(xl, m, nw, w_in, w_o, (li, 1), k0=6, nrm=2, **lat)
        if ctx_out:
            co_na = _ctx_attn(cseq(pc_na), qw=D_NA, kw=D_NA, shared_kv=False)
            co_gq = _ctx_attn(cseq(pc_gq), qw=D_GQ, kw=D_KV, shared_kv=True)
            xc = _merge(xc, m, nw, flat(hcf), flat(hcb), pc_ml, mlw, gmat, flat(co_na), flat(co_gq),
                        wg, wml, wna, wgq, wo, li, **con)
            xc = _ffn(xc, m, nw, w_in, w_o, (li, 1), k0=6, nrm=2, **con)
    return xl.reshape(b, s, d)
```

```python
import functools

import numpy as np
import jax
import jax.numpy as jnp
from jax import lax
from jax.experimental import pallas as pl
from jax.experimental.pallas import tpu as pltpu

F32 = jnp.float32
BF16 = jnp.bfloat16

HEAD_DIM = 64
LANES = 128
H_ML, H_NA, H_GQ, H_KV = 4, 6, 6, 2
D_ML, D_NA, D_GQ, D_KV = 256, 384, 384, 128
GRID_W = 64
NA_WIN_R, NA_WIN_C = 8, 16
ROPE_THETA = 10000.0
EPS = 1e-6
N_MOD = 9
ATTN_SCALE = HEAD_DIM ** -0.5
NEG = -1e30

ML_CHUNK = 256
ML_ROWS = 2
NA_QROWS = 16
NA_HALF = 4
NA_BAND = 24
NA_CBAND = 12
NA_SUBS = 6
TM = 1024
GQ_TQ = 512
FFN_CHUNK = 768
Q_PRESCALE = ATTN_SCALE * float(np.log2(np.e))
MOD_ROWS = 16
VMEM_LIMIT = 56 * 1024 * 1024

C_ML, C_MLG, C_NA, C_GQ, C_END = 0, 1024, 1152, 2304, 2944
N_PROJ = C_END
GQ_HEAD_ORDER = (0, 3, 1, 4, 2, 5)
ML_GATE_ORDER = np.array([0, 1, 2, 3, 8, 9, 10, 11, 4, 5, 6, 7, 12, 13, 14, 15])


def _dot(a, b):
    return jnp.dot(a, b, preferred_element_type=F32)


def _dot_nt(a, b):
    return lax.dot_general(a, b, (((1,), (1,)), ((), ())), preferred_element_type=F32)


def _sigmoid(x):
    return 1.0 / (1.0 + jnp.exp(-x))


def _log_sigmoid(x):
    return jnp.minimum(x, 0.0) - jnp.log1p(jnp.exp(-jnp.abs(x)))


def _split_bf16(x):
    parts = []
    for _ in range(3):
        p = x.astype(BF16)
        parts.append(p)
        x = x - p.astype(F32)
    return parts


def _dot_f32_left(x, m01):
    return functools.reduce(jnp.add, [_dot(p, m01) for p in _split_bf16(x)])


def _modnorm(x, nw, shift, scale):
    ms = jnp.mean(x * x, axis=-1, keepdims=True)
    return (x * lax.rsqrt(ms + EPS) * nw) * (1.0 + scale) + shift


def _head_norm(t, gmat, wrow):
    ss = _dot((t * t).astype(BF16), gmat)
    return t * lax.rsqrt(ss * (1.0 / HEAD_DIM) + EPS) * wrow


def _resident(shape, lead=()):
    return pl.BlockSpec((None,) * len(lead) + tuple(shape), lambda *_: tuple(lead) + (0,) * len(shape),
                        pipeline_mode=pl.Buffered(1))


def _params(sem):
    return pltpu.CompilerParams(dimension_semantics=sem, vmem_limit_bytes=VMEM_LIMIT)


def _ada_kernel(c_ref, w_ref, b_ref, o_ref):
    c = c_ref[...]
    s = (c * _sigmoid(c)).astype(BF16)
    o_ref[0] = _dot(s, w_ref[0].astype(BF16)) + b_ref[0]


def _ada(cvec, ada_w, ada_b):
    depth, d, n = ada_w.shape
    tn = n // 8
    return pl.pallas_call(
        _ada_kernel,
        grid=(depth, n // tn),
        in_specs=[pl.BlockSpec((MOD_ROWS, d), lambda l, j: (0, 0)),
                  pl.BlockSpec((1, d, tn), lambda l, j: (l, 0, j)),
                  pl.BlockSpec((1, 1, tn), lambda l, j: (l, 0, j))],
        out_specs=pl.BlockSpec((1, MOD_ROWS, tn), lambda l, j: (l, 0, j)),
        out_shape=jax.ShapeDtypeStruct((depth, MOD_ROWS, n), F32),
        compiler_params=_params(("arbitrary", "arbitrary")),
        name="ada_mod",
    )(cvec, ada_w, ada_b.reshape(depth, 1, n))


def _mod_index(tiles_per_batch, ctx_row):
    if tiles_per_batch is None:
        return lambda i, *_: (ctx_row, 0, 0)
    return lambda i, *_: (i // tiles_per_batch, 0, 0)


def _ffn_kernel(x_ref, mod_ref, nw_ref, wi_ref, wo_ref, o_ref, *, k0, nrm, chunks):
    x = x_ref[...]
    h = _modnorm(x, nw_ref[nrm:nrm + 1, :], mod_ref[0, k0:k0 + 1, :], mod_ref[0, k0 + 1:k0 + 2, :]).astype(BF16)
    dff = wo_ref.shape[0]
    y = None
    for c0, c1 in chunks:
        g = _dot(h, wi_ref[:, c0:c1])
        u = _dot(h, wi_ref[:, dff + c0:dff + c1])
        part = _dot((g * _sigmoid(g) * u).astype(BF16), wo_ref[c0:c1, :])
        y = part if y is None else y + part
    o_ref[...] = x + (0.5 * mod_ref[0, k0 + 2:k0 + 3, :]) * y


def _ffn(x, mod, nw, w_in, w_out, lead, *, k0, nrm, tiles_per_batch, ctx_row):
    t, d = x.shape
    dff = w_out.shape[-2]
    tm = min(TM, t)
    edges = list(range(0, dff, FFN_CHUNK)) + [dff]
    chunks = tuple(zip(edges[:-1], edges[1:]))
    kern = functools.partial(_ffn_kernel, k0=k0, nrm=nrm, chunks=chunks)
    return pl.pallas_call(
        kern,
        grid=(t // tm,),
        in_specs=[pl.BlockSpec((tm, d), lambda i: (i, 0)),
                  pl.BlockSpec((1, N_MOD, d), _mod_index(tiles_per_batch, ctx_row)),
                  pl.BlockSpec((3, d), lambda i: (0, 0)),
                  _resident((d, 2 * dff), lead),
                  _resident((dff, d), lead)],
        out_specs=pl.BlockSpec((tm, d), lambda i: (i, 0)),
        out_shape=jax.ShapeDtypeStruct((t, d), F32),
        compiler_params=_params(("parallel",)),
        name="ffn_swiglu",
    )(x, mod, nw, w_in, w_out)


def _proj_kernel(*refs, rope):
    if rope:
        (x_ref, mod_ref, nw_ref, w_ref, g_ref, qkw_ref, cos_ref, sin_ref,
         ml_ref, mlg_ref, na_ref, gq_ref) = refs
    else:
        (x_ref, mod_ref, nw_ref, w_ref, g_ref, qkw_ref,
         ml_ref, mlg_ref, na_ref, gq_ref) = refs
    h = _modnorm(x_ref[...], nw_ref[1:2, :], mod_ref[0, 3:4, :], mod_ref[0, 4:5, :]).astype(BF16)
    gmat = g_ref[...]

    ml_ref[...] = _dot(h, w_ref[:, C_ML:C_MLG]).astype(BF16)
    gates = _dot(h, w_ref[:, C_MLG:C_NA]) + qkw_ref[2:3, 0:LANES]
    glane = lax.broadcasted_iota(jnp.int32, (1, LANES), 1)
    mlg_ref[...] = jnp.where((glane >= 2 * H_ML) & (glane < 4 * H_ML), _log_sigmoid(gates), gates)

    na = _dot(h, w_ref[:, C_NA:C_GQ])
    for j in range(3):
        sl = slice(2 * LANES * j, 2 * LANES * (j + 1))
        na_ref[:, sl] = _head_norm(na[:, sl], gmat, qkw_ref[0:1, sl]).astype(BF16)
    na_ref[:, 2 * D_NA:] = na[:, 2 * D_NA:].astype(BF16)

    gq = _dot(h, w_ref[:, C_GQ:C_END])
    if rope:
        lane = lax.broadcasted_iota(jnp.int32, (1, LANES), 1)
        first_half = (lane % HEAD_DIM) < (HEAD_DIM // 2)
        cos = cos_ref[...]
        sin = sin_ref[...]
    for j in range(2):
        t2 = _head_norm(gq[:, 2 * LANES * j:2 * LANES * (j + 1)], gmat, qkw_ref[1:2, 2 * LANES * j:2 * LANES * (j + 1)])
        for half in range(2):
            t = t2[:, LANES * half:LANES * (half + 1)]
            if rope:
                rot = jnp.where(first_half, pltpu.roll(t, LANES - HEAD_DIM // 2, axis=1),
                                pltpu.roll(t, HEAD_DIM // 2, axis=1))
                t = t * cos + rot * sin
            gq_ref[:, LANES * (2 * j + half):LANES * (2 * j + half + 1)] = t.astype(BF16)
    gq_ref[:, D_GQ + D_KV:] = gq[:, D_GQ + D_KV:].astype(BF16)


def _proj(x, mod, nw, w, li, gmat, qkw, rope_tabs, *, tiles_per_batch, ctx_row):
    t, d = x.shape
    tm = min(TM, t)
    rope = rope_tabs is not None
    in_specs = [pl.BlockSpec((tm, d), lambda i: (i, 0)),
                pl.BlockSpec((1, N_MOD, d), _mod_index(tiles_per_batch, ctx_row)),
                pl.BlockSpec((3, d), lambda i: (0, 0)),
                _resident((d, N_PROJ), (li,)),
                _resident(gmat.shape),
                _resident(qkw.shape)]
    args = [x, mod, nw, w, gmat, qkw]
    if rope:
        in_specs += [pl.BlockSpec((tm, LANES), lambda i: (i % tiles_per_batch, 0))] * 2
        args += list(rope_tabs)
    widths = (1024, LANES, 3 * D_NA, D_GQ + 2 * D_KV)
    dtypes = (BF16, F32, BF16, BF16)
    return pl.pallas_call(
        functools.partial(_proj_kernel, rope=rope),
        grid=(t // tm,),
        in_specs=in_specs,
        out_specs=[pl.BlockSpec((tm, wd), lambda i: (i, 0)) for wd in widths],
        out_shape=[jax.ShapeDtypeStruct((t, wd), dt) for wd, dt in zip(widths, dtypes)],
        compiler_params=_params(("parallel",)),
        name="mix_in_proj",
    )(*args)


def _ml_prep(d, g_ref, tri_ref):
    log2e = float(np.log2(np.e))
    lc = g_ref.shape[0]
    ng = 2 * H_ML
    gates = g_ref[...] * log2e
    gates_t = gates.T
    ig_t = gates_t[0:ng]
    lf_t = gates_t[0:2 * ng]
    b_t = _dot_f32_left(lf_t, tri_ref[1 - d])[ng:2 * ng]
    btot_t = _dot_f32_left(lf_t, jnp.ones((lc, lc), BF16))[ng:2 * ng]
    lf_al = pltpu.roll(gates, LANES - ng, axis=1)
    c_mat = gates - functools.reduce(jnp.add, [_dot(tri_ref[d], p) for p in _split_bf16(lf_al)])
    return ig_t, b_t, btot_t, c_mat


def _ml_head(d, h, prep, m_ref):
    ig_t, b_t, btot_t, c_mat = prep
    lc = c_mat.shape[0]
    r = H_ML * d + h
    row = lax.broadcasted_iota(jnp.int32, (lc, lc), 0)
    col = lax.broadcasted_iota(jnp.int32, (lc, lc), 1)
    visible = (row <= col) if d == 0 else (row >= col)
    ig, b, b_tot = ig_t[r:r + 1], b_t[r:r + 1], btot_t[r:r + 1]
    m_prev = m_ref[r:r + 1, :]
    w_end = b_tot - b + ig
    m_new = jnp.maximum(b_tot + m_prev, jnp.max(w_end, axis=1, keepdims=True))
    a = jnp.exp2(w_end - m_new)
    decay = jnp.exp2(b_tot + m_prev - m_new)
    m_inter = b + m_prev
    logw = jnp.where(visible, c_mat[:, r:r + 1] + b, NEG)
    m_j = jnp.maximum(m_inter, jnp.max(logw, axis=0, keepdims=True))
    w = jnp.exp2(logw - m_j)
    m_ref[r:r + 1, :] = m_new
    return w, a, jnp.exp2(m_inter - m_j), jnp.exp2(-m_j), decay


def _ml_pair(d, p, head_a, head_b, q_ref, k_ref, v_ref, o_ref, st_ref):
    (w_a, a_a, g_a, fl_a, dec_a), (w_b, a_b, g_b, fl_b, dec_b) = head_a, head_b
    lc = q_ref.shape[0]
    low = _low_lanes()
    sl = slice(LANES * p, LANES * (p + 1))
    q = q_ref[:, sl]
    k = k_ref[:, sl] * ATTN_SCALE
    vt = jnp.concatenate([v_ref[:, sl].astype(F32).T, jnp.ones((LANES, lc), F32)], axis=0)
    head_row = (lax.broadcasted_iota(jnp.int32, (2 * LANES, 1), 0) % LANES) < HEAD_DIM
    vt16 = vt.astype(BF16)
    r_a = _dot(vt16, (_dot_nt(k, _one_head(q, low, 0)) * w_a).astype(BF16))
    r_b = _dot(vt16, (_dot_nt(k, _one_head(q, low, 1)) * w_b).astype(BF16))
    state = st_ref[d, p]
    r_i = _dot_nt(state.astype(BF16), q)
    r = jnp.where(head_row, r_a, r_b) + jnp.where(head_row, g_a, g_b) * r_i
    num, den = r[:LANES], r[LANES:]
    h_t = num / jnp.maximum(jnp.abs(den), jnp.where(head_row[:LANES], fl_a, fl_b))
    o_ref[:, sl] = h_t.T.astype(BF16)

    upd = _dot((vt * jnp.where(head_row, a_a, a_b)).astype(BF16), k)
    same_head = head_row == low
    dec = jnp.where(head_row, dec_a[:, :LANES], dec_b[:, :LANES])
    st_ref[d, p] = dec * state + jnp.where(same_head, upd, 0.0)


def _ml_step(fwd, bwd, tri_ref, st_ref, m_ref):
    dirs = (fwd, bwd)
    preps = [_ml_prep(d, refs[3], tri_ref) for d, refs in enumerate(dirs)]
    heads = [[_ml_head(d, h, preps[d], m_ref) for h in range(H_ML)] for d in range(2)]
    for p in range(H_ML // 2):
        for d, (q_ref, k_ref, v_ref, _, o_ref) in enumerate(dirs):
            _ml_pair(d, p, heads[d][2 * p], heads[d][2 * p + 1], q_ref, k_ref, v_ref, o_ref, st_ref)


def _mlstm_kernel(qf, kf, vf, gf, qb, kb, vb, gb, qc, kc, vc, gc, tri_ref,
                  hf_ref, hb_ref, hcf_ref, hcb_ref, st_ref, m_ref):
    c = pl.program_id(1)

    def rows(bi, *refs):
        return tuple(r.at[bi] for r in refs)

    @pl.when(c == 0)
    def _():
        st_ref[...] = jnp.zeros_like(st_ref)
        m_ref[...] = jnp.zeros_like(m_ref)
        for bi in range(ML_ROWS):
            _ml_step(rows(bi, qc, kc, vc, gc, hcf_ref), rows(bi, qc, kc, vc, gc, hcb_ref), tri_ref,
                     st_ref.at[bi], m_ref.at[bi])

    @pl.when(c > 0)
    def _():
        for bi in range(ML_ROWS):
            _ml_step(rows(bi, qf, kf, vf, gf, hf_ref), rows(bi, qb, kb, vb, gb, hb_ref), tri_ref,
                     st_ref.at[bi], m_ref.at[bi])


def _mlstm(p_ml, p_mlg, pc_ml, pc_mlg, tri):
    b, s, _ = p_ml.shape
    l = pc_ml.shape[1]
    lc = ML_CHUNK
    assert l == lc and s % lc == 0 and b % ML_ROWS == 0
    nl = s // lc
    fwd = lambda c: jnp.maximum(c - 1, 0)
    bwd = lambda c: nl - 1 - jnp.maximum(c - 1, 0)

    def lat(idx, blk, width):
        return pl.BlockSpec((ML_ROWS, lc, width), lambda i, c: (i, idx(c), blk))

    def ctx(blk, width):
        return pl.BlockSpec((ML_ROWS, lc, width), lambda i, c: (i, 0, blk))

    in_specs = ([lat(fwd, 0, D_ML), lat(fwd, 1, D_ML), lat(fwd, 2, D_ML), lat(fwd, 0, LANES)]
                + [lat(bwd, 0, D_ML), lat(bwd, 1, D_ML), lat(bwd, 2, D_ML), lat(bwd, 0, LANES)]
                + [ctx(0, D_ML), ctx(1, D_ML), ctx(2, D_ML), ctx(0, LANES)]
                + [_resident((2, lc, lc))])
    out_specs = [lat(fwd, 0, D_ML), lat(bwd, 0, D_ML), ctx(0, D_ML), ctx(0, D_ML)]
    out_shape = [jax.ShapeDtypeStruct((b, s, D_ML), BF16)] * 2 + [jax.ShapeDtypeStruct((b, l, D_ML), BF16)] * 2
    return pl.pallas_call(
        _mlstm_kernel,
        grid=(b // ML_ROWS, nl + 1),
        in_specs=in_specs,
        out_specs=out_specs,
        out_shape=out_shape,
        scratch_shapes=[pltpu.VMEM((ML_ROWS, 2, H_ML // 2, 2 * LANES, LANES), F32),
                        pltpu.VMEM((ML_ROWS, 2 * H_ML, lc), F32)],
        compiler_params=_params(("parallel", "arbitrary")),
        name="mlstm_bidir",
    )(p_ml, p_ml, p_ml, p_mlg, p_ml, p_ml, p_ml, p_mlg, pc_ml, pc_ml, pc_ml, pc_mlg, tri)


def _attend_heads(n_heads, score_fn, value_fn, s_scr, p_scr):
    s_scr[0] = score_fn(0)
    outs = []
    for i in range(n_heads):
        if i + 1 < n_heads:
            s_scr[(i + 1) % 2] = score_fn(i + 1)
        s = s_scr[i % 2]
        p_scr[i % 2] = jnp.exp2(s - jnp.max(s, axis=1, keepdims=True)).astype(BF16)
        r = _dot(p_scr[i % 2], value_fn(i))
        outs.append(r[:, :LANES] / r[:, LANES:])
    return outs


def _low_lanes():
    return lax.broadcasted_iota(jnp.int32, (1, LANES), 1) < HEAD_DIM


def _one_head(q, low, half):
    zero = jnp.zeros_like(q)
    return jnp.where(low, q, zero) if half == 0 else jnp.where(low, zero, q)


def _pair_outputs(o_ref, outs, low):
    for j in range(len(outs) // 2):
        o_ref[0, :, LANES * j:LANES * (j + 1)] = jnp.where(low, outs[2 * j], outs[2 * j + 1]).astype(BF16)


def _gqa_kernel(q_ref, k_ref, v_ref, kc_ref, vc_ref, o_ref, kall, vall, s_scr, p_scr):
    n_lat = k_ref.shape[1]

    @pl.when(pl.program_id(1) == 0)
    def _():
        kall[0:n_lat, :] = k_ref[0]
        kall[n_lat:, :] = kc_ref[0]
        vall[0:n_lat, 0:LANES] = v_ref[0]
        vall[n_lat:, 0:LANES] = vc_ref[0]
        vall[:, LANES:] = jnp.ones((vall.shape[0], LANES), BF16)

    low = _low_lanes()

    def score(i):
        j, half = divmod(i, 2)
        return _dot_nt(_one_head(q_ref[0, :, LANES * j:LANES * (j + 1)], low, half), kall[...])

    _pair_outputs(o_ref, _attend_heads(H_GQ, score, lambda i: vall[...], s_scr, p_scr), low)


def _gqa(p_gq, pc_gq):
    b, s, _ = p_gq.shape
    l = pc_gq.shape[1]
    tq = min(GQ_TQ, s)
    kblk, vblk = D_GQ // LANES, D_GQ // LANES + 1
    return pl.pallas_call(
        _gqa_kernel,
        grid=(b, s // tq),
        in_specs=[pl.BlockSpec((1, tq, D_GQ), lambda i, t: (i, t, 0)),
                  pl.BlockSpec((1, s, LANES), lambda i, t: (i, 0, kblk)),
                  pl.BlockSpec((1, s, LANES), lambda i, t: (i, 0, vblk)),
                  pl.BlockSpec((1, l, LANES), lambda i, t: (i, 0, kblk)),
                  pl.BlockSpec((1, l, LANES), lambda i, t: (i, 0, vblk))],
        out_specs=pl.BlockSpec((1, tq, D_GQ), lambda i, t: (i, t, 0)),
        out_shape=jax.ShapeDtypeStruct((b, s, D_GQ), BF16),
        scratch_shapes=[pltpu.VMEM((s + l, LANES), BF16), pltpu.VMEM((s + l, 2 * LANES), BF16),
                        pltpu.VMEM((2, tq, s + l), F32), pltpu.VMEM((2, tq, s + l), BF16)],
        compiler_params=_params(("parallel", "arbitrary")),
        name="gqa_latent",
    )(p_gq, p_gq, p_gq, pc_gq, pc_gq)


def _ctx_attn_kernel(q_ref, k_ref, v_ref, o_ref, s_scr, p_scr, *, shared_kv):
    low = _low_lanes()
    ones = jnp.ones((v_ref.shape[1], LANES), BF16)

    def kv_lanes(i):
        return slice(0, LANES) if shared_kv else slice(LANES * (i // 2), LANES * (i // 2 + 1))

    def score(i):
        j, half = divmod(i, 2)
        return _dot_nt(_one_head(q_ref[0, :, LANES * j:LANES * (j + 1)], low, half), k_ref[0, :, kv_lanes(i)])

    def value(i):
        return jnp.concatenate([v_ref[0, :, kv_lanes(i)], ones], axis=1)

    n_heads = 2 * (q_ref.shape[2] // LANES)
    _pair_outputs(o_ref, _attend_heads(n_heads, score, value, s_scr, p_scr), low)


def _ctx_attn(pc, *, qw, kw, shared_kv):
    b, l, _ = pc.shape
    kb = qw // kw
    return pl.pallas_call(
        functools.partial(_ctx_attn_kernel, shared_kv=shared_kv),
        grid=(b,),
        in_specs=[pl.BlockSpec((1, l, qw), lambda i: (i, 0, 0)),
                  pl.BlockSpec((1, l, kw), lambda i: (i, 0, kb)),
                  pl.BlockSpec((1, l, kw), lambda i: (i, 0, kb + 1))],
        out_specs=pl.BlockSpec((1, l, qw), lambda i: (i, 0, 0)),
        out_shape=jax.ShapeDtypeStruct((b, l, qw), BF16),
        scratch_shapes=[pltpu.VMEM((2, l, l), F32), pltpu.VMEM((2, l, l), BF16)],
        compiler_params=_params(("parallel",)),
        name="ctx_attn",
    )(pc, pc, pc)


def _na_kernel(*refs, n_rows):
    q_ref, k_subs, v_subs = refs[0], refs[1:1 + NA_SUBS], refs[1 + NA_SUBS:1 + 2 * NA_SUBS]
    kc_ref, vc_ref, tz_ref, o_ref, kwin, vwin, vctx, s_scr, p_scr = refs[1 + 2 * NA_SUBS:]
    r0 = pl.program_id(1) * NA_QROWS
    start = jnp.clip(r0 - NA_WIN_R // 2, 0, n_rows - NA_BAND)
    nq, nk = NA_HALF * GRID_W, NA_CBAND * GRID_W
    sub = NA_BAND * GRID_W // NA_SUBS
    n_pairs = D_NA // LANES

    for t, (kr, vr) in enumerate(zip(k_subs, v_subs)):
        rows = slice(sub * t, sub * (t + 1))
        kwin[rows, :] = kr[0]
        for j in range(n_pairs):
            vwin[j, rows, 0:LANES] = vr[0, :, LANES * j:LANES * (j + 1)]
    vwin[:, :, LANES:] = jnp.ones((n_pairs,) + vwin.shape[1:2] + (LANES,), BF16)
    for j in range(n_pairs):
        vctx[j, :, 0:LANES] = vc_ref[0, :, LANES * j:LANES * (j + 1)]
    vctx[:, :, LANES:] = jnp.ones((n_pairs,) + vctx.shape[1:2] + (LANES,), BF16)

    low = _low_lanes()
    halves = []
    for hf in range(NA_QROWS // NA_HALF):
        rq = r0 + NA_HALF * hf
        first_key = jnp.clip(rq - NA_WIN_R // 2, 0, n_rows - NA_CBAND)
        off = pl.multiple_of((first_key - start) * GRID_W, sub)
        qrow = rq + lax.broadcasted_iota(jnp.int32, (nq, nk), 0) // GRID_W
        krow = first_key + lax.broadcasted_iota(jnp.int32, (nq, nk), 1) // GRID_W
        first = jnp.clip(qrow - NA_WIN_R // 2, 0, n_rows - NA_WIN_R)
        row_mask = jnp.where((krow >= first) & (krow < first + NA_WIN_R), 0.0, NEG)
        halves.append((off, first_key - rq, row_mask))

    n_inst = len(halves) * H_NA

    def score(i, dst):
        hf, h = divmod(i, H_NA)
        j, half = divmod(h, 2)
        off, delta, row_mask = halves[hf]
        sl = slice(LANES * j, LANES * (j + 1))
        qm = _one_head(q_ref[0, nq * hf:nq * (hf + 1), sl], low, half)
        slabs = []
        for a in range(NA_HALF):
            pieces = [tz_ref[h, jnp.clip(delta + 2 * bp - a, -NA_WIN_R, NA_WIN_R - 1) + NA_WIN_R]
                      for bp in range(NA_CBAND // 2)]
            slabs.append(jnp.concatenate(pieces, axis=1))
        dst[:, :nk] = _dot_nt(qm, kwin[pl.ds(off, nk), sl]) + (jnp.concatenate(slabs, axis=0) + row_mask)
        dst[:, nk:] = _dot_nt(qm, kc_ref[0, :, sl])

    score(0, s_scr.at[0])
    outs = []
    for i in range(n_inst):
        if i + 1 < n_inst:
            score(i + 1, s_scr.at[(i + 1) % 2])
        hf, h = divmod(i, H_NA)
        s = s_scr[i % 2]
        p_scr[i % 2] = jnp.exp2(s - jnp.max(s, axis=1, keepdims=True)).astype(BF16)
        r = (_dot(p_scr[i % 2, :, :nk], vwin[h // 2, pl.ds(halves[hf][0], nk), :])
             + _dot(p_scr[i % 2, :, nk:], vctx[h // 2]))
        outs.append(r[:, :LANES] / r[:, LANES:])
    for hf in range(len(halves)):
        for j in range(n_pairs):
            pair = jnp.where(low, outs[hf * H_NA + 2 * j], outs[hf * H_NA + 2 * j + 1])
            o_ref[0, nq * hf:nq * (hf + 1), LANES * j:LANES * (j + 1)] = pair.astype(BF16)


def _na(p_na, pc_na, tz, li):
    b, s, _ = p_na.shape
    l = pc_na.shape[1]
    n_rows = s // GRID_W
    assert n_rows % NA_QROWS == 0 and n_rows >= NA_BAND
    nq, nk = NA_QROWS * GRID_W, NA_BAND * GRID_W
    nh, nkc = NA_HALF * GRID_W, NA_CBAND * GRID_W
    sub = nk // NA_SUBS
    rows_per_sub = NA_BAND // NA_SUBS
    assert all(v % rows_per_sub == 0 for v in (NA_QROWS, NA_HALF, NA_WIN_R // 2, n_rows - NA_BAND,
                                               n_rows - NA_CBAND))

    def band(t, blk):
        def idx(i, r):
            start = jnp.clip(r * NA_QROWS - NA_WIN_R // 2, 0, n_rows - NA_BAND)
            return (i, start // rows_per_sub + t, blk)
        return pl.BlockSpec((1, sub, D_NA), idx)

    in_specs = ([pl.BlockSpec((1, nq, D_NA), lambda i, r: (i, r, 0))]
                + [band(t, 1) for t in range(NA_SUBS)] + [band(t, 2) for t in range(NA_SUBS)]
                + [pl.BlockSpec((1, l, D_NA), lambda i, r: (i, 0, 1)),
                   pl.BlockSpec((1, l, D_NA), lambda i, r: (i, 0, 2)),
                   _resident(tz.shape[1:], (li,))])
    return pl.pallas_call(
        functools.partial(_na_kernel, n_rows=n_rows),
        grid=(b, n_rows // NA_QROWS),
        in_specs=in_specs,
        out_specs=pl.BlockSpec((1, nq, D_NA), lambda i, r: (i, r, 0)),
        out_shape=jax.ShapeDtypeStruct((b, s, D_NA), BF16),
        scratch_shapes=[pltpu.VMEM((nk, D_NA), BF16), pltpu.VMEM((D_NA // LANES, nk, 2 * LANES), BF16),
                        pltpu.VMEM((D_NA // LANES, l, 2 * LANES), BF16),
                        pltpu.VMEM((2, nh, nkc + l), F32), pltpu.VMEM((2, nh, nkc + l), BF16)],
        compiler_params=_params(("parallel", "arbitrary")),
        name="na_latent",
    )(p_na, *([p_na] * (2 * NA_SUBS)), pc_na, pc_na, tz)


def _merge_kernel(x_ref, mod_ref, nw_ref, hf_ref, hb_ref, og_ref, mlw_ref, g_ref, na_ref, gq_ref,
                  wg_ref, wml_ref, wna_ref, wgq_ref, wo_ref, o_ref):
    x = x_ref[...]
    d = x.shape[1]
    hx = _modnorm(x, nw_ref[1:2, :], mod_ref[0, 3:4, :], mod_ref[0, 4:5, :]).astype(BF16)
    h = hf_ref[...].astype(F32) + hb_ref[...].astype(F32)
    o_ml = (_head_norm(h, g_ref[...], mlw_ref[...]) * _sigmoid(og_ref[...].astype(F32))).astype(BF16)
    y = None
    for j, (o_br, w_br) in enumerate(((o_ml, wml_ref), (na_ref[...], wna_ref), (gq_ref[...], wgq_ref))):
        part = _sigmoid(_dot(hx, wg_ref[:, d * j:d * (j + 1)])) * _dot(o_br, w_br[...])
        y = part if y is None else y + part
    o_ref[...] = x + mod_ref[0, 5:6, :] * _dot(y.astype(BF16), wo_ref[...])


def _merge(x, mod, nw, hf, hb, p_ml, mlw, gmat, o_na, o_gq, wg, wml, wna, wgq, wo, li,
           *, tiles_per_batch, ctx_row):
    t, d = x.shape
    tm = min(TM, t)
    row = lambda wd, blk=0: pl.BlockSpec((tm, wd), lambda i: (i, blk))
    lead = (li,)
    return pl.pallas_call(
        _merge_kernel,
        grid=(t // tm,),
        in_specs=[row(d),
                  pl.BlockSpec((1, N_MOD, d), _mod_index(tiles_per_batch, ctx_row)),
                  pl.BlockSpec((3, d), lambda i: (0, 0)),
                  row(D_ML), row(D_ML), row(D_ML, 3),
                  _resident((1, D_ML)), _resident(gmat.shape),
                  row(D_NA), row(D_GQ),
                  _resident((d, 3 * d), lead), _resident((D_ML, d), lead), _resident((D_NA, d), lead),
                  _resident((D_GQ, d), lead), _resident((d, d), lead)],
        out_specs=row(d),
        out_shape=jax.ShapeDtypeStruct((t, d), F32),
        compiler_params=_params(("parallel",)),
        name="branch_merge",
    )(x, mod, nw, hf, hb, p_ml, mlw, gmat, o_na, o_gq, wg, wml, wna, wgq, wo)


def _proj_weight(w):
    d = w.shape[-2]
    o = 0
    seg = {}
    for name, width in (("ml_k", D_ML), ("ml_v", D_ML), ("ml_g", 4 * H_ML), ("na_k", D_NA), ("na_v", D_NA),
                        ("gq_k", D_KV), ("gq_v", D_KV), ("ml_q", D_ML), ("ml_o", D_ML), ("na_q", D_NA),
                        ("gq_q", D_GQ), ("br_g", 3 * d)):
        seg[name] = w[..., o:o + width].astype(BF16)
        o += width
    gq_q = jnp.concatenate([seg["gq_q"][..., HEAD_DIM * h:HEAD_DIM * (h + 1)] for h in GQ_HEAD_ORDER], axis=-1)
    pad = jnp.zeros(w.shape[:-1] + (LANES - 4 * H_ML,), BF16)
    out = jnp.concatenate([seg["ml_q"], seg["ml_k"], seg["ml_v"], seg["ml_o"], seg["ml_g"][..., ML_GATE_ORDER], pad,
                           seg["na_q"], seg["na_k"], seg["na_v"], gq_q, seg["gq_k"], seg["gq_v"]], axis=-1)
    return out, seg["br_g"]


def _rope_tables(n_tok):
    t = np.arange(n_tok)
    row = (t // GRID_W).astype(np.float64)
    col = (t % GRID_W).astype(np.float64)
    n_freq = HEAD_DIM // 4
    inv = ROPE_THETA ** (-np.arange(n_freq, dtype=np.float64) / n_freq)
    ang = np.concatenate([row[:, None] * inv, col[:, None] * inv], axis=-1)
    cos, sin = np.cos(ang), np.sin(ang)
    cos_t = np.tile(cos, (1, LANES // (HEAD_DIM // 2)))
    sin_t = np.tile(np.concatenate([-sin, sin], axis=-1), (1, LANES // HEAD_DIM))
    return jnp.asarray(cos_t, F32), jnp.asarray(sin_t, F32)


def _na_bias_table(rpb):
    col = np.arange(GRID_W)
    first = np.clip(col - NA_WIN_C // 2, 0, GRID_W - NA_WIN_C)
    in_win = (col[None, :] >= first[:, None]) & (col[None, :] < first[:, None] + NA_WIN_C)
    side = GRID_W - NA_WIN_C
    width = 2 * GRID_W
    rows = jnp.pad(rpb, ((0, 0), (0, 0), (1, 1), (side, width - side - rpb.shape[-1])))
    lead = rows.shape[:-1]
    flat = jnp.broadcast_to(rows[..., None, :], lead + (GRID_W, width)).reshape(lead + (GRID_W * width,))
    skew = flat[..., :GRID_W * (width - 1)].reshape(lead + (GRID_W, width - 1))
    full = skew[..., GRID_W - 1:2 * GRID_W - 1] * float(np.log2(np.e))
    row_ok = np.zeros((2 * NA_WIN_R + 1,), bool)
    row_ok[1:-1] = True
    full = jnp.where(jnp.asarray(in_win[None, None, None] & row_ok[None, None, :, None, None]), full, NEG)
    return jnp.concatenate([full[:, :, :-1], full[:, :, 1:]], axis=-1).astype(F32)


def kernel(x, c, ctx, c_ctx, ada_w, ada_b, norm_w, ffn_w_in, ffn_w_out, mix_w_in, ml_gate_b, ml_norm_w,
           na_qk_w, na_rpb, gq_qk_w, w_br_ml, w_br_na, w_br_gq, w_out):
    b, s, d = x.shape
    l = ctx.shape[1]
    depth = ada_w.shape[0]
    assert b < MOD_ROWS and s % TM == 0 and (b * l) % min(TM, b * l) == 0
    ctx_row = b
    tiles_per_batch = s // TM

    cvec = jnp.zeros((MOD_ROWS, d), F32).at[:b].set(c).at[b].set(c_ctx)
    mod = _ada(cvec, ada_w, ada_b).reshape(depth, MOD_ROWS, N_MOD, d)

    lane = np.arange(2 * LANES)
    gmat = jnp.asarray((lane[:, None] // HEAD_DIM) == (lane[None, :] // HEAD_DIM), BF16)
    idx = np.arange(ML_CHUNK)
    tri = jnp.asarray(np.stack([idx[:, None] >= idx[None, :], idx[:, None] <= idx[None, :]]), BF16)
    rope_tabs = _rope_tables(s)

    w_in, w_o = ffn_w_in.astype(BF16), ffn_w_out.astype(BF16)
    w_proj, wg = _proj_weight(mix_w_in)
    wml, wna, wo = w_br_ml.astype(BF16), w_br_na.astype(BF16), w_out.astype(BF16)
    wgq = jnp.concatenate([w_br_gq[:, HEAD_DIM * h:HEAD_DIM * (h + 1)] for h in GQ_HEAD_ORDER], axis=1).astype(BF16)
    tz = _na_bias_table(na_rpb)

    xl = x.reshape(b * s, d)
    xc = ctx.reshape(b * l, d)
    lat = dict(tiles_per_batch=tiles_per_batch, ctx_row=ctx_row)
    con = dict(tiles_per_batch=None, ctx_row=ctx_row)
    for li in range(depth):
        ctx_out = li < depth - 1
        qkw = jnp.zeros((8, 2 * D_NA), F32)
        qkw = qkw.at[0, :D_NA].set(jnp.tile(na_qk_w[li, 0], H_NA) * Q_PRESCALE)
        qkw = qkw.at[0, D_NA:].set(jnp.tile(na_qk_w[li, 1], H_NA))
        qkw = qkw.at[1, :D_GQ].set(jnp.tile(gq_qk_w[li, 0], H_GQ) * Q_PRESCALE)
        qkw = qkw.at[1, D_GQ:D_GQ + D_KV].set(jnp.tile(gq_qk_w[li, 1], H_KV))
        qkw = qkw.at[2, :4 * H_ML].set(ml_gate_b[li][ML_GATE_ORDER])
        mlw = ml_norm_w[li].reshape(1, D_ML)
        m, nw = mod[li], norm_w[li]

        xl = _ffn(xl, m, nw, w_in, w_o, (li, 0), k0=0, nrm=0, **lat)
        xc = _ffn(xc, m, nw, w_in, w_o, (li, 0), k0=0, nrm=0, **con)

        p_ml, p_mlg, p_na, p_gq = _proj(xl, m, nw, w_proj, li, gmat, qkw, rope_tabs, **lat)
        pc_ml, pc_mlg, pc_na, pc_gq = _proj(xc, m, nw, w_proj, li, gmat, qkw, None, **con)
        seq = lambda a: a.reshape(b, s, a.shape[-1])
        cseq = lambda a: a.reshape(b, l, a.shape[-1])

        hf, hb, hcf, hcb = _mlstm(seq(p_ml), seq(p_mlg), cseq(pc_ml), cseq(pc_mlg), tri)
        o_na = _na(seq(p_na), cseq(pc_na), tz, li)
        o_gq = _gqa(seq(p_gq), cseq(pc_gq))
        flat = lambda a: a.reshape(-1, a.shape[-1])
        xl = _merge(xl, m, nw, flat(hf), flat(hb), p_ml, mlw, gmat, flat(o_na), flat(o_gq),
                    wg, wml, wna, wgq, wo, li, **lat)
        xl = _ffn(xl, m, nw, w_in, w_o, (li, 1), k0=6, nrm=2, **lat)
        if ctx_out:
            co_na = _ctx_attn(cseq(pc_na), qw=D_NA, kw=D_NA, shared_kv=False)
            co_gq = _ctx_attn(cseq(pc_gq), qw=D_GQ, kw=D_KV, shared_kv=True)
            xc = _merge(xc, m, nw, flat(hcf), flat(hcb), pc_ml, mlw, gmat, flat(co_na), flat(co_gq),
                        wg, wml, wna, wgq, wo, li, **con)
            xc = _ffn(xc, m, nw, w_in, w_o, (li, 1), k0=6, nrm=2, **con)
    return xl.reshape(b, s, d)
```

```python
import functools

import numpy as np
import jax
import jax.numpy as jnp
from jax import lax
from jax.experimental import pallas as pl
from jax.experimental.pallas import tpu as pltpu

F32 = jnp.float32
BF16 = jnp.bfloat16

HEAD_DIM = 64
LANES = 128
H_ML, H_NA, H_GQ, H_KV = 4, 6, 6, 2
D_ML, D_NA, D_GQ, D_KV = 256, 384, 384, 128
GRID_W = 64
NA_WIN_R, NA_WIN_C = 8, 16
ROPE_THETA = 10000.0
EPS = 1e-6
N_MOD = 9
ATTN_SCALE = HEAD_DIM ** -0.5
NEG = -1e30

ML_CHUNK = 256
ML_ROWS = 4
NA_QROWS = 32
NA_HALF = 4
NA_BAND = 40
NA_CBAND = 12
NA_SUBS = 10
TM = 1024
GQ_TQ = 512
FFN_CHUNK = 768
Q_PRESCALE = ATTN_SCALE * float(np.log2(np.e))
MOD_ROWS = 16
VMEM_LIMIT = 56 * 1024 * 1024

C_ML, C_MLG, C_NA, C_GQ, C_END = 0, 1024, 1152, 2304, 2944
N_PROJ = C_END
GQ_HEAD_ORDER = (0, 3, 1, 4, 2, 5)
ML_GATE_ORDER = np.array([0, 1, 2, 3, 8, 9, 10, 11, 4, 5, 6, 7, 12, 13, 14, 15])


def _dot(a, b):
    return jnp.dot(a, b, preferred_element_type=F32)


def _dot_nt(a, b):
    return lax.dot_general(a, b, (((1,), (1,)), ((), ())), preferred_element_type=F32)


def _sigmoid(x):
    return 1.0 / (1.0 + jnp.exp(-x))


def _log_sigmoid(x):
    return jnp.minimum(x, 0.0) - jnp.log1p(jnp.exp(-jnp.abs(x)))


def _split_bf16(x):
    parts = []
    for _ in range(3):
        p = x.astype(BF16)
        parts.append(p)
        x = x - p.astype(F32)
    return parts


def _dot_f32_left(x, m01):
    return functools.reduce(jnp.add, [_dot(p, m01) for p in _split_bf16(x)])


def _modnorm(x, nw, shift, scale):
    ms = jnp.mean(x * x, axis=-1, keepdims=True)
    return (x * lax.rsqrt(ms + EPS) * nw) * (1.0 + scale) + shift


def _head_norm(t, gmat, wrow):
    ss = _dot((t * t).astype(BF16), gmat)
    return t * lax.rsqrt(ss * (1.0 / HEAD_DIM) + EPS) * wrow


def _resident(shape, lead=()):
    return pl.BlockSpec((None,) * len(lead) + tuple(shape), lambda *_: tuple(lead) + (0,) * len(shape),
                        pipeline_mode=pl.Buffered(1))


def _params(sem):
    return pltpu.CompilerParams(dimension_semantics=sem, vmem_limit_bytes=VMEM_LIMIT)


def _ada_kernel(c_ref, w_ref, b_ref, o_ref):
    c = c_ref[...]
    s = (c * _sigmoid(c)).astype(BF16)
    o_ref[0] = _dot(s, w_ref[0].astype(BF16)) + b_ref[0]


def _ada(cvec, ada_w, ada_b):
    depth, d, n = ada_w.shape
    tn = n // 8
    return pl.pallas_call(
        _ada_kernel,
        grid=(depth, n // tn),
        in_specs=[pl.BlockSpec((MOD_ROWS, d), lambda l, j: (0, 0)),
                  pl.BlockSpec((1, d, tn), lambda l, j: (l, 0, j)),
                  pl.BlockSpec((1, 1, tn), lambda l, j: (l, 0, j))],
        out_specs=pl.BlockSpec((1, MOD_ROWS, tn), lambda l, j: (l, 0, j)),
        out_shape=jax.ShapeDtypeStruct((depth, MOD_ROWS, n), F32),
        compiler_params=_params(("arbitrary", "arbitrary")),
        name="ada_mod",
    )(cvec, ada_w, ada_b.reshape(depth, 1, n))


def _mod_index(tiles_per_batch, ctx_row):
    if tiles_per_batch is None:
        return lambda i, *_: (ctx_row, 0, 0)
    return lambda i, *_: (i // tiles_per_batch, 0, 0)


def _ffn_kernel(x_ref, mod_ref, nw_ref, wi_ref, wo_ref, o_ref, *, k0, nrm, chunks):
    x = x_ref[...]
    h = _modnorm(x, nw_ref[nrm:nrm + 1, :], mod_ref[0, k0:k0 + 1, :], mod_ref[0, k0 + 1:k0 + 2, :]).astype(BF16)
    dff = wo_ref.shape[0]
    y = None
    for c0, c1 in chunks:
        g = _dot(h, wi_ref[:, c0:c1])
        u = _dot(h, wi_ref[:, dff + c0:dff + c1])
        part = _dot((g * _sigmoid(g) * u).astype(BF16), wo_ref[c0:c1, :])
        y = part if y is None else y + part
    o_ref[...] = x + (0.5 * mod_ref[0, k0 + 2:k0 + 3, :]) * y


def _ffn(x, mod, nw, w_in, w_out, lead, *, k0, nrm, tiles_per_batch, ctx_row):
    t, d = x.shape
    dff = w_out.shape[-2]
    tm = min(TM, t)
    edges = list(range(0, dff, FFN_CHUNK)) + [dff]
    chunks = tuple(zip(edges[:-1], edges[1:]))
    kern = functools.partial(_ffn_kernel, k0=k0, nrm=nrm, chunks=chunks)
    return pl.pallas_call(
        kern,
        grid=(t // tm,),
        in_specs=[pl.BlockSpec((tm, d), lambda i: (i, 0)),
                  pl.BlockSpec((1, N_MOD, d), _mod_index(tiles_per_batch, ctx_row)),
                  pl.BlockSpec((3, d), lambda i: (0, 0)),
                  _resident((d, 2 * dff), lead),
                  _resident((dff, d), lead)],
        out_specs=pl.BlockSpec((tm, d), lambda i: (i, 0)),
        out_shape=jax.ShapeDtypeStruct((t, d), F32),
        compiler_params=_params(("parallel",)),
        name="ffn_swiglu",
    )(x, mod, nw, w_in, w_out)


def _proj_kernel(*refs, rope):
    if rope:
        (x_ref, mod_ref, nw_ref, w_ref, g_ref, qkw_ref, cos_ref, sin_ref,
         ml_ref, mlg_ref, na_ref, gq_ref) = refs
    else:
        (x_ref, mod_ref, nw_ref, w_ref, g_ref, qkw_ref,
         ml_ref, mlg_ref, na_ref, gq_ref) = refs
    h = _modnorm(x_ref[...], nw_ref[1:2, :], mod_ref[0, 3:4, :], mod_ref[0, 4:5, :]).astype(BF16)
    gmat = g_ref[...]

    ml_ref[...] = _dot(h, w_ref[:, C_ML:C_MLG]).astype(BF16)
    gates = _dot(h, w_ref[:, C_MLG:C_NA]) + qkw_ref[2:3, 0:LANES]
    glane = lax.broadcasted_iota(jnp.int32, (1, LANES), 1)
    mlg_ref[...] = jnp.where((glane >= 2 * H_ML) & (glane < 4 * H_ML), _log_sigmoid(gates), gates)

    na = _dot(h, w_ref[:, C_NA:C_GQ])
    for j in range(3):
        sl = slice(2 * LANES * j, 2 * LANES * (j + 1))
        na_ref[:, sl] = _head_norm(na[:, sl], gmat, qkw_ref[0:1, sl]).astype(BF16)
    na_ref[:, 2 * D_NA:] = na[:, 2 * D_NA:].astype(BF16)

    gq = _dot(h, w_ref[:, C_GQ:C_END])
    if rope:
        lane = lax.broadcasted_iota(jnp.int32, (1, LANES), 1)
        first_half = (lane % HEAD_DIM) < (HEAD_DIM // 2)
        cos = cos_ref[...]
        sin = sin_ref[...]
    for j in range(2):
        t2 = _head_norm(gq[:, 2 * LANES * j:2 * LANES * (j + 1)], gmat, qkw_ref[1:2, 2 * LANES * j:2 * LANES * (j + 1)])
        for half in range(2):
            t = t2[:, LANES * half:LANES * (half + 1)]
            if rope:
                rot = jnp.where(first_half, pltpu.roll(t, LANES - HEAD_DIM // 2, axis=1),
                                pltpu.roll(t, HEAD_DIM // 2, axis=1))
                t = t * cos + rot * sin
            gq_ref[:, LANES * (2 * j + half):LANES * (2 * j + half + 1)] = t.astype(BF16)
    gq_ref[:, D_GQ + D_KV:] = gq[:, D_GQ + D_KV:].astype(BF16)


def _proj(x, mod, nw, w, li, gmat, qkw, rope_tabs, *, tiles_per_batch, ctx_row):
    t, d = x.shape
    tm = min(TM, t)
    rope = rope_tabs is not None
    in_specs = [pl.BlockSpec((tm, d), lambda i: (i, 0)),
                pl.BlockSpec((1, N_MOD, d), _mod_index(tiles_per_batch, ctx_row)),
                pl.BlockSpec((3, d), lambda i: (0, 0)),
                _resident((d, N_PROJ), (li,)),
                _resident(gmat.shape),
                _resident(qkw.shape)]
    args = [x, mod, nw, w, gmat, qkw]
    if rope:
        in_specs += [pl.BlockSpec((tm, LANES), lambda i: (i % tiles_per_batch, 0))] * 2
        args += list(rope_tabs)
    widths = (1024, LANES, 3 * D_NA, D_GQ + 2 * D_KV)
    dtypes = (BF16, F32, BF16, BF16)
    return pl.pallas_call(
        functools.partial(_proj_kernel, rope=rope),
        grid=(t // tm,),
        in_specs=in_specs,
        out_specs=[pl.BlockSpec((tm, wd), lambda i: (i, 0)) for wd in widths],
        out_shape=[jax.ShapeDtypeStruct((t, wd), dt) for wd, dt in zip(widths, dtypes)],
        compiler_params=_params(("parallel",)),
        name="mix_in_proj",
    )(*args)


def _ml_prep(d, g_ref, tri_ref):
    log2e = float(np.log2(np.e))
    lc = g_ref.shape[0]
    ng = 2 * H_ML
    gates = g_ref[...] * log2e
    gates_t = gates.T
    ig_t = gates_t[0:ng]
    lf_t = gates_t[0:2 * ng]
    b_t = _dot_f32_left(lf_t, tri_ref[1 - d])[ng:2 * ng]
    btot_t = _dot_f32_left(lf_t, jnp.ones((lc, lc), BF16))[ng:2 * ng]
    lf_al = pltpu.roll(gates, LANES - ng, axis=1)
    c_mat = gates - functools.reduce(jnp.add, [_dot(tri_ref[d], p) for p in _split_bf16(lf_al)])
    return ig_t, b_t, btot_t, c_mat


def _ml_head(d, h, prep, m_ref):
    ig_t, b_t, btot_t, c_mat = prep
    lc = c_mat.shape[0]
    r = H_ML * d + h
    row = lax.broadcasted_iota(jnp.int32, (lc, lc), 0)
    col = lax.broadcasted_iota(jnp.int32, (lc, lc), 1)
    visible = (row <= col) if d == 0 else (row >= col)
    ig, b, b_tot = ig_t[r:r + 1], b_t[r:r + 1], btot_t[r:r + 1]
    m_prev = m_ref[r:r + 1, :]
    w_end = b_tot - b + ig
    m_new = jnp.maximum(b_tot + m_prev, jnp.max(w_end, axis=1, keepdims=True))
    a = jnp.exp2(w_end - m_new)
    decay = jnp.exp2(b_tot + m_prev - m_new)
    m_inter = b + m_prev
    logw = jnp.where(visible, c_mat[:, r:r + 1] + b, NEG)
    m_j = jnp.maximum(m_inter, jnp.max(logw, axis=0, keepdims=True))
    w = jnp.exp2(logw - m_j)
    m_ref[r:r + 1, :] = m_new
    return w, a, jnp.exp2(m_inter - m_j), jnp.exp2(-m_j), decay


def _ml_pair(d, p, head_a, head_b, q_ref, k_ref, v_ref, o_ref, st_ref):
    (w_a, a_a, g_a, fl_a, dec_a), (w_b, a_b, g_b, fl_b, dec_b) = head_a, head_b
    lc = q_ref.shape[0]
    low = _low_lanes()
    sl = slice(LANES * p, LANES * (p + 1))
    q = q_ref[:, sl]
    k = k_ref[:, sl] * ATTN_SCALE
    vt = jnp.concatenate([v_ref[:, sl].astype(F32).T, jnp.ones((LANES, lc), F32)], axis=0)
    head_row = (lax.broadcasted_iota(jnp.int32, (2 * LANES, 1), 0) % LANES) < HEAD_DIM
    vt16 = vt.astype(BF16)
    r_a = _dot(vt16, (_dot_nt(k, _one_head(q, low, 0)) * w_a).astype(BF16))
    r_b = _dot(vt16, (_dot_nt(k, _one_head(q, low, 1)) * w_b).astype(BF16))
    state = st_ref[d, p]
    r_i = _dot_nt(state.astype(BF16), q)
    r = jnp.where(head_row, r_a, r_b) + jnp.where(head_row, g_a, g_b) * r_i
    num, den = r[:LANES], r[LANES:]
    h_t = num / jnp.maximum(jnp.abs(den), jnp.where(head_row[:LANES], fl_a, fl_b))
    o_ref[:, sl] = h_t.T.astype(BF16)

    upd = _dot((vt * jnp.where(head_row, a_a, a_b)).astype(BF16), k)
    same_head = head_row == low
    dec = jnp.where(head_row, dec_a[:, :LANES], dec_b[:, :LANES])
    st_ref[d, p] = dec * state + jnp.where(same_head, upd, 0.0)


def _ml_step(fwd, bwd, tri_ref, st_ref, m_ref):
    dirs = (fwd, bwd)
    preps = [_ml_prep(d, refs[3], tri_ref) for d, refs in enumerate(dirs)]
    heads = [[_ml_head(d, h, preps[d], m_ref) for h in range(H_ML)] for d in range(2)]
    for p in range(H_ML // 2):
        for d, (q_ref, k_ref, v_ref, _, o_ref) in enumerate(dirs):
            _ml_pair(d, p, heads[d][2 * p], heads[d][2 * p + 1], q_ref, k_ref, v_ref, o_ref, st_ref)


def _mlstm_kernel(qf, kf, vf, gf, qb, kb, vb, gb, qc, kc, vc, gc, tri_ref,
                  hf_ref, hb_ref, hcf_ref, hcb_ref, st_ref, m_ref):
    c = pl.program_id(1)

    def rows(bi, *refs):
        return tuple(r.at[bi] for r in refs)

    @pl.when(c == 0)
    def _():
        st_ref[...] = jnp.zeros_like(st_ref)
        m_ref[...] = jnp.zeros_like(m_ref)
        for bi in range(ML_ROWS):
            _ml_step(rows(bi, qc, kc, vc, gc, hcf_ref), rows(bi, qc, kc, vc, gc, hcb_ref), tri_ref,
                     st_ref.at[bi], m_ref.at[bi])

    @pl.when(c > 0)
    def _():
        for bi in range(ML_ROWS):
            _ml_step(rows(bi, qf, kf, vf, gf, hf_ref), rows(bi, qb, kb, vb, gb, hb_ref), tri_ref,
                     st_ref.at[bi], m_ref.at[bi])


def _mlstm(p_ml, p_mlg, pc_ml, pc_mlg, tri):
    b, s, _ = p_ml.shape
    l = pc_ml.shape[1]
    lc = ML_CHUNK
    assert l == lc and s % lc == 0 and b % ML_ROWS == 0
    nl = s // lc
    fwd = lambda c: jnp.maximum(c - 1, 0)
    bwd = lambda c: nl - 1 - jnp.maximum(c - 1, 0)

    def lat(idx, blk, width):
        return pl.BlockSpec((ML_ROWS, lc, width), lambda i, c: (i, idx(c), blk))

    def ctx(blk, width):
        return pl.BlockSpec((ML_ROWS, lc, width), lambda i, c: (i, 0, blk))

    in_specs = ([lat(fwd, 0, D_ML), lat(fwd, 1, D_ML), lat(fwd, 2, D_ML), lat(fwd, 0, LANES)]
                + [lat(bwd, 0, D_ML), lat(bwd, 1, D_ML), lat(bwd, 2, D_ML), lat(bwd, 0, LANES)]
                + [ctx(0, D_ML), ctx(1, D_ML), ctx(2, D_ML), ctx(0, LANES)]
                + [_resident((2, lc, lc))])
    out_specs = [lat(fwd, 0, D_ML), lat(bwd, 0, D_ML), ctx(0, D_ML), ctx(0, D_ML)]
    out_shape = [jax.ShapeDtypeStruct((b, s, D_ML), BF16)] * 2 + [jax.ShapeDtypeStruct((b, l, D_ML), BF16)] * 2
    return pl.pallas_call(
        _mlstm_kernel,
        grid=(b // ML_ROWS, nl + 1),
        in_specs=in_specs,
        out_specs=out_specs,
        out_shape=out_shape,
        scratch_shapes=[pltpu.VMEM((ML_ROWS, 2, H_ML // 2, 2 * LANES, LANES), F32),
                        pltpu.VMEM((ML_ROWS, 2 * H_ML, lc), F32)],
        compiler_params=_params(("parallel", "arbitrary")),
        name="mlstm_bidir",
    )(p_ml, p_ml, p_ml, p_mlg, p_ml, p_ml, p_ml, p_mlg, pc_ml, pc_ml, pc_ml, pc_mlg, tri)


def _attend_heads(n_heads, score_fn, value_fn, s_scr, p_scr):
    s_scr[0] = score_fn(0)
    outs = []
    for i in range(n_heads):
        if i + 1 < n_heads:
            s_scr[(i + 1) % 2] = score_fn(i + 1)
        s = s_scr[i % 2]
        p_scr[i % 2] = jnp.exp2(s - jnp.max(s, axis=1, keepdims=True)).astype(BF16)
        r = _dot(p_scr[i % 2], value_fn(i))
        outs.append(r[:, :LANES] / r[:, LANES:])
    return outs


def _low_lanes():
    return lax.broadcasted_iota(jnp.int32, (1, LANES), 1) < HEAD_DIM


def _one_head(q, low, half):
    zero = jnp.zeros_like(q)
    return jnp.where(low, q, zero) if half == 0 else jnp.where(low, zero, q)


def _pair_outputs(o_ref, outs, low):
    for j in range(len(outs) // 2):
        o_ref[0, :, LANES * j:LANES * (j + 1)] = jnp.where(low, outs[2 * j], outs[2 * j + 1]).astype(BF16)


def _gqa_kernel(q_ref, k_ref, v_ref, kc_ref, vc_ref, o_ref, kall, vall, s_scr, p_scr):
    n_lat = k_ref.shape[1]

    @pl.when(pl.program_id(1) == 0)
    def _():
        kall[0:n_lat, :] = k_ref[0]
        kall[n_lat:, :] = kc_ref[0]
        vall[0:n_lat, 0:LANES] = v_ref[0]
        vall[n_lat:, 0:LANES] = vc_ref[0]
        vall[:, LANES:] = jnp.ones((vall.shape[0], LANES), BF16)

    low = _low_lanes()

    def score(i):
        j, half = divmod(i, 2)
        return _dot_nt(_one_head(q_ref[0, :, LANES * j:LANES * (j + 1)], low, half), kall[...])

    _pair_outputs(o_ref, _attend_heads(H_GQ, score, lambda i: vall[...], s_scr, p_scr), low)


def _gqa(p_gq, pc_gq):
    b, s, _ = p_gq.shape
    l = pc_gq.shape[1]
    tq = min(GQ_TQ, s)
    kblk, vblk = D_GQ // LANES, D_GQ // LANES + 1
    return pl.pallas_call(
        _gqa_kernel,
        grid=(b, s // tq),
        in_specs=[pl.BlockSpec((1, tq, D_GQ), lambda i, t: (i, t, 0)),
                  pl.BlockSpec((1, s, LANES), lambda i, t: (i, 0, kblk)),
                  pl.BlockSpec((1, s, LANES), lambda i, t: (i, 0, vblk)),
                  pl.BlockSpec((1, l, LANES), lambda i, t: (i, 0, kblk)),
                  pl.BlockSpec((1, l, LANES), lambda i, t: (i, 0, vblk))],
        out_specs=pl.BlockSpec((1, tq, D_GQ), lambda i, t: (i, t, 0)),
        out_shape=jax.ShapeDtypeStruct((b, s, D_GQ), BF16),
        scratch_shapes=[pltpu.VMEM((s + l, LANES), BF16), pltpu.VMEM((s + l, 2 * LANES), BF16),
                        pltpu.VMEM((2, tq, s + l), F32), pltpu.VMEM((2, tq, s + l), BF16)],
        compiler_params=_params(("parallel", "arbitrary")),
        name="gqa_latent",
    )(p_gq, p_gq, p_gq, pc_gq, pc_gq)


def _ctx_attn_kernel(q_ref, k_ref, v_ref, o_ref, s_scr, p_scr, *, shared_kv):
    low = _low_lanes()
    ones = jnp.ones((v_ref.shape[1], LANES), BF16)

    def kv_lanes(i):
        return slice(0, LANES) if shared_kv else slice(LANES * (i // 2), LANES * (i // 2 + 1))

    def score(i):
        j, half = divmod(i, 2)
        return _dot_nt(_one_head(q_ref[0, :, LANES * j:LANES * (j + 1)], low, half), k_ref[0, :, kv_lanes(i)])

    def value(i):
        return jnp.concatenate([v_ref[0, :, kv_lanes(i)], ones], axis=1)

    n_heads = 2 * (q_ref.shape[2] // LANES)
    _pair_outputs(o_ref, _attend_heads(n_heads, score, value, s_scr, p_scr), low)


def _ctx_attn(pc, *, qw, kw, shared_kv):
    b, l, _ = pc.shape
    kb = qw // kw
    return pl.pallas_call(
        functools.partial(_ctx_attn_kernel, shared_kv=shared_kv),
        grid=(b,),
        in_specs=[pl.BlockSpec((1, l, qw), lambda i: (i, 0, 0)),
                  pl.BlockSpec((1, l, kw), lambda i: (i, 0, kb)),
                  pl.BlockSpec((1, l, kw), lambda i: (i, 0, kb + 1))],
        out_specs=pl.BlockSpec((1, l, qw), lambda i: (i, 0, 0)),
        out_shape=jax.ShapeDtypeStruct((b, l, qw), BF16),
        scratch_shapes=[pltpu.VMEM((2, l, l), F32), pltpu.VMEM((2, l, l), BF16)],
        compiler_params=_params(("parallel",)),
        name="ctx_attn",
    )(pc, pc, pc)


def _na_kernel(*refs, n_rows):
    q_ref, k_subs, v_subs = refs[0], refs[1:1 + NA_SUBS], refs[1 + NA_SUBS:1 + 2 * NA_SUBS]
    kc_ref, vc_ref, tz_ref, o_ref, kwin, vwin, vctx, s_scr, p_scr = refs[1 + 2 * NA_SUBS:]
    r0 = pl.program_id(1) * NA_QROWS
    start = jnp.clip(r0 - NA_WIN_R // 2, 0, n_rows - NA_BAND)
    nq, nk = NA_HALF * GRID_W, NA_CBAND * GRID_W
    sub = NA_BAND * GRID_W // NA_SUBS
    n_pairs = D_NA // LANES

    for t, (kr, vr) in enumerate(zip(k_subs, v_subs)):
        rows = slice(sub * t, sub * (t + 1))
        kwin[rows, :] = kr[0]
        for j in range(n_pairs):
            vwin[j, rows, 0:LANES] = vr[0, :, LANES * j:LANES * (j + 1)]
    vwin[:, :, LANES:] = jnp.ones((n_pairs,) + vwin.shape[1:2] + (LANES,), BF16)
    for j in range(n_pairs):
        vctx[j, :, 0:LANES] = vc_ref[0, :, LANES * j:LANES * (j + 1)]
    vctx[:, :, LANES:] = jnp.ones((n_pairs,) + vctx.shape[1:2] + (LANES,), BF16)

    low = _low_lanes()
    halves = []
    for hf in range(NA_QROWS // NA_HALF):
        rq = r0 + NA_HALF * hf
        first_key = jnp.clip(rq - NA_WIN_R // 2, 0, n_rows - NA_CBAND)
        off = pl.multiple_of((first_key - start) * GRID_W, sub)
        qrow = rq + lax.broadcasted_iota(jnp.int32, (nq, nk), 0) // GRID_W
        krow = first_key + lax.broadcasted_iota(jnp.int32, (nq, nk), 1) // GRID_W
        first = jnp.clip(qrow - NA_WIN_R // 2, 0, n_rows - NA_WIN_R)
        row_mask = jnp.where((krow >= first) & (krow < first + NA_WIN_R), 0.0, NEG)
        halves.append((off, first_key - rq, row_mask))

    n_inst = len(halves) * H_NA

    def score(i, dst):
        hf, h = divmod(i, H_NA)
        j, half = divmod(h, 2)
        off, delta, row_mask = halves[hf]
        sl = slice(LANES * j, LANES * (j + 1))
        qm = _one_head(q_ref[0, nq * hf:nq * (hf + 1), sl], low, half)
        slabs = []
        for a in range(NA_HALF):
            pieces = [tz_ref[h, jnp.clip(delta + 2 * bp - a, -NA_WIN_R, NA_WIN_R - 1) + NA_WIN_R]
                      for bp in range(NA_CBAND // 2)]
            slabs.append(jnp.concatenate(pieces, axis=1))
        dst[:, :nk] = _dot_nt(qm, kwin[pl.ds(off, nk), sl]) + (jnp.concatenate(slabs, axis=0) + row_mask)
        dst[:, nk:] = _dot_nt(qm, kc_ref[0, :, sl])

    score(0, s_scr.at[0])
    outs = []
    for i in range(n_inst):
        if i + 1 < n_inst:
            score(i + 1, s_scr.at[(i + 1) % 2])
        hf, h = divmod(i, H_NA)
        s = s_scr[i % 2]
        p_scr[i % 2] = jnp.exp2(s - jnp.max(s, axis=1, keepdims=True)).astype(BF16)
        r = (_dot(p_scr[i % 2, :, :nk], vwin[h // 2, pl.ds(halves[hf][0], nk), :])
             + _dot(p_scr[i % 2, :, nk:], vctx[h // 2]))
        outs.append(r[:, :LANES] / r[:, LANES:])
    for hf in range(len(halves)):
        for j in range(n_pairs):
            pair = jnp.where(low, outs[hf * H_NA + 2 * j], outs[hf * H_NA + 2 * j + 1])
            o_ref[0, nq * hf:nq * (hf + 1), LANES * j:LANES * (j + 1)] = pair.astype(BF16)


def _na(p_na, pc_na, tz, li):
    b, s, _ = p_na.shape
    l = pc_na.shape[1]
    n_rows = s // GRID_W
    assert n_rows % NA_QROWS == 0 and n_rows >= NA_BAND
    nq, nk = NA_QROWS * GRID_W, NA_BAND * GRID_W
    nh, nkc = NA_HALF * GRID_W, NA_CBAND * GRID_W
    sub = nk // NA_SUBS
    rows_per_sub = NA_BAND // NA_SUBS
    assert all(v % rows_per_sub == 0 for v in (NA_QROWS, NA_HALF, NA_WIN_R // 2, n_rows - NA_BAND,
                                               n_rows - NA_CBAND))

    def band(t, blk):
        def idx(i, r):
            start = jnp.clip(r * NA_QROWS - NA_WIN_R // 2, 0, n_rows - NA_BAND)
            return (i, start // rows_per_sub + t, blk)
        return pl.BlockSpec((1, sub, D_NA), idx)

    in_specs = ([pl.BlockSpec((1, nq, D_NA), lambda i, r: (i, r, 0))]
                + [band(t, 1) for t in range(NA_SUBS)] + [band(t, 2) for t in range(NA_SUBS)]
                + [pl.BlockSpec((1, l, D_NA), lambda i, r: (i, 0, 1)),
                   pl.BlockSpec((1, l, D_NA), lambda i, r: (i, 0, 2)),
                   _resident(tz.shape[1:], (li,))])
    return pl.pallas_call(
        functools.partial(_na_kernel, n_rows=n_rows),
        grid=(b, n_rows // NA_QROWS),
        in_specs=in_specs,
        out_specs=pl.BlockSpec((1, nq, D_NA), lambda i, r: (i, r, 0)),
        out_shape=jax.ShapeDtypeStruct((b, s, D_NA), BF16),
        scratch_shapes=[pltpu.VMEM((nk, D_NA), BF16), pltpu.VMEM((D_NA // LANES, nk, 2 * LANES), BF16),
                        pltpu.VMEM((D_NA // LANES, l, 2 * LANES), BF16),
                        pltpu.VMEM((2, nh, nkc + l), F32), pltpu.VMEM((2, nh, nkc + l), BF16)],
        compiler_params=_params(("parallel", "arbitrary")),
        name="na_latent",
    )(p_na, *([p_na] * (2 * NA_SUBS)), pc_na, pc_na, tz)


def _merge_kernel(x_ref, mod_ref, nw_ref, hf_ref, hb_ref, og_ref, mlw_ref, g_ref, na_ref, gq_ref,
                  wg_ref, wml_ref, wna_ref, wgq_ref, wo_ref, o_ref):
    x = x_ref[...]
    d = x.shape[1]
    hx = _modnorm(x, nw_ref[1:2, :], mod_ref[0, 3:4, :], mod_ref[0, 4:5, :]).astype(BF16)
    h = hf_ref[...].astype(F32) + hb_ref[...].astype(F32)
    o_ml = (_head_norm(h, g_ref[...], mlw_ref[...]) * _sigmoid(og_ref[...].astype(F32))).astype(BF16)
    y = None
    for j, (o_br, w_br) in enumerate(((o_ml, wml_ref), (na_ref[...], wna_ref), (gq_ref[...], wgq_ref))):
        part = _sigmoid(_dot(hx, wg_ref[:, d * j:d * (j + 1)])) * _dot(o_br, w_br[...])
        y = part if y is None else y + part
    o_ref[...] = x + mod_ref[0, 5:6, :] * _dot(y.astype(BF16), wo_ref[...])


def _merge(x, mod, nw, hf, hb, p_ml, mlw, gmat, o_na, o_gq, wg, wml, wna, wgq, wo, li,
           *, tiles_per_batch, ctx_row):
    t, d = x.shape
    tm = min(TM, t)
    row = lambda wd, blk=0: pl.BlockSpec((tm, wd), lambda i: (i, blk))
    lead = (li,)
    return pl.pallas_call(
        _merge_kernel,
        grid=(t // tm,),
        in_specs=[row(d),
                  pl.BlockSpec((1, N_MOD, d), _mod_index(tiles_per_batch, ctx_row)),
                  pl.BlockSpec((3, d), lambda i: (0, 0)),
                  row(D_ML), row(D_ML), row(D_ML, 3),
                  _resident((1, D_ML)), _resident(gmat.shape),
                  row(D_NA), row(D_GQ),
                  _resident((d, 3 * d), lead), _resident((D_ML, d), lead), _resident((D_NA, d), lead),
                  _resident((D_GQ, d), lead), _resident((d, d), lead)],
        out_specs=row(d),
        out_shape=jax.ShapeDtypeStruct((t, d), F32),
        compiler_params=_params(("parallel",)),
        name="branch_merge",
    )(x, mod, nw, hf, hb, p_ml, mlw, gmat, o_na, o_gq, wg, wml, wna, wgq, wo)


def _proj_weight(w):
    d = w.shape[-2]
    o = 0
    seg = {}
    for name, width in (("ml_k", D_ML), ("ml_v", D_ML), ("ml_g", 4 * H_ML), ("na_k", D_NA), ("na_v", D_NA),
                        ("gq_k", D_KV), ("gq_v", D_KV), ("ml_q", D_ML), ("ml_o", D_ML), ("na_q", D_NA),
                        ("gq_q", D_GQ), ("br_g", 3 * d)):
        seg[name] = w[..., o:o + width].astype(BF16)
        o += width
    gq_q = jnp.concatenate([seg["gq_q"][..., HEAD_DIM * h:HEAD_DIM * (h + 1)] for h in GQ_HEAD_ORDER], axis=-1)
    pad = jnp.zeros(w.shape[:-1] + (LANES - 4 * H_ML,), BF16)
    out = jnp.concatenate([seg["ml_q"], seg["ml_k"], seg["ml_v"], seg["ml_o"], seg["ml_g"][..., ML_GATE_ORDER], pad,
                           seg["na_q"], seg["na_k"], seg["na_v"], gq_q, seg["gq_k"], seg["gq_v"]], axis=-1)
    return out, seg["br_g"]


def _rope_tables(n_tok):
    t = np.arange(n_tok)
    row = (t // GRID_W).astype(np.float64)
    col = (t % GRID_W).astype(np.float64)
    n_freq = HEAD_DIM // 4
    inv = ROPE_THETA ** (-np.arange(n_freq, dtype=np.float64) / n_freq)
    ang = np.concatenate([row[:, None] * inv, col[:, None] * inv], axis=-1)
    cos, sin = np.cos(ang), np.sin(ang)
    cos_t = np.tile(cos, (1, LANES // (HEAD_DIM // 2)))
    sin_t = np.tile(np.concatenate([-sin, sin], axis=-1), (1, LANES // HEAD_DIM))
    return jnp.asarray(cos_t, F32), jnp.asarray(sin_t, F32)


def _na_bias_table(rpb):
    col = np.arange(GRID_W)
    first = np.clip(col - NA_WIN_C // 2, 0, GRID_W - NA_WIN_C)
    in_win = (col[None, :] >= first[:, None]) & (col[None, :] < first[:, None] + NA_WIN_C)
    side = GRID_W - NA_WIN_C
    width = 2 * GRID_W
    rows = jnp.pad(rpb, ((0, 0), (0, 0), (1, 1), (side, width - side - rpb.shape[-1])))
    lead = rows.shape[:-1]
    flat = jnp.broadcast_to(rows[..., None, :], lead + (GRID_W, width)).reshape(lead + (GRID_W * width,))
    skew = flat[..., :GRID_W * (width - 1)].reshape(lead + (GRID_W, width - 1))
    full = skew[..., GRID_W - 1:2 * GRID_W - 1] * float(np.log2(np.e))
    row_ok = np.zeros((2 * NA_WIN_R + 1,), bool)
    row_ok[1:-1] = True
    full = jnp.where(jnp.asarray(in_win[None, None, None] & row_ok[None, None, :, None, None]), full, NEG)
    return jnp.concatenate([full[:, :, :-1], full[:, :, 1:]], axis=-1).astype(F32)


def kernel(x, c, ctx, c_ctx, ada_w, ada_b, norm_w, ffn_w_in, ffn_w_out, mix_w_in, ml_gate_b, ml_norm_w,
           na_qk_w, na_rpb, gq_qk_w, w_br_ml, w_br_na, w_br_gq, w_out):
    b, s, d = x.shape
    l = ctx.shape[1]
    depth = ada_w.shape[0]
    assert b < MOD_ROWS and s % TM == 0 and (b * l) % min(TM, b * l) == 0
    ctx_row = b
    tiles_per_batch = s // TM

    cvec = jnp.zeros((MOD_ROWS, d), F32).at[:b].set(c).at[b].set(c_ctx)
    mod = _ada(cvec, ada_w, ada_b).reshape(depth, MOD_ROWS, N_MOD, d)

    lane = np.arange(2 * LANES)
    gmat = jnp.asarray((lane[:, None] // HEAD_DIM) == (lane[None, :] // HEAD_DIM), BF16)
    idx = np.arange(ML_CHUNK)
    tri = jnp.asarray(np.stack([idx[:, None] >= idx[None, :], idx[:, None] <= idx[None, :]]), BF16)
    rope_tabs = _rope_tables(s)

    w_in, w_o = ffn_w_in.astype(BF16), ffn_w_out.astype(BF16)
    w_proj, wg = _proj_weight(mix_w_in)
    wml, wna, wo = w_br_ml.astype(BF16), w_br_na.astype(BF16), w_out.astype(BF16)
    wgq = jnp.concatenate([w_br_gq[:, HEAD_DIM * h:HEAD_DIM * (h + 1)] for h in GQ_HEAD_ORDER], axis=1).astype(BF16)
    tz = _na_bias_table(na_rpb)

    xl = x.reshape(b * s, d)
    xc = ctx.reshape(b * l, d)
    lat = dict(tiles_per_batch=tiles_per_batch, ctx_row=ctx_row)
    con = dict(tiles_per_batch=None, ctx_row=ctx_row)
    for li in range(depth):
        ctx_out = li < depth - 1
        qkw = jnp.zeros((8, 2 * D_NA), F32)
        qkw = qkw.at[0, :D_NA].set(jnp.tile(na_qk_w[li, 0], H_NA) * Q_PRESCALE)
        qkw = qkw.at[0, D_NA:].set(jnp.tile(na_qk_w[li, 1], H_NA))
        qkw = qkw.at[1, :D_GQ].set(jnp.tile(gq_qk_w[li, 0], H_GQ) * Q_PRESCALE)
        qkw = qkw.at[1, D_GQ:D_GQ + D_KV].set(jnp.tile(gq_qk_w[li, 1], H_KV))
        qkw = qkw.at[2, :4 * H_ML].set(ml_gate_b[li][ML_GATE_ORDER])
        mlw = ml_norm_w[li].reshape(1, D_ML)
        m, nw = mod[li], norm_w[li]

        xl = _ffn(xl, m, nw, w_in, w_o, (li, 0), k0=0, nrm=0, **lat)
        xc = _ffn(xc, m, nw, w_in, w_o, (li, 0), k0=0, nrm=0, **con)

        p_ml, p_mlg, p_na, p_gq = _proj(xl, m, nw, w_proj, li, gmat, qkw, rope_tabs, **lat)
        pc_ml, pc_mlg, pc_na, pc_gq = _proj(xc, m, nw, w_proj, li, gmat, qkw, None, **con)
        seq = lambda a: a.reshape(b, s, a.shape[-1])
        cseq = lambda a: a.reshape(b, l, a.shape[-1])

        hf, hb, hcf, hcb = _mlstm(seq(p_ml), seq(p_mlg), cseq(pc_ml), cseq(pc_mlg), tri)
        o_na = _na(seq(p_na), cseq(pc_na), tz, li)
        o_gq = _gqa(seq(p_gq), cseq(pc_gq))
        flat = lambda a: a.reshape(-1, a.shape[-1])
        xl = _merge(xl, m, nw, flat(hf), flat(hb), p_ml, mlw, gmat, flat(o_na), flat(o_gq),
                    wg, wml, wna, wgq, wo, li, **lat)
        xl = _ffn(xl, m, nw, w_in, w_o, (li, 1), k0=6, nrm=2, **lat)
        if ctx_out:
            co_na = _ctx_attn(cseq(pc_na), qw=D_NA, kw=D_NA, shared_kv=False)
            co_gq = _ctx_attn(cseq(pc_gq), qw=D_GQ, kw=D_KV, shared_kv=True)
            xc = _merge(xc, m, nw, flat(hcf), flat(hcb), pc_ml, mlw, gmat, flat(co_na), flat(co_gq),
                        wg, wml, wna, wgq, wo, li, **con)
            xc = _ffn(xc, m, nw, w_in, w_o, (li, 1), k0=6, nrm=2, **con)
    return xl.reshape(b, s, d)
```

```python
import functools

import numpy as np
import jax
import jax.numpy as jnp
from jax import lax
from jax.experimental import pallas as pl
from jax.experimental.pallas import tpu as pltpu

F32 = jnp.float32
BF16 = jnp.bfloat16

HEAD_DIM = 64
LANES = 128
H_ML, H_NA, H_GQ, H_KV = 4, 6, 6, 2
D_ML, D_NA, D_GQ, D_KV = 256, 384, 384, 128
GRID_W = 64
NA_WIN_R, NA_WIN_C = 8, 16
ROPE_THETA = 10000.0
EPS = 1e-6
N_MOD = 9
ATTN_SCALE = HEAD_DIM ** -0.5
NEG = -1e30

ML_CHUNK = 256
ML_ROWS = 8
NA_QROWS = 32
NA_HALF = 4
NA_BAND = 40
NA_CBAND = 12
NA_SUBS = 10
TM = 1024
GQ_TQ = 512
FFN_CHUNK = 768
Q_PRESCALE = ATTN_SCALE * float(np.log2(np.e))
MOD_ROWS = 16
VMEM_LIMIT = 56 * 1024 * 1024

C_ML, C_MLG, C_NA, C_GQ, C_END = 0, 1024, 1152, 2304, 2944
N_PROJ = C_END
GQ_HEAD_ORDER = (0, 3, 1, 4, 2, 5)
ML_GATE_ORDER = np.array([0, 1, 2, 3, 8, 9, 10, 11, 4, 5, 6, 7, 12, 13, 14, 15])


def _dot(a, b):
    return jnp.dot(a, b, preferred_element_type=F32)


def _dot_nt(a, b):
    return lax.dot_general(a, b, (((1,), (1,)), ((), ())), preferred_element_type=F32)


def _sigmoid(x):
    return 1.0 / (1.0 + jnp.exp(-x))


def _log_sigmoid(x):
    return jnp.minimum(x, 0.0) - jnp.log1p(jnp.exp(-jnp.abs(x)))


def _split_bf16(x):
    parts = []
    for _ in range(3):
        p = x.astype(BF16)
        parts.append(p)
        x = x - p.astype(F32)
    return parts


def _dot_f32_left(x, m01):
    return functools.reduce(jnp.add, [_dot(p, m01) for p in _split_bf16(x)])


def _modnorm(x, nw, shift, scale):
    ms = jnp.mean(x * x, axis=-1, keepdims=True)
    return (x * lax.rsqrt(ms + EPS) * nw) * (1.0 + scale) + shift


def _head_norm(t, gmat, wrow):
    ss = _dot((t * t).astype(BF16), gmat)
    return t * lax.rsqrt(ss * (1.0 / HEAD_DIM) + EPS) * wrow


def _resident(shape, lead=()):
    return pl.BlockSpec((None,) * len(lead) + tuple(shape), lambda *_: tuple(lead) + (0,) * len(shape),
                        pipeline_mode=pl.Buffered(1))


def _params(sem):
    return pltpu.CompilerParams(dimension_semantics=sem, vmem_limit_bytes=VMEM_LIMIT)


def _ada_kernel(c_ref, w_ref, b_ref, o_ref):
    c = c_ref[...]
    s = (c * _sigmoid(c)).astype(BF16)
    o_ref[0] = _dot(s, w_ref[0].astype(BF16)) + b_ref[0]


def _ada(cvec, ada_w, ada_b):
    depth, d, n = ada_w.shape
    tn = n // 8
    return pl.pallas_call(
        _ada_kernel,
        grid=(depth, n // tn),
        in_specs=[pl.BlockSpec((MOD_ROWS, d), lambda l, j: (0, 0)),
                  pl.BlockSpec((1, d, tn), lambda l, j: (l, 0, j)),
                  pl.BlockSpec((1, 1, tn), lambda l, j: (l, 0, j))],
        out_specs=pl.BlockSpec((1, MOD_ROWS, tn), lambda l, j: (l, 0, j)),
        out_shape=jax.ShapeDtypeStruct((depth, MOD_ROWS, n), F32),
        compiler_params=_params(("arbitrary", "arbitrary")),
        name="ada_mod",
    )(cvec, ada_w, ada_b.reshape(depth, 1, n))


def _mod_index(tiles_per_batch, ctx_row):
    if tiles_per_batch is None:
        return lambda i, *_: (ctx_row, 0, 0)
    return lambda i, *_: (i // tiles_per_batch, 0, 0)


def _ffn_kernel(x_ref, mod_ref, nw_ref, wi_ref, wo_ref, o_ref, *, k0, nrm, chunks):
    x = x_ref[...]
    h = _modnorm(x, nw_ref[nrm:nrm + 1, :], mod_ref[0, k0:k0 + 1, :], mod_ref[0, k0 + 1:k0 + 2, :]).astype(BF16)
    dff = wo_ref.shape[0]
    y = None
    for c0, c1 in chunks:
        g = _dot(h, wi_ref[:, c0:c1])
        u = _dot(h, wi_ref[:, dff + c0:dff + c1])
        part = _dot((g * _sigmoid(g) * u).astype(BF16), wo_ref[c0:c1, :])
        y = part if y is None else y + part
    o_ref[...] = x + (0.5 * mod_ref[0, k0 + 2:k0 + 3, :]) * y


def _ffn(x, mod, nw, w_in, w_out, lead, *, k0, nrm, tiles_per_batch, ctx_row):
    t, d = x.shape
    dff = w_out.shape[-2]
    tm = min(TM, t)
    edges = list(range(0, dff, FFN_CHUNK)) + [dff]
    chunks = tuple(zip(edges[:-1], edges[1:]))
    kern = functools.partial(_ffn_kernel, k0=k0, nrm=nrm, chunks=chunks)
    return pl.pallas_call(
        kern,
        grid=(t // tm,),
        in_specs=[pl.BlockSpec((tm, d), lambda i: (i, 0)),
                  pl.BlockSpec((1, N_MOD, d), _mod_index(tiles_per_batch, ctx_row)),
                  pl.BlockSpec((3, d), lambda i: (0, 0)),
                  _resident((d, 2 * dff), lead),
                  _resident((dff, d), lead)],
        out_specs=pl.BlockSpec((tm, d), lambda i: (i, 0)),
        out_shape=jax.ShapeDtypeStruct((t, d), F32),
        compiler_params=_params(("parallel",)),
        name="ffn_swiglu",
    )(x, mod, nw, w_in, w_out)


def _proj_kernel(*refs, rope):
    if rope:
        (x_ref, mod_ref, nw_ref, w_ref, g_ref, qkw_ref, cos_ref, sin_ref,
         ml_ref, mlg_ref, na_ref, gq_ref) = refs
    else:
        (x_ref, mod_ref, nw_ref, w_ref, g_ref, qkw_ref,
         ml_ref, mlg_ref, na_ref, gq_ref) = refs
    h = _modnorm(x_ref[...], nw_ref[1:2, :], mod_ref[0, 3:4, :], mod_ref[0, 4:5, :]).astype(BF16)
    gmat = g_ref[...]

    gates = _dot(h, w_ref[:, C_MLG:C_NA]) + qkw_ref[2:3, 0:LANES]
    glane = lax.broadcasted_iota(jnp.int32, (1, LANES), 1)
    mlg_ref[...] = jnp.where((glane >= 2 * H_ML) & (glane < 4 * H_ML), _log_sigmoid(gates), gates)

    na = _dot(h, w_ref[:, C_NA:C_GQ])
    for j in range(3):
        sl = slice(2 * LANES * j, 2 * LANES * (j + 1))
        na_ref[:, sl] = _head_norm(na[:, sl], gmat, qkw_ref[0:1, sl]).astype(BF16)
    na_ref[:, 2 * D_NA:] = na[:, 2 * D_NA:].astype(BF16)

    gq = _dot(h, w_ref[:, C_GQ:C_END])
    if rope:
        lane = lax.broadcasted_iota(jnp.int32, (1, LANES), 1)
        first_half = (lane % HEAD_DIM) < (HEAD_DIM // 2)
        cos = cos_ref[...]
        sin = sin_ref[...]
    for j in range(2):
        t2 = _head_norm(gq[:, 2 * LANES * j:2 * LANES * (j + 1)], gmat, qkw_ref[1:2, 2 * LANES * j:2 * LANES * (j + 1)])
        for half in range(2):
            t = t2[:, LANES * half:LANES * (half + 1)]
            if rope:
                rot = jnp.where(first_half, pltpu.roll(t, LANES - HEAD_DIM // 2, axis=1),
                                pltpu.roll(t, HEAD_DIM // 2, axis=1))
                t = t * cos + rot * sin
            gq_ref[:, LANES * (2 * j + half):LANES * (2 * j + half + 1)] = t.astype(BF16)
    gq_ref[:, D_GQ + D_KV:] = gq[:, D_GQ + D_KV:].astype(BF16)

    ml_ref[...] = _dot(h, w_ref[:, C_ML:C_MLG]).astype(BF16)


def _proj(x, mod, nw, w, li, gmat, qkw, rope_tabs, *, tiles_per_batch, ctx_row):
    t, d = x.shape
    tm = min(TM, t)
    rope = rope_tabs is not None
    in_specs = [pl.BlockSpec((tm, d), lambda i: (i, 0)),
                pl.BlockSpec((1, N_MOD, d), _mod_index(tiles_per_batch, ctx_row)),
                pl.BlockSpec((3, d), lambda i: (0, 0)),
                _resident((d, N_PROJ), (li,)),
                _resident(gmat.shape),
                _resident(qkw.shape)]
    args = [x, mod, nw, w, gmat, qkw]
    if rope:
        in_specs += [pl.BlockSpec((tm, LANES), lambda i: (i % tiles_per_batch, 0))] * 2
        args += list(rope_tabs)
    widths = (1024, LANES, 3 * D_NA, D_GQ + 2 * D_KV)
    dtypes = (BF16, F32, BF16, BF16)
    return pl.pallas_call(
        functools.partial(_proj_kernel, rope=rope),
        grid=(t // tm,),
        in_specs=in_specs,
        out_specs=[pl.BlockSpec((tm, wd), lambda i: (i, 0)) for wd in widths],
        out_shape=[jax.ShapeDtypeStruct((t, wd), dt) for wd, dt in zip(widths, dtypes)],
        compiler_params=_params(("parallel",)),
        name="mix_in_proj",
    )(*args)


def _ml_prep(d, g_ref, tri_ref):
    log2e = float(np.log2(np.e))
    lc = g_ref.shape[0]
    ng = 2 * H_ML
    gates = g_ref[...] * log2e
    gates_t = gates.T
    ig_t = gates_t[0:ng]
    lf_t = gates_t[0:2 * ng]
    b_t = _dot_f32_left(lf_t, tri_ref[1 - d])[ng:2 * ng]
    btot_t = _dot_f32_left(lf_t, jnp.ones((lc, lc), BF16))[ng:2 * ng]
    lf_al = pltpu.roll(gates, LANES - ng, axis=1)
    c_mat = gates - functools.reduce(jnp.add, [_dot(tri_ref[d], p) for p in _split_bf16(lf_al)])
    return ig_t, b_t, btot_t, c_mat


def _ml_head(d, h, prep, m_ref):
    ig_t, b_t, btot_t, c_mat = prep
    lc = c_mat.shape[0]
    r = H_ML * d + h
    row = lax.broadcasted_iota(jnp.int32, (lc, lc), 0)
    col = lax.broadcasted_iota(jnp.int32, (lc, lc), 1)
    visible = (row <= col) if d == 0 else (row >= col)
    ig, b, b_tot = ig_t[r:r + 1], b_t[r:r + 1], btot_t[r:r + 1]
    m_prev = m_ref[r:r + 1, :]
    w_end = b_tot - b + ig
    m_new = jnp.maximum(b_tot + m_prev, jnp.max(w_end, axis=1, keepdims=True))
    a = jnp.exp2(w_end - m_new)
    decay = jnp.exp2(b_tot + m_prev - m_new)
    m_inter = b + m_prev
    logw = jnp.where(visible, c_mat[:, r:r + 1] + b, NEG)
    m_j = jnp.maximum(m_inter, jnp.max(logw, axis=0, keepdims=True))
    w = jnp.exp2(logw - m_j)
    m_ref[r:r + 1, :] = m_new
    return w, a, jnp.exp2(m_inter - m_j), jnp.exp2(-m_j), decay


def _ml_pair(d, p, head_a, head_b, q_ref, k_ref, v_ref, o_ref, st_ref):
    (w_a, a_a, g_a, fl_a, dec_a), (w_b, a_b, g_b, fl_b, dec_b) = head_a, head_b
    lc = q_ref.shape[0]
    low = _low_lanes()
    sl = slice(LANES * p, LANES * (p + 1))
    q = q_ref[:, sl]
    k = k_ref[:, sl] * ATTN_SCALE
    vt = jnp.concatenate([v_ref[:, sl].astype(F32).T, jnp.ones((LANES, lc), F32)], axis=0)
    head_row = (lax.broadcasted_iota(jnp.int32, (2 * LANES, 1), 0) % LANES) < HEAD_DIM
    vt16 = vt.astype(BF16)
    r_a = _dot(vt16, (_dot_nt(k, _one_head(q, low, 0)) * w_a).astype(BF16))
    r_b = _dot(vt16, (_dot_nt(k, _one_head(q, low, 1)) * w_b).astype(BF16))
    state = st_ref[d, p]
    r_i = _dot_nt(state.astype(BF16), q)
    r = jnp.where(head_row, r_a, r_b) + jnp.where(head_row, g_a, g_b) * r_i
    num, den = r[:LANES], r[LANES:]
    h_t = num / jnp.maximum(jnp.abs(den), jnp.where(head_row[:LANES], fl_a, fl_b))
    o_ref[:, sl] = h_t.T.astype(BF16)

    upd = _dot((vt * jnp.where(head_row, a_a, a_b)).astype(BF16), k)
    same_head = head_row == low
    dec = jnp.where(head_row, dec_a[:, :LANES], dec_b[:, :LANES])
    st_ref[d, p] = dec * state + jnp.where(same_head, upd, 0.0)


def _ml_step(fwd, bwd, tri_ref, st_ref, m_ref):
    dirs = (fwd, bwd)
    preps = [_ml_prep(d, refs[3], tri_ref) for d, refs in enumerate(dirs)]
    heads = [[_ml_head(d, h, preps[d], m_ref) for h in range(H_ML)] for d in range(2)]
    for p in range(H_ML // 2):
        for d, (q_ref, k_ref, v_ref, _, o_ref) in enumerate(dirs):
            _ml_pair(d, p, heads[d][2 * p], heads[d][2 * p + 1], q_ref, k_ref, v_ref, o_ref, st_ref)


def _mlstm_kernel(qf, kf, vf, gf, qb, kb, vb, gb, qc, kc, vc, gc, tri_ref,
                  hf_ref, hb_ref, hcf_ref, hcb_ref, st_ref, m_ref):
    c = pl.program_id(1)

    def rows(bi, *refs):
        return tuple(r.at[bi] for r in refs)

    @pl.when(c == 0)
    def _():
        st_ref[...] = jnp.zeros_like(st_ref)
        m_ref[...] = jnp.zeros_like(m_ref)
        for bi in range(ML_ROWS):
            _ml_step(rows(bi, qc, kc, vc, gc, hcf_ref), rows(bi, qc, kc, vc, gc, hcb_ref), tri_ref,
                     st_ref.at[bi], m_ref.at[bi])

    @pl.when(c > 0)
    def _():
        for bi in range(ML_ROWS):
            _ml_step(rows(bi, qf, kf, vf, gf, hf_ref), rows(bi, qb, kb, vb, gb, hb_ref), tri_ref,
                     st_ref.at[bi], m_ref.at[bi])


def _mlstm(p_ml, p_mlg, pc_ml, pc_mlg, tri):
    b, s, _ = p_ml.shape
    l = pc_ml.shape[1]
    lc = ML_CHUNK
    assert l == lc and s % lc == 0 and b % ML_ROWS == 0
    nl = s // lc
    fwd = lambda c: jnp.maximum(c - 1, 0)
    bwd = lambda c: nl - 1 - jnp.maximum(c - 1, 0)

    def lat(idx, blk, width):
        return pl.BlockSpec((ML_ROWS, lc, width), lambda i, c: (i, idx(c), blk))

    def ctx(blk, width):
        return pl.BlockSpec((ML_ROWS, lc, width), lambda i, c: (i, 0, blk))

    in_specs = ([lat(fwd, 0, D_ML), lat(fwd, 1, D_ML), lat(fwd, 2, D_ML), lat(fwd, 0, LANES)]
                + [lat(bwd, 0, D_ML), lat(bwd, 1, D_ML), lat(bwd, 2, D_ML), lat(bwd, 0, LANES)]
                + [ctx(0, D_ML), ctx(1, D_ML), ctx(2, D_ML), ctx(0, LANES)]
                + [_resident((2, lc, lc))])
    out_specs = [lat(fwd, 0, D_ML), lat(bwd, 0, D_ML), ctx(0, D_ML), ctx(0, D_ML)]
    out_shape = [jax.ShapeDtypeStruct((b, s, D_ML), BF16)] * 2 + [jax.ShapeDtypeStruct((b, l, D_ML), BF16)] * 2
    return pl.pallas_call(
        _mlstm_kernel,
        grid=(b // ML_ROWS, nl + 1),
        in_specs=in_specs,
        out_specs=out_specs,
        out_shape=out_shape,
        scratch_shapes=[pltpu.VMEM((ML_ROWS, 2, H_ML // 2, 2 * LANES, LANES), F32),
                        pltpu.VMEM((ML_ROWS, 2 * H_ML, lc), F32)],
        compiler_params=_params(("parallel", "arbitrary")),
        name="mlstm_bidir",
    )(p_ml, p_ml, p_ml, p_mlg, p_ml, p_ml, p_ml, p_mlg, pc_ml, pc_ml, pc_ml, pc_mlg, tri)


def _attend_heads(n_heads, score_fn, value_fn, s_scr, p_scr):
    s_scr[0] = score_fn(0)
    outs = []
    for i in range(n_heads):
        if i + 1 < n_heads:
            s_scr[(i + 1) % 2] = score_fn(i + 1)
        s = s_scr[i % 2]
        p_scr[i % 2] = jnp.exp2(s - jnp.max(s, axis=1, keepdims=True)).astype(BF16)
        r = _dot(p_scr[i % 2], value_fn(i))
        outs.append(r[:, :LANES] / r[:, LANES:])
    return outs


def _low_lanes():
    return lax.broadcasted_iota(jnp.int32, (1, LANES), 1) < HEAD_DIM


def _one_head(q, low, half):
    zero = jnp.zeros_like(q)
    return jnp.where(low, q, zero) if half == 0 else jnp.where(low, zero, q)


def _pair_outputs(o_ref, outs, low):
    for j in range(len(outs) // 2):
        o_ref[0, :, LANES * j:LANES * (j + 1)] = jnp.where(low, outs[2 * j], outs[2 * j + 1]).astype(BF16)


def _gqa_kernel(q_ref, k_ref, v_ref, kc_ref, vc_ref, o_ref, kall, vall, s_scr, p_scr):
    n_lat = k_ref.shape[1]

    @pl.when(pl.program_id(1) == 0)
    def _():
        kall[0:n_lat, :] = k_ref[0]
        kall[n_lat:, :] = kc_ref[0]
        vall[0:n_lat, 0:LANES] = v_ref[0]
        vall[n_lat:, 0:LANES] = vc_ref[0]
        vall[:, LANES:] = jnp.ones((vall.shape[0], LANES), BF16)

    low = _low_lanes()

    def score(i):
        j, half = divmod(i, 2)
        return _dot_nt(_one_head(q_ref[0, :, LANES * j:LANES * (j + 1)], low, half), kall[...])

    _pair_outputs(o_ref, _attend_heads(H_GQ, score, lambda i: vall[...], s_scr, p_scr), low)


def _gqa(p_gq, pc_gq):
    b, s, _ = p_gq.shape
    l = pc_gq.shape[1]
    tq = min(GQ_TQ, s)
    kblk, vblk = D_GQ // LANES, D_GQ // LANES + 1
    return pl.pallas_call(
        _gqa_kernel,
        grid=(b, s // tq),
        in_specs=[pl.BlockSpec((1, tq, D_GQ), lambda i, t: (i, t, 0)),
                  pl.BlockSpec((1, s, LANES), lambda i, t: (i, 0, kblk)),
                  pl.BlockSpec((1, s, LANES), lambda i, t: (i, 0, vblk)),
                  pl.BlockSpec((1, l, LANES), lambda i, t: (i, 0, kblk)),
                  pl.BlockSpec((1, l, LANES), lambda i, t: (i, 0, vblk))],
        out_specs=pl.BlockSpec((1, tq, D_GQ), lambda i, t: (i, t, 0)),
        out_shape=jax.ShapeDtypeStruct((b, s, D_GQ), BF16),
        scratch_shapes=[pltpu.VMEM((s + l, LANES), BF16), pltpu.VMEM((s + l, 2 * LANES), BF16),
                        pltpu.VMEM((2, tq, s + l), F32), pltpu.VMEM((2, tq, s + l), BF16)],
        compiler_params=_params(("parallel", "arbitrary")),
        name="gqa_latent",
    )(p_gq, p_gq, p_gq, pc_gq, pc_gq)


def _ctx_attn_kernel(q_ref, k_ref, v_ref, o_ref, s_scr, p_scr, *, shared_kv):
    low = _low_lanes()
    ones = jnp.ones((v_ref.shape[1], LANES), BF16)

    def kv_lanes(i):
        return slice(0, LANES) if shared_kv else slice(LANES * (i // 2), LANES * (i // 2 + 1))

    def score(i):
        j, half = divmod(i, 2)
        return _dot_nt(_one_head(q_ref[0, :, LANES * j:LANES * (j + 1)], low, half), k_ref[0, :, kv_lanes(i)])

    def value(i):
        return jnp.concatenate([v_ref[0, :, kv_lanes(i)], ones], axis=1)

    n_heads = 2 * (q_ref.shape[2] // LANES)
    _pair_outputs(o_ref, _attend_heads(n_heads, score, value, s_scr, p_scr), low)


def _ctx_attn(pc, *, qw, kw, shared_kv):
    b, l, _ = pc.shape
    kb = qw // kw
    return pl.pallas_call(
        functools.partial(_ctx_attn_kernel, shared_kv=shared_kv),
        grid=(b,),
        in_specs=[pl.BlockSpec((1, l, qw), lambda i: (i, 0, 0)),
                  pl.BlockSpec((1, l, kw), lambda i: (i, 0, kb)),
                  pl.BlockSpec((1, l, kw), lambda i: (i, 0, kb + 1))],
        out_specs=pl.BlockSpec((1, l, qw), lambda i: (i, 0, 0)),
        out_shape=jax.ShapeDtypeStruct((b, l, qw), BF16),
        scratch_shapes=[pltpu.VMEM((2, l, l), F32), pltpu.VMEM((2, l, l), BF16)],
        compiler_params=_params(("parallel",)),
        name="ctx_attn",
    )(pc, pc, pc)


def _na_kernel(*refs, n_rows):
    q_ref, k_subs, v_subs = refs[0], refs[1:1 + NA_SUBS], refs[1 + NA_SUBS:1 + 2 * NA_SUBS]
    kc_ref, vc_ref, tz_ref, o_ref, kwin, vwin, vctx, s_scr, p_scr = refs[1 + 2 * NA_SUBS:]
    r0 = pl.program_id(1) * NA_QROWS
    start = jnp.clip(r0 - NA_WIN_R // 2, 0, n_rows - NA_BAND)
    nq, nk = NA_HALF * GRID_W, NA_CBAND * GRID_W
    sub = NA_BAND * GRID_W // NA_SUBS
    n_pairs = D_NA // LANES

    for t, (kr, vr) in enumerate(zip(k_subs, v_subs)):
        rows = slice(sub * t, sub * (t + 1))
        kwin[rows, :] = kr[0]
        for j in range(n_pairs):
            vwin[j, rows, 0:LANES] = vr[0, :, LANES * j:LANES * (j + 1)]
    vwin[:, :, LANES:] = jnp.ones((n_pairs,) + vwin.shape[1:2] + (LANES,), BF16)
    for j in range(n_pairs):
        vctx[j, :, 0:LANES] = vc_ref[0, :, LANES * j:LANES * (j + 1)]
    vctx[:, :, LANES:] = jnp.ones((n_pairs,) + vctx.shape[1:2] + (LANES,), BF16)

    low = _low_lanes()
    halves = []
    for hf in range(NA_QROWS // NA_HALF):
        rq = r0 + NA_HALF * hf
        first_key = jnp.clip(rq - NA_WIN_R // 2, 0, n_rows - NA_CBAND)
        off = pl.multiple_of((first_key - start) * GRID_W, sub)
        qrow = rq + lax.broadcasted_iota(jnp.int32, (nq, nk), 0) // GRID_W
        krow = first_key + lax.broadcasted_iota(jnp.int32, (nq, nk), 1) // GRID_W
        first = jnp.clip(qrow - NA_WIN_R // 2, 0, n_rows - NA_WIN_R)
        row_mask = jnp.where((krow >= first) & (krow < first + NA_WIN_R), 0.0, NEG)
        halves.append((off, first_key - rq, row_mask))

    n_inst = len(halves) * H_NA

    def score(i, dst):
        hf, h = divmod(i, H_NA)
        j, half = divmod(h, 2)
        off, delta, row_mask = halves[hf]
        sl = slice(LANES * j, LANES * (j + 1))
        qm = _one_head(q_ref[0, nq * hf:nq * (hf + 1), sl], low, half)
        slabs = []
        for a in range(NA_HALF):
            pieces = [tz_ref[h, jnp.clip(delta + 2 * bp - a, -NA_WIN_R, NA_WIN_R - 1) + NA_WIN_R]
                      for bp in range(NA_CBAND // 2)]
            slabs.append(jnp.concatenate(pieces, axis=1))
        dst[:, :nk] = _dot_nt(qm, kwin[pl.ds(off, nk), sl]) + (jnp.concatenate(slabs, axis=0) + row_mask)
        dst[:, nk:] = _dot_nt(qm, kc_ref[0, :, sl])

    score(0, s_scr.at[0])
    outs = []
    for i in range(n_inst):
        if i + 1 < n_inst:
            score(i + 1, s_scr.at[(i + 1) % 2])
        hf, h = divmod(i, H_NA)
        s = s_scr[i % 2]
        p_scr[i % 2] = jnp.exp2(s - jnp.max(s, axis=1, keepdims=True)).astype(BF16)
        r = (_dot(p_scr[i % 2, :, :nk], vwin[h // 2, pl.ds(halves[hf][0], nk), :])
             + _dot(p_scr[i % 2, :, nk:], vctx[h // 2]))
        outs.append(r[:, :LANES] / r[:, LANES:])
    for hf in range(len(halves)):
        for j in range(n_pairs):
            pair = jnp.where(low, outs[hf * H_NA + 2 * j], outs[hf * H_NA + 2 * j + 1])
            o_ref[0, nq * hf:nq * (hf + 1), LANES * j:LANES * (j + 1)] = pair.astype(BF16)


def _na(p_na, pc_na, tz, li):
    b, s, _ = p_na.shape
    l = pc_na.shape[1]
    n_rows = s // GRID_W
    assert n_rows % NA_QROWS == 0 and n_rows >= NA_BAND
    nq, nk = NA_QROWS * GRID_W, NA_BAND * GRID_W
    nh, nkc = NA_HALF * GRID_W, NA_CBAND * GRID_W
    sub = nk // NA_SUBS
    rows_per_sub = NA_BAND // NA_SUBS
    assert all(v % rows_per_sub == 0 for v in (NA_QROWS, NA_HALF, NA_WIN_R // 2, n_rows - NA_BAND,
                                               n_rows - NA_CBAND))

    def band(t, blk):
        def idx(i, r):
            start = jnp.clip(r * NA_QROWS - NA_WIN_R // 2, 0, n_rows - NA_BAND)
            return (i, start // rows_per_sub + t, blk)
        return pl.BlockSpec((1, sub, D_NA), idx)

    in_specs = ([pl.BlockSpec((1, nq, D_NA), lambda i, r: (i, r, 0))]
                + [band(t, 1) for t in range(NA_SUBS)] + [band(t, 2) for t in range(NA_SUBS)]
                + [pl.BlockSpec((1, l, D_NA), lambda i, r: (i, 0, 1)),
                   pl.BlockSpec((1, l, D_NA), lambda i, r: (i, 0, 2)),
                   _resident(tz.shape[1:], (li,))])
    return pl.pallas_call(
        functools.partial(_na_kernel, n_rows=n_rows),
        grid=(b, n_rows // NA_QROWS),
        in_specs=in_specs,
        out_specs=pl.BlockSpec((1, nq, D_NA), lambda i, r: (i, r, 0)),
        out_shape=jax.ShapeDtypeStruct((b, s, D_NA), BF16),
        scratch_shapes=[pltpu.VMEM((nk, D_NA), BF16), pltpu.VMEM((D_NA // LANES, nk, 2 * LANES), BF16),
                        pltpu.VMEM((D_NA // LANES, l, 2 * LANES), BF16),
                        pltpu.VMEM((2, nh, nkc + l), F32), pltpu.VMEM((2, nh, nkc + l), BF16)],
        compiler_params=_params(("parallel", "arbitrary")),
        name="na_latent",
    )(p_na, *([p_na] * (2 * NA_SUBS)), pc_na, pc_na, tz)


def _merge_kernel(x_ref, mod_ref, nw_ref, hf_ref, hb_ref, og_ref, mlw_ref, g_ref, na_ref, gq_ref,
                  wg_ref, wml_ref, wna_ref, wgq_ref, wo_ref, o_ref):
    x = x_ref[...]
    d = x.shape[1]
    hx = _modnorm(x, nw_ref[1:2, :], mod_ref[0, 3:4, :], mod_ref[0, 4:5, :]).astype(BF16)
    h = hf_ref[...].astype(F32) + hb_ref[...].astype(F32)
    o_ml = (_head_norm(h, g_ref[...], mlw_ref[...]) * _sigmoid(og_ref[...].astype(F32))).astype(BF16)
    y = None
    for j, (o_br, w_br) in enumerate(((o_ml, wml_ref), (na_ref[...], wna_ref), (gq_ref[...], wgq_ref))):
        part = _sigmoid(_dot(hx, wg_ref[:, d * j:d * (j + 1)])) * _dot(o_br, w_br[...])
        y = part if y is None else y + part
    o_ref[...] = x + mod_ref[0, 5:6, :] * _dot(y.astype(BF16), wo_ref[...])


def _merge(x, mod, nw, hf, hb, p_ml, mlw, gmat, o_na, o_gq, wg, wml, wna, wgq, wo, li,
           *, tiles_per_batch, ctx_row):
    t, d = x.shape
    tm = min(TM, t)
    row = lambda wd, blk=0: pl.BlockSpec((tm, wd), lambda i: (i, blk))
    lead = (li,)
    return pl.pallas_call(
        _merge_kernel,
        grid=(t // tm,),
        in_specs=[row(d),
                  pl.BlockSpec((1, N_MOD, d), _mod_index(tiles_per_batch, ctx_row)),
                  pl.BlockSpec((3, d), lambda i: (0, 0)),
                  row(D_ML), row(D_ML), row(D_ML, 3),
                  _resident((1, D_ML)), _resident(gmat.shape),
                  row(D_NA), row(D_GQ),
                  _resident((d, 3 * d), lead), _resident((D_ML, d), lead), _resident((D_NA, d), lead),
                  _resident((D_GQ, d), lead), _resident((d, d), lead)],
        out_specs=row(d),
        out_shape=jax.ShapeDtypeStruct((t, d), F32),
        compiler_params=_params(("parallel",)),
        name="branch_merge",
    )(x, mod, nw, hf, hb, p_ml, mlw, gmat, o_na, o_gq, wg, wml, wna, wgq, wo)


def _proj_weight(w):
    d = w.shape[-2]
    o = 0
    seg = {}
    for name, width in (("ml_k", D_ML), ("ml_v", D_ML), ("ml_g", 4 * H_ML), ("na_k", D_NA), ("na_v", D_NA),
                        ("gq_k", D_KV), ("gq_v", D_KV), ("ml_q", D_ML), ("ml_o", D_ML), ("na_q", D_NA),
                        ("gq_q", D_GQ), ("br_g", 3 * d)):
        seg[name] = w[..., o:o + width].astype(BF16)
        o += width
    gq_q = jnp.concatenate([seg["gq_q"][..., HEAD_DIM * h:HEAD_DIM * (h + 1)] for h in GQ_HEAD_ORDER], axis=-1)
    pad = jnp.zeros(w.shape[:-1] + (LANES - 4 * H_ML,), BF16)
    out = jnp.concatenate([seg["ml_q"], seg["ml_k"], seg["ml_v"], seg["ml_o"], seg["ml_g"][..., ML_GATE_ORDER], pad,
                           seg["na_q"], seg["na_k"], seg["na_v"], gq_q, seg["gq_k"], seg["gq_v"]], axis=-1)
    return out, seg["br_g"]


def _rope_tables(n_tok):
    t = np.arange(n_tok)
    row = (t // GRID_W).astype(np.float64)
    col = (t % GRID_W).astype(np.float64)
    n_freq = HEAD_DIM // 4
    inv = ROPE_THETA ** (-np.arange(n_freq, dtype=np.float64) / n_freq)
    ang = np.concatenate([row[:, None] * inv, col[:, None] * inv], axis=-1)
    cos, sin = np.cos(ang), np.sin(ang)
    cos_t = np.tile(cos, (1, LANES // (HEAD_DIM // 2)))
    sin_t = np.tile(np.concatenate([-sin, sin], axis=-1), (1, LANES // HEAD_DIM))
    return jnp.asarray(cos_t, F32), jnp.asarray(sin_t, F32)


def _na_bias_table(rpb):
    col = np.arange(GRID_W)
    first = np.clip(col - NA_WIN_C // 2, 0, GRID_W - NA_WIN_C)
    in_win = (col[None, :] >= first[:, None]) & (col[None, :] < first[:, None] + NA_WIN_C)
    side = GRID_W - NA_WIN_C
    width = 2 * GRID_W
    rows = jnp.pad(rpb, ((0, 0), (0, 0), (1, 1), (side, width - side - rpb.shape[-1])))
    lead = rows.shape[:-1]
    flat = jnp.broadcast_to(rows[..., None, :], lead + (GRID_W, width)).reshape(lead + (GRID_W * width,))
    skew = flat[..., :GRID_W * (width - 1)].reshape(lead + (GRID_W, width - 1))
    full = skew[..., GRID_W - 1:2 * GRID_W - 1] * float(np.log2(np.e))
    row_ok = np.zeros((2 * NA_WIN_R + 1,), bool)
    row_ok[1:-1] = True
    full = jnp.where(jnp.asarray(in_win[None, None, None] & row_ok[None, None, :, None, None]), full, NEG)
    return jnp.concatenate([full[:, :, :-1], full[:, :, 1:]], axis=-1).astype(F32)


def kernel(x, c, ctx, c_ctx, ada_w, ada_b, norm_w, ffn_w_in, ffn_w_out, mix_w_in, ml_gate_b, ml_norm_w,
           na_qk_w, na_rpb, gq_qk_w, w_br_ml, w_br_na, w_br_gq, w_out):
    b, s, d = x.shape
    l = ctx.shape[1]
    depth = ada_w.shape[0]
    assert b < MOD_ROWS and s % TM == 0 and (b * l) % min(TM, b * l) == 0
    ctx_row = b
    tiles_per_batch = s // TM

    cvec = jnp.zeros((MOD_ROWS, d), F32).at[:b].set(c).at[b].set(c_ctx)
    mod = _ada(cvec, ada_w, ada_b).reshape(depth, MOD_ROWS, N_MOD, d)

    lane = np.arange(2 * LANES)
    gmat = jnp.asarray((lane[:, None] // HEAD_DIM) == (lane[None, :] // HEAD_DIM), BF16)
    idx = np.arange(ML_CHUNK)
    tri = jnp.asarray(np.stack([idx[:, None] >= idx[None, :], idx[:, None] <= idx[None, :]]), BF16)
    rope_tabs = _rope_tables(s)

    w_in, w_o = ffn_w_in.astype(BF16), ffn_w_out.astype(BF16)
    w_proj, wg = _proj_weight(mix_w_in)
    wml, wna, wo = w_br_ml.astype(BF16), w_br_na.astype(BF16), w_out.astype(BF16)
    wgq = jnp.concatenate([w_br_gq[:, HEAD_DIM * h:HEAD_DIM * (h + 1)] for h in GQ_HEAD_ORDER], axis=1).astype(BF16)
    tz = _na_bias_table(na_rpb)

    xl = x.reshape(b * s, d)
    xc = ctx.reshape(b * l, d)
    lat = dict(tiles_per_batch=tiles_per_batch, ctx_row=ctx_row)
    con = dict(tiles_per_batch=None, ctx_row=ctx_row)
    for li in range(depth):
        ctx_out = li < depth - 1
        qkw = jnp.zeros((8, 2 * D_NA), F32)
        qkw = qkw.at[0, :D_NA].set(jnp.tile(na_qk_w[li, 0], H_NA) * Q_PRESCALE)
        qkw = qkw.at[0, D_NA:].set(jnp.tile(na_qk_w[li, 1], H_NA))
        qkw = qkw.at[1, :D_GQ].set(jnp.tile(gq_qk_w[li, 0], H_GQ) * Q_PRESCALE)
        qkw = qkw.at[1, D_GQ:D_GQ + D_KV].set(jnp.tile(gq_qk_w[li, 1], H_KV))
        qkw = qkw.at[2, :4 * H_ML].set(ml_gate_b[li][ML_GATE_ORDER])
        mlw = ml_norm_w[li].reshape(1, D_ML)
        m, nw = mod[li], norm_w[li]

        xl = _ffn(xl, m, nw, w_in, w_o, (li, 0), k0=0, nrm=0, **lat)
        xc = _ffn(xc, m, nw, w_in, w_o, (li, 0), k0=0, nrm=0, **con)

        p_ml, p_mlg, p_na, p_gq = _proj(xl, m, nw, w_proj, li, gmat, qkw, rope_tabs, **lat)
        pc_ml, pc_mlg, pc_na, pc_gq = _proj(xc, m, nw, w_proj, li, gmat, qkw, None, **con)
        seq = lambda a: a.reshape(b, s, a.shape[-1])
        cseq = lambda a: a.reshape(b, l, a.shape[-1])

        hf, hb, hcf, hcb = _mlstm(seq(p_ml), seq(p_mlg), cseq(pc_ml), cseq(pc_mlg), tri)
        o_na = _na(seq(p_na), cseq(pc_na), tz, li)
        o_gq = _gqa(seq(p_gq), cseq(pc_gq))
        flat = lambda a: a.reshape(-1, a.shape[-1])
        xl = _merge(xl, m, nw, flat(hf), flat(hb), p_ml, mlw, gmat, flat(o_na), flat(o_gq),
                    wg, wml, wna, wgq, wo, li, **lat)
        xl = _ffn(xl, m, nw, w_in, w_o, (li, 1), k0=6, nrm=2, **lat)
        if ctx_out:
            co_na = _ctx_attn(cseq(pc_na), qw=D_NA, kw=D_NA, shared_kv=False)
            co_gq = _ctx_attn(cseq(pc_gq), qw=D_GQ, kw=D_KV, shared_kv=True)
            xc = _merge(xc, m, nw, flat(hcf), flat(hcb), pc_ml, mlw, gmat, flat(co_na), flat(co_gq),
                        wg, wml, wna, wgq, wo, li, **con)
            xc = _ffn(xc, m, nw, w_in, w_o, (li, 1), k0=6, nrm=2, **con)
    return xl.reshape(b, s, d)
```

```python
import functools

import numpy as np
import jax
import jax.numpy as jnp
from jax import lax
from jax.experimental import pallas as pl
from jax.experimental.pallas import tpu as pltpu

F32 = jnp.float32
BF16 = jnp.bfloat16

HEAD_DIM = 64
LANES = 128
H_ML, H_NA, H_GQ, H_KV = 4, 6, 6, 2
D_ML, D_NA, D_GQ, D_KV = 256, 384, 384, 128
GRID_W = 64
NA_WIN_R, NA_WIN_C = 8, 16
ROPE_THETA = 10000.0
EPS = 1e-6
N_MOD = 9
ATTN_SCALE = HEAD_DIM ** -0.5
NEG = -1e30

ML_CHUNK = 256
ML_ROWS = 4
NA_QROWS = 32
NA_HALF = 4
NA_BAND = 40
NA_CBAND = 12
NA_SUBS = 10
TM = 1024
GQ_TQ = 512
FFN_CHUNK = 768
Q_PRESCALE = ATTN_SCALE * float(np.log2(np.e))
MOD_ROWS = 16
VMEM_LIMIT = 56 * 1024 * 1024

C_ML, C_MLG, C_NA, C_GQ, C_END = 0, 1024, 1152, 2304, 2944
N_PROJ = C_END
GQ_HEAD_ORDER = (0, 3, 1, 4, 2, 5)
ML_GATE_ORDER = np.array([0, 1, 2, 3, 8, 9, 10, 11, 4, 5, 6, 7, 12, 13, 14, 15])


def _dot(a, b):
    return jnp.dot(a, b, preferred_element_type=F32)


def _dot_nt(a, b):
    return lax.dot_general(a, b, (((1,), (1,)), ((), ())), preferred_element_type=F32)


def _sigmoid(x):
    return 1.0 / (1.0 + jnp.exp(-x))


def _log_sigmoid(x):
    return jnp.minimum(x, 0.0) - jnp.log1p(jnp.exp(-jnp.abs(x)))


def _split_bf16(x):
    parts = []
    for _ in range(3):
        p = x.astype(BF16)
        parts.append(p)
        x = x - p.astype(F32)
    return parts


def _dot_f32_left(x, m01):
    return functools.reduce(jnp.add, [_dot(p, m01) for p in _split_bf16(x)])


def _modnorm(x, nw, shift, scale):
    ms = jnp.mean(x * x, axis=-1, keepdims=True)
    return (x * lax.rsqrt(ms + EPS) * nw) * (1.0 + scale) + shift


def _head_norm(t, gmat, wrow):
    ss = _dot((t * t).astype(BF16), gmat)
    return t * lax.rsqrt(ss * (1.0 / HEAD_DIM) + EPS) * wrow


def _resident(shape, lead=()):
    return pl.BlockSpec((None,) * len(lead) + tuple(shape), lambda *_: tuple(lead) + (0,) * len(shape),
                        pipeline_mode=pl.Buffered(1))


def _params(sem):
    return pltpu.CompilerParams(dimension_semantics=sem, vmem_limit_bytes=VMEM_LIMIT)


def _ada_kernel(c_ref, w_ref, b_ref, o_ref):
    c = c_ref[...]
    s = (c * _sigmoid(c)).astype(BF16)
    o_ref[0] = _dot(s, w_ref[0].astype(BF16)) + b_ref[0]


def _ada(cvec, ada_w, ada_b):
    depth, d, n = ada_w.shape
    tn = n // 8
    return pl.pallas_call(
        _ada_kernel,
        grid=(depth, n // tn),
        in_specs=[pl.BlockSpec((MOD_ROWS, d), lambda l, j: (0, 0)),
                  pl.BlockSpec((1, d, tn), lambda l, j: (l, 0, j)),
                  pl.BlockSpec((1, 1, tn), lambda l, j: (l, 0, j))],
        out_specs=pl.BlockSpec((1, MOD_ROWS, tn), lambda l, j: (l, 0, j)),
        out_shape=jax.ShapeDtypeStruct((depth, MOD_ROWS, n), F32),
        compiler_params=_params(("arbitrary", "arbitrary")),
        name="ada_mod",
    )(cvec, ada_w, ada_b.reshape(depth, 1, n))


def _mod_index(tiles_per_batch, ctx_row):
    if tiles_per_batch is None:
        return lambda i, *_: (ctx_row, 0, 0)
    return lambda i, *_: (i // tiles_per_batch, 0, 0)


def _ffn_kernel(x_ref, mod_ref, nw_ref, wi_ref, wo_ref, o_ref, *, k0, nrm, chunks):
    x = x_ref[...]
    h = _modnorm(x, nw_ref[nrm:nrm + 1, :], mod_ref[0, k0:k0 + 1, :], mod_ref[0, k0 + 1:k0 + 2, :]).astype(BF16)
    dff = wo_ref.shape[0]
    y = None
    for c0, c1 in chunks:
        g = _dot(h, wi_ref[:, c0:c1])
        u = _dot(h, wi_ref[:, dff + c0:dff + c1])
        part = _dot((g * _sigmoid(g) * u).astype(BF16), wo_ref[c0:c1, :])
        y = part if y is None else y + part
    o_ref[...] = x + (0.5 * mod_ref[0, k0 + 2:k0 + 3, :]) * y


def _ffn(x, mod, nw, w_in, w_out, lead, *, k0, nrm, tiles_per_batch, ctx_row):
    t, d = x.shape
    dff = w_out.shape[-2]
    tm = min(TM, t)
    edges = list(range(0, dff, FFN_CHUNK)) + [dff]
    chunks = tuple(zip(edges[:-1], edges[1:]))
    kern = functools.partial(_ffn_kernel, k0=k0, nrm=nrm, chunks=chunks)
    return pl.pallas_call(
        kern,
        grid=(t // tm,),
        in_specs=[pl.BlockSpec((tm, d), lambda i: (i, 0)),
                  pl.BlockSpec((1, N_MOD, d), _mod_index(tiles_per_batch, ctx_row)),
                  pl.BlockSpec((3, d), lambda i: (0, 0)),
                  _resident((d, 2 * dff), lead),
                  _resident((dff, d), lead)],
        out_specs=pl.BlockSpec((tm, d), lambda i: (i, 0)),
        out_shape=jax.ShapeDtypeStruct((t, d), F32),
        compiler_params=_params(("parallel",)),
        name="ffn_swiglu",
    )(x, mod, nw, w_in, w_out)


def _proj_kernel(*refs, rope):
    if rope:
        (x_ref, mod_ref, nw_ref, w_ref, g_ref, qkw_ref, cos_ref, sin_ref,
         ml_ref, mlg_ref, na_ref, gq_ref) = refs
    else:
        (x_ref, mod_ref, nw_ref, w_ref, g_ref, qkw_ref,
         ml_ref, mlg_ref, na_ref, gq_ref) = refs
    h = _modnorm(x_ref[...], nw_ref[1:2, :], mod_ref[0, 3:4, :], mod_ref[0, 4:5, :]).astype(BF16)
    gmat = g_ref[...]

    gates = _dot(h, w_ref[:, C_MLG:C_NA]) + qkw_ref[2:3, 0:LANES]
    glane = lax.broadcasted_iota(jnp.int32, (1, LANES), 1)
    mlg_ref[...] = jnp.where((glane >= 2 * H_ML) & (glane < 4 * H_ML), _log_sigmoid(gates), gates)

    na = _dot(h, w_ref[:, C_NA:C_GQ])
    for j in range(3):
        sl = slice(2 * LANES * j, 2 * LANES * (j + 1))
        na_ref[:, sl] = _head_norm(na[:, sl], gmat, qkw_ref[0:1, sl]).astype(BF16)
    na_ref[:, 2 * D_NA:] = na[:, 2 * D_NA:].astype(BF16)

    gq = _dot(h, w_ref[:, C_GQ:C_END])
    if rope:
        lane = lax.broadcasted_iota(jnp.int32, (1, LANES), 1)
        first_half = (lane % HEAD_DIM) < (HEAD_DIM // 2)
        cos = cos_ref[...]
        sin = sin_ref[...]
    for j in range(2):
        t2 = _head_norm(gq[:, 2 * LANES * j:2 * LANES * (j + 1)], gmat, qkw_ref[1:2, 2 * LANES * j:2 * LANES * (j + 1)])
        for half in range(2):
            t = t2[:, LANES * half:LANES * (half + 1)]
            if rope:
                rot = jnp.where(first_half, pltpu.roll(t, LANES - HEAD_DIM // 2, axis=1),
                                pltpu.roll(t, HEAD_DIM // 2, axis=1))
                t = t * cos + rot * sin
            gq_ref[:, LANES * (2 * j + half):LANES * (2 * j + half + 1)] = t.astype(BF16)
    gq_ref[:, D_GQ + D_KV:] = gq[:, D_GQ + D_KV:].astype(BF16)

    ml_ref[...] = _dot(h, w_ref[:, C_ML:C_MLG]).astype(BF16)


def _proj(x, mod, nw, w, li, gmat, qkw, rope_tabs, *, tiles_per_batch, ctx_row):
    t, d = x.shape
    tm = min(TM, t)
    rope = rope_tabs is not None
    in_specs = [pl.BlockSpec((tm, d), lambda i: (i, 0)),
                pl.BlockSpec((1, N_MOD, d), _mod_index(tiles_per_batch, ctx_row)),
                pl.BlockSpec((3, d), lambda i: (0, 0)),
                _resident((d, N_PROJ), (li,)),
                _resident(gmat.shape),
                _resident(qkw.shape)]
    args = [x, mod, nw, w, gmat, qkw]
    if rope:
        in_specs += [pl.BlockSpec((tm, LANES), lambda i: (i % tiles_per_batch, 0))] * 2
        args += list(rope_tabs)
    widths = (1024, LANES, 3 * D_NA, D_GQ + 2 * D_KV)
    dtypes = (BF16, F32, BF16, BF16)
    return pl.pallas_call(
        functools.partial(_proj_kernel, rope=rope),
        grid=(t // tm,),
        in_specs=in_specs,
        out_specs=[pl.BlockSpec((tm, wd), lambda i: (i, 0)) for wd in widths],
        out_shape=[jax.ShapeDtypeStruct((t, wd), dt) for wd, dt in zip(widths, dtypes)],
        compiler_params=_params(("parallel",)),
        name="mix_in_proj",
    )(*args)


def _ml_prep(d, g_ref, tri_ref):
    log2e = float(np.log2(np.e))
    lc = g_ref.shape[0]
    ng = 2 * H_ML
    gates = g_ref[...] * log2e
    gates_t = gates.T
    ig_t = gates_t[0:ng]
    lf_t = gates_t[0:2 * ng]
    b_t = _dot_f32_left(lf_t, tri_ref[1 - d])[ng:2 * ng]
    btot_t = _dot_f32_left(lf_t, jnp.ones((lc, lc), BF16))[ng:2 * ng]
    lf_al = pltpu.roll(gates, LANES - ng, axis=1)
    c_mat = gates - functools.reduce(jnp.add, [_dot(tri_ref[d], p) for p in _split_bf16(lf_al)])
    return ig_t, b_t, btot_t, c_mat


def _ml_head(d, h, prep, m_ref):
    ig_t, b_t, btot_t, c_mat = prep
    lc = c_mat.shape[0]
    r = H_ML * d + h
    row = lax.broadcasted_iota(jnp.int32, (lc, lc), 0)
    col = lax.broadcasted_iota(jnp.int32, (lc, lc), 1)
    visible = (row <= col) if d == 0 else (row >= col)
    ig, b, b_tot = ig_t[r:r + 1], b_t[r:r + 1], btot_t[r:r + 1]
    m_prev = m_ref[r:r + 1, :]
    w_end = b_tot - b + ig
    m_new = jnp.maximum(b_tot + m_prev, jnp.max(w_end, axis=1, keepdims=True))
    a = jnp.exp2(w_end - m_new)
    decay = jnp.exp2(b_tot + m_prev - m_new)
    m_inter = b + m_prev
    logw = jnp.where(visible, c_mat[:, r:r + 1] + b, NEG)
    m_j = jnp.maximum(m_inter, jnp.max(logw, axis=0, keepdims=True))
    w = jnp.exp2(logw - m_j)
    m_ref[r:r + 1, :] = m_new
    return w, a, jnp.exp2(m_inter - m_j), jnp.exp2(-m_j), decay


def _ml_pair(d, p, head_a, head_b, q_ref, k_ref, v_ref, o_ref, st_ref):
    (w_a, a_a, g_a, fl_a, dec_a), (w_b, a_b, g_b, fl_b, dec_b) = head_a, head_b
    lc = q_ref.shape[0]
    low = _low_lanes()
    sl = slice(LANES * p, LANES * (p + 1))
    q = q_ref[:, sl]
    k = k_ref[:, sl] * ATTN_SCALE
    vt = jnp.concatenate([v_ref[:, sl].astype(F32).T, jnp.ones((LANES, lc), F32)], axis=0)
    head_row = (lax.broadcasted_iota(jnp.int32, (2 * LANES, 1), 0) % LANES) < HEAD_DIM
    vt16 = vt.astype(BF16)
    r_a = _dot(vt16, (_dot_nt(k, _one_head(q, low, 0)) * w_a).astype(BF16))
    r_b = _dot(vt16, (_dot_nt(k, _one_head(q, low, 1)) * w_b).astype(BF16))
    state = st_ref[d, p]
    r_i = _dot_nt(state.astype(BF16), q)
    r = jnp.where(head_row, r_a, r_b) + jnp.where(head_row, g_a, g_b) * r_i
    num, den = r[:LANES], r[LANES:]
    h_t = num / jnp.maximum(jnp.abs(den), jnp.where(head_row[:LANES], fl_a, fl_b))
    o_ref[:, sl] = h_t.T.astype(BF16)

    upd = _dot((vt * jnp.where(head_row, a_a, a_b)).astype(BF16), k)
    same_head = head_row == low
    dec = jnp.where(head_row, dec_a[:, :LANES], dec_b[:, :LANES])
    st_ref[d, p] = dec * state + jnp.where(same_head, upd, 0.0)


def _ml_step(fwd, bwd, tri_ref, st_ref, m_ref):
    dirs = (fwd, bwd)
    preps = [_ml_prep(d, refs[3], tri_ref) for d, refs in enumerate(dirs)]
    heads = [[_ml_head(d, h, preps[d], m_ref) for h in range(H_ML)] for d in range(2)]
    for p in range(H_ML // 2):
        for d, (q_ref, k_ref, v_ref, _, o_ref) in enumerate(dirs):
            _ml_pair(d, p, heads[d][2 * p], heads[d][2 * p + 1], q_ref, k_ref, v_ref, o_ref, st_ref)


def _mlstm_kernel(qf, kf, vf, gf, qb, kb, vb, gb, qc, kc, vc, gc, tri_ref,
                  hf_ref, hb_ref, hcf_ref, hcb_ref, st_ref, m_ref):
    c = pl.program_id(1)

    def rows(bi, *refs):
        return tuple(r.at[bi] for r in refs)

    @pl.when(c == 0)
    def _():
        st_ref[...] = jnp.zeros_like(st_ref)
        m_ref[...] = jnp.zeros_like(m_ref)
        for bi in range(ML_ROWS):
            _ml_step(rows(bi, qc, kc, vc, gc, hcf_ref), rows(bi, qc, kc, vc, gc, hcb_ref), tri_ref,
                     st_ref.at[bi], m_ref.at[bi])

    @pl.when(c > 0)
    def _():
        for bi in range(ML_ROWS):
            _ml_step(rows(bi, qf, kf, vf, gf, hf_ref), rows(bi, qb, kb, vb, gb, hb_ref), tri_ref,
                     st_ref.at[bi], m_ref.at[bi])


def _mlstm(p_ml, p_mlg, pc_ml, pc_mlg, tri):
    b, s, _ = p_ml.shape
    l = pc_ml.shape[1]
    lc = ML_CHUNK
    assert l == lc and s % lc == 0 and b % ML_ROWS == 0
    nl = s // lc
    fwd = lambda c: jnp.maximum(c - 1, 0)
    bwd = lambda c: nl - 1 - jnp.maximum(c - 1, 0)

    def lat(idx, blk, width):
        return pl.BlockSpec((ML_ROWS, lc, width), lambda i, c: (i, idx(c), blk))

    def ctx(blk, width):
        return pl.BlockSpec((ML_ROWS, lc, width), lambda i, c: (i, 0, blk))

    in_specs = ([lat(fwd, 0, D_ML), lat(fwd, 1, D_ML), lat(fwd, 2, D_ML), lat(fwd, 0, LANES)]
                + [lat(bwd, 0, D_ML), lat(bwd, 1, D_ML), lat(bwd, 2, D_ML), lat(bwd, 0, LANES)]
                + [ctx(0, D_ML), ctx(1, D_ML), ctx(2, D_ML), ctx(0, LANES)]
                + [_resident((2, lc, lc))])
    out_specs = [lat(fwd, 0, D_ML), lat(bwd, 0, D_ML), ctx(0, D_ML), ctx(0, D_ML)]
    out_shape = [jax.ShapeDtypeStruct((b, s, D_ML), BF16)] * 2 + [jax.ShapeDtypeStruct((b, l, D_ML), BF16)] * 2
    return pl.pallas_call(
        _mlstm_kernel,
        grid=(b // ML_ROWS, nl + 1),
        in_specs=in_specs,
        out_specs=out_specs,
        out_shape=out_shape,
        scratch_shapes=[pltpu.VMEM((ML_ROWS, 2, H_ML // 2, 2 * LANES, LANES), F32),
                        pltpu.VMEM((ML_ROWS, 2 * H_ML, lc), F32)],
        compiler_params=_params(("parallel", "arbitrary")),
        name="mlstm_bidir",
    )(p_ml, p_ml, p_ml, p_mlg, p_ml, p_ml, p_ml, p_mlg, pc_ml, pc_ml, pc_ml, pc_mlg, tri)


def _attend_heads(n_heads, score_fn, value_fn, s_scr, p_scr):
    s_scr[0] = score_fn(0)
    outs = []
    for i in range(n_heads):
        if i + 1 < n_heads:
            s_scr[(i + 1) % 2] = score_fn(i + 1)
        s = s_scr[i % 2]
        p_scr[i % 2] = jnp.exp2(s - jnp.max(s, axis=1, keepdims=True)).astype(BF16)
        r = _dot(p_scr[i % 2], value_fn(i))
        outs.append(r[:, :LANES] / r[:, LANES:])
    return outs


def _low_lanes():
    return lax.broadcasted_iota(jnp.int32, (1, LANES), 1) < HEAD_DIM


def _one_head(q, low, half):
    zero = jnp.zeros_like(q)
    return jnp.where(low, q, zero) if half == 0 else jnp.where(low, zero, q)


def _pair_outputs(o_ref, outs, low):
    for j in range(len(outs) // 2):
        o_ref[0, :, LANES * j:LANES * (j + 1)] = jnp.where(low, outs[2 * j], outs[2 * j + 1]).astype(BF16)


def _gqa_kernel(q_ref, k_ref, v_ref, kc_ref, vc_ref, o_ref, kall, vall, s_scr, p_scr):
    n_lat = k_ref.shape[1]

    @pl.when(pl.program_id(1) == 0)
    def _():
        kall[0:n_lat, :] = k_ref[0]
        kall[n_lat:, :] = kc_ref[0]
        vall[0:n_lat, 0:LANES] = v_ref[0]
        vall[n_lat:, 0:LANES] = vc_ref[0]
        vall[:, LANES:] = jnp.ones((vall.shape[0], LANES), BF16)

    low = _low_lanes()

    def score(i):
        j, half = divmod(i, 2)
        return _dot_nt(_one_head(q_ref[0, :, LANES * j:LANES * (j + 1)], low, half), kall[...])

    _pair_outputs(o_ref, _attend_heads(H_GQ, score, lambda i: vall[...], s_scr, p_scr), low)


def _gqa(p_gq, pc_gq):
    b, s, _ = p_gq.shape
    l = pc_gq.shape[1]
    tq = min(GQ_TQ, s)
    kblk, vblk = D_GQ // LANES, D_GQ // LANES + 1
    return pl.pallas_call(
        _gqa_kernel,
        grid=(b, s // tq),
        in_specs=[pl.BlockSpec((1, tq, D_GQ), lambda i, t: (i, t, 0)),
                  pl.BlockSpec((1, s, LANES), lambda i, t: (i, 0, kblk)),
                  pl.BlockSpec((1, s, LANES), lambda i, t: (i, 0, vblk)),
                  pl.BlockSpec((1, l, LANES), lambda i, t: (i, 0, kblk)),
                  pl.BlockSpec((1, l, LANES), lambda i, t: (i, 0, vblk))],
        out_specs=pl.BlockSpec((1, tq, D_GQ), lambda i, t: (i, t, 0)),
        out_shape=jax.ShapeDtypeStruct((b, s, D_GQ), BF16),
        scratch_shapes=[pltpu.VMEM((s + l, LANES), BF16), pltpu.VMEM((s + l, 2 * LANES), BF16),
                        pltpu.VMEM((2, tq, s + l), F32), pltpu.VMEM((2, tq, s + l), BF16)],
        compiler_params=_params(("parallel", "arbitrary")),
        name="gqa_latent",
    )(p_gq, p_gq, p_gq, pc_gq, pc_gq)


def _ctx_attn_kernel(q_ref, k_ref, v_ref, o_ref, s_scr, p_scr, *, shared_kv):
    low = _low_lanes()
    ones = jnp.ones((v_ref.shape[1], LANES), BF16)

    def kv_lanes(i):
        return slice(0, LANES) if shared_kv else slice(LANES * (i // 2), LANES * (i // 2 + 1))

    def score(i):
        j, half = divmod(i, 2)
        return _dot_nt(_one_head(q_ref[0, :, LANES * j:LANES * (j + 1)], low, half), k_ref[0, :, kv_lanes(i)])

    def value(i):
        return jnp.concatenate([v_ref[0, :, kv_lanes(i)], ones], axis=1)

    n_heads = 2 * (q_ref.shape[2] // LANES)
    _pair_outputs(o_ref, _attend_heads(n_heads, score, value, s_scr, p_scr), low)


def _ctx_attn(pc, *, qw, kw, shared_kv):
    b, l, _ = pc.shape
    kb = qw // kw
    return pl.pallas_call(
        functools.partial(_ctx_attn_kernel, shared_kv=shared_kv),
        grid=(b,),
        in_specs=[pl.BlockSpec((1, l, qw), lambda i: (i, 0, 0)),
                  pl.BlockSpec((1, l, kw), lambda i: (i, 0, kb)),
                  pl.BlockSpec((1, l, kw), lambda i: (i, 0, kb + 1))],
        out_specs=pl.BlockSpec((1, l, qw), lambda i: (i, 0, 0)),
        out_shape=jax.ShapeDtypeStruct((b, l, qw), BF16),
        scratch_shapes=[pltpu.VMEM((2, l, l), F32), pltpu.VMEM((2, l, l), BF16)],
        compiler_params=_params(("parallel",)),
        name="ctx_attn",
    )(pc, pc, pc)


def _na_kernel(*refs, n_rows):
    q_ref, k_subs, v_subs = refs[0], refs[1:1 + NA_SUBS], refs[1 + NA_SUBS:1 + 2 * NA_SUBS]
    kc_ref, vc_ref, tz_ref, o_ref, kwin, vwin, vctx, s_scr, p_scr = refs[1 + 2 * NA_SUBS:]
    r0 = pl.program_id(1) * NA_QROWS
    start = jnp.clip(r0 - NA_WIN_R // 2, 0, n_rows - NA_BAND)
    nq, nk = NA_HALF * GRID_W, NA_CBAND * GRID_W
    sub = NA_BAND * GRID_W // NA_SUBS
    n_pairs = D_NA // LANES

    for t, (kr, vr) in enumerate(zip(k_subs, v_subs)):
        rows = slice(sub * t, sub * (t + 1))
        kwin[rows, :] = kr[0]
        for j in range(n_pairs):
            vwin[j, rows, 0:LANES] = vr[0, :, LANES * j:LANES * (j + 1)]
    vwin[:, :, LANES:] = jnp.ones((n_pairs,) + vwin.shape[1:2] + (LANES,), BF16)
    for j in range(n_pairs):
        vctx[j, :, 0:LANES] = vc_ref[0, :, LANES * j:LANES * (j + 1)]
    vctx[:, :, LANES:] = jnp.ones((n_pairs,) + vctx.shape[1:2] + (LANES,), BF16)

    low = _low_lanes()
    halves = []
    for hf in range(NA_QROWS // NA_HALF):
        rq = r0 + NA_HALF * hf
        first_key = jnp.clip(rq - NA_WIN_R // 2, 0, n_rows - NA_CBAND)
        off = pl.multiple_of((first_key - start) * GRID_W, sub)
        qrow = rq + lax.broadcasted_iota(jnp.int32, (nq, nk), 0) // GRID_W
        krow = first_key + lax.broadcasted_iota(jnp.int32, (nq, nk), 1) // GRID_W
        first = jnp.clip(qrow - NA_WIN_R // 2, 0, n_rows - NA_WIN_R)
        row_mask = jnp.where((krow >= first) & (krow < first + NA_WIN_R), 0.0, NEG)
        halves.append((off, first_key - rq, row_mask))

    n_inst = len(halves) * H_NA

    def score(i, dst):
        hf, h = divmod(i, H_NA)
        j, half = divmod(h, 2)
        off, delta, row_mask = halves[hf]
        sl = slice(LANES * j, LANES * (j + 1))
        qm = _one_head(q_ref[0, nq * hf:nq * (hf + 1), sl], low, half)
        slabs = []
        for a in range(NA_HALF):
            pieces = [tz_ref[h, jnp.clip(delta + 2 * bp - a, -NA_WIN_R, NA_WIN_R - 1) + NA_WIN_R]
                      for bp in range(NA_CBAND // 2)]
            slabs.append(jnp.concatenate(pieces, axis=1))
        dst[:, :nk] = _dot_nt(qm, kwin[pl.ds(off, nk), sl]) + (jnp.concatenate(slabs, axis=0) + row_mask)
        dst[:, nk:] = _dot_nt(qm, kc_ref[0, :, sl])

    score(0, s_scr.at[0])
    outs = []
    for i in range(n_inst):
        if i + 1 < n_inst:
            score(i + 1, s_scr.at[(i + 1) % 2])
        hf, h = divmod(i, H_NA)
        s = s_scr[i % 2]
        p_scr[i % 2] = jnp.exp2(s - jnp.max(s, axis=1, keepdims=True)).astype(BF16)
        r = (_dot(p_scr[i % 2, :, :nk], vwin[h // 2, pl.ds(halves[hf][0], nk), :])
             + _dot(p_scr[i % 2, :, nk:], vctx[h // 2]))
        outs.append(r[:, :LANES] / r[:, LANES:])
    for hf in range(len(halves)):
        for j in range(n_pairs):
            pair = jnp.where(low, outs[hf * H_NA + 2 * j], outs[hf * H_NA + 2 * j + 1])
            o_ref[0, nq * hf:nq * (hf + 1), LANES * j:LANES * (j + 1)] = pair.astype(BF16)


def _na(p_na, pc_na, tz, li):
    b, s, _ = p_na.shape
    l = pc_na.shape[1]
    n_rows = s // GRID_W
    assert n_rows % NA_QROWS == 0 and n_rows >= NA_BAND
    nq, nk = NA_QROWS * GRID_W, NA_BAND * GRID_W
    nh, nkc = NA_HALF * GRID_W, NA_CBAND * GRID_W
    sub = nk // NA_SUBS
    rows_per_sub = NA_BAND // NA_SUBS
    assert all(v % rows_per_sub == 0 for v in (NA_QROWS, NA_HALF, NA_WIN_R // 2, n_rows - NA_BAND,
                                               n_rows - NA_CBAND))

    def band(t, blk):
        def idx(i, r):
            start = jnp.clip(r * NA_QROWS - NA_WIN_R // 2, 0, n_rows - NA_BAND)
            return (i, start // rows_per_sub + t, blk)
        return pl.BlockSpec((1, sub, D_NA), idx)

    in_specs = ([pl.BlockSpec((1, nq, D_NA), lambda i, r: (i, r, 0))]
                + [band(t, 1) for t in range(NA_SUBS)] + [band(t, 2) for t in range(NA_SUBS)]
                + [pl.BlockSpec((1, l, D_NA), lambda i, r: (i, 0, 1)),
                   pl.BlockSpec((1, l, D_NA), lambda i, r: (i, 0, 2)),
                   _resident(tz.shape[1:], (li,))])
    return pl.pallas_call(
        functools.partial(_na_kernel, n_rows=n_rows),
        grid=(b, n_rows // NA_QROWS),
        in_specs=in_specs,
        out_specs=pl.BlockSpec((1, nq, D_NA), lambda i, r: (i, r, 0)),
        out_shape=jax.ShapeDtypeStruct((b, s, D_NA), BF16),
        scratch_shapes=[pltpu.VMEM((nk, D_NA), BF16), pltpu.VMEM((D_NA // LANES, nk, 2 * LANES), BF16),
                        pltpu.VMEM((D_NA // LANES, l, 2 * LANES), BF16),
                        pltpu.VMEM((2, nh, nkc + l), F32), pltpu.VMEM((2, nh, nkc + l), BF16)],
        compiler_params=_params(("parallel", "arbitrary")),
        name="na_latent",
    )(p_na, *([p_na] * (2 * NA_SUBS)), pc_na, pc_na, tz)


def _merge_kernel(x_ref, mod_ref, nw_ref, hf_ref, hb_ref, og_ref, mlw_ref, g_ref, na_ref, gq_ref,
                  wg_ref, wml_ref, wna_ref, wgq_ref, wo_ref, o_ref):
    x = x_ref[...]
    d = x.shape[1]
    hx = _modnorm(x, nw_ref[1:2, :], mod_ref[0, 3:4, :], mod_ref[0, 4:5, :]).astype(BF16)
    h = hf_ref[...].astype(F32) + hb_ref[...].astype(F32)
    o_ml = (_head_norm(h, g_ref[...], mlw_ref[...]) * _sigmoid(og_ref[...].astype(F32))).astype(BF16)
    y = None
    for j, (o_br, w_br) in enumerate(((o_ml, wml_ref), (na_ref[...], wna_ref), (gq_ref[...], wgq_ref))):
        part = _sigmoid(_dot(hx, wg_ref[:, d * j:d * (j + 1)])) * _dot(o_br, w_br[...])
        y = part if y is None else y + part
    o_ref[...] = x + mod_ref[0, 5:6, :] * _dot(y.astype(BF16), wo_ref[...])


def _merge(x, mod, nw, hf, hb, p_ml, mlw, gmat, o_na, o_gq, wg, wml, wna, wgq, wo, li,
           *, tiles_per_batch, ctx_row):
    t, d = x.shape
    tm = min(TM, t)
    row = lambda wd, blk=0: pl.BlockSpec((tm, wd), lambda i: (i, blk))
    lead = (li,)
    return pl.pallas_call(
        _merge_kernel,
        grid=(t // tm,),
        in_specs=[row(d),
                  pl.BlockSpec((1, N_MOD, d), _mod_index(tiles_per_batch, ctx_row)),
                  pl.BlockSpec((3, d), lambda i: (0, 0)),
                  row(D_ML), row(D_ML), row(D_ML, 3),
                  _resident((1, D_ML)), _resident(gmat.shape),
                  row(D_NA), row(D_GQ),
                  _resident((d, 3 * d), lead), _resident((D_ML, d), lead), _resident((D_NA, d), lead),
                  _resident((D_GQ, d), lead), _resident((d, d), lead)],
        out_specs=row(d),
        out_shape=jax.ShapeDtypeStruct((t, d), F32),
        compiler_params=_params(("parallel",)),
        name="branch_merge",
    )(x, mod, nw, hf, hb, p_ml, mlw, gmat, o_na, o_gq, wg, wml, wna, wgq, wo)


def _proj_weight(w):
    d = w.shape[-2]
    o = 0
    seg = {}
    for name, width in (("ml_k", D_ML), ("ml_v", D_ML), ("ml_g", 4 * H_ML), ("na_k", D_NA), ("na_v", D_NA),
                        ("gq_k", D_KV), ("gq_v", D_KV), ("ml_q", D_ML), ("ml_o", D_ML), ("na_q", D_NA),
                        ("gq_q", D_GQ), ("br_g", 3 * d)):
        seg[name] = w[..., o:o + width].astype(BF16)
        o += width
    gq_q = jnp.concatenate([seg["gq_q"][..., HEAD_DIM * h:HEAD_DIM * (h + 1)] for h in GQ_HEAD_ORDER], axis=-1)
    pad = jnp.zeros(w.shape[:-1] + (LANES - 4 * H_ML,), BF16)
    out = jnp.concatenate([seg["ml_q"], seg["ml_k"], seg["ml_v"], seg["ml_o"], seg["ml_g"][..., ML_GATE_ORDER], pad,
                           seg["na_q"], seg["na_k"], seg["na_v"], gq_q, seg["gq_k"], seg["gq_v"]], axis=-1)
    return out, seg["br_g"]


def _rope_tables(n_tok):
    t = np.arange(n_tok)
    row = (t // GRID_W).astype(np.float64)
    col = (t % GRID_W).astype(np.float64)
    n_freq = HEAD_DIM // 4
    inv = ROPE_THETA ** (-np.arange(n_freq, dtype=np.float64) / n_freq)
    ang = np.concatenate([row[:, None] * inv, col[:, None] * inv], axis=-1)
    cos, sin = np.cos(ang), np.sin(ang)
    cos_t = np.tile(cos, (1, LANES // (HEAD_DIM // 2)))
    sin_t = np.tile(np.concatenate([-sin, sin], axis=-1), (1, LANES // HEAD_DIM))
    return jnp.asarray(cos_t, F32), jnp.asarray(sin_t, F32)


def _na_bias_table(rpb):
    col = np.arange(GRID_W)
    first = np.clip(col - NA_WIN_C // 2, 0, GRID_W - NA_WIN_C)
    in_win = (col[None, :] >= first[:, None]) & (col[None, :] < first[:, None] + NA_WIN_C)
    side = GRID_W - NA_WIN_C
    width = 2 * GRID_W
    rows = jnp.pad(rpb, ((0, 0), (0, 0), (1, 1), (side, width - side - rpb.shape[-1])))
    lead = rows.shape[:-1]
    flat = jnp.broadcast_to(rows[..., None, :], lead + (GRID_W, width)).reshape(lead + (GRID_W * width,))
    skew = flat[..., :GRID_W * (width - 1)].reshape(lead + (GRID_W, width - 1))
    full = skew[..., GRID_W - 1:2 * GRID_W - 1] * float(np.log2(np.e))
    row_ok = np.zeros((2 * NA_WIN_R + 1,), bool)
    row_ok[1:-1] = True
    full = jnp.where(jnp.asarray(in_win[None, None, None] & row_ok[None, None, :, None, None]), full, NEG)
    return jnp.concatenate([full[:, :, :-1], full[:, :, 1:]], axis=-1).astype(F32)


def kernel(x, c, ctx, c_ctx, ada_w, ada_b, norm_w, ffn_w_in, ffn_w_out, mix_w_in, ml_gate_b, ml_norm_w,
           na_qk_w, na_rpb, gq_qk_w, w_br_ml, w_br_na, w_br_gq, w_out):
    b, s, d = x.shape
    l = ctx.shape[1]
    depth = ada_w.shape[0]
    assert b < MOD_ROWS and s % TM == 0 and (b * l) % min(TM, b * l) == 0
    ctx_row = b
    tiles_per_batch = s // TM

    cvec = jnp.zeros((MOD_ROWS, d), F32).at[:b].set(c).at[b].set(c_ctx)
    mod = _ada(cvec, ada_w, ada_b).reshape(depth, MOD_ROWS, N_MOD, d)

    lane = np.arange(2 * LANES)
    gmat = jnp.asarray((lane[:, None] // HEAD_DIM) == (lane[None, :] // HEAD_DIM), BF16)
    idx = np.arange(ML_CHUNK)
    tri = jnp.asarray(np.stack([idx[:, None] >= idx[None, :], idx[:, None] <= idx[None, :]]), BF16)
    rope_tabs = _rope_tables(s)

    w_in, w_o = ffn_w_in.astype(BF16), ffn_w_out.astype(BF16)
    w_proj, wg = _proj_weight(mix_w_in)
    wml, wna, wo = w_br_ml.astype(BF16), w_br_na.astype(BF16), w_out.astype(BF16)
    wgq = jnp.concatenate([w_br_gq[:, HEAD_DIM * h:HEAD_DIM * (h + 1)] for h in GQ_HEAD_ORDER], axis=1).astype(BF16)
    tz = _na_bias_table(na_rpb)

    xl = x.reshape(b * s, d)
    xc = ctx.reshape(b * l, d)
    lat = dict(tiles_per_batch=tiles_per_batch, ctx_row=ctx_row)
    con = dict(tiles_per_batch=None, ctx_row=ctx_row)
    for li in range(depth):
        ctx_out = li < depth - 1
        qkw = jnp.zeros((8, 2 * D_NA), F32)
        qkw = qkw.at[0, :D_NA].set(jnp.tile(na_qk_w[li, 0], H_NA) * Q_PRESCALE)
        qkw = qkw.at[0, D_NA:].set(jnp.tile(na_qk_w[li, 1], H_NA))
        qkw = qkw.at[1, :D_GQ].set(jnp.tile(gq_qk_w[li, 0], H_GQ) * Q_PRESCALE)
        qkw = qkw.at[1, D_GQ:D_GQ + D_KV].set(jnp.tile(gq_qk_w[li, 1], H_KV))
        qkw = qkw.at[2, :4 * H_ML].set(ml_gate_b[li][ML_GATE_ORDER])
        mlw = ml_norm_w[li].reshape(1, D_ML)
        m, nw = mod[li], norm_w[li]

        xl = _ffn(xl, m, nw, w_in, w_o, (li, 0), k0=0, nrm=0, **lat)
        xc = _ffn(xc, m, nw, w_in, w_o, (li, 0), k0=0, nrm=0, **con)

        p_ml, p_mlg, p_na, p_gq = _proj(xl, m, nw, w_proj, li, gmat, qkw, rope_tabs, **lat)
        pc_ml, pc_mlg, pc_na, pc_gq = _proj(xc, m, nw, w_proj, li, gmat, qkw, None, **con)
        seq = lambda a: a.reshape(b, s, a.shape[-1])
        cseq = lambda a: a.reshape(b, l, a.shape[-1])

        hf, hb, hcf, hcb = _mlstm(seq(p_ml), seq(p_mlg), cseq(pc_ml), cseq(pc_mlg), tri)
        o_na = _na(seq(p_na), cseq(pc_na), tz, li)
        o_gq = _gqa(seq(p_gq), cseq(pc_gq))
        flat = lambda a: a.reshape(-1, a.shape[-1])
        xl = _merge(xl, m, nw, flat(hf), flat(hb), p_ml, mlw, gmat, flat(o_na), flat(o_gq),
                    wg, wml, wna, wgq, wo, li, **lat)
        xl = _ffn(xl, m, nw, w_in, w_o, (li, 1), k0=6, nrm=2, **lat)
        if ctx_out:
            co_na = _ctx_attn(cseq(pc_na), qw=D_NA, kw=D_NA, shared_kv=False)
            co_gq = _ctx_attn(cseq(pc_gq), qw=D_GQ, kw=D_KV, shared_kv=True)
            xc = _merge(xc, m, nw, flat(hcf), flat(hcb), pc_ml, mlw, gmat, flat(co_na), flat(co_gq),
                        wg, wml, wna, wgq, wo, li, **con)
            xc = _ffn(xc, m, nw, w_in, w_o, (li, 1), k0=6, nrm=2, **con)
    return xl.reshape(b, s, d)
```
